```python
import math
import jax, jax.numpy as jnp
from jax import lax
import numpy as np

D_MODEL = 1024
BATCH = 4
SEQ = 8192
DEPTH = 2

N_MIXERS = 2
EPS = 1e-6
ML_H = 4
ML_DV = 2 * D_MODEL // ML_H
ML_DQK = ML_DV // 2
ML_QK = ML_H * ML_DQK
ML_V = ML_H * ML_DV
ML_PROJ = 2 * ML_QK + 2 * ML_V + 2 * ML_H
ML_CHUNK = 64
CONV_K = 4
DA_DH = 64
DA_DV = 2 * DA_DH
DA_H = D_MODEL // DA_DV
DA_QK = DA_H * 2 * DA_DH
DA_PROJ = 2 * DA_QK + DA_H * DA_DV
DA_EPS = 1e-5
Q_BLOCK = 128
N_GROUPS = 4
EXPERTS_PER_GROUP = 8
N_EXPERTS = N_GROUPS * EXPERTS_PER_GROUP
TOP_K_IN_GROUP = 2
D_EXPERT = D_MODEL // 4
TOK_BLOCK = 128

kernel_name = 'hybrid_mlstm_diffattn_hmoe'


def rmsnorm(x, g, eps=EPS):
    xf = x.astype(jnp.float32)
    y = xf * lax.rsqrt(jnp.mean(xf * xf, axis=-1, keepdims=True) + eps)
    return (y * g.astype(jnp.float32)).astype(x.dtype)


def causal_conv(u, w):
    K = w.shape[0]
    S = u.shape[1]
    up = jnp.pad(u, ((0, 0), (K - 1, 0), (0, 0)))
    return sum(up[:, j:j + S] * w[j] for j in range(K))


def mlstm_mixer(h, w_in, conv_w, b_i, b_f, norm_g, w_out):
    B, S, _ = h.shape
    f32 = jnp.float32
    p = h @ w_in
    qk = jax.nn.silu(causal_conv(p[..., :2 * ML_QK], conv_w))
    q = qk[..., :ML_QK] * (ML_DQK ** -0.5)
    k = qk[..., ML_QK:]
    v = p[..., 2 * ML_QK:2 * ML_QK + ML_V]
    o = p[..., 2 * ML_QK + ML_V:2 * ML_QK + 2 * ML_V]
    gates = p[..., 2 * ML_QK + 2 * ML_V:].astype(f32)
    i_pre = gates[..., :ML_H] + b_i.astype(f32)
    log_f = jax.nn.log_sigmoid(gates[..., ML_H:] + b_f.astype(f32))
    nc = S // ML_CHUNK

    def to_chunks(t, d):
        return t.astype(f32).reshape(B, nc, ML_CHUNK, ML_H, d).transpose(1, 0, 3, 2, 4)

    def gate_chunks(g):
        return g.reshape(B, nc, ML_CHUNK, ML_H).transpose(1, 0, 3, 2)

    qc, kc, vc = to_chunks(q, ML_DQK), to_chunks(k, ML_DQK), to_chunks(v, ML_DV)
    ic, fc = gate_chunks(i_pre), gate_chunks(log_f)
    causal = jnp.tril(jnp.ones((ML_CHUNK, ML_CHUNK), dtype=bool))

    def step(carry, inp):
        C, n, m = carry
        qt, kt, vt, it, lf = inp
        b = jnp.cumsum(lf, axis=-1)
        dmat = b[..., :, None] - b[..., None, :] + it[..., None, :]
        dmat = jnp.where(causal, dmat, -jnp.inf)
        inter_log = b + m[..., None]
        m_t = jnp.maximum(inter_log, jnp.max(dmat, axis=-1))
        wts = jnp.exp(dmat - m_t[..., None])
        scores = jnp.einsum('bhtd,bhsd->bhts', qt, kt) * wts
        inter_scale = jnp.exp(inter_log - m_t)
        num = jnp.einsum('bhts,bhsv->bhtv', scores, vt) + inter_scale[..., None] * jnp.einsum('bhvd,bhtd->bhtv', C, qt)
        den = jnp.sum(scores, axis=-1) + inter_scale * jnp.einsum('bhd,bhtd->bht', n, qt)
        h_out = num / jnp.maximum(jnp.abs(den), jnp.exp(-m_t))[..., None]
        bL = b[..., -1]
        log_w = bL[..., None] - b + it
        m_new = jnp.maximum(bL + m, jnp.max(log_w, axis=-1))
        ws = jnp.exp(log_w - m_new[..., None])
        decay = jnp.exp(bL + m - m_new)
        C_new = decay[..., None, None] * C + jnp.einsum('bhsv,bhsd->bhvd', vt * ws[..., None], kt)
        n_new = decay[..., None] * n + jnp.einsum('bhs,bhsd->bhd', ws, kt)
        return (C_new, n_new, m_new), h_out

    init = (jnp.zeros((B, ML_H, ML_DV, ML_DQK), f32), jnp.zeros((B, ML_H, ML_DQK), f32), jnp.zeros((B, ML_H), f32))
    _, hs = lax.scan(step, init, (qc, kc, vc, ic, fc))
    hs = hs.transpose(1, 0, 3, 2, 4).reshape(B, S, ML_H, ML_DV)
    hs = rmsnorm(hs, norm_g.reshape(ML_H, ML_DV)).reshape(B, S, ML_V)
    y = (hs * jax.nn.sigmoid(o.astype(f32))).astype(h.dtype)
    return y @ w_out


def diff_attention(h, w_in, lam_q1, lam_k1, lam_q2, lam_k2, norm_g, w_out, lambda_init):
    B, S, _ = h.shape
    f32 = jnp.float32
    p = (h @ w_in).astype(f32)
    q = p[..., :DA_QK].reshape(B, S, DA_H, 2, DA_DH) * (DA_DH ** -0.5)
    k = p[..., DA_QK:2 * DA_QK].reshape(B, S, DA_H, 2, DA_DH)
    v = p[..., 2 * DA_QK:].reshape(B, S, DA_H, DA_DV)
    lam = (jnp.exp(jnp.sum(lam_q1.astype(f32) * lam_k1.astype(f32)))
           - jnp.exp(jnp.sum(lam_q2.astype(f32) * lam_k2.astype(f32))) + lambda_init)
    nq = S // Q_BLOCK
    qb = q.reshape(B, nq, Q_BLOCK, DA_H, 2, DA_DH).transpose(1, 0, 2, 3, 4, 5)
    kpos = jnp.arange(S)

    def one_block(args):
        qblk, start = args
        s = jnp.einsum('bqhcd,bkhcd->bhcqk', qblk, k)
        qpos = start + jnp.arange(Q_BLOCK)
        causal = kpos[None, :] <= qpos[:, None]
        pr = jax.nn.softmax(jnp.where(causal, s, -jnp.inf), axis=-1)
        a = pr[:, :, 0] - lam * pr[:, :, 1]
        return jnp.einsum('bhqk,bkhe->bqhe', a, v)

    o = lax.map(one_block, (qb, jnp.arange(nq) * Q_BLOCK))
    o = o.transpose(1, 0, 2, 3, 4).reshape(B, S, DA_H, DA_DV)
    o = rmsnorm(o, norm_g, DA_EPS) * (1.0 - lambda_init)
    return o.reshape(B, S, DA_H * DA_DV).astype(h.dtype) @ w_out


def hier_moe(h, w_group, b_group, w_expert, b_expert, w_gu, w_down):
    B, S, D = h.shape
    T = B * S
    f32 = jnp.float32
    xt = h.reshape(T, D)
    g_logits = (xt @ w_group).astype(f32) + b_group.astype(f32)
    g_prob = jax.nn.softmax(g_logits, axis=-1)
    _, g_idx = lax.top_k(g_logits, 1)
    g_w = jnp.take_along_axis(g_prob, g_idx, axis=-1)
    e_logits = ((xt @ w_expert).astype(f32) + b_expert.astype(f32)).reshape(T, N_GROUPS, EXPERTS_PER_GROUP)
    e_logits = jnp.take_along_axis(e_logits, g_idx[:, :, None], axis=1)[:, 0]
    e_prob = jax.nn.softmax(e_logits, axis=-1)
    top_w, top_idx = lax.top_k(e_prob, TOP_K_IN_GROUP)
    top_w = top_w / jnp.sum(top_w, axis=-1, keepdims=True)
    within = jnp.sum(jax.nn.one_hot(top_idx, EXPERTS_PER_GROUP, dtype=f32) * top_w[..., None], axis=1)
    gate = (jax.nn.one_hot(g_idx[:, 0], N_GROUPS, dtype=f32)[:, :, None]
            * (g_w * within)[:, None, :]).reshape(T, N_EXPERTS)
    nb = T // TOK_BLOCK

    def expert_block(args):
        xb, gb = args
        gu = jnp.einsum('td,edf->tef', xb, w_gu)
        a = jax.nn.silu(gu[..., :D_EXPERT]) * gu[..., D_EXPERT:] * gb[..., None].astype(xb.dtype)
        return jnp.einsum('tef,efd->td', a, w_down)

    y = lax.map(expert_block, (xt.reshape(nb, TOK_BLOCK, D), gate.reshape(nb, TOK_BLOCK, N_EXPERTS)))
    return y.reshape(B, S, D)


def setup_inputs(seed: int = 0) -> dict:
    key = jax.random.key(seed)
    ks = jax.random.split(key, 32)
    n_a = (DEPTH + 1) // 2
    n_b = DEPTH // 2
    D = D_MODEL

    def nrm(k, shape, scale):
        return jax.random.normal(k, shape, jnp.float32) * scale

    return {
        'x': nrm(ks[0], (BATCH, SEQ, D), 1.0),
        'norm_mix': 1.0 + nrm(ks[1], (DEPTH, D), 0.01),
        'norm_ffn': 1.0 + nrm(ks[2], (DEPTH, D), 0.01),
        'ml_w_in': nrm(ks[3], (n_a, D, ML_PROJ), D ** -0.5),
        'ml_conv': nrm(ks[4], (n_a, CONV_K, 2 * ML_QK), 0.5),
        'ml_b_i': nrm(ks[5], (n_a, ML_H), 0.1),
        'ml_b_f': jnp.linspace(3.0, 6.0, ML_H)[None, :] + nrm(ks[6], (n_a, ML_H), 0.01),
        'ml_norm': 1.0 + nrm(ks[7], (n_a, ML_V), 0.01),
        'ml_w_out': nrm(ks[8], (n_a, ML_V, D), ML_V ** -0.5),
        'da_w_in': nrm(ks[9], (n_b, D, DA_PROJ), D ** -0.5),
        'da_lq1': nrm(ks[10], (n_b, DA_DH), 0.1),
        'da_lk1': nrm(ks[11], (n_b, DA_DH), 0.1),
        'da_lq2': nrm(ks[12], (n_b, DA_DH), 0.1),
        'da_lk2': nrm(ks[13], (n_b, DA_DH), 0.1),
        'da_norm': 1.0 + nrm(ks[14], (n_b, DA_DV), 0.01),
        'da_w_out': nrm(ks[15], (n_b, DA_H * DA_DV, D), (DA_H * DA_DV) ** -0.5),
        'moe_w_group': nrm(ks[16], (DEPTH, D, N_GROUPS), D ** -0.5),
        'moe_b_group': nrm(ks[17], (DEPTH, N_GROUPS), 0.01),
        'moe_w_expert': nrm(ks[18], (DEPTH, D, N_EXPERTS), D ** -0.5),
        'moe_b_expert': nrm(ks[19], (DEPTH, N_EXPERTS), 0.01),
        'moe_w_gu': nrm(ks[20], (DEPTH, N_EXPERTS, D, 2 * D_EXPERT), D ** -0.5),
        'moe_w_down': nrm(ks[21], (DEPTH, N_EXPERTS, D_EXPERT, D), D_EXPERT ** -0.5),
        'final_norm': 1.0 + nrm(ks[22], (D,), 0.01),
    }


def reference(x, norm_mix, norm_ffn, ml_w_in, ml_conv, ml_b_i, ml_b_f, ml_norm, ml_w_out,
              da_w_in, da_lq1, da_lk1, da_lq2, da_lk2, da_norm, da_w_out,
              moe_w_group, moe_b_group, moe_w_expert, moe_b_expert, moe_w_gu, moe_w_down,
              final_norm):
    for i in range(DEPTH):
        hn = rmsnorm(x, norm_mix[i])
        j = i // N_MIXERS
        if i % N_MIXERS == 0:
            mix = mlstm_mixer(hn, ml_w_in[j], ml_conv[j], ml_b_i[j], ml_b_f[j], ml_norm[j], ml_w_out[j])
        else:
            lambda_init = 0.8 - 0.6 * math.exp(-0.3 * i)
            mix = diff_attention(hn, da_w_in[j], da_lq1[j], da_lk1[j], da_lq2[j], da_lk2[j],
                                 da_norm[j], da_w_out[j], lambda_init)
        x = x + mix
        x = x + hier_moe(rmsnorm(x, norm_ffn[i]), moe_w_group[i], moe_b_group[i], moe_w_expert[i],
                         moe_b_expert[i], moe_w_gu[i], moe_w_down[i])
    return rmsnorm(x, final_norm)
```

```python
import functools
import math

import jax
import jax.numpy as jnp
from jax import lax
from jax.experimental import pallas as pl
from jax.experimental.pallas import tpu as pltpu

F32 = jnp.float32
BF16 = jnp.bfloat16

D_MODEL = 1024
EPS = 1e-6
ML_H = 4
ML_DV = 512
ML_DQK = 256
ML_QK = ML_H * ML_DQK
ML_V = ML_H * ML_DV
CONV_K = 4
ML_CHUNK = 256
CONV_TAIL = 8
DA_H = 8
DA_DH = 64
DA_DV = 128
DA_QK = DA_H * 2 * DA_DH
DA_EPS = 1e-5
DA_TQ = 512
DA_TK = 512
N_GROUPS = 4
EPG = 8
N_EXPERTS = 32
D_EXPERT = 256
LANES = 128

VMEM_LIMIT = 48 * 1024 * 1024


def _cparams(sem):
    return pltpu.CompilerParams(dimension_semantics=sem, vmem_limit_bytes=VMEM_LIMIT)


def _norm_matmul_kernel(x_ref, g_ref, w_ref, o_ref, xn_ref):
    @pl.when(pl.program_id(1) == 0)
    def _():
        x = x_ref[...]
        ms = jnp.mean(x * x, axis=-1, keepdims=True)
        xn_ref[...] = (x * lax.rsqrt(ms + EPS) * g_ref[...]).astype(BF16)

    o_ref[...] = jnp.dot(xn_ref[...], w_ref[...], preferred_element_type=F32).astype(o_ref.dtype)


def _norm_matmul(x, g, w, *, tm, tn, out_dtype):
    t, d = x.shape
    n = w.shape[1]
    return pl.pallas_call(
        _norm_matmul_kernel,
        out_shape=jax.ShapeDtypeStruct((t, n), out_dtype),
        grid=(t // tm, n // tn),
        in_specs=[
            pl.BlockSpec((tm, d), lambda i, j: (i, 0)),
            pl.BlockSpec((1, d), lambda i, j: (0, 0)),
            pl.BlockSpec((d, tn), lambda i, j: (0, j)),
        ],
        out_specs=pl.BlockSpec((tm, tn), lambda i, j: (i, j)),
        scratch_shapes=[pltpu.VMEM((tm, d), BF16)],
        compiler_params=_cparams(("parallel", "arbitrary")),
        name="norm_matmul",
    )(x, g, w)


def _matmul_res_kernel(a_ref, w_ref, r_ref, o_ref):
    o_ref[...] = r_ref[...] + jnp.dot(a_ref[...], w_ref[...], preferred_element_type=F32)


def _matmul_res(a, w, res, *, tm):
    t, k = a.shape
    n = w.shape[1]
    return pl.pallas_call(
        _matmul_res_kernel,
        out_shape=jax.ShapeDtypeStruct((t, n), F32),
        grid=(t // tm,),
        in_specs=[
            pl.BlockSpec((tm, k), lambda i: (i, 0)),
            pl.BlockSpec((k, n), lambda i: (0, 0)),
            pl.BlockSpec((tm, n), lambda i: (i, 0)),
        ],
        out_specs=pl.BlockSpec((tm, n), lambda i: (i, 0)),
        compiler_params=_cparams(("parallel",)),
        name="matmul_res",
    )(a, w, res)


def _log_sigmoid(x):
    return jnp.minimum(x, 0.0) - jnp.log1p(jnp.exp(-jnp.abs(x)))


def _ml_gates_kernel(x_ref, g_ref, wc_ref, wr_ref, bc_ref, br_ref, oc_ref, or_ref):
    x = x_ref[...]
    ms = jnp.mean(x * x, axis=-1, keepdims=True)
    xn = (x * lax.rsqrt(ms + EPS) * g_ref[...]).astype(BF16)
    gc = jnp.dot(xn, wc_ref[...], preferred_element_type=F32) + bc_ref[...]
    lane = lax.broadcasted_iota(jnp.int32, gc.shape, 1)
    oc_ref[...] = jnp.where(lane < ML_H, gc, _log_sigmoid(gc))
    gr = lax.dot_general(wr_ref[...], xn, (((1,), (1,)), ((), ())),
                         preferred_element_type=F32) + br_ref[...]
    row = lax.broadcasted_iota(jnp.int32, gr.shape, 0)
    or_ref[...] = jnp.where(row < ML_H, gr, _log_sigmoid(gr))


def _ml_gates(x, g, w_gates, b_i, b_f, *, tm):
    t, d = x.shape
    ng = 2 * ML_H
    wc = jnp.zeros((d, LANES), BF16).at[:, :ng].set(w_gates.astype(BF16))
    wr = w_gates.T.astype(BF16)
    bias = jnp.concatenate([b_i, b_f]).astype(F32)
    bc = jnp.zeros((1, LANES), F32).at[0, :ng].set(bias)
    br = bias.reshape(ng, 1)
    return pl.pallas_call(
        _ml_gates_kernel,
        out_shape=(jax.ShapeDtypeStruct((t, LANES), F32), jax.ShapeDtypeStruct((ng, t), F32)),
        grid=(t // tm,),
        in_specs=[
            pl.BlockSpec((tm, d), lambda i: (i, 0)),
            pl.BlockSpec((1, d), lambda i: (0, 0)),
            pl.BlockSpec((d, LANES), lambda i: (0, 0)),
            pl.BlockSpec((ng, d), lambda i: (0, 0)),
            pl.BlockSpec((1, LANES), lambda i: (0, 0)),
            pl.BlockSpec((ng, 1), lambda i: (0, 0)),
        ],
        out_specs=(pl.BlockSpec((tm, LANES), lambda i: (i, 0)),
                   pl.BlockSpec((ng, tm), lambda i: (0, i))),
        compiler_params=_cparams(("parallel",)),
        name="ml_gates",
    )(x, g, wc, wr, bc, br)


def _split3(x):
    hi = x.astype(BF16)
    r = x - hi.astype(F32)
    mid = r.astype(BF16)
    lo = (r - mid.astype(F32)).astype(BF16)
    return hi, mid, lo


def _mlstm_kernel(q_ref, k_ref, v_ref, o_ref, gc_ref, gr_ref, cw_ref, ng_ref, y_ref,
                  ext_ref, ct_ref, n_ref, m_ref):
    L = ML_CHUNK
    c = pl.program_id(1)

    @pl.when(c == 0)
    def _():
        ext_ref[0:CONV_TAIL, :] = jnp.zeros((CONV_TAIL, 2 * ML_QK), F32)
        ct_ref[...] = jnp.zeros_like(ct_ref)
        n_ref[...] = jnp.zeros_like(n_ref)
        m_ref[...] = jnp.zeros_like(m_ref)

    ext_ref[CONV_TAIL:CONV_TAIL + L, 0:ML_QK] = q_ref[...].astype(F32)
    ext_ref[CONV_TAIL:CONV_TAIL + L, ML_QK:2 * ML_QK] = k_ref[...].astype(F32)
    conv = None
    for j in range(CONV_K):
        sh = ext_ref[pl.ds(CONV_TAIL - (CONV_K - 1) + j, L), :] * cw_ref[j:j + 1, :]
        conv = sh if conv is None else conv + sh
    ext_ref[0:CONV_TAIL, :] = ext_ref[L:L + CONV_TAIL, :]
    qk = conv * jax.nn.sigmoid(conv)
    q_all = (qk[:, :ML_QK] * (ML_DQK ** -0.5)).astype(BF16)
    k_all = qk[:, ML_QK:]

    r_i = lax.broadcasted_iota(jnp.int32, (L, L), 0)
    c_i = lax.broadcasted_iota(jnp.int32, (L, L), 1)
    causal = c_i <= r_i
    tril = causal.astype(BF16)
    triu = (r_i <= c_i).astype(BF16)
    gc = gc_ref[...]
    gr = gr_ref[...]
    bc_all = sum(jnp.dot(tril, p, preferred_element_type=F32) for p in _split3(gc))
    br_all = sum(jnp.dot(p, triu, preferred_element_type=F32) for p in _split3(gr))

    for h in range(ML_H):
        qh = q_all[:, h * ML_DQK:(h + 1) * ML_DQK]
        kh_f = k_all[:, h * ML_DQK:(h + 1) * ML_DQK]
        kh = kh_f.astype(BF16)
        vh = v_ref[:, h * ML_DV:(h + 1) * ML_DV]
        it_col = gc[:, h:h + 1]
        it_row = gr[h:h + 1, :]
        b_col = bc_all[:, ML_H + h:ML_H + h + 1]
        b_row = br_all[ML_H + h:ML_H + h + 1, :]
        m_prev = m_ref[h][:, 0:1]

        dmat = jnp.where(causal, b_col - b_row + it_row, -jnp.inf)
        inter_log = b_col + m_prev
        m_t = jnp.maximum(inter_log, jnp.max(dmat, axis=1, keepdims=True))
        wts = jnp.exp(dmat - m_t)
        s = lax.dot_general(qh, kh, (((1,), (1,)), ((), ())), preferred_element_type=F32)
        sc = s * wts
        inter_scale = jnp.exp(inter_log - m_t)
        ct = ct_ref[h]
        num = (jnp.dot(sc.astype(BF16), vh, preferred_element_type=F32)
               + inter_scale * jnp.dot(qh, ct.astype(BF16), preferred_element_type=F32))
        n_row = n_ref[h]
        den = (jnp.sum(sc, axis=1, keepdims=True)
               + inter_scale * jnp.sum(qh.astype(F32) * n_row, axis=1, keepdims=True))
        h_out = num / jnp.maximum(jnp.abs(den), jnp.exp(-m_t))

        b_last = b_col[L - 1:L, :]
        lw_col = b_last - b_col + it_col
        lw_row = b_last - b_row + it_row
        m_new = jnp.maximum(b_last + m_prev, jnp.max(lw_row, axis=1, keepdims=True))
        ws_col = jnp.exp(lw_col - m_new)
        decay = jnp.exp(b_last + m_prev - m_new)
        vw = (vh.astype(F32) * ws_col).astype(BF16)
        ct_ref[h] = decay * ct + lax.dot_general(kh, vw, (((0,), (0,)), ((), ())),
                                                  preferred_element_type=F32)
        n_ref[h] = decay * n_row + jnp.sum(kh_f * ws_col, axis=0, keepdims=True)
        m_ref[h] = jnp.broadcast_to(m_new, (1, LANES))

        ms = jnp.mean(h_out * h_out, axis=1, keepdims=True)
        hn = h_out * lax.rsqrt(ms + EPS) * ng_ref[:, h * ML_DV:(h + 1) * ML_DV]
        og = o_ref[:, h * ML_DV:(h + 1) * ML_DV].astype(F32)
        y_ref[:, h * ML_DV:(h + 1) * ML_DV] = (hn * jax.nn.sigmoid(og)).astype(y_ref.dtype)


def _mlstm_core(p, gcol, grow, conv_w, norm_g, *, batch, seq):
    L = ML_CHUNK
    nc = seq // L
    p3 = p.reshape(batch, seq, 2 * ML_QK + 2 * ML_V)
    gc3 = gcol.reshape(batch, seq, LANES)
    gr3 = grow.reshape(2 * ML_H, batch, seq).transpose(1, 0, 2)
    y = pl.pallas_call(
        _mlstm_kernel,
        out_shape=jax.ShapeDtypeStruct((batch, seq, ML_V), BF16),
        grid=(batch, nc),
        in_specs=[
            pl.BlockSpec((None, L, ML_QK), lambda b, c: (b, c, 0)),
            pl.BlockSpec((None, L, ML_QK), lambda b, c: (b, c, 1)),
            pl.BlockSpec((None, L, ML_V), lambda b, c: (b, c, 1)),
            pl.BlockSpec((None, L, ML_V), lambda b, c: (b, c, 2)),
            pl.BlockSpec((None, L, LANES), lambda b, c: (b, c, 0)),
            pl.BlockSpec((None, 2 * ML_H, L), lambda b, c: (b, 0, c)),
            pl.BlockSpec((CONV_K, 2 * ML_QK), lambda b, c: (0, 0)),
            pl.BlockSpec((1, ML_V), lambda b, c: (0, 0)),
        ],
        out_specs=pl.BlockSpec((None, L, ML_V), lambda b, c: (b, c, 0)),
        scratch_shapes=[
            pltpu.VMEM((L + CONV_TAIL, 2 * ML_QK), F32),
            pltpu.VMEM((ML_H, ML_DQK, ML_DV), F32),
            pltpu.VMEM((ML_H, 1, ML_DQK), F32),
            pltpu.VMEM((ML_H, 1, LANES), F32),
        ],
        compiler_params=_cparams(("parallel", "arbitrary")),
        name="mlstm_core",
    )(p3, p3, p3, p3, gc3, gr3, conv_w, norm_g)
    return y.reshape(batch * seq, ML_V)


def _diff_attn_kernel(q_ref, k_ref, v_ref, lp_ref, ng_ref, o_ref, m_ref, l_ref, acc_ref,
                      *, lambda_init):
    tq, tk = DA_TQ, DA_TK
    qi = pl.program_id(2)
    q = q_ref[...].astype(F32) * (DA_DH ** -0.5)
    lane = lax.broadcasted_iota(jnp.int32, q.shape, 1)
    q2 = jnp.concatenate([jnp.where(lane < DA_DH, q, 0.0),
                          jnp.where(lane >= DA_DH, q, 0.0)], axis=0).astype(BF16)

    m_ref[...] = jnp.full(m_ref.shape, -jnp.inf, F32)
    l_ref[...] = jnp.zeros_like(l_ref)
    acc_ref[...] = jnp.zeros_like(acc_ref)

    def block(j, masked):
        start = pl.multiple_of(j * tk, tk)
        kb = k_ref[pl.ds(start, tk), :]
        vb = v_ref[pl.ds(start, tk), :]
        s = lax.dot_general(q2, kb, (((1,), (1,)), ((), ())), preferred_element_type=F32)
        if masked:
            r_i = lax.broadcasted_iota(jnp.int32, (tq, tk), 0)
            c_i = lax.broadcasted_iota(jnp.int32, (tq, tk), 1)
            keep = c_i <= r_i
            keep2 = jnp.concatenate([keep, keep], axis=0)
            s = jnp.where(keep2, s, -jnp.inf)
        m_old = m_ref[...]
        m_new = jnp.maximum(m_old, jnp.max(s, axis=1, keepdims=True))
        alpha = jnp.exp(m_old - m_new)
        p = jnp.exp(s - m_new)
        l_ref[...] = alpha * l_ref[...] + jnp.sum(p, axis=1, keepdims=True)
        acc_ref[...] = alpha * acc_ref[...] + jnp.dot(p.astype(BF16), vb,
                                                       preferred_element_type=F32)
        m_ref[...] = m_new

    def body(j, carry):
        block(j, False)
        return carry

    lax.fori_loop(0, qi, body, 0)
    block(qi, True)

    lp = lp_ref[...]
    lam = (jnp.exp(jnp.sum(lp[0:1, :] * lp[1:2, :], axis=1, keepdims=True))
           - jnp.exp(jnp.sum(lp[2:3, :] * lp[3:4, :], axis=1, keepdims=True)) + lambda_init)
    out = acc_ref[...] / l_ref[...]
    o = out[:tq, :] - lam * out[tq:, :]
    ms = jnp.mean(o * o, axis=1, keepdims=True)
    on = o * lax.rsqrt(ms + DA_EPS) * ng_ref[...]
    o_ref[...] = (on * (1.0 - lambda_init)).astype(o_ref.dtype)


def _diff_attn(p, lam_params, norm_g, lambda_init, *, batch, seq):
    tq = DA_TQ
    p3 = p.reshape(batch, seq, 2 * DA_QK + DA_H * DA_DV)
    kern = functools.partial(_diff_attn_kernel, lambda_init=lambda_init)
    o = pl.pallas_call(
        kern,
        out_shape=jax.ShapeDtypeStruct((batch, seq, DA_H * DA_DV), BF16),
        grid=(batch, DA_H, seq // tq),
        in_specs=[
            pl.BlockSpec((None, tq, 2 * DA_DH), lambda b, h, i: (b, i, h)),
            pl.BlockSpec((None, seq, 2 * DA_DH), lambda b, h, i: (b, 0, DA_H + h)),
            pl.BlockSpec((None, seq, DA_DV), lambda b, h, i: (b, 0, 2 * DA_H + h)),
            pl.BlockSpec((4, DA_DH), lambda b, h, i: (0, 0)),
            pl.BlockSpec((1, DA_DV), lambda b, h, i: (0, 0)),
        ],
        out_specs=pl.BlockSpec((None, tq, DA_DV), lambda b, h, i: (b, i, h)),
        scratch_shapes=[
            pltpu.VMEM((2 * tq, 1), F32),
            pltpu.VMEM((2 * tq, 1), F32),
            pltpu.VMEM((2 * tq, DA_DV), F32),
        ],
        compiler_params=_cparams(("parallel", "parallel", "arbitrary")),
        name="diff_attn",
    )(p3, p3, p3, lam_params, norm_g)
    return o.reshape(batch * seq, DA_H * DA_DV)


def _router_kernel(x_ref, g_ref, w_ref, b_ref, hn_ref, gate_ref):
    x = x_ref[...]
    ms = jnp.mean(x * x, axis=-1, keepdims=True)
    hn = (x * lax.rsqrt(ms + EPS) * g_ref[...]).astype(BF16)
    hn_ref[...] = hn
    lg = jnp.dot(hn, w_ref[...], preferred_element_type=F32) + b_ref[...]
    lane = lax.broadcasted_iota(jnp.int32, lg.shape, 1)
    neg = -jnp.inf

    gmask = (lane >= N_EXPERTS) & (lane < N_EXPERTS + N_GROUPS)
    gl = jnp.where(gmask, lg, neg)
    gmax = jnp.max(gl, axis=1, keepdims=True)
    gidx = jnp.min(jnp.where(gl == gmax, lane, LANES), axis=1, keepdims=True) - N_EXPERTS
    gsum = jnp.sum(jnp.where(gmask, jnp.exp(gl - gmax), 0.0), axis=1, keepdims=True)
    g_w = 1.0 / gsum

    emask = (lane >= gidx * EPG) & (lane < gidx * EPG + EPG)
    el = jnp.where(emask, lg, neg)
    emax = jnp.max(el, axis=1, keepdims=True)
    eexp = jnp.where(emask, jnp.exp(el - emax), 0.0)
    ep = eexp / jnp.sum(eexp, axis=1, keepdims=True)
    ep = jnp.where(emask, ep, -1.0)
    p1 = jnp.max(ep, axis=1, keepdims=True)
    i1 = jnp.min(jnp.where(ep == p1, lane, LANES), axis=1, keepdims=True)
    ep2 = jnp.where(lane == i1, -1.0, ep)
    p2 = jnp.max(ep2, axis=1, keepdims=True)
    i2 = jnp.min(jnp.where(ep2 == p2, lane, LANES), axis=1, keepdims=True)
    wsum = p1 + p2
    gate_ref[...] = (jnp.where(lane == i1, g_w * (p1 / wsum), 0.0)
                     + jnp.where(lane == i2, g_w * (p2 / wsum), 0.0))


def _router(x, g, w_group, b_group, w_expert, b_expert, *, tm):
    t, d = x.shape
    w = jnp.zeros((d, LANES), BF16)
    w = w.at[:, :N_EXPERTS].set(w_expert.astype(BF16))
    w = w.at[:, N_EXPERTS:N_EXPERTS + N_GROUPS].set(w_group.astype(BF16))
    b = jnp.zeros((1, LANES), F32)
    b = b.at[0, :N_EXPERTS].set(b_expert.astype(F32))
    b = b.at[0, N_EXPERTS:N_EXPERTS + N_GROUPS].set(b_group.astype(F32))
    return pl.pallas_call(
        _router_kernel,
        out_shape=(jax.ShapeDtypeStruct((t, d), BF16), jax.ShapeDtypeStruct((t, LANES), F32)),
        grid=(t // tm,),
        in_specs=[
            pl.BlockSpec((tm, d), lambda i: (i, 0)),
            pl.BlockSpec((1, d), lambda i: (0, 0)),
            pl.BlockSpec((d, LANES), lambda i: (0, 0)),
            pl.BlockSpec((1, LANES), lambda i: (0, 0)),
        ],
        out_specs=(pl.BlockSpec((tm, d), lambda i: (i, 0)),
                   pl.BlockSpec((tm, LANES), lambda i: (i, 0))),
        compiler_params=_cparams(("parallel",)),
        name="moe_router",
    )(x, g, w, b)


def _experts_kernel(hn_ref, gate_ref, wgu_ref, wd_ref, r_ref, o_ref):
    e = pl.program_id(1)

    @pl.when(e == 0)
    def _():
        o_ref[...] = r_ref[...]

    gate = gate_ref[...]
    lane = lax.broadcasted_iota(jnp.int32, gate.shape, 1)
    ge = jnp.sum(jnp.where(lane == e, gate, 0.0), axis=1, keepdims=True)
    gu = jnp.dot(hn_ref[...], wgu_ref[...], preferred_element_type=F32)
    gpart = gu[:, :D_EXPERT]
    a = gpart * jax.nn.sigmoid(gpart) * gu[:, D_EXPERT:] * ge
    o_ref[...] += jnp.dot(a.astype(BF16), wd_ref[...], preferred_element_type=F32)


def _experts(hn, gate, w_gu, w_down, res, *, tm):
    t, d = hn.shape
    return pl.pallas_call(
        _experts_kernel,
        out_shape=jax.ShapeDtypeStruct((t, d), F32),
        grid=(t // tm, N_EXPERTS),
        in_specs=[
            pl.BlockSpec((tm, d), lambda i, e: (i, 0)),
            pl.BlockSpec((tm, LANES), lambda i, e: (i, 0)),
            pl.BlockSpec((None, d, 2 * D_EXPERT), lambda i, e: (e, 0, 0)),
            pl.BlockSpec((None, D_EXPERT, d), lambda i, e: (e, 0, 0)),
            pl.BlockSpec((tm, d), lambda i, e: (i, 0)),
        ],
        out_specs=pl.BlockSpec((tm, d), lambda i, e: (i, 0)),
        compiler_params=_cparams(("parallel", "arbitrary")),
        name="moe_experts",
    )(hn, gate, w_gu, w_down, res)


def _rmsnorm_kernel(x_ref, g_ref, o_ref):
    x = x_ref[...]
    ms = jnp.mean(x * x, axis=-1, keepdims=True)
    o_ref[...] = x * lax.rsqrt(ms + EPS) * g_ref[...]


def _rmsnorm(x, g, *, tm):
    t, d = x.shape
    return pl.pallas_call(
        _rmsnorm_kernel,
        out_shape=jax.ShapeDtypeStruct((t, d), F32),
        grid=(t // tm,),
        in_specs=[pl.BlockSpec((tm, d), lambda i: (i, 0)),
                  pl.BlockSpec((1, d), lambda i: (0, 0))],
        out_specs=pl.BlockSpec((tm, d), lambda i: (i, 0)),
        compiler_params=_cparams(("parallel",)),
        name="final_rmsnorm",
    )(x, g)


def _moe_layer(x, norm_g, w_group, b_group, w_expert, b_expert, w_gu, w_down):
    hn, gate = _router(x, norm_g.reshape(1, -1), w_group, b_group, w_expert, b_expert, tm=1024)
    return _experts(hn, gate, w_gu.astype(BF16), w_down.astype(BF16), x, tm=1024)


def kernel(x, norm_mix, norm_ffn, ml_w_in, ml_conv, ml_b_i, ml_b_f, ml_norm, ml_w_out, da_w_in, da_lq1, da_lk1, da_lq2, da_lk2, da_norm, da_w_out, moe_w_group, moe_b_group, moe_w_expert, moe_b_expert, moe_w_gu, moe_w_down, final_norm):
    batch, seq, d = x.shape
    xt = x.reshape(batch * seq, d)

    g0 = norm_mix[0].reshape(1, d)
    n_main = 2 * ML_QK + 2 * ML_V
    p = _norm_matmul(xt, g0, ml_w_in[0][:, :n_main].astype(BF16), tm=1024, tn=1024,
                     out_dtype=BF16)
    gcol, grow = _ml_gates(xt, g0, ml_w_in[0][:, n_main:], ml_b_i[0], ml_b_f[0], tm=1024)
    y = _mlstm_core(p, gcol, grow, ml_conv[0], ml_norm[0].reshape(1, ML_V), batch=batch, seq=seq)
    xt = _matmul_res(y, ml_w_out[0].astype(BF16), xt, tm=512)
    xt = _moe_layer(xt, norm_ffn[0], moe_w_group[0], moe_b_group[0], moe_w_expert[0],
                    moe_b_expert[0], moe_w_gu[0], moe_w_down[0])

    lambda_init = 0.8 - 0.6 * math.exp(-0.3 * 1)
    p = _norm_matmul(xt, norm_mix[1].reshape(1, d), da_w_in[0].astype(BF16), tm=1024, tn=1024,
                     out_dtype=BF16)
    lam_params = jnp.stack([da_lq1[0], da_lk1[0], da_lq2[0], da_lk2[0]]).astype(F32)
    a = _diff_attn(p, lam_params, da_norm[0].reshape(1, DA_DV), lambda_init, batch=batch, seq=seq)
    xt = _matmul_res(a, da_w_out[0].astype(BF16), xt, tm=512)
    xt = _moe_layer(xt, norm_ffn[1], moe_w_group[1], moe_b_group[1], moe_w_expert[1],
                    moe_b_expert[1], moe_w_gu[1], moe_w_down[1])

    out = _rmsnorm(xt, final_norm.reshape(1, d), tm=1024)
    return out.reshape(batch, seq, d)
```

```python
import functools
import math

import jax
import jax.numpy as jnp
from jax import lax
from jax.experimental import pallas as pl
from jax.experimental.pallas import tpu as pltpu

F32 = jnp.float32
BF16 = jnp.bfloat16

D_MODEL = 1024
EPS = 1e-6
ML_H = 4
ML_DV = 512
ML_DQK = 256
ML_QK = ML_H * ML_DQK
ML_V = ML_H * ML_DV
CONV_K = 4
ML_CHUNK = 256
CONV_TAIL = 8
DA_H = 8
DA_DH = 64
DA_DV = 128
DA_QK = DA_H * 2 * DA_DH
DA_EPS = 1e-5
DA_TQ = 1024
DA_TK = 512
DA_CW = 512
LOG2E = 1.4426950408889634
N_GROUPS = 4
EPG = 8
N_EXPERTS = 32
D_EXPERT = 256
LANES = 128

VMEM_LIMIT = 48 * 1024 * 1024


def _cparams(sem):
    return pltpu.CompilerParams(dimension_semantics=sem, vmem_limit_bytes=VMEM_LIMIT)


def _norm_matmul_kernel(x_ref, g_ref, w_ref, o_ref, xn_ref):
    @pl.when(pl.program_id(1) == 0)
    def _():
        x = x_ref[...]
        ms = jnp.mean(x * x, axis=-1, keepdims=True)
        xn_ref[...] = (x * lax.rsqrt(ms + EPS) * g_ref[...]).astype(BF16)

    o_ref[...] = jnp.dot(xn_ref[...], w_ref[...], preferred_element_type=F32).astype(o_ref.dtype)


def _norm_matmul(x, g, w, *, tm, tn, out_dtype):
    t, d = x.shape
    n = w.shape[1]
    return pl.pallas_call(
        _norm_matmul_kernel,
        out_shape=jax.ShapeDtypeStruct((t, n), out_dtype),
        grid=(t // tm, n // tn),
        in_specs=[
            pl.BlockSpec((tm, d), lambda i, j: (i, 0)),
            pl.BlockSpec((1, d), lambda i, j: (0, 0)),
            pl.BlockSpec((d, tn), lambda i, j: (0, j)),
        ],
        out_specs=pl.BlockSpec((tm, tn), lambda i, j: (i, j)),
        scratch_shapes=[pltpu.VMEM((tm, d), BF16)],
        compiler_params=_cparams(("parallel", "arbitrary")),
        name="norm_matmul",
    )(x, g, w)


def _matmul_res_kernel(a_ref, w_ref, r_ref, o_ref):
    o_ref[...] = r_ref[...] + jnp.dot(a_ref[...], w_ref[...], preferred_element_type=F32)


def _matmul_res(a, w, res, *, tm):
    t, k = a.shape
    n = w.shape[1]
    return pl.pallas_call(
        _matmul_res_kernel,
        out_shape=jax.ShapeDtypeStruct((t, n), F32),
        grid=(t // tm,),
        in_specs=[
            pl.BlockSpec((tm, k), lambda i: (i, 0)),
            pl.BlockSpec((k, n), lambda i: (0, 0)),
            pl.BlockSpec((tm, n), lambda i: (i, 0)),
        ],
        out_specs=pl.BlockSpec((tm, n), lambda i: (i, 0)),
        compiler_params=_cparams(("parallel",)),
        name="matmul_res",
    )(a, w, res)


def _log_sigmoid(x):
    return jnp.minimum(x, 0.0) - jnp.log1p(jnp.exp(-jnp.abs(x)))


def _ml_gates_kernel(x_ref, g_ref, wc_ref, wr_ref, bc_ref, br_ref, oc_ref, or_ref):
    x = x_ref[...]
    ms = jnp.mean(x * x, axis=-1, keepdims=True)
    xn = (x * lax.rsqrt(ms + EPS) * g_ref[...]).astype(BF16)
    gc = jnp.dot(xn, wc_ref[...], preferred_element_type=F32) + bc_ref[...]
    lane = lax.broadcasted_iota(jnp.int32, gc.shape, 1)
    oc_ref[...] = jnp.where(lane < ML_H, gc, _log_sigmoid(gc))
    gr = lax.dot_general(wr_ref[...], xn, (((1,), (1,)), ((), ())),
                         preferred_element_type=F32) + br_ref[...]
    row = lax.broadcasted_iota(jnp.int32, gr.shape, 0)
    or_ref[...] = jnp.where(row < ML_H, gr, _log_sigmoid(gr))


def _ml_gates(x, g, w_gates, b_i, b_f, *, tm):
    t, d = x.shape
    ng = 2 * ML_H
    wc = jnp.zeros((d, LANES), BF16).at[:, :ng].set(w_gates.astype(BF16))
    wr = w_gates.T.astype(BF16)
    bias = jnp.concatenate([b_i, b_f]).astype(F32)
    bc = jnp.zeros((1, LANES), F32).at[0, :ng].set(bias)
    br = bias.reshape(ng, 1)
    return pl.pallas_call(
        _ml_gates_kernel,
        out_shape=(jax.ShapeDtypeStruct((t, LANES), F32), jax.ShapeDtypeStruct((ng, t), F32)),
        grid=(t // tm,),
        in_specs=[
            pl.BlockSpec((tm, d), lambda i: (i, 0)),
            pl.BlockSpec((1, d), lambda i: (0, 0)),
            pl.BlockSpec((d, LANES), lambda i: (0, 0)),
            pl.BlockSpec((ng, d), lambda i: (0, 0)),
            pl.BlockSpec((1, LANES), lambda i: (0, 0)),
            pl.BlockSpec((ng, 1), lambda i: (0, 0)),
        ],
        out_specs=(pl.BlockSpec((tm, LANES), lambda i: (i, 0)),
                   pl.BlockSpec((ng, tm), lambda i: (0, i))),
        compiler_params=_cparams(("parallel",)),
        name="ml_gates",
    )(x, g, wc, wr, bc, br)


def _split3(x):
    hi = x.astype(BF16)
    r = x - hi.astype(F32)
    mid = r.astype(BF16)
    lo = (r - mid.astype(F32)).astype(BF16)
    return hi, mid, lo


def _mlstm_kernel(q_ref, k_ref, v_ref, o_ref, gc_ref, gr_ref, cw_ref, ng_ref, y_ref,
                  ext_ref, ct_ref, n_ref, m_ref):
    L = ML_CHUNK
    c = pl.program_id(1)

    @pl.when(c == 0)
    def _():
        ext_ref[0:CONV_TAIL, :] = jnp.zeros((CONV_TAIL, 2 * ML_QK), F32)
        ct_ref[...] = jnp.zeros_like(ct_ref)
        n_ref[...] = jnp.zeros_like(n_ref)
        m_ref[...] = jnp.zeros_like(m_ref)

    ext_ref[CONV_TAIL:CONV_TAIL + L, 0:ML_QK] = q_ref[...].astype(F32)
    ext_ref[CONV_TAIL:CONV_TAIL + L, ML_QK:2 * ML_QK] = k_ref[...].astype(F32)
    conv = None
    for j in range(CONV_K):
        sh = ext_ref[pl.ds(CONV_TAIL - (CONV_K - 1) + j, L), :] * cw_ref[j:j + 1, :]
        conv = sh if conv is None else conv + sh
    ext_ref[0:CONV_TAIL, :] = ext_ref[L:L + CONV_TAIL, :]
    qk = conv * jax.nn.sigmoid(conv)
    q_all = (qk[:, :ML_QK] * (ML_DQK ** -0.5)).astype(BF16)
    k_all = qk[:, ML_QK:]

    r_i = lax.broadcasted_iota(jnp.int32, (L, L), 0)
    c_i = lax.broadcasted_iota(jnp.int32, (L, L), 1)
    causal = c_i <= r_i
    tril = causal.astype(BF16)
    triu = (r_i <= c_i).astype(BF16)
    gc = gc_ref[...]
    gr = gr_ref[...]
    bc_all = sum(jnp.dot(tril, p, preferred_element_type=F32) for p in _split3(gc))
    br_all = sum(jnp.dot(p, triu, preferred_element_type=F32) for p in _split3(gr))

    for h in range(ML_H):
        qh = q_all[:, h * ML_DQK:(h + 1) * ML_DQK]
        kh_f = k_all[:, h * ML_DQK:(h + 1) * ML_DQK]
        kh = kh_f.astype(BF16)
        vh = v_ref[:, h * ML_DV:(h + 1) * ML_DV]
        it_col = gc[:, h:h + 1]
        it_row = gr[h:h + 1, :]
        b_col = bc_all[:, ML_H + h:ML_H + h + 1]
        b_row = br_all[ML_H + h:ML_H + h + 1, :]
        m_prev = m_ref[h][:, 0:1]

        dmat = jnp.where(causal, b_col - b_row + it_row, -jnp.inf)
        inter_log = b_col + m_prev
        m_t = jnp.maximum(inter_log, jnp.max(dmat, axis=1, keepdims=True))
        wts = jnp.exp(dmat - m_t)
        s = lax.dot_general(qh, kh, (((1,), (1,)), ((), ())), preferred_element_type=F32)
        sc = s * wts
        inter_scale = jnp.exp(inter_log - m_t)
        ct = ct_ref[h]
        num = (jnp.dot(sc.astype(BF16), vh, preferred_element_type=F32)
               + inter_scale * jnp.dot(qh, ct.astype(BF16), preferred_element_type=F32))
        n_row = n_ref[h]
        den = (jnp.sum(sc, axis=1, keepdims=True)
               + inter_scale * jnp.sum(qh.astype(F32) * n_row, axis=1, keepdims=True))
        h_out = num / jnp.maximum(jnp.abs(den), jnp.exp(-m_t))

        b_last = b_col[L - 1:L, :]
        lw_col = b_last - b_col + it_col
        lw_row = b_last - b_row + it_row
        m_new = jnp.maximum(b_last + m_prev, jnp.max(lw_row, axis=1, keepdims=True))
        ws_col = jnp.exp(lw_col - m_new)
        decay = jnp.exp(b_last + m_prev - m_new)
        vw = (vh.astype(F32) * ws_col).astype(BF16)
        ct_ref[h] = decay * ct + lax.dot_general(kh, vw, (((0,), (0,)), ((), ())),
                                                  preferred_element_type=F32)
        n_ref[h] = decay * n_row + jnp.sum(kh_f * ws_col, axis=0, keepdims=True)
        m_ref[h] = jnp.broadcast_to(m_new, (1, LANES))

        ms = jnp.mean(h_out * h_out, axis=1, keepdims=True)
        hn = h_out * lax.rsqrt(ms + EPS) * ng_ref[:, h * ML_DV:(h + 1) * ML_DV]
        og = o_ref[:, h * ML_DV:(h + 1) * ML_DV].astype(F32)
        y_ref[:, h * ML_DV:(h + 1) * ML_DV] = (hn * jax.nn.sigmoid(og)).astype(y_ref.dtype)


def _mlstm_core(p, gcol, grow, conv_w, norm_g, *, batch, seq):
    L = ML_CHUNK
    nc = seq // L
    p3 = p.reshape(batch, seq, 2 * ML_QK + 2 * ML_V)
    gc3 = gcol.reshape(batch, seq, LANES)
    gr3 = grow.reshape(2 * ML_H, batch, seq).transpose(1, 0, 2)
    y = pl.pallas_call(
        _mlstm_kernel,
        out_shape=jax.ShapeDtypeStruct((batch, seq, ML_V), BF16),
        grid=(batch, nc),
        in_specs=[
            pl.BlockSpec((None, L, ML_QK), lambda b, c: (b, c, 0)),
            pl.BlockSpec((None, L, ML_QK), lambda b, c: (b, c, 1)),
            pl.BlockSpec((None, L, ML_V), lambda b, c: (b, c, 1)),
            pl.BlockSpec((None, L, ML_V), lambda b, c: (b, c, 2)),
            pl.BlockSpec((None, L, LANES), lambda b, c: (b, c, 0)),
            pl.BlockSpec((None, 2 * ML_H, L), lambda b, c: (b, 0, c)),
            pl.BlockSpec((CONV_K, 2 * ML_QK), lambda b, c: (0, 0)),
            pl.BlockSpec((1, ML_V), lambda b, c: (0, 0)),
        ],
        out_specs=pl.BlockSpec((None, L, ML_V), lambda b, c: (b, c, 0)),
        scratch_shapes=[
            pltpu.VMEM((L + CONV_TAIL, 2 * ML_QK), F32),
            pltpu.VMEM((ML_H, ML_DQK, ML_DV), F32),
            pltpu.VMEM((ML_H, 1, ML_DQK), F32),
            pltpu.VMEM((ML_H, 1, LANES), F32),
        ],
        compiler_params=_cparams(("parallel", "arbitrary")),
        name="mlstm_core",
    )(p3, p3, p3, p3, gc3, gr3, conv_w, norm_g)
    return y.reshape(batch * seq, ML_V)


def _diff_attn_kernel(q_ref, k_ref, vt_ref, lp_ref, ng_ref, o_ref, q2_ref, m_ref, l_ref, acc_ref,
                      *, lambda_init):
    tq, tk, cw = DA_TQ, DA_TK, DA_CW
    qi = pl.program_id(2)
    q = q_ref[...].astype(F32) * (DA_DH ** -0.5 * LOG2E)
    lane = lax.broadcasted_iota(jnp.int32, q.shape, 1)
    q2_ref[0:tq, :] = jnp.where(lane < DA_DH, q, 0.0).astype(BF16)
    q2_ref[tq:2 * tq, :] = jnp.where(lane >= DA_DH, q, 0.0).astype(BF16)

    m_ref[...] = jnp.full(m_ref.shape, -jnp.inf, F32)
    l_ref[...] = jnp.zeros_like(l_ref)
    acc_ref[...] = jnp.zeros_like(acc_ref)

    nch = 2 * tq // cw

    def scores(j, c):
        kb = k_ref[pl.ds(pl.multiple_of(j * tk, tk), tk), :]
        return lax.dot_general(kb, q2_ref[c * cw:(c + 1) * cw, :], (((1,), (1,)), ((), ())),
                               preferred_element_type=F32)

    def block(j, s_first, diag=None):
        vbt = vt_ref[:, pl.ds(pl.multiple_of(j * tk, tk), tk)]
        last = diag is not None and diag == tq // tk - 1
        s_next = s_first
        for c in range(nch):
            cs = slice(c * cw, (c + 1) * cw)
            s = s_next
            if c + 1 < nch:
                s_next = scores(j, c + 1)
            else:
                s_next = None if last else scores(j + 1, 0)
            if diag is not None:
                key = lax.broadcasted_iota(jnp.int32, (tk, cw), 0) + diag * tk
                qry = lax.broadcasted_iota(jnp.int32, (tk, cw), 1) + (c * cw) % tq
                s = jnp.where(key <= qry, s, -jnp.inf)
            m_old = m_ref[:, cs]
            m_new = jnp.maximum(m_old, jnp.max(s, axis=0, keepdims=True))
            alpha = jnp.exp2(m_old - m_new)
            p = jnp.exp2(s - m_new)
            l_ref[:, cs] = alpha * l_ref[:, cs] + jnp.sum(p, axis=0, keepdims=True)
            acc_ref[:, cs] = alpha * acc_ref[:, cs] + jnp.dot(vbt, p.astype(BF16),
                                                               preferred_element_type=F32)
            m_ref[:, cs] = m_new
        return s_next

    n_full = qi * (tq // tk)
    s_cur = lax.fori_loop(0, n_full, lambda j, s0: block(j, s0), scores(0, 0))
    for d in range(tq // tk):
        s_cur = block(n_full + d, s_cur, diag=d)

    lp = lp_ref[...]
    lam = (jnp.exp(jnp.sum(lp[0:1, :] * lp[1:2, :], axis=1, keepdims=True))
           - jnp.exp(jnp.sum(lp[2:3, :] * lp[3:4, :], axis=1, keepdims=True)) + lambda_init)
    out = acc_ref[...] / l_ref[...]
    o = out[:, :tq] - lam * out[:, tq:]
    ms = jnp.mean(o * o, axis=0, keepdims=True)
    on = o * lax.rsqrt(ms + DA_EPS) * ng_ref[...] * (1.0 - lambda_init)
    o_ref[...] = on.T.astype(o_ref.dtype)


def _diff_attn(p, lam_params, norm_g, lambda_init, *, batch, seq):
    tq = DA_TQ
    p3 = p.reshape(batch, seq, 2 * DA_QK + DA_H * DA_DV)
    vt = jnp.swapaxes(p3[:, :, 2 * DA_QK:], 1, 2)
    kern = functools.partial(_diff_attn_kernel, lambda_init=lambda_init)
    o = pl.pallas_call(
        kern,
        out_shape=jax.ShapeDtypeStruct((batch, seq, DA_H * DA_DV), BF16),
        grid=(batch, DA_H, seq // tq),
        in_specs=[
            pl.BlockSpec((None, tq, 2 * DA_DH), lambda b, h, i: (b, i, h)),
            pl.BlockSpec((None, seq, 2 * DA_DH), lambda b, h, i: (b, 0, DA_H + h)),
            pl.BlockSpec((None, DA_DV, seq), lambda b, h, i: (b, h, 0)),
            pl.BlockSpec((4, DA_DH), lambda b, h, i: (0, 0)),
            pl.BlockSpec((DA_DV, 1), lambda b, h, i: (0, 0)),
        ],
        out_specs=pl.BlockSpec((None, tq, DA_DV), lambda b, h, i: (b, i, h)),
        scratch_shapes=[
            pltpu.VMEM((2 * tq, 2 * DA_DH), BF16),
            pltpu.VMEM((1, 2 * tq), F32),
            pltpu.VMEM((1, 2 * tq), F32),
            pltpu.VMEM((DA_DV, 2 * tq), F32),
        ],
        compiler_params=_cparams(("parallel", "parallel", "arbitrary")),
        name="diff_attn",
    )(p3, p3, vt, lam_params, norm_g)
    return o.reshape(batch * seq, DA_H * DA_DV)


def _router_kernel(x_ref, g_ref, w_ref, b_ref, hn_ref, gate_ref):
    x = x_ref[...]
    ms = jnp.mean(x * x, axis=-1, keepdims=True)
    hn = (x * lax.rsqrt(ms + EPS) * g_ref[...]).astype(BF16)
    hn_ref[...] = hn
    lg = jnp.dot(hn, w_ref[...], preferred_element_type=F32) + b_ref[...]
    lane = lax.broadcasted_iota(jnp.int32, lg.shape, 1)
    neg = -jnp.inf

    gmask = (lane >= N_EXPERTS) & (lane < N_EXPERTS + N_GROUPS)
    gl = jnp.where(gmask, lg, neg)
    gmax = jnp.max(gl, axis=1, keepdims=True)
    gidx = jnp.min(jnp.where(gl == gmax, lane, LANES), axis=1, keepdims=True) - N_EXPERTS
    gsum = jnp.sum(jnp.where(gmask, jnp.exp(gl - gmax), 0.0), axis=1, keepdims=True)
    g_w = 1.0 / gsum

    emask = (lane >= gidx * EPG) & (lane < gidx * EPG + EPG)
    el = jnp.where(emask, lg, neg)
    emax = jnp.max(el, axis=1, keepdims=True)
    eexp = jnp.where(emask, jnp.exp(el - emax), 0.0)
    ep = eexp / jnp.sum(eexp, axis=1, keepdims=True)
    ep = jnp.where(emask, ep, -1.0)
    p1 = jnp.max(ep, axis=1, keepdims=True)
    i1 = jnp.min(jnp.where(ep == p1, lane, LANES), axis=1, keepdims=True)
    ep2 = jnp.where(lane == i1, -1.0, ep)
    p2 = jnp.max(ep2, axis=1, keepdims=True)
    i2 = jnp.min(jnp.where(ep2 == p2, lane, LANES), axis=1, keepdims=True)
    wsum = p1 + p2
    gate_ref[...] = (jnp.where(lane == i1, g_w * (p1 / wsum), 0.0)
                     + jnp.where(lane == i2, g_w * (p2 / wsum), 0.0))


def _router(x, g, w_group, b_group, w_expert, b_expert, *, tm):
    t, d = x.shape
    w = jnp.zeros((d, LANES), BF16)
    w = w.at[:, :N_EXPERTS].set(w_expert.astype(BF16))
    w = w.at[:, N_EXPERTS:N_EXPERTS + N_GROUPS].set(w_group.astype(BF16))
    b = jnp.zeros((1, LANES), F32)
    b = b.at[0, :N_EXPERTS].set(b_expert.astype(F32))
    b = b.at[0, N_EXPERTS:N_EXPERTS + N_GROUPS].set(b_group.astype(F32))
    return pl.pallas_call(
        _router_kernel,
        out_shape=(jax.ShapeDtypeStruct((t, d), BF16), jax.ShapeDtypeStruct((t, LANES), F32)),
        grid=(t // tm,),
        in_specs=[
            pl.BlockSpec((tm, d), lambda i: (i, 0)),
            pl.BlockSpec((1, d), lambda i: (0, 0)),
            pl.BlockSpec((d, LANES), lambda i: (0, 0)),
            pl.BlockSpec((1, LANES), lambda i: (0, 0)),
        ],
        out_specs=(pl.BlockSpec((tm, d), lambda i: (i, 0)),
                   pl.BlockSpec((tm, LANES), lambda i: (i, 0))),
        compiler_params=_cparams(("parallel",)),
        name="moe_router",
    )(x, g, w, b)


def _experts_kernel(hn_ref, gate_ref, wgu_ref, wd_ref, r_ref, o_ref):
    e = pl.program_id(1)

    @pl.when(e == 0)
    def _():
        o_ref[...] = r_ref[...]

    gate = gate_ref[...]
    lane = lax.broadcasted_iota(jnp.int32, gate.shape, 1)
    ge = jnp.sum(jnp.where(lane == e, gate, 0.0), axis=1, keepdims=True)
    gu = jnp.dot(hn_ref[...], wgu_ref[...], preferred_element_type=F32)
    gpart = gu[:, :D_EXPERT]
    a = gpart * jax.nn.sigmoid(gpart) * gu[:, D_EXPERT:] * ge
    o_ref[...] += jnp.dot(a.astype(BF16), wd_ref[...], preferred_element_type=F32)


def _experts(hn, gate, w_gu, w_down, res, *, tm):
    t, d = hn.shape
    return pl.pallas_call(
        _experts_kernel,
        out_shape=jax.ShapeDtypeStruct((t, d), F32),
        grid=(t // tm, N_EXPERTS),
        in_specs=[
            pl.BlockSpec((tm, d), lambda i, e: (i, 0)),
            pl.BlockSpec((tm, LANES), lambda i, e: (i, 0)),
            pl.BlockSpec((None, d, 2 * D_EXPERT), lambda i, e: (e, 0, 0)),
            pl.BlockSpec((None, D_EXPERT, d), lambda i, e: (e, 0, 0)),
            pl.BlockSpec((tm, d), lambda i, e: (i, 0)),
        ],
        out_specs=pl.BlockSpec((tm, d), lambda i, e: (i, 0)),
        compiler_params=_cparams(("parallel", "arbitrary")),
        name="moe_experts",
    )(hn, gate, w_gu, w_down, res)


def _rmsnorm_kernel(x_ref, g_ref, o_ref):
    x = x_ref[...]
    ms = jnp.mean(x * x, axis=-1, keepdims=True)
    o_ref[...] = x * lax.rsqrt(ms + EPS) * g_ref[...]


def _rmsnorm(x, g, *, tm):
    t, d = x.shape
    return pl.pallas_call(
        _rmsnorm_kernel,
        out_shape=jax.ShapeDtypeStruct((t, d), F32),
        grid=(t // tm,),
        in_specs=[pl.BlockSpec((tm, d), lambda i: (i, 0)),
                  pl.BlockSpec((1, d), lambda i: (0, 0))],
        out_specs=pl.BlockSpec((tm, d), lambda i: (i, 0)),
        compiler_params=_cparams(("parallel",)),
        name="final_rmsnorm",
    )(x, g)


def _moe_layer(x, norm_g, w_group, b_group, w_expert, b_expert, w_gu, w_down):
    hn, gate = _router(x, norm_g.reshape(1, -1), w_group, b_group, w_expert, b_expert, tm=1024)
    return _experts(hn, gate, w_gu.astype(BF16), w_down.astype(BF16), x, tm=1024)


def kernel(x, norm_mix, norm_ffn, ml_w_in, ml_conv, ml_b_i, ml_b_f, ml_norm, ml_w_out, da_w_in, da_lq1, da_lk1, da_lq2, da_lk2, da_norm, da_w_out, moe_w_group, moe_b_group, moe_w_expert, moe_b_expert, moe_w_gu, moe_w_down, final_norm):
    batch, seq, d = x.shape
    xt = x.reshape(batch * seq, d)

    g0 = norm_mix[0].reshape(1, d)
    n_main = 2 * ML_QK + 2 * ML_V
    p = _norm_matmul(xt, g0, ml_w_in[0][:, :n_main].astype(BF16), tm=1024, tn=1024,
                     out_dtype=BF16)
    gcol, grow = _ml_gates(xt, g0, ml_w_in[0][:, n_main:], ml_b_i[0], ml_b_f[0], tm=1024)
    y = _mlstm_core(p, gcol, grow, ml_conv[0], ml_norm[0].reshape(1, ML_V), batch=batch, seq=seq)
    xt = _matmul_res(y, ml_w_out[0].astype(BF16), xt, tm=512)
    xt = _moe_layer(xt, norm_ffn[0], moe_w_group[0], moe_b_group[0], moe_w_expert[0],
                    moe_b_expert[0], moe_w_gu[0], moe_w_down[0])

    lambda_init = 0.8 - 0.6 * math.exp(-0.3 * 1)
    p = _norm_matmul(xt, norm_mix[1].reshape(1, d), da_w_in[0].astype(BF16), tm=1024, tn=1024,
                     out_dtype=BF16)
    lam_params = jnp.stack([da_lq1[0], da_lk1[0], da_lq2[0], da_lk2[0]]).astype(F32)
    a = _diff_attn(p, lam_params, da_norm[0].reshape(DA_DV, 1), lambda_init, batch=batch, seq=seq)
    xt = _matmul_res(a, da_w_out[0].astype(BF16), xt, tm=512)
    xt = _moe_layer(xt, norm_ffn[1], moe_w_group[1], moe_b_group[1], moe_w_expert[1],
                    moe_b_expert[1], moe_w_gu[1], moe_w_down[1])

    out = _rmsnorm(xt, final_norm.reshape(1, d), tm=1024)
    return out.reshape(batch, seq, d)
```

```python
import functools
import math

import jax
import jax.numpy as jnp
from jax import lax
from jax.experimental import pallas as pl
from jax.experimental.pallas import tpu as pltpu
from jax.experimental.pallas import tpu_sc as plsc

F32 = jnp.float32
BF16 = jnp.bfloat16
U32 = jnp.uint32

D_MODEL = 1024
EPS = 1e-6
ML_H = 4
ML_DV = 512
ML_DQK = 256
ML_QK = ML_H * ML_DQK
ML_V = ML_H * ML_DV
CONV_K = 4
ML_CHUNK = 256
CONV_TAIL = 8
DA_H = 8
DA_DH = 64
DA_DV = 128
DA_QK = DA_H * 2 * DA_DH
DA_EPS = 1e-5
DA_TQ = 1024
DA_TK = 512
DA_CW = 512
LOG2E = 1.4426950408889634
N_GROUPS = 4
EPG = 8
N_EXPERTS = 32
D_EXPERT = 256
MOE_TR = 256
HALF_W = D_MODEL // 4
SC_WINDOW = 128
LANES = 128

VMEM_LIMIT = 48 * 1024 * 1024


def _cparams(sem):
    return pltpu.CompilerParams(dimension_semantics=sem, vmem_limit_bytes=VMEM_LIMIT)


def _norm_matmul_kernel(x_ref, g_ref, w_ref, o_ref, xn_ref):
    @pl.when(pl.program_id(1) == 0)
    def _():
        x = x_ref[...]
        ms = jnp.mean(x * x, axis=-1, keepdims=True)
        xn_ref[...] = (x * lax.rsqrt(ms + EPS) * g_ref[...]).astype(BF16)

    o_ref[...] = jnp.dot(xn_ref[...], w_ref[...], preferred_element_type=F32).astype(o_ref.dtype)


def _norm_matmul(x, g, w, *, tm, tn, out_dtype):
    t, d = x.shape
    n = w.shape[1]
    return pl.pallas_call(
        _norm_matmul_kernel,
        out_shape=jax.ShapeDtypeStruct((t, n), out_dtype),
        grid=(t // tm, n // tn),
        in_specs=[
            pl.BlockSpec((tm, d), lambda i, j: (i, 0)),
            pl.BlockSpec((1, d), lambda i, j: (0, 0)),
            pl.BlockSpec((d, tn), lambda i, j: (0, j)),
        ],
        out_specs=pl.BlockSpec((tm, tn), lambda i, j: (i, j)),
        scratch_shapes=[pltpu.VMEM((tm, d), BF16)],
        compiler_params=_cparams(("parallel", "arbitrary")),
        name="norm_matmul",
    )(x, g, w)


def _matmul_res_kernel(a_ref, w_ref, r_ref, o_ref):
    o_ref[...] = r_ref[...] + jnp.dot(a_ref[...], w_ref[...], preferred_element_type=F32)


def _matmul_res(a, w, res, *, tm):
    t, k = a.shape
    n = w.shape[1]
    return pl.pallas_call(
        _matmul_res_kernel,
        out_shape=jax.ShapeDtypeStruct((t, n), F32),
        grid=(t // tm,),
        in_specs=[
            pl.BlockSpec((tm, k), lambda i: (i, 0)),
            pl.BlockSpec((k, n), lambda i: (0, 0)),
            pl.BlockSpec((tm, n), lambda i: (i, 0)),
        ],
        out_specs=pl.BlockSpec((tm, n), lambda i: (i, 0)),
        compiler_params=_cparams(("parallel",)),
        name="matmul_res",
    )(a, w, res)


def _log_sigmoid(x):
    return jnp.minimum(x, 0.0) - jnp.log1p(jnp.exp(-jnp.abs(x)))


def _ml_gates_kernel(x_ref, g_ref, wc_ref, wr_ref, bc_ref, br_ref, oc_ref, or_ref):
    x = x_ref[...]
    ms = jnp.mean(x * x, axis=-1, keepdims=True)
    xn = (x * lax.rsqrt(ms + EPS) * g_ref[...]).astype(BF16)
    gc = jnp.dot(xn, wc_ref[...], preferred_element_type=F32) + bc_ref[...]
    lane = lax.broadcasted_iota(jnp.int32, gc.shape, 1)
    oc_ref[...] = jnp.where(lane < ML_H, gc, _log_sigmoid(gc))
    gr = lax.dot_general(wr_ref[...], xn, (((1,), (1,)), ((), ())),
                         preferred_element_type=F32) + br_ref[...]
    row = lax.broadcasted_iota(jnp.int32, gr.shape, 0)
    or_ref[...] = jnp.where(row < ML_H, gr, _log_sigmoid(gr))


def _ml_gates(x, g, w_gates, b_i, b_f, *, tm):
    t, d = x.shape
    ng = 2 * ML_H
    wc = jnp.zeros((d, LANES), BF16).at[:, :ng].set(w_gates.astype(BF16))
    wr = w_gates.T.astype(BF16)
    bias = jnp.concatenate([b_i, b_f]).astype(F32)
    bc = jnp.zeros((1, LANES), F32).at[0, :ng].set(bias)
    br = bias.reshape(ng, 1)
    return pl.pallas_call(
        _ml_gates_kernel,
        out_shape=(jax.ShapeDtypeStruct((t, LANES), F32), jax.ShapeDtypeStruct((ng, t), F32)),
        grid=(t // tm,),
        in_specs=[
            pl.BlockSpec((tm, d), lambda i: (i, 0)),
            pl.BlockSpec((1, d), lambda i: (0, 0)),
            pl.BlockSpec((d, LANES), lambda i: (0, 0)),
            pl.BlockSpec((ng, d), lambda i: (0, 0)),
            pl.BlockSpec((1, LANES), lambda i: (0, 0)),
            pl.BlockSpec((ng, 1), lambda i: (0, 0)),
        ],
        out_specs=(pl.BlockSpec((tm, LANES), lambda i: (i, 0)),
                   pl.BlockSpec((ng, tm), lambda i: (0, i))),
        compiler_params=_cparams(("parallel",)),
        name="ml_gates",
    )(x, g, wc, wr, bc, br)


def _split3(x):
    hi = x.astype(BF16)
    r = x - hi.astype(F32)
    mid = r.astype(BF16)
    lo = (r - mid.astype(F32)).astype(BF16)
    return hi, mid, lo


def _mlstm_kernel(q_ref, k_ref, v_ref, o_ref, gc_ref, gr_ref, cw_ref, ng_ref, y_ref,
                  ext_ref, ct_ref, n_ref, m_ref):
    L = ML_CHUNK
    c = pl.program_id(1)

    @pl.when(c == 0)
    def _():
        ext_ref[0:CONV_TAIL, :] = jnp.zeros((CONV_TAIL, 2 * ML_QK), F32)
        ct_ref[...] = jnp.zeros_like(ct_ref)
        n_ref[...] = jnp.zeros_like(n_ref)
        m_ref[...] = jnp.zeros_like(m_ref)

    ext_ref[CONV_TAIL:CONV_TAIL + L, 0:ML_QK] = q_ref[...].astype(F32)
    ext_ref[CONV_TAIL:CONV_TAIL + L, ML_QK:2 * ML_QK] = k_ref[...].astype(F32)
    conv = None
    for j in range(CONV_K):
        sh = ext_ref[pl.ds(CONV_TAIL - (CONV_K - 1) + j, L), :] * cw_ref[j:j + 1, :]
        conv = sh if conv is None else conv + sh
    ext_ref[0:CONV_TAIL, :] = ext_ref[L:L + CONV_TAIL, :]
    qk = conv * jax.nn.sigmoid(conv)
    q_all = (qk[:, :ML_QK] * (ML_DQK ** -0.5)).astype(BF16)
    k_all = qk[:, ML_QK:]

    r_i = lax.broadcasted_iota(jnp.int32, (L, L), 0)
    c_i = lax.broadcasted_iota(jnp.int32, (L, L), 1)
    causal = c_i <= r_i
    tril = causal.astype(BF16)
    triu = (r_i <= c_i).astype(BF16)
    gc = gc_ref[...]
    gr = gr_ref[...]
    bc_all = sum(jnp.dot(tril, p, preferred_element_type=F32) for p in _split3(gc))
    br_all = sum(jnp.dot(p, triu, preferred_element_type=F32) for p in _split3(gr))

    for h in range(ML_H):
        qh = q_all[:, h * ML_DQK:(h + 1) * ML_DQK]
        kh_f = k_all[:, h * ML_DQK:(h + 1) * ML_DQK]
        kh = kh_f.astype(BF16)
        vh = v_ref[:, h * ML_DV:(h + 1) * ML_DV]
        it_col = gc[:, h:h + 1]
        it_row = gr[h:h + 1, :]
        b_col = bc_all[:, ML_H + h:ML_H + h + 1]
        b_row = br_all[ML_H + h:ML_H + h + 1, :]
        m_prev = m_ref[h][:, 0:1]

        dmat = jnp.where(causal, b_col - b_row + it_row, -jnp.inf)
        inter_log = b_col + m_prev
        m_t = jnp.maximum(inter_log, jnp.max(dmat, axis=1, keepdims=True))
        wts = jnp.exp(dmat - m_t)
        s = lax.dot_general(qh, kh, (((1,), (1,)), ((), ())), preferred_element_type=F32)
        sc = s * wts
        inter_scale = jnp.exp(inter_log - m_t)
        ct = ct_ref[h]
        num = (jnp.dot(sc.astype(BF16), vh, preferred_element_type=F32)
               + inter_scale * jnp.dot(qh, ct.astype(BF16), preferred_element_type=F32))
        n_row = n_ref[h]
        den = (jnp.sum(sc, axis=1, keepdims=True)
               + inter_scale * jnp.sum(qh.astype(F32) * n_row, axis=1, keepdims=True))
        h_out = num / jnp.maximum(jnp.abs(den), jnp.exp(-m_t))

        b_last = b_col[L - 1:L, :]
        lw_col = b_last - b_col + it_col
        lw_row = b_last - b_row + it_row
        m_new = jnp.maximum(b_last + m_prev, jnp.max(lw_row, axis=1, keepdims=True))
        ws_col = jnp.exp(lw_col - m_new)
        decay = jnp.exp(b_last + m_prev - m_new)
        vw = (vh.astype(F32) * ws_col).astype(BF16)
        ct_ref[h] = decay * ct + lax.dot_general(kh, vw, (((0,), (0,)), ((), ())),
                                                  preferred_element_type=F32)
        n_ref[h] = decay * n_row + jnp.sum(kh_f * ws_col, axis=0, keepdims=True)
        m_ref[h] = jnp.broadcast_to(m_new, (1, LANES))

        ms = jnp.mean(h_out * h_out, axis=1, keepdims=True)
        hn = h_out * lax.rsqrt(ms + EPS) * ng_ref[:, h * ML_DV:(h + 1) * ML_DV]
        og = o_ref[:, h * ML_DV:(h + 1) * ML_DV].astype(F32)
        y_ref[:, h * ML_DV:(h + 1) * ML_DV] = (hn * jax.nn.sigmoid(og)).astype(y_ref.dtype)


def _mlstm_core(p, gcol, grow, conv_w, norm_g, *, batch, seq):
    L = ML_CHUNK
    nc = seq // L
    p3 = p.reshape(batch, seq, 2 * ML_QK + 2 * ML_V)
    gc3 = gcol.reshape(batch, seq, LANES)
    gr3 = grow.reshape(2 * ML_H, batch, seq).transpose(1, 0, 2)
    y = pl.pallas_call(
        _mlstm_kernel,
        out_shape=jax.ShapeDtypeStruct((batch, seq, ML_V), BF16),
        grid=(batch, nc),
        in_specs=[
            pl.BlockSpec((None, L, ML_QK), lambda b, c: (b, c, 0)),
            pl.BlockSpec((None, L, ML_QK), lambda b, c: (b, c, 1)),
            pl.BlockSpec((None, L, ML_V), lambda b, c: (b, c, 1)),
            pl.BlockSpec((None, L, ML_V), lambda b, c: (b, c, 2)),
            pl.BlockSpec((None, L, LANES), lambda b, c: (b, c, 0)),
            pl.BlockSpec((None, 2 * ML_H, L), lambda b, c: (b, 0, c)),
            pl.BlockSpec((CONV_K, 2 * ML_QK), lambda b, c: (0, 0)),
            pl.BlockSpec((1, ML_V), lambda b, c: (0, 0)),
        ],
        out_specs=pl.BlockSpec((None, L, ML_V), lambda b, c: (b, c, 0)),
        scratch_shapes=[
            pltpu.VMEM((L + CONV_TAIL, 2 * ML_QK), F32),
            pltpu.VMEM((ML_H, ML_DQK, ML_DV), F32),
            pltpu.VMEM((ML_H, 1, ML_DQK), F32),
            pltpu.VMEM((ML_H, 1, LANES), F32),
        ],
        compiler_params=_cparams(("parallel", "arbitrary")),
        name="mlstm_core",
    )(p3, p3, p3, p3, gc3, gr3, conv_w, norm_g)
    return y.reshape(batch * seq, ML_V)


def _diff_attn_kernel(q_ref, k_ref, vt_ref, lp_ref, ng_ref, o_ref, q2_ref, m_ref, l_ref, acc_ref,
                      *, lambda_init):
    tq, tk, cw = DA_TQ, DA_TK, DA_CW
    qi = pl.program_id(2)
    q = q_ref[...].astype(F32) * (DA_DH ** -0.5 * LOG2E)
    lane = lax.broadcasted_iota(jnp.int32, q.shape, 1)
    q2_ref[0:tq, :] = jnp.where(lane < DA_DH, q, 0.0).astype(BF16)
    q2_ref[tq:2 * tq, :] = jnp.where(lane >= DA_DH, q, 0.0).astype(BF16)

    m_ref[...] = jnp.full(m_ref.shape, -jnp.inf, F32)
    l_ref[...] = jnp.zeros_like(l_ref)
    acc_ref[...] = jnp.zeros_like(acc_ref)

    nch = 2 * tq // cw

    def scores(j, c):
        kb = k_ref[pl.ds(pl.multiple_of(j * tk, tk), tk), :]
        return lax.dot_general(kb, q2_ref[c * cw:(c + 1) * cw, :], (((1,), (1,)), ((), ())),
                               preferred_element_type=F32)

    def block(j, s_first, diag=None):
        vbt = vt_ref[:, pl.ds(pl.multiple_of(j * tk, tk), tk)]
        last = diag is not None and diag == tq // tk - 1
        s_next = s_first
        for c in range(nch):
            cs = slice(c * cw, (c + 1) * cw)
            s = s_next
            if c + 1 < nch:
                s_next = scores(j, c + 1)
            else:
                s_next = None if last else scores(j + 1, 0)
            if diag is not None:
                key = lax.broadcasted_iota(jnp.int32, (tk, cw), 0) + diag * tk
                qry = lax.broadcasted_iota(jnp.int32, (tk, cw), 1) + (c * cw) % tq
                s = jnp.where(key <= qry, s, -jnp.inf)
            m_old = m_ref[:, cs]
            m_new = jnp.maximum(m_old, jnp.max(s, axis=0, keepdims=True))
            alpha = jnp.exp2(m_old - m_new)
            p = jnp.exp2(s - m_new)
            l_ref[:, cs] = alpha * l_ref[:, cs] + jnp.sum(p, axis=0, keepdims=True)
            acc_ref[:, cs] = alpha * acc_ref[:, cs] + jnp.dot(vbt, p.astype(BF16),
                                                               preferred_element_type=F32)
            m_ref[:, cs] = m_new
        return s_next

    n_full = qi * (tq // tk)
    s_cur = lax.fori_loop(0, n_full, lambda j, s0: block(j, s0), scores(0, 0))
    for d in range(tq // tk):
        s_cur = block(n_full + d, s_cur, diag=d)

    lp = lp_ref[...]
    lam = (jnp.exp(jnp.sum(lp[0:1, :] * lp[1:2, :], axis=1, keepdims=True))
           - jnp.exp(jnp.sum(lp[2:3, :] * lp[3:4, :], axis=1, keepdims=True)) + lambda_init)
    out = acc_ref[...] / l_ref[...]
    o = out[:, :tq] - lam * out[:, tq:]
    ms = jnp.mean(o * o, axis=0, keepdims=True)
    on = o * lax.rsqrt(ms + DA_EPS) * ng_ref[...] * (1.0 - lambda_init)
    o_ref[...] = on.T.astype(o_ref.dtype)


def _diff_attn(p, lam_params, norm_g, lambda_init, *, batch, seq):
    tq = DA_TQ
    p3 = p.reshape(batch, seq, 2 * DA_QK + DA_H * DA_DV)
    vt = jnp.swapaxes(p3[:, :, 2 * DA_QK:], 1, 2)
    kern = functools.partial(_diff_attn_kernel, lambda_init=lambda_init)
    o = pl.pallas_call(
        kern,
        out_shape=jax.ShapeDtypeStruct((batch, seq, DA_H * DA_DV), BF16),
        grid=(batch, DA_H, seq // tq),
        in_specs=[
            pl.BlockSpec((None, tq, 2 * DA_DH), lambda b, h, i: (b, i, h)),
            pl.BlockSpec((None, seq, 2 * DA_DH), lambda b, h, i: (b, 0, DA_H + h)),
            pl.BlockSpec((None, DA_DV, seq), lambda b, h, i: (b, h, 0)),
            pl.BlockSpec((4, DA_DH), lambda b, h, i: (0, 0)),
            pl.BlockSpec((DA_DV, 1), lambda b, h, i: (0, 0)),
        ],
        out_specs=pl.BlockSpec((None, tq, DA_DV), lambda b, h, i: (b, i, h)),
        scratch_shapes=[
            pltpu.VMEM((2 * tq, 2 * DA_DH), BF16),
            pltpu.VMEM((1, 2 * tq), F32),
            pltpu.VMEM((1, 2 * tq), F32),
            pltpu.VMEM((DA_DV, 2 * tq), F32),
        ],
        compiler_params=_cparams(("parallel", "parallel", "arbitrary")),
        name="diff_attn",
    )(p3, p3, vt, lam_params, norm_g)
    return o.reshape(batch * seq, DA_H * DA_DV)


def _pack_halves(y):
    halves = []
    for h in range(2):
        base = h * 2 * HALF_W
        lo = y[:, base:base + HALF_W].astype(BF16).astype(F32)
        hi = y[:, base + HALF_W:base + 2 * HALF_W].astype(BF16).astype(F32)
        lo_bits = lax.bitcast_convert_type(lo, U32) >> 16
        hi_bits = lax.bitcast_convert_type(hi, U32)
        halves.append(hi_bits | lo_bits)
    return halves


def _unpack_halves(w0, w1):
    parts = []
    for w in (w0, w1):
        parts.append(lax.bitcast_convert_type(w << 16, F32))
        parts.append(lax.bitcast_convert_type(w & jnp.uint32(0xFFFF0000), F32))
    return jnp.concatenate(parts, axis=1)


def _router_kernel(x_ref, g_ref, w_ref, b_ref, hp_ref, meta_ref, metat_ref, cnt_ref,
                   run_ref, ls_ref):
    tm = x_ref.shape[0]

    @pl.when(pl.program_id(0) == 0)
    def _():
        run_ref[...] = jnp.zeros_like(run_ref)
        r_i = lax.broadcasted_iota(jnp.int32, (tm, tm), 0)
        c_i = lax.broadcasted_iota(jnp.int32, (tm, tm), 1)
        ls_ref[...] = (c_i < r_i).astype(BF16)

    x = x_ref[...]
    ms = jnp.mean(x * x, axis=-1, keepdims=True)
    hn32 = x * lax.rsqrt(ms + EPS) * g_ref[...]
    hn = hn32.astype(BF16)
    halves = _pack_halves(hn32)
    hp_ref[0] = halves[0]
    hp_ref[1] = halves[1]
    lg = jnp.dot(hn, w_ref[...], preferred_element_type=F32) + b_ref[...]
    lane = lax.broadcasted_iota(jnp.int32, lg.shape, 1)
    neg = -jnp.inf

    gmask = (lane >= N_EXPERTS) & (lane < N_EXPERTS + N_GROUPS)
    gl = jnp.where(gmask, lg, neg)
    gmax = jnp.max(gl, axis=1, keepdims=True)
    gidx = jnp.min(jnp.where(gl == gmax, lane, LANES), axis=1, keepdims=True) - N_EXPERTS
    gsum = jnp.sum(jnp.where(gmask, jnp.exp(gl - gmax), 0.0), axis=1, keepdims=True)
    g_w = 1.0 / gsum

    emask = (lane >= gidx * EPG) & (lane < gidx * EPG + EPG)
    el = jnp.where(emask, lg, neg)
    emax = jnp.max(el, axis=1, keepdims=True)
    eexp = jnp.where(emask, jnp.exp(el - emax), 0.0)
    ep = eexp / jnp.sum(eexp, axis=1, keepdims=True)
    ep = jnp.where(emask, ep, -1.0)
    p1 = jnp.max(ep, axis=1, keepdims=True)
    i1 = jnp.min(jnp.where(ep == p1, lane, LANES), axis=1, keepdims=True)
    ep2 = jnp.where(lane == i1, -1.0, ep)
    p2 = jnp.max(ep2, axis=1, keepdims=True)
    i2 = jnp.min(jnp.where(ep2 == p2, lane, LANES), axis=1, keepdims=True)
    wsum = p1 + p2
    w1 = g_w * (p1 / wsum)
    w2 = g_w * (p2 / wsum)

    a1 = lane == i1
    a2 = lane == i2
    onehot = (a1 | a2).astype(BF16)
    before = jnp.dot(ls_ref[...], onehot, preferred_element_type=F32) + run_ref[...]
    rank1 = jnp.sum(jnp.where(a1, before, 0.0), axis=1, keepdims=True)
    rank2 = jnp.sum(jnp.where(a2, before, 0.0), axis=1, keepdims=True)
    run_ref[...] += jnp.sum(onehot.astype(F32), axis=0, keepdims=True)
    cnt_ref[...] = run_ref[...]

    meta = (jnp.where(lane == 0, i1.astype(F32), 0.0) + jnp.where(lane == 1, i2.astype(F32), 0.0)
            + jnp.where(lane == 2, rank1, 0.0) + jnp.where(lane == 3, rank2, 0.0)
            + jnp.where(lane == 4, w1, 0.0) + jnp.where(lane == 5, w2, 0.0))
    meta_ref[...] = meta
    metat_ref[...] = meta.T[0:8, :]


def _router(x, g, w_group, b_group, w_expert, b_expert, *, tm):
    t, d = x.shape
    w = jnp.zeros((d, LANES), BF16)
    w = w.at[:, :N_EXPERTS].set(w_expert.astype(BF16))
    w = w.at[:, N_EXPERTS:N_EXPERTS + N_GROUPS].set(w_group.astype(BF16))
    b = jnp.zeros((1, LANES), F32)
    b = b.at[0, :N_EXPERTS].set(b_expert.astype(F32))
    b = b.at[0, N_EXPERTS:N_EXPERTS + N_GROUPS].set(b_group.astype(F32))
    return pl.pallas_call(
        _router_kernel,
        out_shape=(jax.ShapeDtypeStruct((2, t, HALF_W), U32),
                   jax.ShapeDtypeStruct((t, LANES), F32),
                   jax.ShapeDtypeStruct((8, t), F32),
                   jax.ShapeDtypeStruct((1, LANES), F32)),
        grid=(t // tm,),
        in_specs=[
            pl.BlockSpec((tm, d), lambda i: (i, 0)),
            pl.BlockSpec((1, d), lambda i: (0, 0)),
            pl.BlockSpec((d, LANES), lambda i: (0, 0)),
            pl.BlockSpec((1, LANES), lambda i: (0, 0)),
        ],
        out_specs=(pl.BlockSpec((2, tm, HALF_W), lambda i: (0, i, 0)),
                   pl.BlockSpec((tm, LANES), lambda i: (i, 0)),
                   pl.BlockSpec((8, tm), lambda i: (0, i)),
                   pl.BlockSpec((1, LANES), lambda i: (0, 0))),
        scratch_shapes=[pltpu.VMEM((1, LANES), F32), pltpu.VMEM((tm, tm), BF16)],
        compiler_params=_cparams(("arbitrary",)),
        name="moe_router",
    )(x, g, w, b)


def _sc_mesh():
    return plsc.VectorSubcoreMesh(core_axis_name="c", subcore_axis_name="s")


def _sc_scatter2(x, i0, i1, n_out):
    n, d = x.shape

    @pl.kernel(out_type=jax.ShapeDtypeStruct((n_out, d), x.dtype), mesh=_sc_mesh())
    def k(x_hbm, i0_hbm, i1_hbm, o_hbm):
        def body(x_vmem, i0_vmem, i1_vmem):
            pltpu.sync_copy(x_vmem, o_hbm.at[i0_vmem.at[0]])
            pltpu.sync_copy(x_vmem, o_hbm.at[i1_vmem.at[0]])

        pltpu.emit_pipeline(
            body,
            grid=(n // SC_WINDOW,),
            in_specs=[pl.BlockSpec((SC_WINDOW, d), lambda i: (i, 0)),
                      pl.BlockSpec((1, SC_WINDOW), lambda i: (0, i)),
                      pl.BlockSpec((1, SC_WINDOW), lambda i: (0, i))],
            out_specs=[],
            core_axis_name=("c", "s"),
            dimension_semantics=(pltpu.PARALLEL,),
        )(x_hbm, i0_hbm, i1_hbm)

    return k(x, i0.reshape(1, n), i1.reshape(1, n))


def _sc_gather(x, idx):
    n = idx.shape[0]
    d = x.shape[1]

    @pl.kernel(out_type=jax.ShapeDtypeStruct((n, d), x.dtype), mesh=_sc_mesh())
    def k(x_hbm, i_hbm, o_hbm):
        def body(i_vmem, o_vmem):
            pltpu.sync_copy(x_hbm.at[i_vmem.at[0]], o_vmem)

        pltpu.emit_pipeline(
            body,
            grid=(n // SC_WINDOW,),
            in_specs=[pl.BlockSpec((1, SC_WINDOW), lambda i: (0, i))],
            out_specs=[pl.BlockSpec((SC_WINDOW, d), lambda i: (i, 0))],
            core_axis_name=("c", "s"),
            dimension_semantics=(pltpu.PARALLEL,),
        )(i_hbm, o_hbm)

    return k(x, idx.reshape(1, n))


def _experts_kernel(te_ref, nu_ref, xs_ref, wgu_ref, wd_ref, ys_ref):
    del te_ref

    @pl.when(pl.program_id(0) < nu_ref[0])
    def _():
        x = _unpack_halves(xs_ref[0], xs_ref[1]).astype(BF16)
        gu = jnp.dot(x, wgu_ref[...], preferred_element_type=F32)
        gpart = gu[:, :D_EXPERT]
        a = gpart * jax.nn.sigmoid(gpart) * gu[:, D_EXPERT:]
        y = jnp.dot(a.astype(BF16), wd_ref[...], preferred_element_type=F32)
        halves = _pack_halves(y)
        ys_ref[0] = halves[0]
        ys_ref[1] = halves[1]


def _experts(xs, tile_expert, n_used, w_gu, w_down):
    _, rows, _ = xs.shape
    d = w_gu.shape[1]
    return pl.pallas_call(
        _experts_kernel,
        out_shape=jax.ShapeDtypeStruct(xs.shape, U32),
        grid_spec=pltpu.PrefetchScalarGridSpec(
            num_scalar_prefetch=2,
            grid=(rows // MOE_TR,),
            in_specs=[
                pl.BlockSpec((2, MOE_TR, HALF_W), lambda j, te, nu: (0, j, 0)),
                pl.BlockSpec((None, d, 2 * D_EXPERT), lambda j, te, nu: (te[j], 0, 0)),
                pl.BlockSpec((None, D_EXPERT, d), lambda j, te, nu: (te[j], 0, 0)),
            ],
            out_specs=pl.BlockSpec((2, MOE_TR, HALF_W), lambda j, te, nu: (0, j, 0)),
        ),
        compiler_params=_cparams(("arbitrary",)),
        name="moe_experts",
    )(tile_expert, n_used, xs, w_gu, w_down)


def _combine_kernel(x_ref, z_ref, meta_ref, g_ref, o_ref, *, final_norm):
    meta = meta_ref[...]
    y_a = _unpack_halves(z_ref[0], z_ref[2])
    y_b = _unpack_halves(z_ref[1], z_ref[3])
    out = x_ref[...] + meta[:, 4:5] * y_a + meta[:, 5:6] * y_b
    if final_norm:
        ms = jnp.mean(out * out, axis=-1, keepdims=True)
        out = out * lax.rsqrt(ms + EPS) * g_ref[...]
    o_ref[...] = out


def _combine(x, z, meta, g, *, tm, final_norm):
    t, d = x.shape
    return pl.pallas_call(
        functools.partial(_combine_kernel, final_norm=final_norm),
        out_shape=jax.ShapeDtypeStruct((t, d), F32),
        grid=(t // tm,),
        in_specs=[
            pl.BlockSpec((tm, d), lambda i: (i, 0)),
            pl.BlockSpec((4, tm, HALF_W), lambda i: (0, i, 0)),
            pl.BlockSpec((tm, LANES), lambda i: (i, 0)),
            pl.BlockSpec((1, d), lambda i: (0, 0)),
        ],
        out_specs=pl.BlockSpec((tm, d), lambda i: (i, 0)),
        compiler_params=_cparams(("parallel",)),
        name="moe_combine",
    )(x, z, meta, g)


def _moe_layer(x, norm_g, w_group, b_group, w_expert, b_expert, w_gu, w_down, final_g=None):
    t, d = x.shape
    hp, meta, metat, cnt = _router(x, norm_g.reshape(1, d), w_group, b_group, w_expert, b_expert,
                                   tm=1024)
    n_tiles = 2 * t // MOE_TR + N_EXPERTS
    rows = n_tiles * MOE_TR
    counts = cnt[0, :N_EXPERTS].astype(jnp.int32)
    tiles_e = (counts + MOE_TR - 1) // MOE_TR
    tiles_end = jnp.cumsum(tiles_e)
    row_off = (tiles_end - tiles_e) * MOE_TR
    e_a, e_b = metat[0].astype(jnp.int32), metat[1].astype(jnp.int32)
    pos_a = row_off[e_a] + metat[2].astype(jnp.int32)
    pos_b = row_off[e_b] + metat[3].astype(jnp.int32)
    tile_expert = jnp.minimum(
        jnp.searchsorted(tiles_end, jnp.arange(n_tiles, dtype=jnp.int32), side="right"),
        N_EXPERTS - 1).astype(jnp.int32)
    n_used = tiles_end[-1:].astype(jnp.int32)

    xs = _sc_scatter2(hp.reshape(2 * t, HALF_W),
                      jnp.concatenate([pos_a, pos_a + rows]),
                      jnp.concatenate([pos_b, pos_b + rows]), 2 * rows)
    ys = _experts(xs.reshape(2, rows, HALF_W), tile_expert, n_used,
                  w_gu.astype(BF16), w_down.astype(BF16))
    z = _sc_gather(ys.reshape(2 * rows, HALF_W),
                   jnp.concatenate([pos_a, pos_b, pos_a + rows, pos_b + rows]))
    g = jnp.ones((1, d), F32) if final_g is None else final_g.reshape(1, d)
    return _combine(x, z.reshape(4, t, HALF_W), meta, g, tm=1024, final_norm=final_g is not None)


def kernel(x, norm_mix, norm_ffn, ml_w_in, ml_conv, ml_b_i, ml_b_f, ml_norm, ml_w_out, da_w_in, da_lq1, da_lk1, da_lq2, da_lk2, da_norm, da_w_out, moe_w_group, moe_b_group, moe_w_expert, moe_b_expert, moe_w_gu, moe_w_down, final_norm):
    batch, seq, d = x.shape
    xt = x.reshape(batch * seq, d)

    g0 = norm_mix[0].reshape(1, d)
    n_main = 2 * ML_QK + 2 * ML_V
    p = _norm_matmul(xt, g0, ml_w_in[0][:, :n_main].astype(BF16), tm=1024, tn=1024,
                     out_dtype=BF16)
    gcol, grow = _ml_gates(xt, g0, ml_w_in[0][:, n_main:], ml_b_i[0], ml_b_f[0], tm=1024)
    y = _mlstm_core(p, gcol, grow, ml_conv[0], ml_norm[0].reshape(1, ML_V), batch=batch, seq=seq)
    xt = _matmul_res(y, ml_w_out[0].astype(BF16), xt, tm=512)
    xt = _moe_layer(xt, norm_ffn[0], moe_w_group[0], moe_b_group[0], moe_w_expert[0],
                    moe_b_expert[0], moe_w_gu[0], moe_w_down[0])

    lambda_init = 0.8 - 0.6 * math.exp(-0.3 * 1)
    p = _norm_matmul(xt, norm_mix[1].reshape(1, d), da_w_in[0].astype(BF16), tm=1024, tn=1024,
                     out_dtype=BF16)
    lam_params = jnp.stack([da_lq1[0], da_lk1[0], da_lq2[0], da_lk2[0]]).astype(F32)
    a = _diff_attn(p, lam_params, da_norm[0].reshape(DA_DV, 1), lambda_init, batch=batch, seq=seq)
    xt = _matmul_res(a, da_w_out[0].astype(BF16), xt, tm=512)
    out = _moe_layer(xt, norm_ffn[1], moe_w_group[1], moe_b_group[1], moe_w_expert[1],
                     moe_b_expert[1], moe_w_gu[1], moe_w_down[1], final_g=final_norm)
    return out.reshape(batch, seq, d)
```

```python
import functools
import math

import jax
import jax.numpy as jnp
from jax import lax
from jax.experimental import pallas as pl
from jax.experimental.pallas import tpu as pltpu
from jax.experimental.pallas import tpu_sc as plsc

F32 = jnp.float32
BF16 = jnp.bfloat16
U32 = jnp.uint32

D_MODEL = 1024
EPS = 1e-6
ML_H = 4
ML_DV = 512
ML_DQK = 256
ML_QK = ML_H * ML_DQK
ML_V = ML_H * ML_DV
CONV_K = 4
ML_CHUNK = 256
CONV_TAIL = 8
DA_H = 8
DA_DH = 64
DA_DV = 128
DA_QK = DA_H * 2 * DA_DH
DA_EPS = 1e-5
DA_TQ = 1024
DA_TK = 1024
DA_CW = 512
DA_KEY_ALIGN = 256
LOG2E = 1.4426950408889634
N_GROUPS = 4
EPG = 8
N_EXPERTS = 32
D_EXPERT = 256
MOE_TR = 256
HALF_W = D_MODEL // 4
SC_WINDOW = 128
LANES = 128

VMEM_LIMIT = 48 * 1024 * 1024


def _cparams(sem):
    return pltpu.CompilerParams(dimension_semantics=sem, vmem_limit_bytes=VMEM_LIMIT)


def _norm_matmul_kernel(x_ref, g_ref, w_ref, o_ref, xn_ref):
    @pl.when(pl.program_id(1) == 0)
    def _():
        x = x_ref[...]
        ms = jnp.mean(x * x, axis=-1, keepdims=True)
        xn_ref[...] = (x * lax.rsqrt(ms + EPS) * g_ref[...]).astype(BF16)

    o_ref[...] = jnp.dot(xn_ref[...], w_ref[...], preferred_element_type=F32).astype(o_ref.dtype)


def _norm_matmul(x, g, w, *, tm, tn, out_dtype):
    t, d = x.shape
    n = w.shape[1]
    return pl.pallas_call(
        _norm_matmul_kernel,
        out_shape=jax.ShapeDtypeStruct((t, n), out_dtype),
        grid=(t // tm, n // tn),
        in_specs=[
            pl.BlockSpec((tm, d), lambda i, j: (i, 0)),
            pl.BlockSpec((1, d), lambda i, j: (0, 0)),
            pl.BlockSpec((d, tn), lambda i, j: (0, j)),
        ],
        out_specs=pl.BlockSpec((tm, tn), lambda i, j: (i, j)),
        scratch_shapes=[pltpu.VMEM((tm, d), BF16)],
        compiler_params=_cparams(("parallel", "arbitrary")),
        name="norm_matmul",
    )(x, g, w)


def _matmul_res_kernel(a_ref, w_ref, r_ref, o_ref):
    o_ref[...] = r_ref[...] + jnp.dot(a_ref[...], w_ref[...], preferred_element_type=F32)


def _matmul_res(a, w, res, *, tm):
    t, k = a.shape
    n = w.shape[1]
    return pl.pallas_call(
        _matmul_res_kernel,
        out_shape=jax.ShapeDtypeStruct((t, n), F32),
        grid=(t // tm,),
        in_specs=[
            pl.BlockSpec((tm, k), lambda i: (i, 0)),
            pl.BlockSpec((k, n), lambda i: (0, 0)),
            pl.BlockSpec((tm, n), lambda i: (i, 0)),
        ],
        out_specs=pl.BlockSpec((tm, n), lambda i: (i, 0)),
        compiler_params=_cparams(("parallel",)),
        name="matmul_res",
    )(a, w, res)


def _log_sigmoid(x):
    return jnp.minimum(x, 0.0) - jnp.log1p(jnp.exp(-jnp.abs(x)))


def _ml_gates_kernel(x_ref, g_ref, wc_ref, wr_ref, bc_ref, br_ref, oc_ref, or_ref):
    x = x_ref[...]
    ms = jnp.mean(x * x, axis=-1, keepdims=True)
    xn = (x * lax.rsqrt(ms + EPS) * g_ref[...]).astype(BF16)
    gc = jnp.dot(xn, wc_ref[...], preferred_element_type=F32) + bc_ref[...]
    lane = lax.broadcasted_iota(jnp.int32, gc.shape, 1)
    oc_ref[...] = jnp.where(lane < ML_H, gc, _log_sigmoid(gc))
    gr = lax.dot_general(wr_ref[...], xn, (((1,), (1,)), ((), ())),
                         preferred_element_type=F32) + br_ref[...]
    row = lax.broadcasted_iota(jnp.int32, gr.shape, 0)
    or_ref[...] = jnp.where(row < ML_H, gr, _log_sigmoid(gr))


def _ml_gates(x, g, w_gates, b_i, b_f, *, tm):
    t, d = x.shape
    ng = 2 * ML_H
    wc = jnp.zeros((d, LANES), BF16).at[:, :ng].set(w_gates.astype(BF16))
    wr = w_gates.T.astype(BF16)
    bias = jnp.concatenate([b_i, b_f]).astype(F32)
    bc = jnp.zeros((1, LANES), F32).at[0, :ng].set(bias)
    br = bias.reshape(ng, 1)
    return pl.pallas_call(
        _ml_gates_kernel,
        out_shape=(jax.ShapeDtypeStruct((t, LANES), F32), jax.ShapeDtypeStruct((ng, t), F32)),
        grid=(t // tm,),
        in_specs=[
            pl.BlockSpec((tm, d), lambda i: (i, 0)),
            pl.BlockSpec((1, d), lambda i: (0, 0)),
            pl.BlockSpec((d, LANES), lambda i: (0, 0)),
            pl.BlockSpec((ng, d), lambda i: (0, 0)),
            pl.BlockSpec((1, LANES), lambda i: (0, 0)),
            pl.BlockSpec((ng, 1), lambda i: (0, 0)),
        ],
        out_specs=(pl.BlockSpec((tm, LANES), lambda i: (i, 0)),
                   pl.BlockSpec((ng, tm), lambda i: (0, i))),
        compiler_params=_cparams(("parallel",)),
        name="ml_gates",
    )(x, g, wc, wr, bc, br)


def _split3(x):
    hi = x.astype(BF16)
    r = x - hi.astype(F32)
    mid = r.astype(BF16)
    lo = (r - mid.astype(F32)).astype(BF16)
    return hi, mid, lo


def _mlstm_kernel(q_ref, k_ref, v_ref, o_ref, gc_ref, gr_ref, cw_ref, ng_ref, y_ref,
                  ext_ref, ct_ref, n_ref, m_ref):
    L = ML_CHUNK
    c = pl.program_id(1)

    @pl.when(c == 0)
    def _():
        ext_ref[0:CONV_TAIL, :] = jnp.zeros((CONV_TAIL, 2 * ML_QK), F32)
        ct_ref[...] = jnp.zeros_like(ct_ref)
        n_ref[...] = jnp.zeros_like(n_ref)
        m_ref[...] = jnp.zeros_like(m_ref)

    ext_ref[CONV_TAIL:CONV_TAIL + L, 0:ML_QK] = q_ref[...].astype(F32)
    ext_ref[CONV_TAIL:CONV_TAIL + L, ML_QK:2 * ML_QK] = k_ref[...].astype(F32)
    conv = None
    for j in range(CONV_K):
        sh = ext_ref[pl.ds(CONV_TAIL - (CONV_K - 1) + j, L), :] * cw_ref[j:j + 1, :]
        conv = sh if conv is None else conv + sh
    ext_ref[0:CONV_TAIL, :] = ext_ref[L:L + CONV_TAIL, :]
    qk = conv * jax.nn.sigmoid(conv)
    q_all = (qk[:, :ML_QK] * (ML_DQK ** -0.5)).astype(BF16)
    k_all = qk[:, ML_QK:]

    r_i = lax.broadcasted_iota(jnp.int32, (L, L), 0)
    c_i = lax.broadcasted_iota(jnp.int32, (L, L), 1)
    causal = c_i <= r_i
    tril = causal.astype(BF16)
    triu = (r_i <= c_i).astype(BF16)
    gc = gc_ref[...]
    gr = gr_ref[...]
    bc_all = sum(jnp.dot(tril, p, preferred_element_type=F32) for p in _split3(gc))
    br_all = sum(jnp.dot(p, triu, preferred_element_type=F32) for p in _split3(gr))

    for h in range(ML_H):
        qh = q_all[:, h * ML_DQK:(h + 1) * ML_DQK]
        kh_f = k_all[:, h * ML_DQK:(h + 1) * ML_DQK]
        kh = kh_f.astype(BF16)
        vh = v_ref[:, h * ML_DV:(h + 1) * ML_DV]
        it_col = gc[:, h:h + 1]
        it_row = gr[h:h + 1, :]
        b_col = bc_all[:, ML_H + h:ML_H + h + 1]
        b_row = br_all[ML_H + h:ML_H + h + 1, :]
        m_prev = m_ref[h][:, 0:1]

        dmat = jnp.where(causal, b_col - b_row + it_row, -jnp.inf)
        inter_log = b_col + m_prev
        m_t = jnp.maximum(inter_log, jnp.max(dmat, axis=1, keepdims=True))
        wts = jnp.exp(dmat - m_t)
        s = lax.dot_general(qh, kh, (((1,), (1,)), ((), ())), preferred_element_type=F32)
        sc = s * wts
        inter_scale = jnp.exp(inter_log - m_t)
        ct = ct_ref[h]
        num = (jnp.dot(sc.astype(BF16), vh, preferred_element_type=F32)
               + inter_scale * jnp.dot(qh, ct.astype(BF16), preferred_element_type=F32))
        n_row = n_ref[h]
        den = (jnp.sum(sc, axis=1, keepdims=True)
               + inter_scale * jnp.sum(qh.astype(F32) * n_row, axis=1, keepdims=True))
        h_out = num / jnp.maximum(jnp.abs(den), jnp.exp(-m_t))

        b_last = b_col[L - 1:L, :]
        lw_col = b_last - b_col + it_col
        lw_row = b_last - b_row + it_row
        m_new = jnp.maximum(b_last + m_prev, jnp.max(lw_row, axis=1, keepdims=True))
        ws_col = jnp.exp(lw_col - m_new)
        decay = jnp.exp(b_last + m_prev - m_new)
        vw = (vh.astype(F32) * ws_col).astype(BF16)
        ct_ref[h] = decay * ct + lax.dot_general(kh, vw, (((0,), (0,)), ((), ())),
                                                  preferred_element_type=F32)
        n_ref[h] = decay * n_row + jnp.sum(kh_f * ws_col, axis=0, keepdims=True)
        m_ref[h] = jnp.broadcast_to(m_new, (1, LANES))

        ms = jnp.mean(h_out * h_out, axis=1, keepdims=True)
        hn = h_out * lax.rsqrt(ms + EPS) * ng_ref[:, h * ML_DV:(h + 1) * ML_DV]
        og = o_ref[:, h * ML_DV:(h + 1) * ML_DV].astype(F32)
        y_ref[:, h * ML_DV:(h + 1) * ML_DV] = (hn * jax.nn.sigmoid(og)).astype(y_ref.dtype)


def _mlstm_core(p, gcol, grow, conv_w, norm_g, *, batch, seq):
    L = ML_CHUNK
    nc = seq // L
    p3 = p.reshape(batch, seq, 2 * ML_QK + 2 * ML_V)
    gc3 = gcol.reshape(batch, seq, LANES)
    gr3 = grow.reshape(2 * ML_H, batch, seq).transpose(1, 0, 2)
    y = pl.pallas_call(
        _mlstm_kernel,
        out_shape=jax.ShapeDtypeStruct((batch, seq, ML_V), BF16),
        grid=(batch, nc),
        in_specs=[
            pl.BlockSpec((None, L, ML_QK), lambda b, c: (b, c, 0)),
            pl.BlockSpec((None, L, ML_QK), lambda b, c: (b, c, 1)),
            pl.BlockSpec((None, L, ML_V), lambda b, c: (b, c, 1)),
            pl.BlockSpec((None, L, ML_V), lambda b, c: (b, c, 2)),
            pl.BlockSpec((None, L, LANES), lambda b, c: (b, c, 0)),
            pl.BlockSpec((None, 2 * ML_H, L), lambda b, c: (b, 0, c)),
            pl.BlockSpec((CONV_K, 2 * ML_QK), lambda b, c: (0, 0)),
            pl.BlockSpec((1, ML_V), lambda b, c: (0, 0)),
        ],
        out_specs=pl.BlockSpec((None, L, ML_V), lambda b, c: (b, c, 0)),
        scratch_shapes=[
            pltpu.VMEM((L + CONV_TAIL, 2 * ML_QK), F32),
            pltpu.VMEM((ML_H, ML_DQK, ML_DV), F32),
            pltpu.VMEM((ML_H, 1, ML_DQK), F32),
            pltpu.VMEM((ML_H, 1, LANES), F32),
        ],
        compiler_params=_cparams(("parallel", "arbitrary")),
        name="mlstm_core",
    )(p3, p3, p3, p3, gc3, gr3, conv_w, norm_g)
    return y.reshape(batch * seq, ML_V)


def _diff_attn_kernel(q_ref, k_ref, vt_ref, lp_ref, ng_ref, o_ref, q2_ref, m_ref, l_ref, acc_ref,
                      *, lambda_init):
    tq, tk, cw = DA_TQ, DA_TK, DA_CW
    qi = pl.program_id(2)
    q = q_ref[...].astype(F32) * (DA_DH ** -0.5 * LOG2E)
    lane = lax.broadcasted_iota(jnp.int32, q.shape, 1)
    q2_ref[0:tq, :] = jnp.where(lane < DA_DH, q, 0.0).astype(BF16)
    q2_ref[tq:2 * tq, :] = jnp.where(lane >= DA_DH, q, 0.0).astype(BF16)

    m_ref[...] = jnp.full(m_ref.shape, -jnp.inf, F32)
    l_ref[...] = jnp.zeros_like(l_ref)
    acc_ref[...] = jnp.zeros_like(acc_ref)

    nch = 2 * tq // cw

    def scores(j, c, nk=tk):
        kb = k_ref[pl.ds(pl.multiple_of(j * tk, tk), nk), :]
        return lax.dot_general(kb, q2_ref[c * cw:(c + 1) * cw, :], (((1,), (1,)), ((), ())),
                               preferred_element_type=F32)

    def keys_needed(c, diag):
        if diag is None:
            return tk
        visible = (c * cw) % tq + cw - diag * tk
        return max(0, min(tk, -(-visible // DA_KEY_ALIGN) * DA_KEY_ALIGN))

    def block(j, s_first, diag=None):
        start = pl.multiple_of(j * tk, tk)
        last = diag is not None and diag == tq // tk - 1
        s_next = s_first
        for c in range(nch):
            cs = slice(c * cw, (c + 1) * cw)
            s = s_next
            nk = keys_needed(c, diag)
            if c + 1 < nch:
                nk_next = keys_needed(c + 1, diag)
                s_next = scores(j, c + 1, nk_next) if nk_next else None
            else:
                s_next = None if last else scores(j + 1, 0)
            if nk == 0:
                continue
            s = s[:nk]
            if diag is not None:
                key = lax.broadcasted_iota(jnp.int32, (nk, cw), 0) + diag * tk
                qry = lax.broadcasted_iota(jnp.int32, (nk, cw), 1) + (c * cw) % tq
                s = jnp.where(key <= qry, s, -jnp.inf)
            vbt = vt_ref[:, pl.ds(start, nk)]
            m_old = m_ref[:, cs]
            m_new = jnp.maximum(m_old, jnp.max(s, axis=0, keepdims=True))
            alpha = jnp.exp2(m_old - m_new)
            p = jnp.exp2(s - m_new)
            l_ref[:, cs] = alpha * l_ref[:, cs] + jnp.sum(p, axis=0, keepdims=True)
            acc_ref[:, cs] = alpha * acc_ref[:, cs] + jnp.dot(vbt, p.astype(BF16),
                                                               preferred_element_type=F32)
            m_ref[:, cs] = m_new
        return s_next

    n_full = qi * (tq // tk)
    s_cur = lax.fori_loop(0, n_full, lambda j, s0: block(j, s0), scores(0, 0))
    for d in range(tq // tk):
        s_cur = block(n_full + d, s_cur, diag=d)

    lp = lp_ref[...]
    lam = (jnp.exp(jnp.sum(lp[0:1, :] * lp[1:2, :], axis=1, keepdims=True))
           - jnp.exp(jnp.sum(lp[2:3, :] * lp[3:4, :], axis=1, keepdims=True)) + lambda_init)
    out = acc_ref[...] / l_ref[...]
    o = out[:, :tq] - lam * out[:, tq:]
    ms = jnp.mean(o * o, axis=0, keepdims=True)
    on = o * lax.rsqrt(ms + DA_EPS) * ng_ref[...] * (1.0 - lambda_init)
    o_ref[...] = on.T.astype(o_ref.dtype)


def _diff_attn(p, lam_params, norm_g, lambda_init, *, batch, seq):
    tq = DA_TQ
    p3 = p.reshape(batch, seq, 2 * DA_QK + DA_H * DA_DV)
    vt = jnp.swapaxes(p3[:, :, 2 * DA_QK:], 1, 2)
    kern = functools.partial(_diff_attn_kernel, lambda_init=lambda_init)
    o = pl.pallas_call(
        kern,
        out_shape=jax.ShapeDtypeStruct((batch, seq, DA_H * DA_DV), BF16),
        grid=(batch, DA_H, seq // tq),
        in_specs=[
            pl.BlockSpec((None, tq, 2 * DA_DH), lambda b, h, i: (b, i, h)),
            pl.BlockSpec((None, seq, 2 * DA_DH), lambda b, h, i: (b, 0, DA_H + h)),
            pl.BlockSpec((None, DA_DV, seq), lambda b, h, i: (b, h, 0)),
            pl.BlockSpec((4, DA_DH), lambda b, h, i: (0, 0)),
            pl.BlockSpec((DA_DV, 1), lambda b, h, i: (0, 0)),
        ],
        out_specs=pl.BlockSpec((None, tq, DA_DV), lambda b, h, i: (b, i, h)),
        scratch_shapes=[
            pltpu.VMEM((2 * tq, 2 * DA_DH), BF16),
            pltpu.VMEM((1, 2 * tq), F32),
            pltpu.VMEM((1, 2 * tq), F32),
            pltpu.VMEM((DA_DV, 2 * tq), F32),
        ],
        compiler_params=_cparams(("parallel", "parallel", "arbitrary")),
        name="diff_attn",
    )(p3, p3, vt, lam_params, norm_g)
    return o.reshape(batch * seq, DA_H * DA_DV)


def _pack_halves(y):
    halves = []
    for h in range(2):
        base = h * 2 * HALF_W
        lo = y[:, base:base + HALF_W].astype(BF16).astype(F32)
        hi = y[:, base + HALF_W:base + 2 * HALF_W].astype(BF16).astype(F32)
        lo_bits = lax.bitcast_convert_type(lo, U32) >> 16
        hi_bits = lax.bitcast_convert_type(hi, U32)
        halves.append(hi_bits | lo_bits)
    return halves


def _unpack_halves(w0, w1):
    parts = []
    for w in (w0, w1):
        parts.append(lax.bitcast_convert_type(w << 16, F32))
        parts.append(lax.bitcast_convert_type(w & jnp.uint32(0xFFFF0000), F32))
    return jnp.concatenate(parts, axis=1)


def _router_kernel(x_ref, g_ref, w_ref, b_ref, hp_ref, meta_ref, metat_ref, cnt_ref,
                   run_ref, ls_ref):
    tm = x_ref.shape[0]

    @pl.when(pl.program_id(0) == 0)
    def _():
        run_ref[...] = jnp.zeros_like(run_ref)
        r_i = lax.broadcasted_iota(jnp.int32, (tm, tm), 0)
        c_i = lax.broadcasted_iota(jnp.int32, (tm, tm), 1)
        ls_ref[...] = (c_i < r_i).astype(BF16)

    x = x_ref[...]
    ms = jnp.mean(x * x, axis=-1, keepdims=True)
    hn32 = x * lax.rsqrt(ms + EPS) * g_ref[...]
    hn = hn32.astype(BF16)
    halves = _pack_halves(hn32)
    hp_ref[0] = halves[0]
    hp_ref[1] = halves[1]
    lg = jnp.dot(hn, w_ref[...], preferred_element_type=F32) + b_ref[...]
    lane = lax.broadcasted_iota(jnp.int32, lg.shape, 1)
    neg = -jnp.inf

    gmask = (lane >= N_EXPERTS) & (lane < N_EXPERTS + N_GROUPS)
    gl = jnp.where(gmask, lg, neg)
    gmax = jnp.max(gl, axis=1, keepdims=True)
    gidx = jnp.min(jnp.where(gl == gmax, lane, LANES), axis=1, keepdims=True) - N_EXPERTS
    gsum = jnp.sum(jnp.where(gmask, jnp.exp(gl - gmax), 0.0), axis=1, keepdims=True)
    g_w = 1.0 / gsum

    emask = (lane >= gidx * EPG) & (lane < gidx * EPG + EPG)
    el = jnp.where(emask, lg, neg)
    emax = jnp.max(el, axis=1, keepdims=True)
    eexp = jnp.where(emask, jnp.exp(el - emax), 0.0)
    ep = eexp / jnp.sum(eexp, axis=1, keepdims=True)
    ep = jnp.where(emask, ep, -1.0)
    p1 = jnp.max(ep, axis=1, keepdims=True)
    i1 = jnp.min(jnp.where(ep == p1, lane, LANES), axis=1, keepdims=True)
    ep2 = jnp.where(lane == i1, -1.0, ep)
    p2 = jnp.max(ep2, axis=1, keepdims=True)
    i2 = jnp.min(jnp.where(ep2 == p2, lane, LANES), axis=1, keepdims=True)
    wsum = p1 + p2
    w1 = g_w * (p1 / wsum)
    w2 = g_w * (p2 / wsum)

    a1 = lane == i1
    a2 = lane == i2
    onehot = (a1 | a2).astype(BF16)
    before = jnp.dot(ls_ref[...], onehot, preferred_element_type=F32) + run_ref[...]
    rank1 = jnp.sum(jnp.where(a1, before, 0.0), axis=1, keepdims=True)
    rank2 = jnp.sum(jnp.where(a2, before, 0.0), axis=1, keepdims=True)
    run_ref[...] += jnp.sum(onehot.astype(F32), axis=0, keepdims=True)
    cnt_ref[...] = run_ref[...]

    meta = (jnp.where(lane == 0, i1.astype(F32), 0.0) + jnp.where(lane == 1, i2.astype(F32), 0.0)
            + jnp.where(lane == 2, rank1, 0.0) + jnp.where(lane == 3, rank2, 0.0)
            + jnp.where(lane == 4, w1, 0.0) + jnp.where(lane == 5, w2, 0.0))
    meta_ref[...] = meta
    metat_ref[...] = meta.T[0:8, :]


def _router(x, g, w_group, b_group, w_expert, b_expert, *, tm):
    t, d = x.shape
    w = jnp.zeros((d, LANES), BF16)
    w = w.at[:, :N_EXPERTS].set(w_expert.astype(BF16))
    w = w.at[:, N_EXPERTS:N_EXPERTS + N_GROUPS].set(w_group.astype(BF16))
    b = jnp.zeros((1, LANES), F32)
    b = b.at[0, :N_EXPERTS].set(b_expert.astype(F32))
    b = b.at[0, N_EXPERTS:N_EXPERTS + N_GROUPS].set(b_group.astype(F32))
    return pl.pallas_call(
        _router_kernel,
        out_shape=(jax.ShapeDtypeStruct((2, t, HALF_W), U32),
                   jax.ShapeDtypeStruct((t, LANES), F32),
                   jax.ShapeDtypeStruct((8, t), F32),
                   jax.ShapeDtypeStruct((1, LANES), F32)),
        grid=(t // tm,),
        in_specs=[
            pl.BlockSpec((tm, d), lambda i: (i, 0)),
            pl.BlockSpec((1, d), lambda i: (0, 0)),
            pl.BlockSpec((d, LANES), lambda i: (0, 0)),
            pl.BlockSpec((1, LANES), lambda i: (0, 0)),
        ],
        out_specs=(pl.BlockSpec((2, tm, HALF_W), lambda i: (0, i, 0)),
                   pl.BlockSpec((tm, LANES), lambda i: (i, 0)),
                   pl.BlockSpec((8, tm), lambda i: (0, i)),
                   pl.BlockSpec((1, LANES), lambda i: (0, 0))),
        scratch_shapes=[pltpu.VMEM((1, LANES), F32), pltpu.VMEM((tm, tm), BF16)],
        compiler_params=_cparams(("arbitrary",)),
        name="moe_router",
    )(x, g, w, b)


def _sc_mesh():
    return plsc.VectorSubcoreMesh(core_axis_name="c", subcore_axis_name="s")


def _sc_scatter2(x, i0, i1, n_out):
    n, d = x.shape

    @pl.kernel(out_type=jax.ShapeDtypeStruct((n_out, d), x.dtype), mesh=_sc_mesh())
    def k(x_hbm, i0_hbm, i1_hbm, o_hbm):
        def body(x_vmem, i0_vmem, i1_vmem):
            pltpu.sync_copy(x_vmem, o_hbm.at[i0_vmem.at[0]])
            pltpu.sync_copy(x_vmem, o_hbm.at[i1_vmem.at[0]])

        pltpu.emit_pipeline(
            body,
            grid=(n // SC_WINDOW,),
            in_specs=[pl.BlockSpec((SC_WINDOW, d), lambda i: (i, 0)),
                      pl.BlockSpec((1, SC_WINDOW), lambda i: (0, i)),
                      pl.BlockSpec((1, SC_WINDOW), lambda i: (0, i))],
            out_specs=[],
            core_axis_name=("c", "s"),
            dimension_semantics=(pltpu.PARALLEL,),
        )(x_hbm, i0_hbm, i1_hbm)

    return k(x, i0.reshape(1, n), i1.reshape(1, n))


def _sc_gather(x, idx):
    n = idx.shape[0]
    d = x.shape[1]

    @pl.kernel(out_type=jax.ShapeDtypeStruct((n, d), x.dtype), mesh=_sc_mesh())
    def k(x_hbm, i_hbm, o_hbm):
        def body(i_vmem, o_vmem):
            pltpu.sync_copy(x_hbm.at[i_vmem.at[0]], o_vmem)

        pltpu.emit_pipeline(
            body,
            grid=(n // SC_WINDOW,),
            in_specs=[pl.BlockSpec((1, SC_WINDOW), lambda i: (0, i))],
            out_specs=[pl.BlockSpec((SC_WINDOW, d), lambda i: (i, 0))],
            core_axis_name=("c", "s"),
            dimension_semantics=(pltpu.PARALLEL,),
        )(i_hbm, o_hbm)

    return k(x, idx.reshape(1, n))


def _experts_kernel(te_ref, nu_ref, xs_ref, wgu_ref, wd_ref, ys_ref):
    del te_ref

    @pl.when(pl.program_id(0) < nu_ref[0])
    def _():
        x = _unpack_halves(xs_ref[0], xs_ref[1]).astype(BF16)
        gu = jnp.dot(x, wgu_ref[...].astype(BF16), preferred_element_type=F32)
        gpart = gu[:, :D_EXPERT]
        a = gpart * jax.nn.sigmoid(gpart) * gu[:, D_EXPERT:]
        y = jnp.dot(a.astype(BF16), wd_ref[...].astype(BF16), preferred_element_type=F32)
        halves = _pack_halves(y)
        ys_ref[0] = halves[0]
        ys_ref[1] = halves[1]


def _experts(xs, tile_expert, n_used, w_gu, w_down):
    _, rows, _ = xs.shape
    d = w_gu.shape[1]
    return pl.pallas_call(
        _experts_kernel,
        out_shape=jax.ShapeDtypeStruct(xs.shape, U32),
        grid_spec=pltpu.PrefetchScalarGridSpec(
            num_scalar_prefetch=2,
            grid=(rows // MOE_TR,),
            in_specs=[
                pl.BlockSpec((2, MOE_TR, HALF_W), lambda j, te, nu: (0, j, 0)),
                pl.BlockSpec((None, d, 2 * D_EXPERT), lambda j, te, nu: (te[j], 0, 0)),
                pl.BlockSpec((None, D_EXPERT, d), lambda j, te, nu: (te[j], 0, 0)),
            ],
            out_specs=pl.BlockSpec((2, MOE_TR, HALF_W), lambda j, te, nu: (0, j, 0)),
        ),
        compiler_params=_cparams(("arbitrary",)),
        name="moe_experts",
    )(tile_expert, n_used, xs, w_gu, w_down)


def _combine_kernel(x_ref, z_ref, meta_ref, g_ref, o_ref, *, final_norm):
    meta = meta_ref[...]
    y_a = _unpack_halves(z_ref[0], z_ref[2])
    y_b = _unpack_halves(z_ref[1], z_ref[3])
    out = x_ref[...] + meta[:, 4:5] * y_a + meta[:, 5:6] * y_b
    if final_norm:
        ms = jnp.mean(out * out, axis=-1, keepdims=True)
        out = out * lax.rsqrt(ms + EPS) * g_ref[...]
    o_ref[...] = out


def _combine(x, z, meta, g, *, tm, final_norm):
    t, d = x.shape
    return pl.pallas_call(
        functools.partial(_combine_kernel, final_norm=final_norm),
        out_shape=jax.ShapeDtypeStruct((t, d), F32),
        grid=(t // tm,),
        in_specs=[
            pl.BlockSpec((tm, d), lambda i: (i, 0)),
            pl.BlockSpec((4, tm, HALF_W), lambda i: (0, i, 0)),
            pl.BlockSpec((tm, LANES), lambda i: (i, 0)),
            pl.BlockSpec((1, d), lambda i: (0, 0)),
        ],
        out_specs=pl.BlockSpec((tm, d), lambda i: (i, 0)),
        compiler_params=_cparams(("parallel",)),
        name="moe_combine",
    )(x, z, meta, g)


def _moe_layer(x, norm_g, w_group, b_group, w_expert, b_expert, w_gu, w_down, final_g=None):
    t, d = x.shape
    hp, meta, metat, cnt = _router(x, norm_g.reshape(1, d), w_group, b_group, w_expert, b_expert,
                                   tm=1024)
    n_tiles = 2 * t // MOE_TR + N_EXPERTS
    rows = n_tiles * MOE_TR
    counts = cnt[0, :N_EXPERTS].astype(jnp.int32)
    tiles_e = (counts + MOE_TR - 1) // MOE_TR
    tiles_end = jnp.cumsum(tiles_e)
    row_off = (tiles_end - tiles_e) * MOE_TR
    experts = jnp.arange(N_EXPERTS, dtype=jnp.int32)[:, None]

    def region_start(e_row):
        return jnp.sum(jnp.where(e_row[None, :] == experts, row_off[:, None], 0), axis=0)

    e_a, e_b = metat[0].astype(jnp.int32), metat[1].astype(jnp.int32)
    pos_a = region_start(e_a) + metat[2].astype(jnp.int32)
    pos_b = region_start(e_b) + metat[3].astype(jnp.int32)
    tile_ids = jnp.arange(n_tiles, dtype=jnp.int32)
    tile_expert = jnp.minimum(
        jnp.sum((tile_ids[:, None] >= tiles_end[None, :]).astype(jnp.int32), axis=1),
        N_EXPERTS - 1)
    n_used = tiles_end[-1:].astype(jnp.int32)

    xs = _sc_scatter2(hp.reshape(2 * t, HALF_W),
                      jnp.concatenate([pos_a, pos_a + rows]),
                      jnp.concatenate([pos_b, pos_b + rows]), 2 * rows)
    ys = _experts(xs.reshape(2, rows, HALF_W), tile_expert, n_used,
                  w_gu, w_down)
    z = _sc_gather(ys.reshape(2 * rows, HALF_W),
                   jnp.concatenate([pos_a, pos_b, pos_a + rows, pos_b + rows]))
    g = jnp.ones((1, d), F32) if final_g is None else final_g.reshape(1, d)
    return _combine(x, z.reshape(4, t, HALF_W), meta, g, tm=1024, final_norm=final_g is not None)


def kernel(x, norm_mix, norm_ffn, ml_w_in, ml_conv, ml_b_i, ml_b_f, ml_norm, ml_w_out, da_w_in, da_lq1, da_lk1, da_lq2, da_lk2, da_norm, da_w_out, moe_w_group, moe_b_group, moe_w_expert, moe_b_expert, moe_w_gu, moe_w_down, final_norm):
    batch, seq, d = x.shape
    xt = x.reshape(batch * seq, d)

    g0 = norm_mix[0].reshape(1, d)
    n_main = 2 * ML_QK + 2 * ML_V
    p = _norm_matmul(xt, g0, ml_w_in[0][:, :n_main].astype(BF16), tm=1024, tn=1024,
                     out_dtype=BF16)
    gcol, grow = _ml_gates(xt, g0, ml_w_in[0][:, n_main:], ml_b_i[0], ml_b_f[0], tm=1024)
    y = _mlstm_core(p, gcol, grow, ml_conv[0], ml_norm[0].reshape(1, ML_V), batch=batch, seq=seq)
    xt = _matmul_res(y, ml_w_out[0].astype(BF16), xt, tm=512)
    xt = _moe_layer(xt, norm_ffn[0], moe_w_group[0], moe_b_group[0], moe_w_expert[0],
                    moe_b_expert[0], moe_w_gu[0], moe_w_down[0])

    lambda_init = 0.8 - 0.6 * math.exp(-0.3 * 1)
    p = _norm_matmul(xt, norm_mix[1].reshape(1, d), da_w_in[0].astype(BF16), tm=1024, tn=1024,
                     out_dtype=BF16)
    lam_params = jnp.stack([da_lq1[0], da_lk1[0], da_lq2[0], da_lk2[0]]).astype(F32)
    a = _diff_attn(p, lam_params, da_norm[0].reshape(DA_DV, 1), lambda_init, batch=batch, seq=seq)
    xt = _matmul_res(a, da_w_out[0].astype(BF16), xt, tm=512)
    out = _moe_layer(xt, norm_ffn[1], moe_w_group[1], moe_b_group[1], moe_w_expert[1],
                     moe_b_expert[1], moe_w_gu[1], moe_w_down[1], final_g=final_norm)
    return out.reshape(batch, seq, d)
```

```python
import functools
import math

import jax
import jax.numpy as jnp
from jax import lax
from jax.experimental import pallas as pl
from jax.experimental.pallas import tpu as pltpu
from jax.experimental.pallas import tpu_sc as plsc

F32 = jnp.float32
BF16 = jnp.bfloat16
U32 = jnp.uint32

D_MODEL = 1024
EPS = 1e-6
ML_H = 4
ML_DV = 512
ML_DQK = 256
ML_QK = ML_H * ML_DQK
ML_V = ML_H * ML_DV
CONV_K = 4
ML_CHUNK = 256
CONV_TAIL = 8
DA_H = 8
DA_DH = 64
DA_DV = 128
DA_QK = DA_H * 2 * DA_DH
DA_EPS = 1e-5
DA_TQ = 1024
DA_TK = 1024
DA_CW = 512
DA_KEY_ALIGN = 256
DA_LAZY_LIMIT = 64.0
DA_REBASE = 8.0
LOG2E = 1.4426950408889634
N_GROUPS = 4
EPG = 8
N_EXPERTS = 32
D_EXPERT = 256
MOE_TR = 256
HALF_W = D_MODEL // 4
SC_WINDOW = 128
LANES = 128

VMEM_LIMIT = 48 * 1024 * 1024


def _cparams(sem):
    return pltpu.CompilerParams(dimension_semantics=sem, vmem_limit_bytes=VMEM_LIMIT)


def _norm_matmul_kernel(x_ref, g_ref, w_ref, o_ref, xn_ref):
    @pl.when(pl.program_id(1) == 0)
    def _():
        x = x_ref[...]
        ms = jnp.mean(x * x, axis=-1, keepdims=True)
        xn_ref[...] = (x * lax.rsqrt(ms + EPS) * g_ref[...]).astype(BF16)

    o_ref[...] = jnp.dot(xn_ref[...], w_ref[...], preferred_element_type=F32).astype(o_ref.dtype)


def _norm_matmul(x, g, w, *, tm, tn, out_dtype):
    t, d = x.shape
    n = w.shape[1]
    return pl.pallas_call(
        _norm_matmul_kernel,
        out_shape=jax.ShapeDtypeStruct((t, n), out_dtype),
        grid=(t // tm, n // tn),
        in_specs=[
            pl.BlockSpec((tm, d), lambda i, j: (i, 0)),
            pl.BlockSpec((1, d), lambda i, j: (0, 0)),
            pl.BlockSpec((d, tn), lambda i, j: (0, j)),
        ],
        out_specs=pl.BlockSpec((tm, tn), lambda i, j: (i, j)),
        scratch_shapes=[pltpu.VMEM((tm, d), BF16)],
        compiler_params=_cparams(("parallel", "arbitrary")),
        name="norm_matmul",
    )(x, g, w)


def _matmul_res_kernel(a_ref, w_ref, r_ref, o_ref):
    o_ref[...] = r_ref[...] + jnp.dot(a_ref[...], w_ref[...], preferred_element_type=F32)


def _matmul_res(a, w, res, *, tm):
    t, k = a.shape
    n = w.shape[1]
    return pl.pallas_call(
        _matmul_res_kernel,
        out_shape=jax.ShapeDtypeStruct((t, n), F32),
        grid=(t // tm,),
        in_specs=[
            pl.BlockSpec((tm, k), lambda i: (i, 0)),
            pl.BlockSpec((k, n), lambda i: (0, 0)),
            pl.BlockSpec((tm, n), lambda i: (i, 0)),
        ],
        out_specs=pl.BlockSpec((tm, n), lambda i: (i, 0)),
        compiler_params=_cparams(("parallel",)),
        name="matmul_res",
    )(a, w, res)


def _log_sigmoid(x):
    return jnp.minimum(x, 0.0) - jnp.log1p(jnp.exp(-jnp.abs(x)))


def _ml_gates_kernel(x_ref, g_ref, wc_ref, wr_ref, bc_ref, br_ref, oc_ref, or_ref):
    x = x_ref[...]
    ms = jnp.mean(x * x, axis=-1, keepdims=True)
    xn = (x * lax.rsqrt(ms + EPS) * g_ref[...]).astype(BF16)
    gc = jnp.dot(xn, wc_ref[...], preferred_element_type=F32) + bc_ref[...]
    lane = lax.broadcasted_iota(jnp.int32, gc.shape, 1)
    oc_ref[...] = jnp.where(lane < ML_H, gc, _log_sigmoid(gc))
    gr = lax.dot_general(wr_ref[...], xn, (((1,), (1,)), ((), ())),
                         preferred_element_type=F32) + br_ref[...]
    row = lax.broadcasted_iota(jnp.int32, gr.shape, 0)
    or_ref[...] = jnp.where(row < ML_H, gr, _log_sigmoid(gr))


def _ml_gates(x, g, w_gates, b_i, b_f, *, tm):
    t, d = x.shape
    ng = 2 * ML_H
    wc = jnp.zeros((d, LANES), BF16).at[:, :ng].set(w_gates.astype(BF16))
    wr = w_gates.T.astype(BF16)
    bias = jnp.concatenate([b_i, b_f]).astype(F32)
    bc = jnp.zeros((1, LANES), F32).at[0, :ng].set(bias)
    br = bias.reshape(ng, 1)
    return pl.pallas_call(
        _ml_gates_kernel,
        out_shape=(jax.ShapeDtypeStruct((t, LANES), F32), jax.ShapeDtypeStruct((ng, t), F32)),
        grid=(t // tm,),
        in_specs=[
            pl.BlockSpec((tm, d), lambda i: (i, 0)),
            pl.BlockSpec((1, d), lambda i: (0, 0)),
            pl.BlockSpec((d, LANES), lambda i: (0, 0)),
            pl.BlockSpec((ng, d), lambda i: (0, 0)),
            pl.BlockSpec((1, LANES), lambda i: (0, 0)),
            pl.BlockSpec((ng, 1), lambda i: (0, 0)),
        ],
        out_specs=(pl.BlockSpec((tm, LANES), lambda i: (i, 0)),
                   pl.BlockSpec((ng, tm), lambda i: (0, i))),
        compiler_params=_cparams(("parallel",)),
        name="ml_gates",
    )(x, g, wc, wr, bc, br)


def _split3(x):
    hi = x.astype(BF16)
    r = x - hi.astype(F32)
    mid = r.astype(BF16)
    lo = (r - mid.astype(F32)).astype(BF16)
    return hi, mid, lo


def _mlstm_kernel(q_ref, k_ref, v_ref, o_ref, gc_ref, gr_ref, cw_ref, ng_ref, y_ref,
                  ext_ref, ct_ref, n_ref, m_ref):
    L = ML_CHUNK
    c = pl.program_id(1)

    @pl.when(c == 0)
    def _():
        ext_ref[0:CONV_TAIL, :] = jnp.zeros((CONV_TAIL, 2 * ML_QK), F32)
        ct_ref[...] = jnp.zeros_like(ct_ref)
        n_ref[...] = jnp.zeros_like(n_ref)
        m_ref[...] = jnp.zeros_like(m_ref)

    ext_ref[CONV_TAIL:CONV_TAIL + L, 0:ML_QK] = q_ref[...].astype(F32)
    ext_ref[CONV_TAIL:CONV_TAIL + L, ML_QK:2 * ML_QK] = k_ref[...].astype(F32)
    conv = None
    for j in range(CONV_K):
        sh = ext_ref[pl.ds(CONV_TAIL - (CONV_K - 1) + j, L), :] * cw_ref[j:j + 1, :]
        conv = sh if conv is None else conv + sh
    ext_ref[0:CONV_TAIL, :] = ext_ref[L:L + CONV_TAIL, :]
    qk = conv * jax.nn.sigmoid(conv)
    q_all = (qk[:, :ML_QK] * (ML_DQK ** -0.5)).astype(BF16)
    k_all = qk[:, ML_QK:]

    r_i = lax.broadcasted_iota(jnp.int32, (L, L), 0)
    c_i = lax.broadcasted_iota(jnp.int32, (L, L), 1)
    causal = c_i <= r_i
    tril = causal.astype(BF16)
    triu = (r_i <= c_i).astype(BF16)
    gc = gc_ref[...]
    gr = gr_ref[...]
    bc_all = sum(jnp.dot(tril, p, preferred_element_type=F32) for p in _split3(gc))
    br_all = sum(jnp.dot(p, triu, preferred_element_type=F32) for p in _split3(gr))

    for h in range(ML_H):
        qh = q_all[:, h * ML_DQK:(h + 1) * ML_DQK]
        kh_f = k_all[:, h * ML_DQK:(h + 1) * ML_DQK]
        kh = kh_f.astype(BF16)
        vh = v_ref[:, h * ML_DV:(h + 1) * ML_DV]
        it_col = gc[:, h:h + 1]
        it_row = gr[h:h + 1, :]
        b_col = bc_all[:, ML_H + h:ML_H + h + 1]
        b_row = br_all[ML_H + h:ML_H + h + 1, :]
        m_prev = m_ref[h][:, 0:1]

        dmat = jnp.where(causal, b_col - b_row + it_row, -jnp.inf)
        inter_log = b_col + m_prev
        m_t = jnp.maximum(inter_log, jnp.max(dmat, axis=1, keepdims=True))
        wts = jnp.exp(dmat - m_t)
        s = lax.dot_general(qh, kh, (((1,), (1,)), ((), ())), preferred_element_type=F32)
        sc = s * wts
        inter_scale = jnp.exp(inter_log - m_t)
        ct = ct_ref[h]
        num = (jnp.dot(sc.astype(BF16), vh, preferred_element_type=F32)
               + inter_scale * jnp.dot(qh, ct.astype(BF16), preferred_element_type=F32))
        n_row = n_ref[h]
        den = (jnp.sum(sc, axis=1, keepdims=True)
               + inter_scale * jnp.sum(qh.astype(F32) * n_row, axis=1, keepdims=True))
        h_out = num / jnp.maximum(jnp.abs(den), jnp.exp(-m_t))

        b_last = b_col[L - 1:L, :]
        lw_col = b_last - b_col + it_col
        lw_row = b_last - b_row + it_row
        m_new = jnp.maximum(b_last + m_prev, jnp.max(lw_row, axis=1, keepdims=True))
        ws_col = jnp.exp(lw_col - m_new)
        decay = jnp.exp(b_last + m_prev - m_new)
        kw = kh_f * ws_col
        ct_ref[h] = decay * ct + lax.dot_general(kw.astype(BF16), vh, (((0,), (0,)), ((), ())),
                                                  preferred_element_type=F32)
        n_ref[h] = decay * n_row + jnp.sum(kw, axis=0, keepdims=True)
        m_ref[h] = jnp.broadcast_to(m_new, (1, LANES))

        ms = jnp.mean(h_out * h_out, axis=1, keepdims=True)
        hn = h_out * lax.rsqrt(ms + EPS) * ng_ref[:, h * ML_DV:(h + 1) * ML_DV]
        og = o_ref[:, h * ML_DV:(h + 1) * ML_DV].astype(F32)
        y_ref[:, h * ML_DV:(h + 1) * ML_DV] = (hn * jax.nn.sigmoid(og)).astype(y_ref.dtype)


def _mlstm_core(p, gcol, grow, conv_w, norm_g, *, batch, seq):
    L = ML_CHUNK
    nc = seq // L
    p3 = p.reshape(batch, seq, 2 * ML_QK + 2 * ML_V)
    gc3 = gcol.reshape(batch, seq, LANES)
    gr3 = grow.reshape(2 * ML_H, batch, seq).transpose(1, 0, 2)
    y = pl.pallas_call(
        _mlstm_kernel,
        out_shape=jax.ShapeDtypeStruct((batch, seq, ML_V), BF16),
        grid=(batch, nc),
        in_specs=[
            pl.BlockSpec((None, L, ML_QK), lambda b, c: (b, c, 0)),
            pl.BlockSpec((None, L, ML_QK), lambda b, c: (b, c, 1)),
            pl.BlockSpec((None, L, ML_V), lambda b, c: (b, c, 1)),
            pl.BlockSpec((None, L, ML_V), lambda b, c: (b, c, 2)),
            pl.BlockSpec((None, L, LANES), lambda b, c: (b, c, 0)),
            pl.BlockSpec((None, 2 * ML_H, L), lambda b, c: (b, 0, c)),
            pl.BlockSpec((CONV_K, 2 * ML_QK), lambda b, c: (0, 0)),
            pl.BlockSpec((1, ML_V), lambda b, c: (0, 0)),
        ],
        out_specs=pl.BlockSpec((None, L, ML_V), lambda b, c: (b, c, 0)),
        scratch_shapes=[
            pltpu.VMEM((L + CONV_TAIL, 2 * ML_QK), F32),
            pltpu.VMEM((ML_H, ML_DQK, ML_DV), F32),
            pltpu.VMEM((ML_H, 1, ML_DQK), F32),
            pltpu.VMEM((ML_H, 1, LANES), F32),
        ],
        compiler_params=_cparams(("parallel", "arbitrary")),
        name="mlstm_core",
    )(p3, p3, p3, p3, gc3, gr3, conv_w, norm_g)
    return y.reshape(batch * seq, ML_V)


def _diff_attn_kernel(q_ref, k_ref, vt_ref, lp_ref, ng_ref, o_ref, q2_ref, r_ref, m_ref, l_ref,
                      acc_ref, bm_ref, lt_ref, acct_ref, *, lambda_init):
    tq, tk, cw = DA_TQ, DA_TK, DA_CW
    qi = pl.program_id(2)
    q = q_ref[...].astype(F32) * (DA_DH ** -0.5 * LOG2E)
    lane = lax.broadcasted_iota(jnp.int32, q.shape, 1)
    q2_ref[0:tq, :] = jnp.where(lane < DA_DH, q, 0.0).astype(BF16)
    q2_ref[tq:2 * tq, :] = jnp.where(lane >= DA_DH, q, 0.0).astype(BF16)

    r_ref[...] = jnp.full(r_ref.shape, -jnp.inf, F32)
    m_ref[...] = jnp.full(m_ref.shape, -jnp.inf, F32)
    l_ref[...] = jnp.zeros_like(l_ref)
    acc_ref[...] = jnp.zeros_like(acc_ref)

    nch = 2 * tq // cw

    def keys_needed(c, diag):
        if diag is None:
            return tk
        visible = (c * cw) % tq + cw - diag * tk
        return max(0, min(tk, -(-visible // DA_KEY_ALIGN) * DA_KEY_ALIGN))

    def mask(s, c, diag):
        nk = s.shape[0]
        key = lax.broadcasted_iota(jnp.int32, (nk, cw), 0) + diag * tk
        qry = lax.broadcasted_iota(jnp.int32, (nk, cw), 1) + (c * cw) % tq
        return jnp.where(key <= qry, s, -jnp.inf)

    def scores(j, c, diag=None):
        nk = keys_needed(c, diag)
        if nk == 0:
            return None
        kb = k_ref[pl.ds(pl.multiple_of(j * tk, tk), nk), :]
        s = lax.dot_general(kb, q2_ref[c * cw:(c + 1) * cw, :], (((1,), (1,)), ((), ())),
                            preferred_element_type=F32)
        if diag is not None:
            s = mask(s, c, diag)
        return s, jnp.max(s, axis=0, keepdims=True)


    def exact_block(j, diag=None):
        start = pl.multiple_of(j * tk, tk)
        nxt = scores(j, 0, diag)
        for c in range(nch):
            cs = slice(c * cw, (c + 1) * cw)
            cur = nxt
            nxt = scores(j, c + 1, diag) if c + 1 < nch else None
            if cur is None:
                continue
            s, bmax = cur
            vbt = vt_ref[:, pl.ds(start, s.shape[0])]
            m_new = jnp.maximum(m_ref[:, cs], bmax)
            alpha = jnp.exp2(r_ref[:, cs] - m_new)
            p = jnp.exp2(s - m_new)
            l_ref[:, cs] = alpha * l_ref[:, cs] + jnp.sum(p, axis=0, keepdims=True)
            acc_ref[:, cs] = alpha * acc_ref[:, cs] + jnp.dot(vbt, p.astype(BF16),
                                                               preferred_element_type=F32)
            m_ref[:, cs] = m_new
            r_ref[:, cs] = m_new

    def fast_block(j):
        vbt = vt_ref[:, pl.ds(pl.multiple_of(j * tk, tk), tk)]
        nxt = scores(j, 0)
        for c in range(nch):
            cs = slice(c * cw, (c + 1) * cw)
            s, bmax = nxt
            if c + 1 < nch:
                nxt = scores(j, c + 1)
            p = jnp.exp2(s - r_ref[:, cs])
            bm_ref[:, cs] = bmax
            lt_ref[:, cs] = jnp.sum(p, axis=0, keepdims=True)
            acct_ref[:, cs] = jnp.dot(vbt, p.astype(BF16), preferred_element_type=F32)
        safe = jnp.max(bm_ref[...] - r_ref[...]) <= DA_LAZY_LIMIT

        @pl.when(safe)
        def _():
            r_old = r_ref[...]
            m_new = jnp.maximum(m_ref[...], bm_ref[...])
            r_new = jnp.where(m_new - r_old > DA_REBASE, m_new, r_old)
            scale = jnp.exp2(r_old - r_new)
            l_ref[...] = (l_ref[...] + lt_ref[...]) * scale
            acc_ref[...] = (acc_ref[...] + acct_ref[...]) * scale
            m_ref[...] = m_new
            r_ref[...] = r_new

        @pl.when(jnp.logical_not(safe))
        def _():
            exact_block(j)

    def exact_body(j, carry):
        exact_block(j)
        return carry

    def fast_body(j, carry):
        fast_block(j)
        return carry

    n_full = qi * (tq // tk)
    lax.fori_loop(0, jnp.minimum(n_full, 1), exact_body, 0)
    lax.fori_loop(1, n_full, fast_body, 0)
    for d in range(tq // tk):
        exact_block(n_full + d, diag=d)

    lp = lp_ref[...]
    lam = (jnp.exp(jnp.sum(lp[0:1, :] * lp[1:2, :], axis=1, keepdims=True))
           - jnp.exp(jnp.sum(lp[2:3, :] * lp[3:4, :], axis=1, keepdims=True)) + lambda_init)
    out = acc_ref[...] / l_ref[...]
    o = out[:, :tq] - lam * out[:, tq:]
    ms = jnp.mean(o * o, axis=0, keepdims=True)
    on = o * lax.rsqrt(ms + DA_EPS) * ng_ref[...] * (1.0 - lambda_init)
    o_ref[...] = on.T.astype(o_ref.dtype)


def _diff_attn(p, lam_params, norm_g, lambda_init, *, batch, seq):
    tq = DA_TQ
    p3 = p.reshape(batch, seq, 2 * DA_QK + DA_H * DA_DV)
    vt = jnp.swapaxes(p3[:, :, 2 * DA_QK:], 1, 2)
    kern = functools.partial(_diff_attn_kernel, lambda_init=lambda_init)
    o = pl.pallas_call(
        kern,
        out_shape=jax.ShapeDtypeStruct((batch, seq, DA_H * DA_DV), BF16),
        grid=(batch, DA_H, seq // tq),
        in_specs=[
            pl.BlockSpec((None, tq, 2 * DA_DH), lambda b, h, i: (b, i, h)),
            pl.BlockSpec((None, seq, 2 * DA_DH), lambda b, h, i: (b, 0, DA_H + h)),
            pl.BlockSpec((None, DA_DV, seq), lambda b, h, i: (b, h, 0)),
            pl.BlockSpec((4, DA_DH), lambda b, h, i: (0, 0)),
            pl.BlockSpec((DA_DV, 1), lambda b, h, i: (0, 0)),
        ],
        out_specs=pl.BlockSpec((None, tq, DA_DV), lambda b, h, i: (b, i, h)),
        scratch_shapes=[
            pltpu.VMEM((2 * tq, 2 * DA_DH), BF16),
            pltpu.VMEM((1, 2 * tq), F32),
            pltpu.VMEM((1, 2 * tq), F32),
            pltpu.VMEM((1, 2 * tq), F32),
            pltpu.VMEM((DA_DV, 2 * tq), F32),
            pltpu.VMEM((1, 2 * tq), F32),
            pltpu.VMEM((1, 2 * tq), F32),
            pltpu.VMEM((DA_DV, 2 * tq), F32),
        ],
        compiler_params=_cparams(("parallel", "parallel", "arbitrary")),
        name="diff_attn",
    )(p3, p3, vt, lam_params, norm_g)
    return o.reshape(batch * seq, DA_H * DA_DV)


def _pack_halves(y):
    halves = []
    for h in range(2):
        base = h * 2 * HALF_W
        lo = y[:, base:base + HALF_W].astype(BF16).astype(F32)
        hi = y[:, base + HALF_W:base + 2 * HALF_W].astype(BF16).astype(F32)
        lo_bits = lax.bitcast_convert_type(lo, U32) >> 16
        hi_bits = lax.bitcast_convert_type(hi, U32)
        halves.append(hi_bits | lo_bits)
    return halves


def _unpack_halves(w0, w1):
    parts = []
    for w in (w0, w1):
        parts.append(lax.bitcast_convert_type(w << 16, F32))
        parts.append(lax.bitcast_convert_type(w & jnp.uint32(0xFFFF0000), F32))
    return jnp.concatenate(parts, axis=1)


def _router_kernel(x_ref, g_ref, w_ref, b_ref, hp_ref, meta_ref, metat_ref, cnt_ref,
                   run_ref, ls_ref):
    tm = x_ref.shape[0]

    @pl.when(pl.program_id(0) == 0)
    def _():
        run_ref[...] = jnp.zeros_like(run_ref)
        r_i = lax.broadcasted_iota(jnp.int32, (tm, tm), 0)
        c_i = lax.broadcasted_iota(jnp.int32, (tm, tm), 1)
        ls_ref[...] = (c_i < r_i).astype(BF16)

    x = x_ref[...]
    ms = jnp.mean(x * x, axis=-1, keepdims=True)
    hn32 = x * lax.rsqrt(ms + EPS) * g_ref[...]
    hn = hn32.astype(BF16)
    halves = _pack_halves(hn32)
    hp_ref[0] = halves[0]
    hp_ref[1] = halves[1]
    lg = jnp.dot(hn, w_ref[...], preferred_element_type=F32) + b_ref[...]
    lane = lax.broadcasted_iota(jnp.int32, lg.shape, 1)
    neg = -jnp.inf

    gmask = (lane >= N_EXPERTS) & (lane < N_EXPERTS + N_GROUPS)
    gl = jnp.where(gmask, lg, neg)
    gmax = jnp.max(gl, axis=1, keepdims=True)
    gidx = jnp.min(jnp.where(gl == gmax, lane, LANES), axis=1, keepdims=True) - N_EXPERTS
    gsum = jnp.sum(jnp.where(gmask, jnp.exp(gl - gmax), 0.0), axis=1, keepdims=True)
    g_w = 1.0 / gsum

    emask = (lane >= gidx * EPG) & (lane < gidx * EPG + EPG)
    el = jnp.where(emask, lg, neg)
    emax = jnp.max(el, axis=1, keepdims=True)
    eexp = jnp.where(emask, jnp.exp(el - emax), 0.0)
    ep = eexp / jnp.sum(eexp, axis=1, keepdims=True)
    ep = jnp.where(emask, ep, -1.0)
    p1 = jnp.max(ep, axis=1, keepdims=True)
    i1 = jnp.min(jnp.where(ep == p1, lane, LANES), axis=1, keepdims=True)
    ep2 = jnp.where(lane == i1, -1.0, ep)
    p2 = jnp.max(ep2, axis=1, keepdims=True)
    i2 = jnp.min(jnp.where(ep2 == p2, lane, LANES), axis=1, keepdims=True)
    wsum = p1 + p2
    w1 = g_w * (p1 / wsum)
    w2 = g_w * (p2 / wsum)

    a1 = lane == i1
    a2 = lane == i2
    onehot = (a1 | a2).astype(BF16)
    before = jnp.dot(ls_ref[...], onehot, preferred_element_type=F32) + run_ref[...]
    rank1 = jnp.sum(jnp.where(a1, before, 0.0), axis=1, keepdims=True)
    rank2 = jnp.sum(jnp.where(a2, before, 0.0), axis=1, keepdims=True)
    run_ref[...] += jnp.sum(onehot.astype(F32), axis=0, keepdims=True)
    cnt_ref[...] = run_ref[...]

    meta = (jnp.where(lane == 0, i1.astype(F32), 0.0) + jnp.where(lane == 1, i2.astype(F32), 0.0)
            + jnp.where(lane == 2, rank1, 0.0) + jnp.where(lane == 3, rank2, 0.0)
            + jnp.where(lane == 4, w1, 0.0) + jnp.where(lane == 5, w2, 0.0))
    meta_ref[...] = meta
    metat_ref[...] = meta.T[0:8, :]


def _router(x, g, w_group, b_group, w_expert, b_expert, *, tm):
    t, d = x.shape
    w = jnp.zeros((d, LANES), BF16)
    w = w.at[:, :N_EXPERTS].set(w_expert.astype(BF16))
    w = w.at[:, N_EXPERTS:N_EXPERTS + N_GROUPS].set(w_group.astype(BF16))
    b = jnp.zeros((1, LANES), F32)
    b = b.at[0, :N_EXPERTS].set(b_expert.astype(F32))
    b = b.at[0, N_EXPERTS:N_EXPERTS + N_GROUPS].set(b_group.astype(F32))
    return pl.pallas_call(
        _router_kernel,
        out_shape=(jax.ShapeDtypeStruct((2, t, HALF_W), U32),
                   jax.ShapeDtypeStruct((t, LANES), F32),
                   jax.ShapeDtypeStruct((8, t), F32),
                   jax.ShapeDtypeStruct((1, LANES), F32)),
        grid=(t // tm,),
        in_specs=[
            pl.BlockSpec((tm, d), lambda i: (i, 0)),
            pl.BlockSpec((1, d), lambda i: (0, 0)),
            pl.BlockSpec((d, LANES), lambda i: (0, 0)),
            pl.BlockSpec((1, LANES), lambda i: (0, 0)),
        ],
        out_specs=(pl.BlockSpec((2, tm, HALF_W), lambda i: (0, i, 0)),
                   pl.BlockSpec((tm, LANES), lambda i: (i, 0)),
                   pl.BlockSpec((8, tm), lambda i: (0, i)),
                   pl.BlockSpec((1, LANES), lambda i: (0, 0))),
        scratch_shapes=[pltpu.VMEM((1, LANES), F32), pltpu.VMEM((tm, tm), BF16)],
        compiler_params=_cparams(("arbitrary",)),
        name="moe_router",
    )(x, g, w, b)


def _sc_mesh():
    return plsc.VectorSubcoreMesh(core_axis_name="c", subcore_axis_name="s")


def _sc_scatter2(x, i0, i1, n_out):
    n, d = x.shape

    @pl.kernel(out_type=jax.ShapeDtypeStruct((n_out, d), x.dtype), mesh=_sc_mesh())
    def k(x_hbm, i0_hbm, i1_hbm, o_hbm):
        def body(x_vmem, i0_vmem, i1_vmem):
            pltpu.sync_copy(x_vmem, o_hbm.at[i0_vmem.at[0]])
            pltpu.sync_copy(x_vmem, o_hbm.at[i1_vmem.at[0]])

        pltpu.emit_pipeline(
            body,
            grid=(n // SC_WINDOW,),
            in_specs=[pl.BlockSpec((SC_WINDOW, d), lambda i: (i, 0)),
                      pl.BlockSpec((1, SC_WINDOW), lambda i: (0, i)),
                      pl.BlockSpec((1, SC_WINDOW), lambda i: (0, i))],
            out_specs=[],
            core_axis_name=("c", "s"),
            dimension_semantics=(pltpu.PARALLEL,),
        )(x_hbm, i0_hbm, i1_hbm)

    return k(x, i0.reshape(1, n), i1.reshape(1, n))


def _sc_gather(x, idx):
    n = idx.shape[0]
    d = x.shape[1]

    @pl.kernel(out_type=jax.ShapeDtypeStruct((n, d), x.dtype), mesh=_sc_mesh())
    def k(x_hbm, i_hbm, o_hbm):
        def body(i_vmem, o_vmem):
            pltpu.sync_copy(x_hbm.at[i_vmem.at[0]], o_vmem)

        pltpu.emit_pipeline(
            body,
            grid=(n // SC_WINDOW,),
            in_specs=[pl.BlockSpec((1, SC_WINDOW), lambda i: (0, i))],
            out_specs=[pl.BlockSpec((SC_WINDOW, d), lambda i: (i, 0))],
            core_axis_name=("c", "s"),
            dimension_semantics=(pltpu.PARALLEL,),
        )(i_hbm, o_hbm)

    return k(x, idx.reshape(1, n))


def _experts_kernel(te_ref, nu_ref, xs_ref, wgu_ref, wd_ref, ys_ref):
    del te_ref

    @pl.when(pl.program_id(0) < nu_ref[0])
    def _():
        x = _unpack_halves(xs_ref[0], xs_ref[1]).astype(BF16)
        gu = jnp.dot(x, wgu_ref[...].astype(BF16), preferred_element_type=F32)
        gpart = gu[:, :D_EXPERT]
        a = gpart * jax.nn.sigmoid(gpart) * gu[:, D_EXPERT:]
        y = jnp.dot(a.astype(BF16), wd_ref[...].astype(BF16), preferred_element_type=F32)
        halves = _pack_halves(y)
        ys_ref[0] = halves[0]
        ys_ref[1] = halves[1]


def _experts(xs, tile_expert, n_used, w_gu, w_down, layer):
    _, rows, _ = xs.shape
    d = w_gu.shape[2]
    return pl.pallas_call(
        _experts_kernel,
        out_shape=jax.ShapeDtypeStruct(xs.shape, U32),
        grid_spec=pltpu.PrefetchScalarGridSpec(
            num_scalar_prefetch=2,
            grid=(rows // MOE_TR,),
            in_specs=[
                pl.BlockSpec((2, MOE_TR, HALF_W), lambda j, te, nu: (0, j, 0)),
                pl.BlockSpec((None, None, d, 2 * D_EXPERT),
                             lambda j, te, nu: (layer, te[j], 0, 0)),
                pl.BlockSpec((None, None, D_EXPERT, d),
                             lambda j, te, nu: (layer, te[j], 0, 0)),
            ],
            out_specs=pl.BlockSpec((2, MOE_TR, HALF_W), lambda j, te, nu: (0, j, 0)),
        ),
        compiler_params=_cparams(("arbitrary",)),
        name="moe_experts",
    )(tile_expert, n_used, xs, w_gu, w_down)


def _combine_kernel(x_ref, z_ref, meta_ref, g_ref, o_ref, *, final_norm):
    meta = meta_ref[...]
    y_a = _unpack_halves(z_ref[0], z_ref[2])
    y_b = _unpack_halves(z_ref[1], z_ref[3])
    out = x_ref[...] + meta[:, 4:5] * y_a + meta[:, 5:6] * y_b
    if final_norm:
        ms = jnp.mean(out * out, axis=-1, keepdims=True)
        out = out * lax.rsqrt(ms + EPS) * g_ref[...]
    o_ref[...] = out


def _combine(x, z, meta, g, *, tm, final_norm):
    t, d = x.shape
    return pl.pallas_call(
        functools.partial(_combine_kernel, final_norm=final_norm),
        out_shape=jax.ShapeDtypeStruct((t, d), F32),
        grid=(t // tm,),
        in_specs=[
            pl.BlockSpec((tm, d), lambda i: (i, 0)),
            pl.BlockSpec((4, tm, HALF_W), lambda i: (0, i, 0)),
            pl.BlockSpec((tm, LANES), lambda i: (i, 0)),
            pl.BlockSpec((1, d), lambda i: (0, 0)),
        ],
        out_specs=pl.BlockSpec((tm, d), lambda i: (i, 0)),
        compiler_params=_cparams(("parallel",)),
        name="moe_combine",
    )(x, z, meta, g)


def _moe_layer(x, norm_g, w_group, b_group, w_expert, b_expert, w_gu, w_down, layer,
               final_g=None):
    t, d = x.shape
    hp, meta, metat, cnt = _router(x, norm_g.reshape(1, d), w_group, b_group, w_expert, b_expert,
                                   tm=1024)
    n_tiles = 2 * t // MOE_TR + N_EXPERTS
    rows = n_tiles * MOE_TR
    counts = cnt[0, :N_EXPERTS].astype(jnp.int32)
    tiles_e = (counts + MOE_TR - 1) // MOE_TR
    tiles_end = jnp.cumsum(tiles_e)
    row_off = (tiles_end - tiles_e) * MOE_TR
    experts = jnp.arange(N_EXPERTS, dtype=jnp.int32)[:, None]

    def region_start(e_row):
        return jnp.sum(jnp.where(e_row[None, :] == experts, row_off[:, None], 0), axis=0)

    e_a, e_b = metat[0].astype(jnp.int32), metat[1].astype(jnp.int32)
    pos_a = region_start(e_a) + metat[2].astype(jnp.int32)
    pos_b = region_start(e_b) + metat[3].astype(jnp.int32)
    tile_ids = jnp.arange(n_tiles, dtype=jnp.int32)
    tile_expert = jnp.minimum(
        jnp.sum((tile_ids[:, None] >= tiles_end[None, :]).astype(jnp.int32), axis=1),
        N_EXPERTS - 1)
    n_used = tiles_end[-1:].astype(jnp.int32)

    xs = _sc_scatter2(hp.reshape(2 * t, HALF_W),
                      jnp.concatenate([pos_a, pos_a + rows]),
                      jnp.concatenate([pos_b, pos_b + rows]), 2 * rows)
    ys = _experts(xs.reshape(2, rows, HALF_W), tile_expert, n_used,
                  w_gu, w_down, layer)
    z = _sc_gather(ys.reshape(2 * rows, HALF_W),
                   jnp.concatenate([pos_a, pos_b, pos_a + rows, pos_b + rows]))
    g = jnp.ones((1, d), F32) if final_g is None else final_g.reshape(1, d)
    return _combine(x, z.reshape(4, t, HALF_W), meta, g, tm=1024, final_norm=final_g is not None)


def kernel(x, norm_mix, norm_ffn, ml_w_in, ml_conv, ml_b_i, ml_b_f, ml_norm, ml_w_out, da_w_in, da_lq1, da_lk1, da_lq2, da_lk2, da_norm, da_w_out, moe_w_group, moe_b_group, moe_w_expert, moe_b_expert, moe_w_gu, moe_w_down, final_norm):
    batch, seq, d = x.shape
    xt = x.reshape(batch * seq, d)

    g0 = norm_mix[0].reshape(1, d)
    n_main = 2 * ML_QK + 2 * ML_V
    p = _norm_matmul(xt, g0, ml_w_in[0][:, :n_main].astype(BF16), tm=1024, tn=1024,
                     out_dtype=BF16)
    gcol, grow = _ml_gates(xt, g0, ml_w_in[0][:, n_main:], ml_b_i[0], ml_b_f[0], tm=1024)
    y = _mlstm_core(p, gcol, grow, ml_conv[0], ml_norm[0].reshape(1, ML_V), batch=batch, seq=seq)
    xt = _matmul_res(y, ml_w_out[0].astype(BF16), xt, tm=512)
    xt = _moe_layer(xt, norm_ffn[0], moe_w_group[0], moe_b_group[0], moe_w_expert[0],
                    moe_b_expert[0], moe_w_gu, moe_w_down, 0)

    lambda_init = 0.8 - 0.6 * math.exp(-0.3 * 1)
    p = _norm_matmul(xt, norm_mix[1].reshape(1, d), da_w_in[0].astype(BF16), tm=1024, tn=1024,
                     out_dtype=BF16)
    lam_params = jnp.stack([da_lq1[0], da_lk1[0], da_lq2[0], da_lk2[0]]).astype(F32)
    a = _diff_attn(p, lam_params, da_norm[0].reshape(DA_DV, 1), lambda_init, batch=batch, seq=seq)
    xt = _matmul_res(a, da_w_out[0].astype(BF16), xt, tm=512)
    out = _moe_layer(xt, norm_ffn[1], moe_w_group[1], moe_b_group[1], moe_w_expert[1],
                     moe_b_expert[1], moe_w_gu, moe_w_down, 1, final_g=final_norm)
    return out.reshape(batch, seq, d)
```

```python
import functools
import math

import jax
import jax.numpy as jnp
from jax import lax
from jax.experimental import pallas as pl
from jax.experimental.pallas import tpu as pltpu
from jax.experimental.pallas import tpu_sc as plsc

F32 = jnp.float32
BF16 = jnp.bfloat16
U32 = jnp.uint32

D_MODEL = 1024
EPS = 1e-6
ML_H = 4
ML_DV = 512
ML_DQK = 256
ML_QK = ML_H * ML_DQK
ML_V = ML_H * ML_DV
CONV_K = 4
ML_CHUNK = 256
CONV_TAIL = 8
DA_H = 8
DA_DH = 64
DA_DV = 128
DA_QK = DA_H * 2 * DA_DH
DA_EPS = 1e-5
DA_TQ = 1024
DA_TK = 1024
DA_CW = 512
DA_KEY_ALIGN = 256
DA_LAZY_LIMIT = 64.0
DA_REBASE = 8.0
LOG2E = 1.4426950408889634
N_GROUPS = 4
EPG = 8
N_EXPERTS = 32
D_EXPERT = 256
MOE_TR = 256
HALF_W = D_MODEL // 4
SC_WINDOW = 128
LANES = 128

VMEM_LIMIT = 48 * 1024 * 1024


def _cparams(sem):
    return pltpu.CompilerParams(dimension_semantics=sem, vmem_limit_bytes=VMEM_LIMIT)


def _norm_matmul_kernel(x_ref, g_ref, w_ref, o_ref, xn_ref):
    @pl.when(pl.program_id(1) == 0)
    def _():
        x = x_ref[...]
        ms = jnp.mean(x * x, axis=-1, keepdims=True)
        xn_ref[...] = (x * lax.rsqrt(ms + EPS) * g_ref[...]).astype(BF16)

    o_ref[...] = jnp.dot(xn_ref[...], w_ref[...], preferred_element_type=F32).astype(o_ref.dtype)


def _norm_matmul(x, g, w, *, tm, tn, out_dtype):
    t, d = x.shape
    n = w.shape[1] // tn * tn
    return pl.pallas_call(
        _norm_matmul_kernel,
        out_shape=jax.ShapeDtypeStruct((t, n), out_dtype),
        grid=(t // tm, n // tn),
        in_specs=[
            pl.BlockSpec((tm, d), lambda i, j: (i, 0)),
            pl.BlockSpec((1, d), lambda i, j: (0, 0)),
            pl.BlockSpec((d, tn), lambda i, j: (0, j)),
        ],
        out_specs=pl.BlockSpec((tm, tn), lambda i, j: (i, j)),
        scratch_shapes=[pltpu.VMEM((tm, d), BF16)],
        compiler_params=_cparams(("parallel", "arbitrary")),
        name="norm_matmul",
    )(x, g, w)


def _matmul_res_kernel(a_ref, w_ref, r_ref, o_ref):
    o_ref[...] = r_ref[...] + jnp.dot(a_ref[...], w_ref[...], preferred_element_type=F32)


def _matmul_res(a, w, res, *, tm):
    t, k = a.shape
    n = w.shape[1]
    return pl.pallas_call(
        _matmul_res_kernel,
        out_shape=jax.ShapeDtypeStruct((t, n), F32),
        grid=(t // tm,),
        in_specs=[
            pl.BlockSpec((tm, k), lambda i: (i, 0)),
            pl.BlockSpec((k, n), lambda i: (0, 0)),
            pl.BlockSpec((tm, n), lambda i: (i, 0)),
        ],
        out_specs=pl.BlockSpec((tm, n), lambda i: (i, 0)),
        compiler_params=_cparams(("parallel",)),
        name="matmul_res",
    )(a, w, res)


def _log_sigmoid(x):
    return jnp.minimum(x, 0.0) - jnp.log1p(jnp.exp(-jnp.abs(x)))


def _ml_gates_kernel(x_ref, g_ref, wc_ref, wr_ref, bc_ref, br_ref, oc_ref, or_ref):
    x = x_ref[...]
    ms = jnp.mean(x * x, axis=-1, keepdims=True)
    xn = (x * lax.rsqrt(ms + EPS) * g_ref[...]).astype(BF16)
    gc = jnp.dot(xn, wc_ref[...], preferred_element_type=F32) + bc_ref[...]
    lane = lax.broadcasted_iota(jnp.int32, gc.shape, 1)
    oc_ref[...] = jnp.where(lane < ML_H, gc, _log_sigmoid(gc))
    gr = lax.dot_general(wr_ref[...], xn, (((1,), (1,)), ((), ())),
                         preferred_element_type=F32) + br_ref[...]
    row = lax.broadcasted_iota(jnp.int32, gr.shape, 0)
    or_ref[...] = jnp.where(row < ML_H, gr, _log_sigmoid(gr))


def _ml_gates(x, g, w_gates, b_i, b_f, *, tm):
    t, d = x.shape
    ng = 2 * ML_H
    wc = jnp.zeros((d, LANES), BF16).at[:, :ng].set(w_gates.astype(BF16))
    wr = w_gates.T.astype(BF16)
    bias = jnp.concatenate([b_i, b_f]).astype(F32)
    bc = jnp.zeros((1, LANES), F32).at[0, :ng].set(bias)
    br = bias.reshape(ng, 1)
    return pl.pallas_call(
        _ml_gates_kernel,
        out_shape=(jax.ShapeDtypeStruct((t, LANES), F32), jax.ShapeDtypeStruct((ng, t), F32)),
        grid=(t // tm,),
        in_specs=[
            pl.BlockSpec((tm, d), lambda i: (i, 0)),
            pl.BlockSpec((1, d), lambda i: (0, 0)),
            pl.BlockSpec((d, LANES), lambda i: (0, 0)),
            pl.BlockSpec((ng, d), lambda i: (0, 0)),
            pl.BlockSpec((1, LANES), lambda i: (0, 0)),
            pl.BlockSpec((ng, 1), lambda i: (0, 0)),
        ],
        out_specs=(pl.BlockSpec((tm, LANES), lambda i: (i, 0)),
                   pl.BlockSpec((ng, tm), lambda i: (0, i))),
        compiler_params=_cparams(("parallel",)),
        name="ml_gates",
    )(x, g, wc, wr, bc, br)


def _split3(x):
    hi = x.astype(BF16)
    r = x - hi.astype(F32)
    mid = r.astype(BF16)
    lo = (r - mid.astype(F32)).astype(BF16)
    return hi, mid, lo


def _mlstm_kernel(q_ref, k_ref, v_ref, o_ref, gc_ref, gr_ref, cw_ref, ng_ref, y_ref,
                  tail_ref, ct_ref, n_ref, m_ref, shift_ref, tril_ref, triu_ref, neg_ref):
    L = ML_CHUNK
    c = pl.program_id(1)

    @pl.when(c == 0)
    def _():
        tail_ref[...] = jnp.zeros_like(tail_ref)
        ct_ref[...] = jnp.zeros_like(ct_ref)
        n_ref[...] = jnp.zeros_like(n_ref)
        m_ref[...] = jnp.zeros_like(m_ref)
        r_i = lax.broadcasted_iota(jnp.int32, (L, L), 0)
        c_i = lax.broadcasted_iota(jnp.int32, (L, L), 1)
        tril_ref[...] = (c_i <= r_i).astype(BF16)
        triu_ref[...] = (r_i <= c_i).astype(BF16)
        neg_ref[...] = jnp.where(c_i <= r_i, 0.0, -jnp.inf)
        for j in range(CONV_K - 1):
            shift_ref[j * L:(j + 1) * L, :] = (c_i == r_i - (CONV_K - 1 - j)).astype(BF16)

    u = jnp.concatenate([q_ref[...], k_ref[...]], axis=1)
    uf = u.astype(F32)
    shifted = jnp.dot(shift_ref[...], u, preferred_element_type=F32)
    conv = uf * cw_ref[CONV_K - 1:CONV_K, :]
    head = None
    for j in range(CONV_K - 1):
        conv = conv + shifted[j * L:(j + 1) * L, :] * cw_ref[j:j + 1, :]
        part = tail_ref[pl.ds(CONV_TAIL - (CONV_K - 1) + j, CONV_TAIL), :] * cw_ref[j:j + 1, :]
        head = part if head is None else head + part
    conv = jnp.concatenate([conv[:CONV_TAIL] + head, conv[CONV_TAIL:]], axis=0)
    tail_ref[0:CONV_TAIL, :] = uf[L - CONV_TAIL:, :]
    qk = conv * jax.nn.sigmoid(conv)
    q_all = (qk[:, :ML_QK] * (ML_DQK ** -0.5)).astype(BF16)
    k_all = qk[:, ML_QK:]

    tril = tril_ref[...]
    triu = triu_ref[...]
    gc = gc_ref[...]
    gr = gr_ref[...]
    bc_all = sum(jnp.dot(tril, p, preferred_element_type=F32) for p in _split3(gc))
    br_all = sum(jnp.dot(p, triu, preferred_element_type=F32) for p in _split3(gr))

    for h in range(ML_H):
        qh = q_all[:, h * ML_DQK:(h + 1) * ML_DQK]
        kh_f = k_all[:, h * ML_DQK:(h + 1) * ML_DQK]
        kh = kh_f.astype(BF16)
        vh = v_ref[:, h * ML_DV:(h + 1) * ML_DV]
        it_col = gc[:, h:h + 1]
        it_row = gr[h:h + 1, :]
        b_col = bc_all[:, ML_H + h:ML_H + h + 1]
        b_row = br_all[ML_H + h:ML_H + h + 1, :]
        m_prev = m_ref[h][:, 0:1]

        dmat = b_col + (it_row - b_row) + neg_ref[...]
        inter_log = b_col + m_prev
        m_t = jnp.maximum(inter_log, jnp.max(dmat, axis=1, keepdims=True))
        wts = jnp.exp(dmat - m_t)
        s = lax.dot_general(qh, kh, (((1,), (1,)), ((), ())), preferred_element_type=F32)
        sc = s * wts
        inter_scale = jnp.exp(inter_log - m_t)
        ct = ct_ref[h]
        num = (jnp.dot(sc.astype(BF16), vh, preferred_element_type=F32)
               + inter_scale * jnp.dot(qh, ct.astype(BF16), preferred_element_type=F32))
        n_row = n_ref[h]
        den = (jnp.sum(sc, axis=1, keepdims=True)
               + inter_scale * jnp.sum(qh.astype(F32) * n_row, axis=1, keepdims=True))
        h_out = num / jnp.maximum(jnp.abs(den), jnp.exp(-m_t))

        b_last = b_col[L - 1:L, :]
        lw_col = b_last - b_col + it_col
        lw_row = b_last - b_row + it_row
        m_new = jnp.maximum(b_last + m_prev, jnp.max(lw_row, axis=1, keepdims=True))
        ws_col = jnp.exp(lw_col - m_new)
        decay = jnp.exp(b_last + m_prev - m_new)
        kw = kh_f * ws_col
        ct_ref[h] = decay * ct + lax.dot_general(kw.astype(BF16), vh, (((0,), (0,)), ((), ())),
                                                  preferred_element_type=F32)
        n_ref[h] = decay * n_row + jnp.sum(kw, axis=0, keepdims=True)
        m_ref[h] = jnp.broadcast_to(m_new, (1, LANES))

        ms = jnp.mean(h_out * h_out, axis=1, keepdims=True)
        hn = h_out * lax.rsqrt(ms + EPS) * ng_ref[:, h * ML_DV:(h + 1) * ML_DV]
        og = o_ref[:, h * ML_DV:(h + 1) * ML_DV].astype(F32)
        y_ref[:, h * ML_DV:(h + 1) * ML_DV] = (hn * jax.nn.sigmoid(og)).astype(y_ref.dtype)


def _mlstm_core(p, gcol, grow, conv_w, norm_g, *, batch, seq):
    L = ML_CHUNK
    nc = seq // L
    p3 = p.reshape(batch, seq, 2 * ML_QK + 2 * ML_V)
    gc3 = gcol.reshape(batch, seq, LANES)
    gr3 = grow.reshape(2 * ML_H, batch, seq).transpose(1, 0, 2)
    y = pl.pallas_call(
        _mlstm_kernel,
        out_shape=jax.ShapeDtypeStruct((batch, seq, ML_V), BF16),
        grid=(batch, nc),
        in_specs=[
            pl.BlockSpec((None, L, ML_QK), lambda b, c: (b, c, 0)),
            pl.BlockSpec((None, L, ML_QK), lambda b, c: (b, c, 1)),
            pl.BlockSpec((None, L, ML_V), lambda b, c: (b, c, 1)),
            pl.BlockSpec((None, L, ML_V), lambda b, c: (b, c, 2)),
            pl.BlockSpec((None, L, LANES), lambda b, c: (b, c, 0)),
            pl.BlockSpec((None, 2 * ML_H, L), lambda b, c: (b, 0, c)),
            pl.BlockSpec((CONV_K, 2 * ML_QK), lambda b, c: (0, 0)),
            pl.BlockSpec((1, ML_V), lambda b, c: (0, 0)),
        ],
        out_specs=pl.BlockSpec((None, L, ML_V), lambda b, c: (b, c, 0)),
        scratch_shapes=[
            pltpu.VMEM((2 * CONV_TAIL, 2 * ML_QK), F32),
            pltpu.VMEM((ML_H, ML_DQK, ML_DV), F32),
            pltpu.VMEM((ML_H, 1, ML_DQK), F32),
            pltpu.VMEM((ML_H, 1, LANES), F32),
            pltpu.VMEM(((CONV_K - 1) * L, L), BF16),
            pltpu.VMEM((L, L), BF16),
            pltpu.VMEM((L, L), BF16),
            pltpu.VMEM((L, L), F32),
        ],
        compiler_params=_cparams(("parallel", "arbitrary")),
        name="mlstm_core",
    )(p3, p3, p3, p3, gc3, gr3, conv_w, norm_g)
    return y.reshape(batch * seq, ML_V)


def _diff_attn_kernel(q_ref, k_ref, vt_ref, lp_ref, ng_ref, o_ref, q2_ref, r_ref, m_ref, l_ref,
                      acc_ref, bm_ref, lt_ref, acct_ref, *, lambda_init):
    tq, tk, cw = DA_TQ, DA_TK, DA_CW
    qi = pl.program_id(2)
    q = q_ref[...].astype(F32) * (DA_DH ** -0.5 * LOG2E)
    lane = lax.broadcasted_iota(jnp.int32, q.shape, 1)
    q2_ref[0:tq, :] = jnp.where(lane < DA_DH, q, 0.0).astype(BF16)
    q2_ref[tq:2 * tq, :] = jnp.where(lane >= DA_DH, q, 0.0).astype(BF16)

    r_ref[...] = jnp.full(r_ref.shape, -jnp.inf, F32)
    m_ref[...] = jnp.full(m_ref.shape, -jnp.inf, F32)
    l_ref[...] = jnp.zeros_like(l_ref)
    acc_ref[...] = jnp.zeros_like(acc_ref)

    nch = 2 * tq // cw

    def keys_needed(c, diag):
        if diag is None:
            return tk
        visible = (c * cw) % tq + cw - diag * tk
        return max(0, min(tk, -(-visible // DA_KEY_ALIGN) * DA_KEY_ALIGN))

    def mask(s, c, diag):
        nk = s.shape[0]
        key = lax.broadcasted_iota(jnp.int32, (nk, cw), 0) + diag * tk
        qry = lax.broadcasted_iota(jnp.int32, (nk, cw), 1) + (c * cw) % tq
        return jnp.where(key <= qry, s, -jnp.inf)

    def scores(j, c, diag=None):
        nk = keys_needed(c, diag)
        if nk == 0:
            return None
        kb = k_ref[pl.ds(pl.multiple_of(j * tk, tk), nk), :]
        s = lax.dot_general(kb, q2_ref[c * cw:(c + 1) * cw, :], (((1,), (1,)), ((), ())),
                            preferred_element_type=F32)
        if diag is not None:
            s = mask(s, c, diag)
        return s, jnp.max(s, axis=0, keepdims=True)


    def exact_block(j, diag=None):
        start = pl.multiple_of(j * tk, tk)
        nxt = scores(j, 0, diag)
        for c in range(nch):
            cs = slice(c * cw, (c + 1) * cw)
            s, bmax = nxt
            if c + 1 < nch:
                nxt = scores(j, c + 1, diag)
            vbt = vt_ref[:, pl.ds(start, s.shape[0])]
            m_new = jnp.maximum(m_ref[:, cs], bmax)
            alpha = jnp.exp2(r_ref[:, cs] - m_new)
            p = jnp.exp2(s - m_new)
            l_ref[:, cs] = alpha * l_ref[:, cs] + jnp.sum(p, axis=0, keepdims=True)
            acc_ref[:, cs] = alpha * acc_ref[:, cs] + jnp.dot(vbt, p.astype(BF16),
                                                               preferred_element_type=F32)
            m_ref[:, cs] = m_new
            r_ref[:, cs] = m_new

    def fast_block(j, diag=None):
        start = pl.multiple_of(j * tk, tk)
        nxt = scores(j, 0, diag)
        for c in range(nch):
            cs = slice(c * cw, (c + 1) * cw)
            s, bmax = nxt
            if c + 1 < nch:
                nxt = scores(j, c + 1, diag)
            vbt = vt_ref[:, pl.ds(start, s.shape[0])]
            p = jnp.exp2(s - r_ref[:, cs])
            bm_ref[:, cs] = bmax
            lt_ref[:, cs] = jnp.sum(p, axis=0, keepdims=True)
            acct_ref[:, cs] = jnp.dot(vbt, p.astype(BF16), preferred_element_type=F32)
        safe = jnp.max(bm_ref[...] - r_ref[...]) <= DA_LAZY_LIMIT

        @pl.when(safe)
        def _():
            r_old = r_ref[...]
            m_new = jnp.maximum(m_ref[...], bm_ref[...])
            r_new = jnp.where(m_new - r_old > DA_REBASE, m_new, r_old)
            scale = jnp.exp2(r_old - r_new)
            l_ref[...] = (l_ref[...] + lt_ref[...]) * scale
            acc_ref[...] = (acc_ref[...] + acct_ref[...]) * scale
            m_ref[...] = m_new
            r_ref[...] = r_new

        @pl.when(jnp.logical_not(safe))
        def _():
            exact_block(j, diag)

    def exact_body(j, carry):
        exact_block(j)
        return carry

    def fast_body(j, carry):
        fast_block(j)
        return carry

    assert tq == tk
    lax.fori_loop(0, jnp.minimum(qi, 1), exact_body, 0)
    lax.fori_loop(1, qi, fast_body, 0)

    @pl.when(qi > 0)
    def _():
        fast_block(qi, diag=0)

    @pl.when(qi == 0)
    def _():
        exact_block(qi, diag=0)


    lp = lp_ref[...]
    lam = (jnp.exp(jnp.sum(lp[0:1, :] * lp[1:2, :], axis=1, keepdims=True))
           - jnp.exp(jnp.sum(lp[2:3, :] * lp[3:4, :], axis=1, keepdims=True)) + lambda_init)
    out = acc_ref[...] / l_ref[...]
    o = out[:, :tq] - lam * out[:, tq:]
    ms = jnp.mean(o * o, axis=0, keepdims=True)
    on = o * lax.rsqrt(ms + DA_EPS) * ng_ref[...] * (1.0 - lambda_init)
    o_ref[...] = on.T.astype(o_ref.dtype)


def _diff_attn(p, lam_params, norm_g, lambda_init, *, batch, seq):
    tq = DA_TQ
    p3 = p.reshape(batch, seq, 2 * DA_QK + DA_H * DA_DV)
    vt = jnp.swapaxes(p3[:, :, 2 * DA_QK:], 1, 2)
    kern = functools.partial(_diff_attn_kernel, lambda_init=lambda_init)
    o = pl.pallas_call(
        kern,
        out_shape=jax.ShapeDtypeStruct((batch, seq, DA_H * DA_DV), BF16),
        grid=(batch, DA_H, seq // tq),
        in_specs=[
            pl.BlockSpec((None, tq, 2 * DA_DH), lambda b, h, i: (b, i, h)),
            pl.BlockSpec((None, seq, 2 * DA_DH), lambda b, h, i: (b, 0, DA_H + h)),
            pl.BlockSpec((None, DA_DV, seq), lambda b, h, i: (b, h, 0)),
            pl.BlockSpec((4, DA_DH), lambda b, h, i: (0, 0)),
            pl.BlockSpec((DA_DV, 1), lambda b, h, i: (0, 0)),
        ],
        out_specs=pl.BlockSpec((None, tq, DA_DV), lambda b, h, i: (b, i, h)),
        scratch_shapes=[
            pltpu.VMEM((2 * tq, 2 * DA_DH), BF16),
            pltpu.VMEM((1, 2 * tq), F32),
            pltpu.VMEM((1, 2 * tq), F32),
            pltpu.VMEM((1, 2 * tq), F32),
            pltpu.VMEM((DA_DV, 2 * tq), F32),
            pltpu.VMEM((1, 2 * tq), F32),
            pltpu.VMEM((1, 2 * tq), F32),
            pltpu.VMEM((DA_DV, 2 * tq), F32),
        ],
        compiler_params=_cparams(("parallel", "parallel", "arbitrary")),
        name="diff_attn",
    )(p3, p3, vt, lam_params, norm_g)
    return o.reshape(batch * seq, DA_H * DA_DV)


def _pack_halves(y):
    halves = []
    for h in range(2):
        base = h * 2 * HALF_W
        lo = y[:, base:base + HALF_W].astype(BF16).astype(F32)
        hi = y[:, base + HALF_W:base + 2 * HALF_W].astype(BF16).astype(F32)
        lo_bits = lax.bitcast_convert_type(lo, U32) >> 16
        hi_bits = lax.bitcast_convert_type(hi, U32)
        halves.append(hi_bits | lo_bits)
    return halves


def _unpack_halves(w0, w1):
    parts = []
    for w in (w0, w1):
        parts.append(lax.bitcast_convert_type(w << 16, F32))
        parts.append(lax.bitcast_convert_type(w & jnp.uint32(0xFFFF0000), F32))
    return jnp.concatenate(parts, axis=1)


def _router_kernel(x_ref, g_ref, w_ref, b_ref, hp_ref, meta_ref, metat_ref, cnt_ref,
                   run_ref, ls_ref):
    tm = x_ref.shape[0]

    @pl.when(pl.program_id(0) == 0)
    def _():
        run_ref[...] = jnp.zeros_like(run_ref)
        r_i = lax.broadcasted_iota(jnp.int32, (tm, tm), 0)
        c_i = lax.broadcasted_iota(jnp.int32, (tm, tm), 1)
        ls_ref[...] = (c_i < r_i).astype(BF16)

    x = x_ref[...]
    ms = jnp.mean(x * x, axis=-1, keepdims=True)
    hn32 = x * lax.rsqrt(ms + EPS) * g_ref[...]
    hn = hn32.astype(BF16)
    halves = _pack_halves(hn32)
    hp_ref[0] = halves[0]
    hp_ref[1] = halves[1]
    lg = jnp.dot(hn, w_ref[...], preferred_element_type=F32) + b_ref[...]
    lane = lax.broadcasted_iota(jnp.int32, lg.shape, 1)
    neg = -jnp.inf

    gmask = (lane >= N_EXPERTS) & (lane < N_EXPERTS + N_GROUPS)
    gl = jnp.where(gmask, lg, neg)
    gmax = jnp.max(gl, axis=1, keepdims=True)
    gidx = jnp.min(jnp.where(gl == gmax, lane, LANES), axis=1, keepdims=True) - N_EXPERTS
    gsum = jnp.sum(jnp.where(gmask, jnp.exp(gl - gmax), 0.0), axis=1, keepdims=True)
    g_w = 1.0 / gsum

    emask = (lane >= gidx * EPG) & (lane < gidx * EPG + EPG)
    el = jnp.where(emask, lg, neg)
    emax = jnp.max(el, axis=1, keepdims=True)
    eexp = jnp.where(emask, jnp.exp(el - emax), 0.0)
    ep = eexp / jnp.sum(eexp, axis=1, keepdims=True)
    ep = jnp.where(emask, ep, -1.0)
    p1 = jnp.max(ep, axis=1, keepdims=True)
    i1 = jnp.min(jnp.where(ep == p1, lane, LANES), axis=1, keepdims=True)
    ep2 = jnp.where(lane == i1, -1.0, ep)
    p2 = jnp.max(ep2, axis=1, keepdims=True)
    i2 = jnp.min(jnp.where(ep2 == p2, lane, LANES), axis=1, keepdims=True)
    wsum = p1 + p2
    w1 = g_w * (p1 / wsum)
    w2 = g_w * (p2 / wsum)

    a1 = lane == i1
    a2 = lane == i2
    onehot = (a1 | a2).astype(BF16)
    before = jnp.dot(ls_ref[...], onehot, preferred_element_type=F32) + run_ref[...]
    rank1 = jnp.sum(jnp.where(a1, before, 0.0), axis=1, keepdims=True)
    rank2 = jnp.sum(jnp.where(a2, before, 0.0), axis=1, keepdims=True)
    run_ref[...] += jnp.sum(onehot.astype(F32), axis=0, keepdims=True)
    cnt_ref[...] = run_ref[...]

    meta = (jnp.where(lane == 0, i1.astype(F32), 0.0) + jnp.where(lane == 1, i2.astype(F32), 0.0)
            + jnp.where(lane == 2, rank1, 0.0) + jnp.where(lane == 3, rank2, 0.0)
            + jnp.where(lane == 4, w1, 0.0) + jnp.where(lane == 5, w2, 0.0))
    meta_ref[...] = meta
    metat_ref[...] = meta.T[0:8, :]


def _router(x, g, w_group, b_group, w_expert, b_expert, *, tm):
    t, d = x.shape
    w = jnp.zeros((d, LANES), BF16)
    w = w.at[:, :N_EXPERTS].set(w_expert.astype(BF16))
    w = w.at[:, N_EXPERTS:N_EXPERTS + N_GROUPS].set(w_group.astype(BF16))
    b = jnp.zeros((1, LANES), F32)
    b = b.at[0, :N_EXPERTS].set(b_expert.astype(F32))
    b = b.at[0, N_EXPERTS:N_EXPERTS + N_GROUPS].set(b_group.astype(F32))
    return pl.pallas_call(
        _router_kernel,
        out_shape=(jax.ShapeDtypeStruct((2, t, HALF_W), U32),
                   jax.ShapeDtypeStruct((t, LANES), F32),
                   jax.ShapeDtypeStruct((8, t), F32),
                   jax.ShapeDtypeStruct((1, LANES), F32)),
        grid=(t // tm,),
        in_specs=[
            pl.BlockSpec((tm, d), lambda i: (i, 0)),
            pl.BlockSpec((1, d), lambda i: (0, 0)),
            pl.BlockSpec((d, LANES), lambda i: (0, 0)),
            pl.BlockSpec((1, LANES), lambda i: (0, 0)),
        ],
        out_specs=(pl.BlockSpec((2, tm, HALF_W), lambda i: (0, i, 0)),
                   pl.BlockSpec((tm, LANES), lambda i: (i, 0)),
                   pl.BlockSpec((8, tm), lambda i: (0, i)),
                   pl.BlockSpec((1, LANES), lambda i: (0, 0))),
        scratch_shapes=[pltpu.VMEM((1, LANES), F32), pltpu.VMEM((tm, tm), BF16)],
        compiler_params=_cparams(("arbitrary",)),
        name="moe_router",
    )(x, g, w, b)


def _sc_mesh():
    return plsc.VectorSubcoreMesh(core_axis_name="c", subcore_axis_name="s")


def _sc_scatter2(x, i0, i1, n_out):
    n, d = x.shape

    @pl.kernel(out_type=jax.ShapeDtypeStruct((n_out, d), x.dtype), mesh=_sc_mesh())
    def k(x_hbm, i0_hbm, i1_hbm, o_hbm):
        def body(x_vmem, i0_vmem, i1_vmem):
            pltpu.sync_copy(x_vmem, o_hbm.at[i0_vmem.at[0]])
            pltpu.sync_copy(x_vmem, o_hbm.at[i1_vmem.at[0]])

        pltpu.emit_pipeline(
            body,
            grid=(n // SC_WINDOW,),
            in_specs=[pl.BlockSpec((SC_WINDOW, d), lambda i: (i, 0)),
                      pl.BlockSpec((1, SC_WINDOW), lambda i: (0, i)),
                      pl.BlockSpec((1, SC_WINDOW), lambda i: (0, i))],
            out_specs=[],
            core_axis_name=("c", "s"),
            dimension_semantics=(pltpu.PARALLEL,),
        )(x_hbm, i0_hbm, i1_hbm)

    return k(x, i0.reshape(1, n), i1.reshape(1, n))


def _sc_gather(x, idx):
    n = idx.shape[0]
    d = x.shape[1]

    @pl.kernel(out_type=jax.ShapeDtypeStruct((n, d), x.dtype), mesh=_sc_mesh())
    def k(x_hbm, i_hbm, o_hbm):
        def body(i_vmem, o_vmem):
            pltpu.sync_copy(x_hbm.at[i_vmem.at[0]], o_vmem)

        pltpu.emit_pipeline(
            body,
            grid=(n // SC_WINDOW,),
            in_specs=[pl.BlockSpec((1, SC_WINDOW), lambda i: (0, i))],
            out_specs=[pl.BlockSpec((SC_WINDOW, d), lambda i: (i, 0))],
            core_axis_name=("c", "s"),
            dimension_semantics=(pltpu.PARALLEL,),
        )(i_hbm, o_hbm)

    return k(x, idx.reshape(1, n))


def _experts_kernel(te_ref, nu_ref, xs_ref, wgu_ref, wd_ref, ys_ref, wgu_bf_ref, wd_bf_ref):
    j = pl.program_id(0)

    @pl.when(j < nu_ref[0])
    def _():
        @pl.when((j == 0) | (te_ref[j] != te_ref[jnp.maximum(j - 1, 0)]))
        def _():
            wgu_bf_ref[...] = wgu_ref[...].astype(BF16)
            wd_bf_ref[...] = wd_ref[...].astype(BF16)

        x = _unpack_halves(xs_ref[0], xs_ref[1]).astype(BF16)
        gu = jnp.dot(x, wgu_bf_ref[...], preferred_element_type=F32)
        gpart = gu[:, :D_EXPERT]
        a = gpart * jax.nn.sigmoid(gpart) * gu[:, D_EXPERT:]
        y = jnp.dot(a.astype(BF16), wd_bf_ref[...], preferred_element_type=F32)
        halves = _pack_halves(y)
        ys_ref[0] = halves[0]
        ys_ref[1] = halves[1]


def _experts(xs, tile_expert, n_used, w_gu, w_down, layer):
    _, rows, _ = xs.shape
    d = w_gu.shape[2]
    return pl.pallas_call(
        _experts_kernel,
        out_shape=jax.ShapeDtypeStruct(xs.shape, U32),
        grid_spec=pltpu.PrefetchScalarGridSpec(
            num_scalar_prefetch=2,
            grid=(rows // MOE_TR,),
            in_specs=[
                pl.BlockSpec((2, MOE_TR, HALF_W), lambda j, te, nu: (0, j, 0)),
                pl.BlockSpec((None, None, d, 2 * D_EXPERT),
                             lambda j, te, nu: (layer, te[j], 0, 0)),
                pl.BlockSpec((None, None, D_EXPERT, d),
                             lambda j, te, nu: (layer, te[j], 0, 0)),
            ],
            out_specs=pl.BlockSpec((2, MOE_TR, HALF_W), lambda j, te, nu: (0, j, 0)),
            scratch_shapes=[pltpu.VMEM((d, 2 * D_EXPERT), BF16), pltpu.VMEM((D_EXPERT, d), BF16)],
        ),
        compiler_params=_cparams(("arbitrary",)),
        name="moe_experts",
    )(tile_expert, n_used, xs, w_gu, w_down)


def _combine_kernel(x_ref, z_ref, meta_ref, g_ref, o_ref, *, final_norm):
    meta = meta_ref[...]
    y_a = _unpack_halves(z_ref[0], z_ref[2])
    y_b = _unpack_halves(z_ref[1], z_ref[3])
    out = x_ref[...] + meta[:, 4:5] * y_a + meta[:, 5:6] * y_b
    if final_norm:
        ms = jnp.mean(out * out, axis=-1, keepdims=True)
        out = out * lax.rsqrt(ms + EPS) * g_ref[...]
    o_ref[...] = out


def _combine(x, z, meta, g, *, tm, final_norm):
    t, d = x.shape
    return pl.pallas_call(
        functools.partial(_combine_kernel, final_norm=final_norm),
        out_shape=jax.ShapeDtypeStruct((t, d), F32),
        grid=(t // tm,),
        in_specs=[
            pl.BlockSpec((tm, d), lambda i: (i, 0)),
            pl.BlockSpec((4, tm, HALF_W), lambda i: (0, i, 0)),
            pl.BlockSpec((tm, LANES), lambda i: (i, 0)),
            pl.BlockSpec((1, d), lambda i: (0, 0)),
        ],
        out_specs=pl.BlockSpec((tm, d), lambda i: (i, 0)),
        compiler_params=_cparams(("parallel",)),
        name="moe_combine",
    )(x, z, meta, g)


def _moe_layer(x, norm_g, w_group, b_group, w_expert, b_expert, w_gu, w_down, layer,
               final_g=None):
    t, d = x.shape
    hp, meta, metat, cnt = _router(x, norm_g.reshape(1, d), w_group, b_group, w_expert, b_expert,
                                   tm=1024)
    n_tiles = 2 * t // MOE_TR + N_EXPERTS
    rows = n_tiles * MOE_TR
    counts = cnt[0, :N_EXPERTS].astype(jnp.int32)
    tiles_e = (counts + MOE_TR - 1) // MOE_TR
    tiles_end = jnp.cumsum(tiles_e)
    row_off = (tiles_end - tiles_e) * MOE_TR
    experts = jnp.arange(N_EXPERTS, dtype=jnp.int32)[:, None]

    def region_start(e_row):
        return jnp.sum(jnp.where(e_row[None, :] == experts, row_off[:, None], 0), axis=0)

    e_a, e_b = metat[0].astype(jnp.int32), metat[1].astype(jnp.int32)
    pos_a = region_start(e_a) + metat[2].astype(jnp.int32)
    pos_b = region_start(e_b) + metat[3].astype(jnp.int32)
    tile_ids = jnp.arange(n_tiles, dtype=jnp.int32)
    tile_expert = jnp.minimum(
        jnp.sum((tile_ids[:, None] >= tiles_end[None, :]).astype(jnp.int32), axis=1),
        N_EXPERTS - 1)
    n_used = tiles_end[-1:].astype(jnp.int32)

    xs = _sc_scatter2(hp.reshape(2 * t, HALF_W),
                      jnp.concatenate([pos_a, pos_a + rows]),
                      jnp.concatenate([pos_b, pos_b + rows]), 2 * rows)
    ys = _experts(xs.reshape(2, rows, HALF_W), tile_expert, n_used,
                  w_gu, w_down, layer)
    z = _sc_gather(ys.reshape(2 * rows, HALF_W),
                   jnp.concatenate([pos_a, pos_b, pos_a + rows, pos_b + rows]))
    g = jnp.ones((1, d), F32) if final_g is None else final_g.reshape(1, d)
    return _combine(x, z.reshape(4, t, HALF_W), meta, g, tm=1024, final_norm=final_g is not None)


def kernel(x, norm_mix, norm_ffn, ml_w_in, ml_conv, ml_b_i, ml_b_f, ml_norm, ml_w_out, da_w_in, da_lq1, da_lk1, da_lq2, da_lk2, da_norm, da_w_out, moe_w_group, moe_b_group, moe_w_expert, moe_b_expert, moe_w_gu, moe_w_down, final_norm):
    batch, seq, d = x.shape
    xt = x.reshape(batch * seq, d)

    g0 = norm_mix[0].reshape(1, d)
    n_main = 2 * ML_QK + 2 * ML_V
    p = _norm_matmul(xt, g0, ml_w_in[0].astype(BF16), tm=1024, tn=1024, out_dtype=BF16)
    assert p.shape[1] == n_main
    gcol, grow = _ml_gates(xt, g0, ml_w_in[0][:, n_main:], ml_b_i[0], ml_b_f[0], tm=1024)
    y = _mlstm_core(p, gcol, grow, ml_conv[0], ml_norm[0].reshape(1, ML_V), batch=batch, seq=seq)
    xt = _matmul_res(y, ml_w_out[0].astype(BF16), xt, tm=512)
    xt = _moe_layer(xt, norm_ffn[0], moe_w_group[0], moe_b_group[0], moe_w_expert[0],
                    moe_b_expert[0], moe_w_gu, moe_w_down, 0)

    lambda_init = 0.8 - 0.6 * math.exp(-0.3 * 1)
    p = _norm_matmul(xt, norm_mix[1].reshape(1, d), da_w_in[0].astype(BF16), tm=1024, tn=1024,
                     out_dtype=BF16)
    lam_params = jnp.stack([da_lq1[0], da_lk1[0], da_lq2[0], da_lk2[0]]).astype(F32)
    a = _diff_attn(p, lam_params, da_norm[0].reshape(DA_DV, 1), lambda_init, batch=batch, seq=seq)
    xt = _matmul_res(a, da_w_out[0].astype(BF16), xt, tm=512)
    out = _moe_layer(xt, norm_ffn[1], moe_w_group[1], moe_b_group[1], moe_w_expert[1],
                     moe_b_expert[1], moe_w_gu, moe_w_down, 1, final_g=final_norm)
    return out.reshape(batch, seq, d)
```

```python
import functools
import math

import jax
import jax.numpy as jnp
from jax import lax
from jax.experimental import pallas as pl
from jax.experimental.pallas import tpu as pltpu
from jax.experimental.pallas import tpu_sc as plsc

F32 = jnp.float32
BF16 = jnp.bfloat16
U32 = jnp.uint32

D_MODEL = 1024
EPS = 1e-6
ML_H = 4
ML_DV = 512
ML_DQK = 256
ML_QK = ML_H * ML_DQK
ML_V = ML_H * ML_DV
CONV_K = 4
ML_CHUNK = 256
CONV_TAIL = 8
DA_H = 8
DA_DH = 64
DA_DV = 128
DA_QK = DA_H * 2 * DA_DH
DA_EPS = 1e-5
DA_TQ = 1024
DA_TK = 1024
DA_CW = 512
DA_KEY_ALIGN = 256
DA_LAZY_LIMIT = 64.0
DA_REBASE = 8.0
LOG2E = 1.4426950408889634
N_GROUPS = 4
EPG = 8
N_EXPERTS = 32
D_EXPERT = 256
MOE_TR = 512
MOE_SUB = 256
HALF_W = D_MODEL // 4
SC_WINDOW = 128
LANES = 128

VMEM_LIMIT = 48 * 1024 * 1024


def _cparams(sem):
    return pltpu.CompilerParams(dimension_semantics=sem, vmem_limit_bytes=VMEM_LIMIT)


def _norm_matmul_kernel(x_ref, g_ref, w_ref, o_ref, xn_ref):
    @pl.when(pl.program_id(1) == 0)
    def _():
        x = x_ref[...]
        ms = jnp.mean(x * x, axis=-1, keepdims=True)
        xn_ref[...] = (x * lax.rsqrt(ms + EPS) * g_ref[...]).astype(BF16)

    o_ref[...] = jnp.dot(xn_ref[...], w_ref[...], preferred_element_type=F32).astype(o_ref.dtype)


def _norm_matmul(x, g, w, *, tm, tn, out_dtype):
    t, d = x.shape
    n = w.shape[1] // tn * tn
    return pl.pallas_call(
        _norm_matmul_kernel,
        out_shape=jax.ShapeDtypeStruct((t, n), out_dtype),
        grid=(t // tm, n // tn),
        in_specs=[
            pl.BlockSpec((tm, d), lambda i, j: (i, 0)),
            pl.BlockSpec((1, d), lambda i, j: (0, 0)),
            pl.BlockSpec((d, tn), lambda i, j: (0, j)),
        ],
        out_specs=pl.BlockSpec((tm, tn), lambda i, j: (i, j)),
        scratch_shapes=[pltpu.VMEM((tm, d), BF16)],
        compiler_params=_cparams(("parallel", "arbitrary")),
        name="norm_matmul",
    )(x, g, w)


def _norm_matmul_vt_kernel(x_ref, g_ref, w_ref, o_ref, vt_ref, xn_ref, *, vt_block):
    j = pl.program_id(1)

    @pl.when(j == 0)
    def _():
        x = x_ref[...]
        ms = jnp.mean(x * x, axis=-1, keepdims=True)
        xn_ref[...] = (x * lax.rsqrt(ms + EPS) * g_ref[...]).astype(BF16)

    res = jnp.dot(xn_ref[...], w_ref[...], preferred_element_type=F32)
    o_ref[...] = res.astype(o_ref.dtype)

    @pl.when(j == vt_block)
    def _():
        vt_ref[...] = res.T.astype(vt_ref.dtype)


def _norm_matmul_vt(x, g, w, *, tm, tn, vt_block, batch, seq, out_dtype):
    t, d = x.shape
    n = w.shape[1]
    per_batch = seq // tm
    return pl.pallas_call(
        functools.partial(_norm_matmul_vt_kernel, vt_block=vt_block),
        out_shape=(jax.ShapeDtypeStruct((t, n), out_dtype),
                   jax.ShapeDtypeStruct((batch, tn, seq), out_dtype)),
        grid=(t // tm, n // tn),
        in_specs=[
            pl.BlockSpec((tm, d), lambda i, j: (i, 0)),
            pl.BlockSpec((1, d), lambda i, j: (0, 0)),
            pl.BlockSpec((d, tn), lambda i, j: (0, j)),
        ],
        out_specs=(pl.BlockSpec((tm, tn), lambda i, j: (i, j)),
                   pl.BlockSpec((None, tn, tm), lambda i, j: (i // per_batch, 0, i % per_batch))),
        scratch_shapes=[pltpu.VMEM((tm, d), BF16)],
        compiler_params=_cparams(("parallel", "arbitrary")),
        name="norm_matmul_vt",
    )(x, g, w)


def _matmul_res_kernel(a_ref, w_ref, r_ref, o_ref):
    o_ref[...] = r_ref[...] + jnp.dot(a_ref[...], w_ref[...], preferred_element_type=F32)


def _matmul_res(a, w, res, *, tm):
    t, k = a.shape
    n = w.shape[1]
    return pl.pallas_call(
        _matmul_res_kernel,
        out_shape=jax.ShapeDtypeStruct((t, n), F32),
        grid=(t // tm,),
        in_specs=[
            pl.BlockSpec((tm, k), lambda i: (i, 0)),
            pl.BlockSpec((k, n), lambda i: (0, 0)),
            pl.BlockSpec((tm, n), lambda i: (i, 0)),
        ],
        out_specs=pl.BlockSpec((tm, n), lambda i: (i, 0)),
        compiler_params=_cparams(("parallel",)),
        name="matmul_res",
    )(a, w, res)


def _log_sigmoid(x):
    return jnp.minimum(x, 0.0) - jnp.log1p(jnp.exp(-jnp.abs(x)))


def _ml_gates_kernel(x_ref, g_ref, wc_ref, wr_ref, bc_ref, br_ref, oc_ref, or_ref):
    x = x_ref[...]
    ms = jnp.mean(x * x, axis=-1, keepdims=True)
    xn = (x * lax.rsqrt(ms + EPS) * g_ref[...]).astype(BF16)
    gc = jnp.dot(xn, wc_ref[...], preferred_element_type=F32) + bc_ref[...]
    lane = lax.broadcasted_iota(jnp.int32, gc.shape, 1)
    oc_ref[...] = jnp.where(lane < ML_H, gc, _log_sigmoid(gc))
    gr = lax.dot_general(wr_ref[...], xn, (((1,), (1,)), ((), ())),
                         preferred_element_type=F32) + br_ref[...]
    row = lax.broadcasted_iota(jnp.int32, gr.shape, 0)
    or_ref[...] = jnp.where(row < ML_H, gr, _log_sigmoid(gr))


def _ml_gates(x, g, w_gates, b_i, b_f, *, tm):
    t, d = x.shape
    ng = 2 * ML_H
    wc = jnp.zeros((d, LANES), BF16).at[:, :ng].set(w_gates.astype(BF16))
    wr = w_gates.T.astype(BF16)
    bias = jnp.concatenate([b_i, b_f]).astype(F32)
    bc = jnp.zeros((1, LANES), F32).at[0, :ng].set(bias)
    br = bias.reshape(ng, 1)
    return pl.pallas_call(
        _ml_gates_kernel,
        out_shape=(jax.ShapeDtypeStruct((t, LANES), F32), jax.ShapeDtypeStruct((ng, t), F32)),
        grid=(t // tm,),
        in_specs=[
            pl.BlockSpec((tm, d), lambda i: (i, 0)),
            pl.BlockSpec((1, d), lambda i: (0, 0)),
            pl.BlockSpec((d, LANES), lambda i: (0, 0)),
            pl.BlockSpec((ng, d), lambda i: (0, 0)),
            pl.BlockSpec((1, LANES), lambda i: (0, 0)),
            pl.BlockSpec((ng, 1), lambda i: (0, 0)),
        ],
        out_specs=(pl.BlockSpec((tm, LANES), lambda i: (i, 0)),
                   pl.BlockSpec((ng, tm), lambda i: (0, i))),
        compiler_params=_cparams(("parallel",)),
        name="ml_gates",
    )(x, g, wc, wr, bc, br)


def _split3(x):
    hi = x.astype(BF16)
    r = x - hi.astype(F32)
    mid = r.astype(BF16)
    lo = (r - mid.astype(F32)).astype(BF16)
    return hi, mid, lo


def _mlstm_kernel(q_ref, k_ref, v_ref, o_ref, gc_ref, gr_ref, cw_ref, ng_ref, y_ref,
                  tail_ref, ct_ref, n_ref, m_ref, shift_ref, tril_ref, triu_ref, neg_ref):
    L = ML_CHUNK
    c = pl.program_id(1)

    @pl.when(c == 0)
    def _():
        tail_ref[...] = jnp.zeros_like(tail_ref)
        ct_ref[...] = jnp.zeros_like(ct_ref)
        n_ref[...] = jnp.zeros_like(n_ref)
        m_ref[...] = jnp.zeros_like(m_ref)
        r_i = lax.broadcasted_iota(jnp.int32, (L, L), 0)
        c_i = lax.broadcasted_iota(jnp.int32, (L, L), 1)
        tril_ref[...] = (c_i <= r_i).astype(BF16)
        triu_ref[...] = (r_i <= c_i).astype(BF16)
        neg_ref[...] = jnp.where(c_i <= r_i, 0.0, -jnp.inf)
        for j in range(CONV_K - 1):
            shift_ref[j * L:(j + 1) * L, :] = (c_i == r_i - (CONV_K - 1 - j)).astype(BF16)

    u = jnp.concatenate([q_ref[...], k_ref[...]], axis=1)
    uf = u.astype(F32)
    shifted = jnp.dot(shift_ref[...], u, preferred_element_type=F32)
    conv = uf * cw_ref[CONV_K - 1:CONV_K, :]
    head = None
    for j in range(CONV_K - 1):
        conv = conv + shifted[j * L:(j + 1) * L, :] * cw_ref[j:j + 1, :]
        part = tail_ref[pl.ds(CONV_TAIL - (CONV_K - 1) + j, CONV_TAIL), :] * cw_ref[j:j + 1, :]
        head = part if head is None else head + part
    conv = jnp.concatenate([conv[:CONV_TAIL] + head, conv[CONV_TAIL:]], axis=0)
    tail_ref[0:CONV_TAIL, :] = uf[L - CONV_TAIL:, :]
    qk = conv * jax.nn.sigmoid(conv)
    q_all = (qk[:, :ML_QK] * (ML_DQK ** -0.5)).astype(BF16)
    k_all = qk[:, ML_QK:]

    tril = tril_ref[...]
    triu = triu_ref[...]
    gc = gc_ref[...]
    gr = gr_ref[...]
    bc_all = sum(jnp.dot(tril, p, preferred_element_type=F32) for p in _split3(gc))
    br_all = sum(jnp.dot(p, triu, preferred_element_type=F32) for p in _split3(gr))

    for h in range(ML_H):
        qh = q_all[:, h * ML_DQK:(h + 1) * ML_DQK]
        kh_f = k_all[:, h * ML_DQK:(h + 1) * ML_DQK]
        kh = kh_f.astype(BF16)
        vh = v_ref[:, h * ML_DV:(h + 1) * ML_DV]
        it_col = gc[:, h:h + 1]
        it_row = gr[h:h + 1, :]
        b_col = bc_all[:, ML_H + h:ML_H + h + 1]
        b_row = br_all[ML_H + h:ML_H + h + 1, :]
        m_prev = m_ref[h][:, 0:1]

        dmat = b_col + (it_row - b_row) + neg_ref[...]
        inter_log = b_col + m_prev
        m_t = jnp.maximum(inter_log, jnp.max(dmat, axis=1, keepdims=True))
        wts = jnp.exp(dmat - m_t)
        s = lax.dot_general(qh, kh, (((1,), (1,)), ((), ())), preferred_element_type=F32)
        sc = s * wts
        inter_scale = jnp.exp(inter_log - m_t)
        ct = ct_ref[h]
        num = (jnp.dot(sc.astype(BF16), vh, preferred_element_type=F32)
               + inter_scale * jnp.dot(qh, ct.astype(BF16), preferred_element_type=F32))
        n_row = n_ref[h]
        den = (jnp.sum(sc, axis=1, keepdims=True)
               + inter_scale * jnp.sum(qh.astype(F32) * n_row, axis=1, keepdims=True))
        h_out = num / jnp.maximum(jnp.abs(den), jnp.exp(-m_t))

        b_last = b_col[L - 1:L, :]
        lw_col = b_last - b_col + it_col
        lw_row = b_last - b_row + it_row
        m_new = jnp.maximum(b_last + m_prev, jnp.max(lw_row, axis=1, keepdims=True))
        ws_col = jnp.exp(lw_col - m_new)
        decay = jnp.exp(b_last + m_prev - m_new)
        kw = kh_f * ws_col
        ct_ref[h] = decay * ct + lax.dot_general(kw.astype(BF16), vh, (((0,), (0,)), ((), ())),
                                                  preferred_element_type=F32)
        n_ref[h] = decay * n_row + jnp.sum(kw, axis=0, keepdims=True)
        m_ref[h] = jnp.broadcast_to(m_new, (1, LANES))

        ms = jnp.mean(h_out * h_out, axis=1, keepdims=True)
        hn = h_out * lax.rsqrt(ms + EPS) * ng_ref[:, h * ML_DV:(h + 1) * ML_DV]
        og = o_ref[:, h * ML_DV:(h + 1) * ML_DV].astype(F32)
        y_ref[:, h * ML_DV:(h + 1) * ML_DV] = (hn * jax.nn.sigmoid(og)).astype(y_ref.dtype)


def _mlstm_core(p, gcol, grow, conv_w, norm_g, *, batch, seq):
    L = ML_CHUNK
    nc = seq // L
    p3 = p.reshape(batch, seq, 2 * ML_QK + 2 * ML_V)
    gc3 = gcol.reshape(batch, seq, LANES)
    gr3 = grow.reshape(2 * ML_H, batch, seq).transpose(1, 0, 2)
    y = pl.pallas_call(
        _mlstm_kernel,
        out_shape=jax.ShapeDtypeStruct((batch, seq, ML_V), BF16),
        grid=(batch, nc),
        in_specs=[
            pl.BlockSpec((None, L, ML_QK), lambda b, c: (b, c, 0)),
            pl.BlockSpec((None, L, ML_QK), lambda b, c: (b, c, 1)),
            pl.BlockSpec((None, L, ML_V), lambda b, c: (b, c, 1)),
            pl.BlockSpec((None, L, ML_V), lambda b, c: (b, c, 2)),
            pl.BlockSpec((None, L, LANES), lambda b, c: (b, c, 0)),
            pl.BlockSpec((None, 2 * ML_H, L), lambda b, c: (b, 0, c)),
            pl.BlockSpec((CONV_K, 2 * ML_QK), lambda b, c: (0, 0)),
            pl.BlockSpec((1, ML_V), lambda b, c: (0, 0)),
        ],
        out_specs=pl.BlockSpec((None, L, ML_V), lambda b, c: (b, c, 0)),
        scratch_shapes=[
            pltpu.VMEM((2 * CONV_TAIL, 2 * ML_QK), F32),
            pltpu.VMEM((ML_H, ML_DQK, ML_DV), F32),
            pltpu.VMEM((ML_H, 1, ML_DQK), F32),
            pltpu.VMEM((ML_H, 1, LANES), F32),
            pltpu.VMEM(((CONV_K - 1) * L, L), BF16),
            pltpu.VMEM((L, L), BF16),
            pltpu.VMEM((L, L), BF16),
            pltpu.VMEM((L, L), F32),
        ],
        compiler_params=_cparams(("parallel", "arbitrary")),
        name="mlstm_core",
    )(p3, p3, p3, p3, gc3, gr3, conv_w, norm_g)
    return y.reshape(batch * seq, ML_V)


def _diff_attn_kernel(q_ref, k_ref, vt_ref, lp_ref, ng_ref, o_ref, q2_ref, r_ref, m_ref, l_ref,
                      acc_ref, bm_ref, lt_ref, acct_ref, *, lambda_init):
    tq, tk, cw = DA_TQ, DA_TK, DA_CW
    qi = pl.program_id(2)
    q = q_ref[...].astype(F32) * (DA_DH ** -0.5 * LOG2E)
    lane = lax.broadcasted_iota(jnp.int32, q.shape, 1)
    q2_ref[0:tq, :] = jnp.where(lane < DA_DH, q, 0.0).astype(BF16)
    q2_ref[tq:2 * tq, :] = jnp.where(lane >= DA_DH, q, 0.0).astype(BF16)

    r_ref[...] = jnp.full(r_ref.shape, -jnp.inf, F32)
    m_ref[...] = jnp.full(m_ref.shape, -jnp.inf, F32)
    l_ref[...] = jnp.zeros_like(l_ref)
    acc_ref[...] = jnp.zeros_like(acc_ref)

    nch = 2 * tq // cw

    def keys_needed(c, diag):
        if diag is None:
            return tk
        visible = (c * cw) % tq + cw - diag * tk
        return max(0, min(tk, -(-visible // DA_KEY_ALIGN) * DA_KEY_ALIGN))

    def mask(s, c, diag):
        nk = s.shape[0]
        key = lax.broadcasted_iota(jnp.int32, (nk, cw), 0) + diag * tk
        qry = lax.broadcasted_iota(jnp.int32, (nk, cw), 1) + (c * cw) % tq
        return jnp.where(key <= qry, s, -jnp.inf)

    def scores(j, c, diag=None):
        nk = keys_needed(c, diag)
        if nk == 0:
            return None
        kb = k_ref[pl.ds(pl.multiple_of(j * tk, tk), nk), :]
        s = lax.dot_general(kb, q2_ref[c * cw:(c + 1) * cw, :], (((1,), (1,)), ((), ())),
                            preferred_element_type=F32)
        if diag is not None:
            s = mask(s, c, diag)
        return s, jnp.max(s, axis=0, keepdims=True)


    def exact_block(j, diag=None):
        start = pl.multiple_of(j * tk, tk)
        nxt = scores(j, 0, diag)
        for c in range(nch):
            cs = slice(c * cw, (c + 1) * cw)
            s, bmax = nxt
            if c + 1 < nch:
                nxt = scores(j, c + 1, diag)
            vbt = vt_ref[:, pl.ds(start, s.shape[0])]
            m_new = jnp.maximum(m_ref[:, cs], bmax)
            alpha = jnp.exp2(r_ref[:, cs] - m_new)
            p = jnp.exp2(s - m_new)
            l_ref[:, cs] = alpha * l_ref[:, cs] + jnp.sum(p, axis=0, keepdims=True)
            acc_ref[:, cs] = alpha * acc_ref[:, cs] + jnp.dot(vbt, p.astype(BF16),
                                                               preferred_element_type=F32)
            m_ref[:, cs] = m_new
            r_ref[:, cs] = m_new

    def fast_block(j, diag=None):
        start = pl.multiple_of(j * tk, tk)
        nxt = scores(j, 0, diag)
        for c in range(nch):
            cs = slice(c * cw, (c + 1) * cw)
            s, bmax = nxt
            if c + 1 < nch:
                nxt = scores(j, c + 1, diag)
            vbt = vt_ref[:, pl.ds(start, s.shape[0])]
            p = jnp.exp2(s - r_ref[:, cs])
            bm_ref[:, cs] = bmax
            lt_ref[:, cs] = jnp.sum(p, axis=0, keepdims=True)
            acct_ref[:, cs] = jnp.dot(vbt, p.astype(BF16), preferred_element_type=F32)
        safe = jnp.max(bm_ref[...] - r_ref[...]) <= DA_LAZY_LIMIT

        @pl.when(safe)
        def _():
            r_old = r_ref[...]
            m_new = jnp.maximum(m_ref[...], bm_ref[...])
            r_new = jnp.where(m_new - r_old > DA_REBASE, m_new, r_old)
            scale = jnp.exp2(r_old - r_new)
            l_ref[...] = (l_ref[...] + lt_ref[...]) * scale
            acc_ref[...] = (acc_ref[...] + acct_ref[...]) * scale
            m_ref[...] = m_new
            r_ref[...] = r_new

        @pl.when(jnp.logical_not(safe))
        def _():
            exact_block(j, diag)

    def exact_body(j, carry):
        exact_block(j)
        return carry

    def fast_body(j, carry):
        fast_block(j)
        return carry

    assert tq == tk
    lax.fori_loop(0, jnp.minimum(qi, 1), exact_body, 0)
    lax.fori_loop(1, qi, fast_body, 0)

    @pl.when(qi > 0)
    def _():
        fast_block(qi, diag=0)

    @pl.when(qi == 0)
    def _():
        exact_block(qi, diag=0)


    lp = lp_ref[...]
    lam = (jnp.exp(jnp.sum(lp[0:1, :] * lp[1:2, :], axis=1, keepdims=True))
           - jnp.exp(jnp.sum(lp[2:3, :] * lp[3:4, :], axis=1, keepdims=True)) + lambda_init)
    out = acc_ref[...] / l_ref[...]
    o = out[:, :tq] - lam * out[:, tq:]
    ms = jnp.mean(o * o, axis=0, keepdims=True)
    on = o * lax.rsqrt(ms + DA_EPS) * ng_ref[...] * (1.0 - lambda_init)
    o_ref[...] = on.T.astype(o_ref.dtype)


def _diff_attn(p, vt, lam_params, norm_g, lambda_init, *, batch, seq):
    tq = DA_TQ
    p3 = p.reshape(batch, seq, 2 * DA_QK + DA_H * DA_DV)
    kern = functools.partial(_diff_attn_kernel, lambda_init=lambda_init)
    o = pl.pallas_call(
        kern,
        out_shape=jax.ShapeDtypeStruct((batch, seq, DA_H * DA_DV), BF16),
        grid=(batch, DA_H, seq // tq),
        in_specs=[
            pl.BlockSpec((None, tq, 2 * DA_DH), lambda b, h, i: (b, i, h)),
            pl.BlockSpec((None, seq, 2 * DA_DH), lambda b, h, i: (b, 0, DA_H + h)),
            pl.BlockSpec((None, DA_DV, seq), lambda b, h, i: (b, h, 0)),
            pl.BlockSpec((4, DA_DH), lambda b, h, i: (0, 0)),
            pl.BlockSpec((DA_DV, 1), lambda b, h, i: (0, 0)),
        ],
        out_specs=pl.BlockSpec((None, tq, DA_DV), lambda b, h, i: (b, i, h)),
        scratch_shapes=[
            pltpu.VMEM((2 * tq, 2 * DA_DH), BF16),
            pltpu.VMEM((1, 2 * tq), F32),
            pltpu.VMEM((1, 2 * tq), F32),
            pltpu.VMEM((1, 2 * tq), F32),
            pltpu.VMEM((DA_DV, 2 * tq), F32),
            pltpu.VMEM((1, 2 * tq), F32),
            pltpu.VMEM((1, 2 * tq), F32),
            pltpu.VMEM((DA_DV, 2 * tq), F32),
        ],
        compiler_params=_cparams(("parallel", "parallel", "arbitrary")),
        name="diff_attn",
    )(p3, p3, vt, lam_params, norm_g)
    return o.reshape(batch * seq, DA_H * DA_DV)


def _pack_halves(y):
    halves = []
    for h in range(2):
        base = h * 2 * HALF_W
        lo = y[:, base:base + HALF_W].astype(BF16).astype(F32)
        hi = y[:, base + HALF_W:base + 2 * HALF_W].astype(BF16).astype(F32)
        lo_bits = lax.bitcast_convert_type(lo, U32) >> 16
        hi_bits = lax.bitcast_convert_type(hi, U32)
        halves.append(hi_bits | lo_bits)
    return halves


def _unpack_halves(w0, w1):
    parts = []
    for w in (w0, w1):
        parts.append(lax.bitcast_convert_type(w << 16, F32))
        parts.append(lax.bitcast_convert_type(w & jnp.uint32(0xFFFF0000), F32))
    return jnp.concatenate(parts, axis=1)


def _router_kernel(x_ref, g_ref, w_ref, b_ref, hp_ref, meta_ref, metat_ref, cnt_ref,
                   run_ref, ls_ref):
    tm = x_ref.shape[0]

    @pl.when(pl.program_id(0) == 0)
    def _():
        run_ref[...] = jnp.zeros_like(run_ref)
        r_i = lax.broadcasted_iota(jnp.int32, (tm, tm), 0)
        c_i = lax.broadcasted_iota(jnp.int32, (tm, tm), 1)
        ls_ref[...] = (c_i < r_i).astype(BF16)

    x = x_ref[...]
    ms = jnp.mean(x * x, axis=-1, keepdims=True)
    hn32 = x * lax.rsqrt(ms + EPS) * g_ref[...]
    hn = hn32.astype(BF16)
    halves = _pack_halves(hn32)
    hp_ref[0] = halves[0]
    hp_ref[1] = halves[1]
    lg = jnp.dot(hn, w_ref[...], preferred_element_type=F32) + b_ref[...]
    lane = lax.broadcasted_iota(jnp.int32, lg.shape, 1)
    neg = -jnp.inf

    gmask = (lane >= N_EXPERTS) & (lane < N_EXPERTS + N_GROUPS)
    gl = jnp.where(gmask, lg, neg)
    gmax = jnp.max(gl, axis=1, keepdims=True)
    gidx = jnp.min(jnp.where(gl == gmax, lane, LANES), axis=1, keepdims=True) - N_EXPERTS
    gsum = jnp.sum(jnp.where(gmask, jnp.exp(gl - gmax), 0.0), axis=1, keepdims=True)
    g_w = 1.0 / gsum

    emask = (lane >= gidx * EPG) & (lane < gidx * EPG + EPG)
    el = jnp.where(emask, lg, neg)
    emax = jnp.max(el, axis=1, keepdims=True)
    eexp = jnp.where(emask, jnp.exp(el - emax), 0.0)
    ep = eexp / jnp.sum(eexp, axis=1, keepdims=True)
    ep = jnp.where(emask, ep, -1.0)
    p1 = jnp.max(ep, axis=1, keepdims=True)
    i1 = jnp.min(jnp.where(ep == p1, lane, LANES), axis=1, keepdims=True)
    ep2 = jnp.where(lane == i1, -1.0, ep)
    p2 = jnp.max(ep2, axis=1, keepdims=True)
    i2 = jnp.min(jnp.where(ep2 == p2, lane, LANES), axis=1, keepdims=True)
    wsum = p1 + p2
    w1 = g_w * (p1 / wsum)
    w2 = g_w * (p2 / wsum)

    a1 = lane == i1
    a2 = lane == i2
    onehot = (a1 | a2).astype(BF16)
    before = jnp.dot(ls_ref[...], onehot, preferred_element_type=F32) + run_ref[...]
    rank1 = jnp.sum(jnp.where(a1, before, 0.0), axis=1, keepdims=True)
    rank2 = jnp.sum(jnp.where(a2, before, 0.0), axis=1, keepdims=True)
    run_ref[...] += jnp.sum(onehot.astype(F32), axis=0, keepdims=True)
    cnt_ref[...] = run_ref[...]

    meta = (jnp.where(lane == 0, i1.astype(F32), 0.0) + jnp.where(lane == 1, i2.astype(F32), 0.0)
            + jnp.where(lane == 2, rank1, 0.0) + jnp.where(lane == 3, rank2, 0.0)
            + jnp.where(lane == 4, w1, 0.0) + jnp.where(lane == 5, w2, 0.0))
    meta_ref[...] = meta
    metat_ref[...] = meta.T[0:8, :]


def _router(x, g, w_group, b_group, w_expert, b_expert, *, tm):
    t, d = x.shape
    w = jnp.zeros((d, LANES), BF16)
    w = w.at[:, :N_EXPERTS].set(w_expert.astype(BF16))
    w = w.at[:, N_EXPERTS:N_EXPERTS + N_GROUPS].set(w_group.astype(BF16))
    b = jnp.zeros((1, LANES), F32)
    b = b.at[0, :N_EXPERTS].set(b_expert.astype(F32))
    b = b.at[0, N_EXPERTS:N_EXPERTS + N_GROUPS].set(b_group.astype(F32))
    return pl.pallas_call(
        _router_kernel,
        out_shape=(jax.ShapeDtypeStruct((2, t, HALF_W), U32),
                   jax.ShapeDtypeStruct((t, LANES), F32),
                   jax.ShapeDtypeStruct((8, t), F32),
                   jax.ShapeDtypeStruct((1, LANES), F32)),
        grid=(t // tm,),
        in_specs=[
            pl.BlockSpec((tm, d), lambda i: (i, 0)),
            pl.BlockSpec((1, d), lambda i: (0, 0)),
            pl.BlockSpec((d, LANES), lambda i: (0, 0)),
            pl.BlockSpec((1, LANES), lambda i: (0, 0)),
        ],
        out_specs=(pl.BlockSpec((2, tm, HALF_W), lambda i: (0, i, 0)),
                   pl.BlockSpec((tm, LANES), lambda i: (i, 0)),
                   pl.BlockSpec((8, tm), lambda i: (0, i)),
                   pl.BlockSpec((1, LANES), lambda i: (0, 0))),
        scratch_shapes=[pltpu.VMEM((1, LANES), F32), pltpu.VMEM((tm, tm), BF16)],
        compiler_params=_cparams(("arbitrary",)),
        name="moe_router",
    )(x, g, w, b)


def _sc_mesh():
    return plsc.VectorSubcoreMesh(core_axis_name="c", subcore_axis_name="s")


def _sc_scatter2(x, i0, i1, n_out):
    n, d = x.shape

    @pl.kernel(out_type=jax.ShapeDtypeStruct((n_out, d), x.dtype), mesh=_sc_mesh())
    def k(x_hbm, i0_hbm, i1_hbm, o_hbm):
        def body(x_vmem, i0_vmem, i1_vmem):
            pltpu.sync_copy(x_vmem, o_hbm.at[i0_vmem.at[0]])
            pltpu.sync_copy(x_vmem, o_hbm.at[i1_vmem.at[0]])

        pltpu.emit_pipeline(
            body,
            grid=(n // SC_WINDOW,),
            in_specs=[pl.BlockSpec((SC_WINDOW, d), lambda i: (i, 0)),
                      pl.BlockSpec((1, SC_WINDOW), lambda i: (0, i)),
                      pl.BlockSpec((1, SC_WINDOW), lambda i: (0, i))],
            out_specs=[],
            core_axis_name=("c", "s"),
            dimension_semantics=(pltpu.PARALLEL,),
        )(x_hbm, i0_hbm, i1_hbm)

    return k(x, i0.reshape(1, n), i1.reshape(1, n))


def _sc_gather(x, idx):
    n = idx.shape[0]
    d = x.shape[1]

    @pl.kernel(out_type=jax.ShapeDtypeStruct((n, d), x.dtype), mesh=_sc_mesh())
    def k(x_hbm, i_hbm, o_hbm):
        def body(i_vmem, o_vmem):
            pltpu.sync_copy(x_hbm.at[i_vmem.at[0]], o_vmem)

        pltpu.emit_pipeline(
            body,
            grid=(n // SC_WINDOW,),
            in_specs=[pl.BlockSpec((1, SC_WINDOW), lambda i: (0, i))],
            out_specs=[pl.BlockSpec((SC_WINDOW, d), lambda i: (i, 0))],
            core_axis_name=("c", "s"),
            dimension_semantics=(pltpu.PARALLEL,),
        )(i_hbm, o_hbm)

    return k(x, idx.reshape(1, n))


def _experts_kernel(te_ref, nu_ref, xs_ref, wgu_ref, wd_ref, ys_ref, wgu_bf_ref, wd_bf_ref):
    j = pl.program_id(0)

    @pl.when(j < nu_ref[0])
    def _():
        @pl.when((j == 0) | (te_ref[j] != te_ref[jnp.maximum(j - 1, 0)]))
        def _():
            wgu_bf_ref[...] = wgu_ref[...].astype(BF16)
            wd_bf_ref[...] = wd_ref[...].astype(BF16)

        subs = [slice(i * MOE_SUB, (i + 1) * MOE_SUB) for i in range(MOE_TR // MOE_SUB)]
        xs = [_unpack_halves(xs_ref[0, r, :], xs_ref[1, r, :]).astype(BF16) for r in subs]
        gus = [jnp.dot(x, wgu_bf_ref[...], preferred_element_type=F32) for x in xs]
        acts = [(gu[:, :D_EXPERT] * jax.nn.sigmoid(gu[:, :D_EXPERT]) * gu[:, D_EXPERT:]).astype(BF16)
                for gu in gus]
        ys = [jnp.dot(a, wd_bf_ref[...], preferred_element_type=F32) for a in acts]
        for r, y in zip(subs, ys):
            halves = _pack_halves(y)
            ys_ref[0, r, :] = halves[0]
            ys_ref[1, r, :] = halves[1]


def _experts(xs, tile_expert, n_used, w_gu, w_down, layer):
    _, rows, _ = xs.shape
    d = w_gu.shape[2]
    return pl.pallas_call(
        _experts_kernel,
        out_shape=jax.ShapeDtypeStruct(xs.shape, U32),
        grid_spec=pltpu.PrefetchScalarGridSpec(
            num_scalar_prefetch=2,
            grid=(rows // MOE_TR,),
            in_specs=[
                pl.BlockSpec((2, MOE_TR, HALF_W), lambda j, te, nu: (0, j, 0)),
                pl.BlockSpec((None, None, d, 2 * D_EXPERT),
                             lambda j, te, nu: (layer, te[j], 0, 0)),
                pl.BlockSpec((None, None, D_EXPERT, d),
                             lambda j, te, nu: (layer, te[j], 0, 0)),
            ],
            out_specs=pl.BlockSpec((2, MOE_TR, HALF_W), lambda j, te, nu: (0, j, 0)),
            scratch_shapes=[pltpu.VMEM((d, 2 * D_EXPERT), BF16), pltpu.VMEM((D_EXPERT, d), BF16)],
        ),
        compiler_params=_cparams(("arbitrary",)),
        name="moe_experts",
    )(tile_expert, n_used, xs, w_gu, w_down)


def _combine_kernel(x_ref, z_ref, meta_ref, g_ref, o_ref, *, final_norm):
    meta = meta_ref[...]
    y_a = _unpack_halves(z_ref[0], z_ref[2])
    y_b = _unpack_halves(z_ref[1], z_ref[3])
    out = x_ref[...] + meta[:, 4:5] * y_a + meta[:, 5:6] * y_b
    if final_norm:
        ms = jnp.mean(out * out, axis=-1, keepdims=True)
        out = out * lax.rsqrt(ms + EPS) * g_ref[...]
    o_ref[...] = out


def _combine(x, z, meta, g, *, tm, final_norm):
    t, d = x.shape
    return pl.pallas_call(
        functools.partial(_combine_kernel, final_norm=final_norm),
        out_shape=jax.ShapeDtypeStruct((t, d), F32),
        grid=(t // tm,),
        in_specs=[
            pl.BlockSpec((tm, d), lambda i: (i, 0)),
            pl.BlockSpec((4, tm, HALF_W), lambda i: (0, i, 0)),
            pl.BlockSpec((tm, LANES), lambda i: (i, 0)),
            pl.BlockSpec((1, d), lambda i: (0, 0)),
        ],
        out_specs=pl.BlockSpec((tm, d), lambda i: (i, 0)),
        compiler_params=_cparams(("parallel",)),
        name="moe_combine",
    )(x, z, meta, g)


def _moe_layer(x, norm_g, w_group, b_group, w_expert, b_expert, w_gu, w_down, layer,
               final_g=None):
    t, d = x.shape
    hp, meta, metat, cnt = _router(x, norm_g.reshape(1, d), w_group, b_group, w_expert, b_expert,
                                   tm=1024)
    n_tiles = 2 * t // MOE_TR + N_EXPERTS
    rows = n_tiles * MOE_TR
    counts = cnt[0, :N_EXPERTS].astype(jnp.int32)
    tiles_e = (counts + MOE_TR - 1) // MOE_TR
    tiles_end = jnp.cumsum(tiles_e)
    row_off = (tiles_end - tiles_e) * MOE_TR
    experts = jnp.arange(N_EXPERTS, dtype=jnp.int32)[:, None]

    def region_start(e_row):
        return jnp.sum(jnp.where(e_row[None, :] == experts, row_off[:, None], 0), axis=0)

    e_a, e_b = metat[0].astype(jnp.int32), metat[1].astype(jnp.int32)
    pos_a = region_start(e_a) + metat[2].astype(jnp.int32)
    pos_b = region_start(e_b) + metat[3].astype(jnp.int32)
    tile_ids = jnp.arange(n_tiles, dtype=jnp.int32)
    tile_expert = jnp.minimum(
        jnp.sum((tile_ids[:, None] >= tiles_end[None, :]).astype(jnp.int32), axis=1),
        N_EXPERTS - 1)
    n_used = tiles_end[-1:].astype(jnp.int32)

    xs = _sc_scatter2(hp.reshape(2 * t, HALF_W),
                      jnp.concatenate([pos_a, pos_a + rows]),
                      jnp.concatenate([pos_b, pos_b + rows]), 2 * rows)
    ys = _experts(xs.reshape(2, rows, HALF_W), tile_expert, n_used,
                  w_gu, w_down, layer)
    z = _sc_gather(ys.reshape(2 * rows, HALF_W),
                   jnp.concatenate([pos_a, pos_b, pos_a + rows, pos_b + rows]))
    g = jnp.ones((1, d), F32) if final_g is None else final_g.reshape(1, d)
    return _combine(x, z.reshape(4, t, HALF_W), meta, g, tm=1024, final_norm=final_g is not None)


def kernel(x, norm_mix, norm_ffn, ml_w_in, ml_conv, ml_b_i, ml_b_f, ml_norm, ml_w_out, da_w_in, da_lq1, da_lk1, da_lq2, da_lk2, da_norm, da_w_out, moe_w_group, moe_b_group, moe_w_expert, moe_b_expert, moe_w_gu, moe_w_down, final_norm):
    batch, seq, d = x.shape
    xt = x.reshape(batch * seq, d)

    g0 = norm_mix[0].reshape(1, d)
    n_main = 2 * ML_QK + 2 * ML_V
    p = _norm_matmul(xt, g0, ml_w_in[0].astype(BF16), tm=1024, tn=1024, out_dtype=BF16)
    assert p.shape[1] == n_main
    gcol, grow = _ml_gates(xt, g0, ml_w_in[0][:, n_main:], ml_b_i[0], ml_b_f[0], tm=1024)
    y = _mlstm_core(p, gcol, grow, ml_conv[0], ml_norm[0].reshape(1, ML_V), batch=batch, seq=seq)
    xt = _matmul_res(y, ml_w_out[0].astype(BF16), xt, tm=512)
    xt = _moe_layer(xt, norm_ffn[0], moe_w_group[0], moe_b_group[0], moe_w_expert[0],
                    moe_b_expert[0], moe_w_gu, moe_w_down, 0)

    lambda_init = 0.8 - 0.6 * math.exp(-0.3 * 1)
    p, vt = _norm_matmul_vt(xt, norm_mix[1].reshape(1, d), da_w_in[0].astype(BF16), tm=1024,
                            tn=DA_H * DA_DV, vt_block=2 * DA_QK // (DA_H * DA_DV), batch=batch,
                            seq=seq, out_dtype=BF16)
    lam_params = jnp.stack([da_lq1[0], da_lk1[0], da_lq2[0], da_lk2[0]]).astype(F32)
    a = _diff_attn(p, vt, lam_params, da_norm[0].reshape(DA_DV, 1), lambda_init, batch=batch,
                   seq=seq)
    xt = _matmul_res(a, da_w_out[0].astype(BF16), xt, tm=512)
    out = _moe_layer(xt, norm_ffn[1], moe_w_group[1], moe_b_group[1], moe_w_expert[1],
                     moe_b_expert[1], moe_w_gu, moe_w_down, 1, final_g=final_norm)
    return out.reshape(batch, seq, d)
```

```python
import functools
import math

import jax
import jax.numpy as jnp
from jax import lax
from jax.experimental import pallas as pl
from jax.experimental.pallas import tpu as pltpu
from jax.experimental.pallas import tpu_sc as plsc

F32 = jnp.float32
BF16 = jnp.bfloat16
U32 = jnp.uint32

D_MODEL = 1024
EPS = 1e-6
ML_H = 4
ML_DV = 512
ML_DQK = 256
ML_QK = ML_H * ML_DQK
ML_V = ML_H * ML_DV
CONV_K = 4
ML_CHUNK = 256
CONV_TAIL = 8
DA_H = 8
DA_DH = 64
DA_DV = 128
DA_QK = DA_H * 2 * DA_DH
DA_EPS = 1e-5
DA_TQ = 1024
DA_TK = 1024
DA_CW = 512
DA_KEY_ALIGN = 256
DA_LAZY_LIMIT = 64.0
DA_REBASE = 8.0
LOG2E = 1.4426950408889634
N_GROUPS = 4
EPG = 8
N_EXPERTS = 32
D_EXPERT = 256
MOE_TR = 512
MOE_SUB = 256
HALF_W = D_MODEL // 4
SC_WINDOW = 128
LANES = 128

VMEM_LIMIT = 48 * 1024 * 1024


def _cparams(sem):
    return pltpu.CompilerParams(dimension_semantics=sem, vmem_limit_bytes=VMEM_LIMIT)


def _norm_matmul_vt_kernel(x_ref, g_ref, w_ref, o_ref, vt_ref, xn_ref, *, vt_block):
    j = pl.program_id(1)

    @pl.when(j == 0)
    def _():
        x = x_ref[...]
        ms = jnp.mean(x * x, axis=-1, keepdims=True)
        xn_ref[...] = (x * lax.rsqrt(ms + EPS) * g_ref[...]).astype(BF16)

    res = jnp.dot(xn_ref[...], w_ref[...], preferred_element_type=F32)
    o_ref[...] = res.astype(o_ref.dtype)

    @pl.when(j == vt_block)
    def _():
        vt_ref[...] = res.T.astype(vt_ref.dtype)


def _norm_matmul_vt(x, g, w, *, tm, tn, vt_block, batch, seq, out_dtype):
    t, d = x.shape
    n = w.shape[1]
    per_batch = seq // tm
    return pl.pallas_call(
        functools.partial(_norm_matmul_vt_kernel, vt_block=vt_block),
        out_shape=(jax.ShapeDtypeStruct((t, n), out_dtype),
                   jax.ShapeDtypeStruct((batch, tn, seq), out_dtype)),
        grid=(t // tm, n // tn),
        in_specs=[
            pl.BlockSpec((tm, d), lambda i, j: (i, 0)),
            pl.BlockSpec((1, d), lambda i, j: (0, 0)),
            pl.BlockSpec((d, tn), lambda i, j: (0, j)),
        ],
        out_specs=(pl.BlockSpec((tm, tn), lambda i, j: (i, j)),
                   pl.BlockSpec((None, tn, tm), lambda i, j: (i // per_batch, 0, i % per_batch))),
        scratch_shapes=[pltpu.VMEM((tm, d), BF16)],
        compiler_params=_cparams(("parallel", "arbitrary")),
        name="norm_matmul_vt",
    )(x, g, w)


def _matmul_res_kernel(a_ref, w_ref, r_ref, o_ref):
    o_ref[...] = r_ref[...] + jnp.dot(a_ref[...], w_ref[...], preferred_element_type=F32)


def _matmul_res(a, w, res, *, tm):
    t, k = a.shape
    n = w.shape[1]
    return pl.pallas_call(
        _matmul_res_kernel,
        out_shape=jax.ShapeDtypeStruct((t, n), F32),
        grid=(t // tm,),
        in_specs=[
            pl.BlockSpec((tm, k), lambda i: (i, 0)),
            pl.BlockSpec((k, n), lambda i: (0, 0)),
            pl.BlockSpec((tm, n), lambda i: (i, 0)),
        ],
        out_specs=pl.BlockSpec((tm, n), lambda i: (i, 0)),
        compiler_params=_cparams(("parallel",)),
        name="matmul_res",
    )(a, w, res)


def _log_sigmoid(x):
    return jnp.minimum(x, 0.0) - jnp.log1p(jnp.exp(-jnp.abs(x)))


def _ml_proj_kernel(x_ref, g_ref, w_ref, wc_ref, wr_ref, bc_ref, br_ref, o_ref, oc_ref, or_ref,
                    xn_ref):
    @pl.when(pl.program_id(1) == 0)
    def _():
        x = x_ref[...]
        ms = jnp.mean(x * x, axis=-1, keepdims=True)
        xn = (x * lax.rsqrt(ms + EPS) * g_ref[...]).astype(BF16)
        xn_ref[...] = xn
        gc = jnp.dot(xn, wc_ref[...], preferred_element_type=F32) + bc_ref[...]
        lane = lax.broadcasted_iota(jnp.int32, gc.shape, 1)
        oc_ref[...] = jnp.where(lane < ML_H, gc, _log_sigmoid(gc))
        gr = lax.dot_general(wr_ref[...], xn, (((1,), (1,)), ((), ())),
                             preferred_element_type=F32) + br_ref[...]
        row = lax.broadcasted_iota(jnp.int32, gr.shape, 0)
        or_ref[...] = jnp.where(row < ML_H, gr, _log_sigmoid(gr))

    o_ref[...] = jnp.dot(xn_ref[...], w_ref[...], preferred_element_type=F32).astype(o_ref.dtype)


def _ml_proj(x, g, w_in, b_i, b_f, *, tm, tn, out_dtype):
    t, d = x.shape
    ng = 2 * ML_H
    n_main = w_in.shape[1] - ng
    w_gates = w_in[:, n_main:]
    wc = jnp.zeros((d, LANES), BF16).at[:, :ng].set(w_gates.astype(BF16))
    wr = w_gates.T.astype(BF16)
    bias = jnp.concatenate([b_i, b_f]).astype(F32)
    bc = jnp.zeros((1, LANES), F32).at[0, :ng].set(bias)
    br = bias.reshape(ng, 1)
    return pl.pallas_call(
        _ml_proj_kernel,
        out_shape=(jax.ShapeDtypeStruct((t, n_main), out_dtype),
                   jax.ShapeDtypeStruct((t, LANES), F32), jax.ShapeDtypeStruct((ng, t), F32)),
        grid=(t // tm, n_main // tn),
        in_specs=[
            pl.BlockSpec((tm, d), lambda i, j: (i, 0)),
            pl.BlockSpec((1, d), lambda i, j: (0, 0)),
            pl.BlockSpec((d, tn), lambda i, j: (0, j)),
            pl.BlockSpec((d, LANES), lambda i, j: (0, 0)),
            pl.BlockSpec((ng, d), lambda i, j: (0, 0)),
            pl.BlockSpec((1, LANES), lambda i, j: (0, 0)),
            pl.BlockSpec((ng, 1), lambda i, j: (0, 0)),
        ],
        out_specs=(pl.BlockSpec((tm, tn), lambda i, j: (i, j)),
                   pl.BlockSpec((tm, LANES), lambda i, j: (i, 0)),
                   pl.BlockSpec((ng, tm), lambda i, j: (0, i))),
        scratch_shapes=[pltpu.VMEM((tm, d), BF16)],
        compiler_params=_cparams(("parallel", "arbitrary")),
        name="ml_proj",
    )(x, g, w_in.astype(BF16), wc, wr, bc, br)


def _split3(x):
    hi = x.astype(BF16)
    r = x - hi.astype(F32)
    mid = r.astype(BF16)
    lo = (r - mid.astype(F32)).astype(BF16)
    return hi, mid, lo


def _mlstm_kernel(q_ref, k_ref, v_ref, o_ref, gc_ref, gr_ref, cw_ref, ng_ref, y_ref,
                  tail_ref, ct_ref, n_ref, m_ref, shift_ref, tril_ref, triu_ref, neg_ref):
    L = ML_CHUNK
    c = pl.program_id(1)

    @pl.when(c == 0)
    def _():
        tail_ref[...] = jnp.zeros_like(tail_ref)
        ct_ref[...] = jnp.zeros_like(ct_ref)
        n_ref[...] = jnp.zeros_like(n_ref)
        m_ref[...] = jnp.zeros_like(m_ref)
        r_i = lax.broadcasted_iota(jnp.int32, (L, L), 0)
        c_i = lax.broadcasted_iota(jnp.int32, (L, L), 1)
        tril_ref[...] = (c_i <= r_i).astype(BF16)
        triu_ref[...] = (r_i <= c_i).astype(BF16)
        neg_ref[...] = jnp.where(c_i <= r_i, 0.0, -jnp.inf)
        for j in range(CONV_K - 1):
            shift_ref[j * L:(j + 1) * L, :] = (c_i == r_i - (CONV_K - 1 - j)).astype(BF16)

    u = jnp.concatenate([q_ref[...], k_ref[...]], axis=1)
    uf = u.astype(F32)
    shifted = jnp.dot(shift_ref[...], u, preferred_element_type=F32)
    conv = uf * cw_ref[CONV_K - 1:CONV_K, :]
    head = None
    for j in range(CONV_K - 1):
        conv = conv + shifted[j * L:(j + 1) * L, :] * cw_ref[j:j + 1, :]
        part = tail_ref[pl.ds(CONV_TAIL - (CONV_K - 1) + j, CONV_TAIL), :] * cw_ref[j:j + 1, :]
        head = part if head is None else head + part
    conv = jnp.concatenate([conv[:CONV_TAIL] + head, conv[CONV_TAIL:]], axis=0)
    tail_ref[0:CONV_TAIL, :] = uf[L - CONV_TAIL:, :]
    qk = conv * jax.nn.sigmoid(conv)
    q_all = (qk[:, :ML_QK] * (ML_DQK ** -0.5)).astype(BF16)
    k_all = qk[:, ML_QK:]

    tril = tril_ref[...]
    triu = triu_ref[...]
    gc = gc_ref[...]
    gr = gr_ref[...]
    bc_all = sum(jnp.dot(tril, p, preferred_element_type=F32) for p in _split3(gc))
    br_all = sum(jnp.dot(p, triu, preferred_element_type=F32) for p in _split3(gr))

    for h in range(ML_H):
        qh = q_all[:, h * ML_DQK:(h + 1) * ML_DQK]
        kh_f = k_all[:, h * ML_DQK:(h + 1) * ML_DQK]
        kh = kh_f.astype(BF16)
        vh = v_ref[:, h * ML_DV:(h + 1) * ML_DV]
        it_col = gc[:, h:h + 1]
        it_row = gr[h:h + 1, :]
        b_col = bc_all[:, ML_H + h:ML_H + h + 1]
        b_row = br_all[ML_H + h:ML_H + h + 1, :]
        m_prev = m_ref[h][:, 0:1]

        dmat = b_col + (it_row - b_row) + neg_ref[...]
        inter_log = b_col + m_prev
        m_t = jnp.maximum(inter_log, jnp.max(dmat, axis=1, keepdims=True))
        wts = jnp.exp(dmat - m_t)
        s = lax.dot_general(qh, kh, (((1,), (1,)), ((), ())), preferred_element_type=F32)
        sc = s * wts
        inter_scale = jnp.exp(inter_log - m_t)
        ct = ct_ref[h]
        num = (jnp.dot(sc.astype(BF16), vh, preferred_element_type=F32)
               + inter_scale * jnp.dot(qh, ct.astype(BF16), preferred_element_type=F32))
        n_row = n_ref[h]
        den = (jnp.sum(sc, axis=1, keepdims=True)
               + inter_scale * jnp.sum(qh.astype(F32) * n_row, axis=1, keepdims=True))
        h_out = num / jnp.maximum(jnp.abs(den), jnp.exp(-m_t))

        b_last = b_col[L - 1:L, :]
        lw_col = b_last - b_col + it_col
        lw_row = b_last - b_row + it_row
        m_new = jnp.maximum(b_last + m_prev, jnp.max(lw_row, axis=1, keepdims=True))
        ws_col = jnp.exp(lw_col - m_new)
        decay = jnp.exp(b_last + m_prev - m_new)
        kw = kh_f * ws_col
        ct_ref[h] = decay * ct + lax.dot_general(kw.astype(BF16), vh, (((0,), (0,)), ((), ())),
                                                  preferred_element_type=F32)
        n_ref[h] = decay * n_row + jnp.sum(kw, axis=0, keepdims=True)
        m_ref[h] = jnp.broadcast_to(m_new, (1, LANES))

        ms = jnp.mean(h_out * h_out, axis=1, keepdims=True)
        hn = h_out * lax.rsqrt(ms + EPS) * ng_ref[:, h * ML_DV:(h + 1) * ML_DV]
        og = o_ref[:, h * ML_DV:(h + 1) * ML_DV].astype(F32)
        y_ref[:, h * ML_DV:(h + 1) * ML_DV] = (hn * jax.nn.sigmoid(og)).astype(y_ref.dtype)


def _mlstm_core(p, gcol, grow, conv_w, norm_g, *, batch, seq):
    L = ML_CHUNK
    nc = seq // L
    p3 = p.reshape(batch, seq, 2 * ML_QK + 2 * ML_V)
    gc3 = gcol.reshape(batch, seq, LANES)
    gr3 = grow.reshape(2 * ML_H, batch, seq).transpose(1, 0, 2)
    y = pl.pallas_call(
        _mlstm_kernel,
        out_shape=jax.ShapeDtypeStruct((batch, seq, ML_V), BF16),
        grid=(batch, nc),
        in_specs=[
            pl.BlockSpec((None, L, ML_QK), lambda b, c: (b, c, 0)),
            pl.BlockSpec((None, L, ML_QK), lambda b, c: (b, c, 1)),
            pl.BlockSpec((None, L, ML_V), lambda b, c: (b, c, 1)),
            pl.BlockSpec((None, L, ML_V), lambda b, c: (b, c, 2)),
            pl.BlockSpec((None, L, LANES), lambda b, c: (b, c, 0)),
            pl.BlockSpec((None, 2 * ML_H, L), lambda b, c: (b, 0, c)),
            pl.BlockSpec((CONV_K, 2 * ML_QK), lambda b, c: (0, 0)),
            pl.BlockSpec((1, ML_V), lambda b, c: (0, 0)),
        ],
        out_specs=pl.BlockSpec((None, L, ML_V), lambda b, c: (b, c, 0)),
        scratch_shapes=[
            pltpu.VMEM((2 * CONV_TAIL, 2 * ML_QK), F32),
            pltpu.VMEM((ML_H, ML_DQK, ML_DV), F32),
            pltpu.VMEM((ML_H, 1, ML_DQK), F32),
            pltpu.VMEM((ML_H, 1, LANES), F32),
            pltpu.VMEM(((CONV_K - 1) * L, L), BF16),
            pltpu.VMEM((L, L), BF16),
            pltpu.VMEM((L, L), BF16),
            pltpu.VMEM((L, L), F32),
        ],
        compiler_params=_cparams(("parallel", "arbitrary")),
        name="mlstm_core",
    )(p3, p3, p3, p3, gc3, gr3, conv_w, norm_g)
    return y.reshape(batch * seq, ML_V)


def _diff_attn_kernel(q_ref, k_ref, vt_ref, lp_ref, ng_ref, o_ref, q2_ref, r_ref, m_ref, l_ref,
                      acc_ref, *, lambda_init):
    tq, tk, cw = DA_TQ, DA_TK, DA_CW
    qi = pl.program_id(2)
    q = q_ref[...].astype(F32) * (DA_DH ** -0.5 * LOG2E)
    lane = lax.broadcasted_iota(jnp.int32, q.shape, 1)
    q2_ref[0:tq, :] = jnp.where(lane < DA_DH, q, 0.0).astype(BF16)
    q2_ref[tq:2 * tq, :] = jnp.where(lane >= DA_DH, q, 0.0).astype(BF16)

    r_ref[...] = jnp.full(r_ref.shape, -jnp.inf, F32)
    m_ref[...] = jnp.full(m_ref.shape, -jnp.inf, F32)
    l_ref[...] = jnp.zeros_like(l_ref)
    acc_ref[...] = jnp.zeros_like(acc_ref)

    nch = 2 * tq // cw

    def keys_needed(c, diag):
        if diag is None:
            return tk
        visible = (c * cw) % tq + cw - diag * tk
        return max(0, min(tk, -(-visible // DA_KEY_ALIGN) * DA_KEY_ALIGN))

    def mask(s, c, diag):
        nk = s.shape[0]
        key = lax.broadcasted_iota(jnp.int32, (nk, cw), 0) + diag * tk
        qry = lax.broadcasted_iota(jnp.int32, (nk, cw), 1) + (c * cw) % tq
        return jnp.where(key <= qry, s, -jnp.inf)

    def scores(j, c, diag=None):
        nk = keys_needed(c, diag)
        if nk == 0:
            return None
        kb = k_ref[pl.ds(pl.multiple_of(j * tk, tk), nk), :]
        s = lax.dot_general(kb, q2_ref[c * cw:(c + 1) * cw, :], (((1,), (1,)), ((), ())),
                            preferred_element_type=F32)
        if diag is not None:
            s = mask(s, c, diag)
        return s, jnp.max(s, axis=0, keepdims=True)


    def exact_block(j, diag=None):
        start = pl.multiple_of(j * tk, tk)
        nxt = scores(j, 0, diag)
        for c in range(nch):
            cs = slice(c * cw, (c + 1) * cw)
            s, bmax = nxt
            if c + 1 < nch:
                nxt = scores(j, c + 1, diag)
            vbt = vt_ref[:, pl.ds(start, s.shape[0])]
            m_new = jnp.maximum(m_ref[:, cs], bmax)
            alpha = jnp.exp2(r_ref[:, cs] - m_new)
            p = jnp.exp2(s - m_new)
            l_ref[:, cs] = alpha * l_ref[:, cs] + jnp.sum(p, axis=0, keepdims=True)
            acc_ref[:, cs] = alpha * acc_ref[:, cs] + jnp.dot(vbt, p.astype(BF16),
                                                               preferred_element_type=F32)
            m_ref[:, cs] = m_new
            r_ref[:, cs] = m_new

    def fast_block(j, diag=None):
        start = pl.multiple_of(j * tk, tk)
        nxt = scores(j, 0, diag)
        tent = []
        for c in range(nch):
            cs = slice(c * cw, (c + 1) * cw)
            s, bmax = nxt
            if c + 1 < nch:
                nxt = scores(j, c + 1, diag)
            vbt = vt_ref[:, pl.ds(start, s.shape[0])]
            r_old = r_ref[:, cs]
            p = jnp.exp2(s - r_old)
            tent.append((cs, bmax, r_old, jnp.sum(p, axis=0, keepdims=True),
                         jnp.dot(vbt, p.astype(BF16), preferred_element_type=F32)))
        excess = functools.reduce(jnp.maximum, [jnp.max(bmax - r_old, axis=1, keepdims=True)
                                                for _, bmax, r_old, _, _ in tent])
        safe = excess <= DA_LAZY_LIMIT
        for cs, bmax, r_old, lt, acct in tent:
            m_old = m_ref[:, cs]
            m_new = jnp.maximum(m_old, bmax)
            r_new = jnp.where(m_new - r_old > DA_REBASE, m_new, r_old)
            scale = jnp.exp2(r_old - r_new)
            l_old = l_ref[:, cs]
            acc_old = acc_ref[:, cs]
            l_ref[:, cs] = jnp.where(safe, (l_old + lt) * scale, l_old)
            acc_ref[:, cs] = jnp.where(safe, (acc_old + acct) * scale, acc_old)
            m_ref[:, cs] = jnp.where(safe, m_new, m_old)
            r_ref[:, cs] = jnp.where(safe, r_new, r_old)

        @pl.when(jnp.logical_not(jnp.max(excess) <= DA_LAZY_LIMIT))
        def _():
            exact_block(j, diag)

    def exact_body(j, carry):
        exact_block(j)
        return carry

    def fast_body(j, carry):
        fast_block(j)
        return carry

    assert tq == tk
    lax.fori_loop(0, jnp.minimum(qi, 1), exact_body, 0)
    lax.fori_loop(1, qi, fast_body, 0)

    @pl.when(qi > 0)
    def _():
        fast_block(qi, diag=0)

    @pl.when(qi == 0)
    def _():
        exact_block(qi, diag=0)


    lp = lp_ref[...]
    lam = (jnp.exp(jnp.sum(lp[0:1, :] * lp[1:2, :], axis=1, keepdims=True))
           - jnp.exp(jnp.sum(lp[2:3, :] * lp[3:4, :], axis=1, keepdims=True)) + lambda_init)
    out = acc_ref[...] / l_ref[...]
    o = out[:, :tq] - lam * out[:, tq:]
    ms = jnp.mean(o * o, axis=0, keepdims=True)
    on = o * lax.rsqrt(ms + DA_EPS) * ng_ref[...] * (1.0 - lambda_init)
    o_ref[...] = on.T.astype(o_ref.dtype)


def _diff_attn(p, vt, lam_params, norm_g, lambda_init, *, batch, seq):
    tq = DA_TQ
    p3 = p.reshape(batch, seq, 2 * DA_QK + DA_H * DA_DV)
    kern = functools.partial(_diff_attn_kernel, lambda_init=lambda_init)
    o = pl.pallas_call(
        kern,
        out_shape=jax.ShapeDtypeStruct((batch, seq, DA_H * DA_DV), BF16),
        grid=(batch, DA_H, seq // tq),
        in_specs=[
            pl.BlockSpec((None, tq, 2 * DA_DH), lambda b, h, i: (b, i, h)),
            pl.BlockSpec((None, seq, 2 * DA_DH), lambda b, h, i: (b, 0, DA_H + h)),
            pl.BlockSpec((None, DA_DV, seq), lambda b, h, i: (b, h, 0)),
            pl.BlockSpec((4, DA_DH), lambda b, h, i: (0, 0)),
            pl.BlockSpec((DA_DV, 1), lambda b, h, i: (0, 0)),
        ],
        out_specs=pl.BlockSpec((None, tq, DA_DV), lambda b, h, i: (b, i, h)),
        scratch_shapes=[
            pltpu.VMEM((2 * tq, 2 * DA_DH), BF16),
            pltpu.VMEM((1, 2 * tq), F32),
            pltpu.VMEM((1, 2 * tq), F32),
            pltpu.VMEM((1, 2 * tq), F32),
            pltpu.VMEM((DA_DV, 2 * tq), F32),
        ],
        compiler_params=_cparams(("parallel", "parallel", "arbitrary")),
        name="diff_attn",
    )(p3, p3, vt, lam_params, norm_g)
    return o.reshape(batch * seq, DA_H * DA_DV)


def _pack_halves(y):
    halves = []
    for h in range(2):
        base = h * 2 * HALF_W
        lo = y[:, base:base + HALF_W].astype(BF16).astype(F32)
        hi = y[:, base + HALF_W:base + 2 * HALF_W].astype(BF16).astype(F32)
        lo_bits = lax.bitcast_convert_type(lo, U32) >> 16
        hi_bits = lax.bitcast_convert_type(hi, U32)
        halves.append(hi_bits | lo_bits)
    return halves


def _unpack_halves(w0, w1):
    parts = []
    for w in (w0, w1):
        parts.append(lax.bitcast_convert_type(w << 16, F32))
        parts.append(lax.bitcast_convert_type(w & jnp.uint32(0xFFFF0000), F32))
    return jnp.concatenate(parts, axis=1)


def _router_kernel(x_ref, g_ref, w_ref, b_ref, hp_ref, meta_ref, metat_ref, cnt_ref,
                   run_ref, ls_ref):
    tm = x_ref.shape[0]

    @pl.when(pl.program_id(0) == 0)
    def _():
        run_ref[...] = jnp.zeros_like(run_ref)
        r_i = lax.broadcasted_iota(jnp.int32, (tm, tm), 0)
        c_i = lax.broadcasted_iota(jnp.int32, (tm, tm), 1)
        ls_ref[...] = (c_i < r_i).astype(BF16)

    x = x_ref[...]
    ms = jnp.mean(x * x, axis=-1, keepdims=True)
    hn32 = x * lax.rsqrt(ms + EPS) * g_ref[...]
    hn = hn32.astype(BF16)
    halves = _pack_halves(hn32)
    hp_ref[0] = halves[0]
    hp_ref[1] = halves[1]
    lg = jnp.dot(hn, w_ref[...], preferred_element_type=F32) + b_ref[...]
    lane = lax.broadcasted_iota(jnp.int32, lg.shape, 1)
    neg = -jnp.inf

    gmask = (lane >= N_EXPERTS) & (lane < N_EXPERTS + N_GROUPS)
    gl = jnp.where(gmask, lg, neg)
    gmax = jnp.max(gl, axis=1, keepdims=True)
    gidx = jnp.min(jnp.where(gl == gmax, lane, LANES), axis=1, keepdims=True) - N_EXPERTS
    gsum = jnp.sum(jnp.where(gmask, jnp.exp(gl - gmax), 0.0), axis=1, keepdims=True)
    g_w = 1.0 / gsum

    emask = (lane >= gidx * EPG) & (lane < gidx * EPG + EPG)
    el = jnp.where(emask, lg, neg)
    emax = jnp.max(el, axis=1, keepdims=True)
    eexp = jnp.where(emask, jnp.exp(el - emax), 0.0)
    ep = eexp / jnp.sum(eexp, axis=1, keepdims=True)
    ep = jnp.where(emask, ep, -1.0)
    p1 = jnp.max(ep, axis=1, keepdims=True)
    i1 = jnp.min(jnp.where(ep == p1, lane, LANES), axis=1, keepdims=True)
    ep2 = jnp.where(lane == i1, -1.0, ep)
    p2 = jnp.max(ep2, axis=1, keepdims=True)
    i2 = jnp.min(jnp.where(ep2 == p2, lane, LANES), axis=1, keepdims=True)
    wsum = p1 + p2
    w1 = g_w * (p1 / wsum)
    w2 = g_w * (p2 / wsum)

    a1 = lane == i1
    a2 = lane == i2
    onehot = (a1 | a2).astype(BF16)
    before = jnp.dot(ls_ref[...], onehot, preferred_element_type=F32) + run_ref[...]
    rank1 = jnp.sum(jnp.where(a1, before, 0.0), axis=1, keepdims=True)
    rank2 = jnp.sum(jnp.where(a2, before, 0.0), axis=1, keepdims=True)
    run_ref[...] += jnp.sum(onehot.astype(F32), axis=0, keepdims=True)
    cnt_ref[...] = run_ref[...]

    meta = (jnp.where(lane == 0, i1.astype(F32), 0.0) + jnp.where(lane == 1, i2.astype(F32), 0.0)
            + jnp.where(lane == 2, rank1, 0.0) + jnp.where(lane == 3, rank2, 0.0)
            + jnp.where(lane == 4, w1, 0.0) + jnp.where(lane == 5, w2, 0.0))
    meta_ref[...] = meta
    metat_ref[...] = meta.T[0:8, :]


def _router(x, g, w_group, b_group, w_expert, b_expert, *, tm):
    t, d = x.shape
    w = jnp.zeros((d, LANES), BF16)
    w = w.at[:, :N_EXPERTS].set(w_expert.astype(BF16))
    w = w.at[:, N_EXPERTS:N_EXPERTS + N_GROUPS].set(w_group.astype(BF16))
    b = jnp.zeros((1, LANES), F32)
    b = b.at[0, :N_EXPERTS].set(b_expert.astype(F32))
    b = b.at[0, N_EXPERTS:N_EXPERTS + N_GROUPS].set(b_group.astype(F32))
    return pl.pallas_call(
        _router_kernel,
        out_shape=(jax.ShapeDtypeStruct((2, t, HALF_W), U32),
                   jax.ShapeDtypeStruct((t, LANES), F32),
                   jax.ShapeDtypeStruct((8, t), F32),
                   jax.ShapeDtypeStruct((1, LANES), F32)),
        grid=(t // tm,),
        in_specs=[
            pl.BlockSpec((tm, d), lambda i: (i, 0)),
            pl.BlockSpec((1, d), lambda i: (0, 0)),
            pl.BlockSpec((d, LANES), lambda i: (0, 0)),
            pl.BlockSpec((1, LANES), lambda i: (0, 0)),
        ],
        out_specs=(pl.BlockSpec((2, tm, HALF_W), lambda i: (0, i, 0)),
                   pl.BlockSpec((tm, LANES), lambda i: (i, 0)),
                   pl.BlockSpec((8, tm), lambda i: (0, i)),
                   pl.BlockSpec((1, LANES), lambda i: (0, 0))),
        scratch_shapes=[pltpu.VMEM((1, LANES), F32), pltpu.VMEM((tm, tm), BF16)],
        compiler_params=_cparams(("arbitrary",)),
        name="moe_router",
    )(x, g, w, b)


def _sc_mesh():
    return plsc.VectorSubcoreMesh(core_axis_name="c", subcore_axis_name="s")


def _sc_scatter2(x, i0, i1, n_out):
    n, d = x.shape

    @pl.kernel(out_type=jax.ShapeDtypeStruct((n_out, d), x.dtype), mesh=_sc_mesh())
    def k(x_hbm, i0_hbm, i1_hbm, o_hbm):
        def body(x_vmem, i0_vmem, i1_vmem):
            pltpu.sync_copy(x_vmem, o_hbm.at[i0_vmem.at[0]])
            pltpu.sync_copy(x_vmem, o_hbm.at[i1_vmem.at[0]])

        pltpu.emit_pipeline(
            body,
            grid=(n // SC_WINDOW,),
            in_specs=[pl.BlockSpec((SC_WINDOW, d), lambda i: (i, 0)),
                      pl.BlockSpec((1, SC_WINDOW), lambda i: (0, i)),
                      pl.BlockSpec((1, SC_WINDOW), lambda i: (0, i))],
            out_specs=[],
            core_axis_name=("c", "s"),
            dimension_semantics=(pltpu.PARALLEL,),
        )(x_hbm, i0_hbm, i1_hbm)

    return k(x, i0.reshape(1, n), i1.reshape(1, n))


def _sc_gather(x, idx):
    n = idx.shape[0]
    d = x.shape[1]

    @pl.kernel(out_type=jax.ShapeDtypeStruct((n, d), x.dtype), mesh=_sc_mesh())
    def k(x_hbm, i_hbm, o_hbm):
        def body(i_vmem, o_vmem):
            pltpu.sync_copy(x_hbm.at[i_vmem.at[0]], o_vmem)

        pltpu.emit_pipeline(
            body,
            grid=(n // SC_WINDOW,),
            in_specs=[pl.BlockSpec((1, SC_WINDOW), lambda i: (0, i))],
            out_specs=[pl.BlockSpec((SC_WINDOW, d), lambda i: (i, 0))],
            core_axis_name=("c", "s"),
            dimension_semantics=(pltpu.PARALLEL,),
        )(i_hbm, o_hbm)

    return k(x, idx.reshape(1, n))


def _experts_kernel(te_ref, nu_ref, xs_ref, wgu_ref, wd_ref, ys_ref, wgu_bf_ref, wd_bf_ref):
    j = pl.program_id(0)

    @pl.when(j < nu_ref[0])
    def _():
        @pl.when((j == 0) | (te_ref[j] != te_ref[jnp.maximum(j - 1, 0)]))
        def _():
            wgu_bf_ref[...] = wgu_ref[...].astype(BF16)
            wd_bf_ref[...] = wd_ref[...].astype(BF16)

        subs = [slice(i * MOE_SUB, (i + 1) * MOE_SUB) for i in range(MOE_TR // MOE_SUB)]
        xs = [_unpack_halves(xs_ref[0, r, :], xs_ref[1, r, :]).astype(BF16) for r in subs]
        gus = [jnp.dot(x, wgu_bf_ref[...], preferred_element_type=F32) for x in xs]
        acts = [(gu[:, :D_EXPERT] * jax.nn.sigmoid(gu[:, :D_EXPERT]) * gu[:, D_EXPERT:]).astype(BF16)
                for gu in gus]
        ys = [jnp.dot(a, wd_bf_ref[...], preferred_element_type=F32) for a in acts]
        for r, y in zip(subs, ys):
            halves = _pack_halves(y)
            ys_ref[0, r, :] = halves[0]
            ys_ref[1, r, :] = halves[1]


def _experts(xs, tile_expert, n_used, w_gu, w_down, layer):
    _, rows, _ = xs.shape
    d = w_gu.shape[2]
    return pl.pallas_call(
        _experts_kernel,
        out_shape=jax.ShapeDtypeStruct(xs.shape, U32),
        grid_spec=pltpu.PrefetchScalarGridSpec(
            num_scalar_prefetch=2,
            grid=(rows // MOE_TR,),
            in_specs=[
                pl.BlockSpec((2, MOE_TR, HALF_W), lambda j, te, nu: (0, j, 0)),
                pl.BlockSpec((None, None, d, 2 * D_EXPERT),
                             lambda j, te, nu: (layer, te[j], 0, 0)),
                pl.BlockSpec((None, None, D_EXPERT, d),
                             lambda j, te, nu: (layer, te[j], 0, 0)),
            ],
            out_specs=pl.BlockSpec((2, MOE_TR, HALF_W), lambda j, te, nu: (0, j, 0)),
            scratch_shapes=[pltpu.VMEM((d, 2 * D_EXPERT), BF16), pltpu.VMEM((D_EXPERT, d), BF16)],
        ),
        compiler_params=_cparams(("arbitrary",)),
        name="moe_experts",
    )(tile_expert, n_used, xs, w_gu, w_down)


def _combine_kernel(x_ref, z_ref, meta_ref, g_ref, o_ref, *, final_norm):
    meta = meta_ref[...]
    y_a = _unpack_halves(z_ref[0], z_ref[2])
    y_b = _unpack_halves(z_ref[1], z_ref[3])
    out = x_ref[...] + meta[:, 4:5] * y_a + meta[:, 5:6] * y_b
    if final_norm:
        ms = jnp.mean(out * out, axis=-1, keepdims=True)
        out = out * lax.rsqrt(ms + EPS) * g_ref[...]
    o_ref[...] = out


def _combine(x, z, meta, g, *, tm, final_norm):
    t, d = x.shape
    return pl.pallas_call(
        functools.partial(_combine_kernel, final_norm=final_norm),
        out_shape=jax.ShapeDtypeStruct((t, d), F32),
        grid=(t // tm,),
        in_specs=[
            pl.BlockSpec((tm, d), lambda i: (i, 0)),
            pl.BlockSpec((4, tm, HALF_W), lambda i: (0, i, 0)),
            pl.BlockSpec((tm, LANES), lambda i: (i, 0)),
            pl.BlockSpec((1, d), lambda i: (0, 0)),
        ],
        out_specs=pl.BlockSpec((tm, d), lambda i: (i, 0)),
        compiler_params=_cparams(("parallel",)),
        name="moe_combine",
    )(x, z, meta, g)


def _moe_layer(x, norm_g, w_group, b_group, w_expert, b_expert, w_gu, w_down, layer,
               final_g=None):
    t, d = x.shape
    hp, meta, metat, cnt = _router(x, norm_g.reshape(1, d), w_group, b_group, w_expert, b_expert,
                                   tm=1024)
    n_tiles = 2 * t // MOE_TR + N_EXPERTS
    rows = n_tiles * MOE_TR
    counts = cnt[0, :N_EXPERTS].astype(jnp.int32)
    tiles_e = (counts + MOE_TR - 1) // MOE_TR
    tiles_end = jnp.cumsum(tiles_e)
    row_off = (tiles_end - tiles_e) * MOE_TR
    experts = jnp.arange(N_EXPERTS, dtype=jnp.int32)[:, None]

    def region_start(e_row):
        return jnp.sum(jnp.where(e_row[None, :] == experts, row_off[:, None], 0), axis=0)

    e_a, e_b = metat[0].astype(jnp.int32), metat[1].astype(jnp.int32)
    pos_a = region_start(e_a) + metat[2].astype(jnp.int32)
    pos_b = region_start(e_b) + metat[3].astype(jnp.int32)
    tile_ids = jnp.arange(n_tiles, dtype=jnp.int32)
    tile_expert = jnp.minimum(
        jnp.sum((tile_ids[:, None] >= tiles_end[None, :]).astype(jnp.int32), axis=1),
        N_EXPERTS - 1)
    n_used = tiles_end[-1:].astype(jnp.int32)

    xs = _sc_scatter2(hp.reshape(2 * t, HALF_W),
                      jnp.concatenate([pos_a, pos_a + rows]),
                      jnp.concatenate([pos_b, pos_b + rows]), 2 * rows)
    ys = _experts(xs.reshape(2, rows, HALF_W), tile_expert, n_used,
                  w_gu, w_down, layer)
    z = _sc_gather(ys.reshape(2 * rows, HALF_W),
                   jnp.concatenate([pos_a, pos_b, pos_a + rows, pos_b + rows]))
    g = jnp.ones((1, d), F32) if final_g is None else final_g.reshape(1, d)
    return _combine(x, z.reshape(4, t, HALF_W), meta, g, tm=1024, final_norm=final_g is not None)


def kernel(x, norm_mix, norm_ffn, ml_w_in, ml_conv, ml_b_i, ml_b_f, ml_norm, ml_w_out, da_w_in, da_lq1, da_lk1, da_lq2, da_lk2, da_norm, da_w_out, moe_w_group, moe_b_group, moe_w_expert, moe_b_expert, moe_w_gu, moe_w_down, final_norm):
    batch, seq, d = x.shape
    xt = x.reshape(batch * seq, d)

    p, gcol, grow = _ml_proj(xt, norm_mix[0].reshape(1, d), ml_w_in[0], ml_b_i[0], ml_b_f[0],
                             tm=1024, tn=1024, out_dtype=BF16)
    assert p.shape[1] == 2 * ML_QK + 2 * ML_V
    y = _mlstm_core(p, gcol, grow, ml_conv[0], ml_norm[0].reshape(1, ML_V), batch=batch, seq=seq)
    xt = _matmul_res(y, ml_w_out[0].astype(BF16), xt, tm=512)
    xt = _moe_layer(xt, norm_ffn[0], moe_w_group[0], moe_b_group[0], moe_w_expert[0],
                    moe_b_expert[0], moe_w_gu, moe_w_down, 0)

    lambda_init = 0.8 - 0.6 * math.exp(-0.3 * 1)
    p, vt = _norm_matmul_vt(xt, norm_mix[1].reshape(1, d), da_w_in[0].astype(BF16), tm=1024,
                            tn=DA_H * DA_DV, vt_block=2 * DA_QK // (DA_H * DA_DV), batch=batch,
                            seq=seq, out_dtype=BF16)
    lam_params = jnp.stack([da_lq1[0], da_lk1[0], da_lq2[0], da_lk2[0]]).astype(F32)
    a = _diff_attn(p, vt, lam_params, da_norm[0].reshape(DA_DV, 1), lambda_init, batch=batch,
                   seq=seq)
    xt = _matmul_res(a, da_w_out[0].astype(BF16), xt, tm=512)
    out = _moe_layer(xt, norm_ffn[1], moe_w_group[1], moe_b_group[1], moe_w_expert[1],
                     moe_b_expert[1], moe_w_gu, moe_w_down, 1, final_g=final_norm)
    return out.reshape(batch, seq, d)
```

```python
import functools
import math

import jax
import jax.numpy as jnp
from jax import lax
from jax.experimental import pallas as pl
from jax.experimental.pallas import tpu as pltpu
from jax.experimental.pallas import tpu_sc as plsc

F32 = jnp.float32
BF16 = jnp.bfloat16
U32 = jnp.uint32

D_MODEL = 1024
EPS = 1e-6
ML_H = 4
ML_DV = 512
ML_DQK = 256
ML_QK = ML_H * ML_DQK
ML_V = ML_H * ML_DV
CONV_K = 4
ML_CHUNK = 256
CONV_TAIL = 8
DA_H = 8
DA_DH = 64
DA_DV = 128
DA_QK = DA_H * 2 * DA_DH
DA_EPS = 1e-5
DA_TQ = 1024
DA_TK = 1024
DA_CW = 512
DA_KEY_ALIGN = 256
DA_LAZY_LIMIT = 64.0
DA_REBASE = 8.0
LOG2E = 1.4426950408889634
N_GROUPS = 4
EPG = 8
N_EXPERTS = 32
D_EXPERT = 256
MOE_TR = 512
MOE_SUB = 256
HALF_W = D_MODEL // 4
SC_WINDOW = 128
LANES = 128

VMEM_LIMIT = 48 * 1024 * 1024


def _cparams(sem):
    return pltpu.CompilerParams(dimension_semantics=sem, vmem_limit_bytes=VMEM_LIMIT)


def _norm_matmul_vt_kernel(x_ref, g_ref, w_ref, o_ref, vt_ref, xn_ref, *, vt_block):
    j = pl.program_id(1)

    @pl.when(j == 0)
    def _():
        x = x_ref[...]
        ms = jnp.mean(x * x, axis=-1, keepdims=True)
        xn_ref[...] = (x * lax.rsqrt(ms + EPS) * g_ref[...]).astype(BF16)

    res = jnp.dot(xn_ref[...], w_ref[...], preferred_element_type=F32)
    o_ref[...] = res.astype(o_ref.dtype)

    @pl.when(j == vt_block)
    def _():
        vt_ref[...] = res.T.astype(vt_ref.dtype)


def _norm_matmul_vt(x, g, w, *, tm, tn, vt_block, batch, seq, out_dtype):
    t, d = x.shape
    n = w.shape[1]
    per_batch = seq // tm
    return pl.pallas_call(
        functools.partial(_norm_matmul_vt_kernel, vt_block=vt_block),
        out_shape=(jax.ShapeDtypeStruct((t, n), out_dtype),
                   jax.ShapeDtypeStruct((batch, tn, seq), out_dtype)),
        grid=(t // tm, n // tn),
        in_specs=[
            pl.BlockSpec((tm, d), lambda i, j: (i, 0)),
            pl.BlockSpec((1, d), lambda i, j: (0, 0)),
            pl.BlockSpec((d, tn), lambda i, j: (0, j)),
        ],
        out_specs=(pl.BlockSpec((tm, tn), lambda i, j: (i, j)),
                   pl.BlockSpec((None, tn, tm), lambda i, j: (i // per_batch, 0, i % per_batch))),
        scratch_shapes=[pltpu.VMEM((tm, d), BF16)],
        compiler_params=_cparams(("parallel", "arbitrary")),
        name="norm_matmul_vt",
    )(x, g, w)


def _matmul_res_kernel(a_ref, w_ref, r_ref, o_ref):
    o_ref[...] = r_ref[...] + jnp.dot(a_ref[...], w_ref[...], preferred_element_type=F32)


def _matmul_res(a, w, res, *, tm):
    t, k = a.shape
    n = w.shape[1]
    return pl.pallas_call(
        _matmul_res_kernel,
        out_shape=jax.ShapeDtypeStruct((t, n), F32),
        grid=(t // tm,),
        in_specs=[
            pl.BlockSpec((tm, k), lambda i: (i, 0)),
            pl.BlockSpec((k, n), lambda i: (0, 0)),
            pl.BlockSpec((tm, n), lambda i: (i, 0)),
        ],
        out_specs=pl.BlockSpec((tm, n), lambda i: (i, 0)),
        compiler_params=_cparams(("parallel",)),
        name="matmul_res",
    )(a, w, res)


def _log_sigmoid(x):
    return jnp.minimum(x, 0.0) - jnp.log1p(jnp.exp(-jnp.abs(x)))


def _ml_proj_kernel(x_ref, g_ref, w_ref, wc_ref, wr_ref, bc_ref, br_ref, o_ref, oc_ref, or_ref,
                    xn_ref):
    @pl.when(pl.program_id(1) == 0)
    def _():
        x = x_ref[...]
        ms = jnp.mean(x * x, axis=-1, keepdims=True)
        xn = (x * lax.rsqrt(ms + EPS) * g_ref[...]).astype(BF16)
        xn_ref[...] = xn
        gc = jnp.dot(xn, wc_ref[...], preferred_element_type=F32) + bc_ref[...]
        lane = lax.broadcasted_iota(jnp.int32, gc.shape, 1)
        oc_ref[...] = jnp.where(lane < ML_H, gc, _log_sigmoid(gc))
        gr = lax.dot_general(wr_ref[...], xn, (((1,), (1,)), ((), ())),
                             preferred_element_type=F32) + br_ref[...]
        row = lax.broadcasted_iota(jnp.int32, gr.shape, 0)
        or_ref[...] = jnp.where(row < ML_H, gr, _log_sigmoid(gr))

    o_ref[...] = jnp.dot(xn_ref[...], w_ref[...], preferred_element_type=F32).astype(o_ref.dtype)


def _ml_proj(x, g, w_in, b_i, b_f, *, tm, tn, out_dtype):
    t, d = x.shape
    ng = 2 * ML_H
    n_main = w_in.shape[1] - ng
    w_gates = w_in[:, n_main:]
    wc = jnp.zeros((d, LANES), BF16).at[:, :ng].set(w_gates.astype(BF16))
    wr = w_gates.T.astype(BF16)
    bias = jnp.concatenate([b_i, b_f]).astype(F32)
    bc = jnp.zeros((1, LANES), F32).at[0, :ng].set(bias)
    br = bias.reshape(ng, 1)
    return pl.pallas_call(
        _ml_proj_kernel,
        out_shape=(jax.ShapeDtypeStruct((t, n_main), out_dtype),
                   jax.ShapeDtypeStruct((t, LANES), F32), jax.ShapeDtypeStruct((ng, t), F32)),
        grid=(t // tm, n_main // tn),
        in_specs=[
            pl.BlockSpec((tm, d), lambda i, j: (i, 0)),
            pl.BlockSpec((1, d), lambda i, j: (0, 0)),
            pl.BlockSpec((d, tn), lambda i, j: (0, j)),
            pl.BlockSpec((d, LANES), lambda i, j: (0, 0)),
            pl.BlockSpec((ng, d), lambda i, j: (0, 0)),
            pl.BlockSpec((1, LANES), lambda i, j: (0, 0)),
            pl.BlockSpec((ng, 1), lambda i, j: (0, 0)),
        ],
        out_specs=(pl.BlockSpec((tm, tn), lambda i, j: (i, j)),
                   pl.BlockSpec((tm, LANES), lambda i, j: (i, 0)),
                   pl.BlockSpec((ng, tm), lambda i, j: (0, i))),
        scratch_shapes=[pltpu.VMEM((tm, d), BF16)],
        compiler_params=_cparams(("parallel", "arbitrary")),
        name="ml_proj",
    )(x, g, w_in.astype(BF16), wc, wr, bc, br)


def _split3(x):
    hi = x.astype(BF16)
    r = x - hi.astype(F32)
    mid = r.astype(BF16)
    lo = (r - mid.astype(F32)).astype(BF16)
    return hi, mid, lo


def _mlstm_kernel(q_ref, k_ref, v_ref, o_ref, gc_ref, gr_ref, cw_ref, ng_ref, y_ref,
                  tail_ref, ct_ref, n_ref, m_ref, shift_ref, tril_ref, triu_ref, neg_ref):
    L = ML_CHUNK
    c = pl.program_id(1)

    @pl.when(c == 0)
    def _():
        tail_ref[...] = jnp.zeros_like(tail_ref)
        ct_ref[...] = jnp.zeros_like(ct_ref)
        n_ref[...] = jnp.zeros_like(n_ref)
        m_ref[...] = jnp.zeros_like(m_ref)
        r_i = lax.broadcasted_iota(jnp.int32, (L, L), 0)
        c_i = lax.broadcasted_iota(jnp.int32, (L, L), 1)
        tril_ref[...] = (c_i <= r_i).astype(BF16)
        triu_ref[...] = (r_i <= c_i).astype(BF16)
        neg_ref[...] = jnp.where(c_i <= r_i, 0.0, -jnp.inf)
        for j in range(CONV_K - 1):
            shift_ref[j * L:(j + 1) * L, :] = (c_i == r_i - (CONV_K - 1 - j)).astype(BF16)

    u = jnp.concatenate([q_ref[...], k_ref[...]], axis=1)
    uf = u.astype(F32)
    shifted = jnp.dot(shift_ref[...], u, preferred_element_type=F32)
    conv = uf * cw_ref[CONV_K - 1:CONV_K, :]
    head = None
    for j in range(CONV_K - 1):
        conv = conv + shifted[j * L:(j + 1) * L, :] * cw_ref[j:j + 1, :]
        part = tail_ref[pl.ds(CONV_TAIL - (CONV_K - 1) + j, CONV_TAIL), :] * cw_ref[j:j + 1, :]
        head = part if head is None else head + part
    conv = jnp.concatenate([conv[:CONV_TAIL] + head, conv[CONV_TAIL:]], axis=0)
    tail_ref[0:CONV_TAIL, :] = uf[L - CONV_TAIL:, :]
    qk = conv * jax.nn.sigmoid(conv)
    q_all = (qk[:, :ML_QK] * (ML_DQK ** -0.5)).astype(BF16)
    k_all = qk[:, ML_QK:]

    tril = tril_ref[...]
    triu = triu_ref[...]
    gc = gc_ref[...]
    gr = gr_ref[...]
    bc_all = sum(jnp.dot(tril, p, preferred_element_type=F32) for p in _split3(gc))
    br_all = sum(jnp.dot(p, triu, preferred_element_type=F32) for p in _split3(gr))

    for h in range(ML_H):
        qh = q_all[:, h * ML_DQK:(h + 1) * ML_DQK]
        kh_f = k_all[:, h * ML_DQK:(h + 1) * ML_DQK]
        kh = kh_f.astype(BF16)
        vh = v_ref[:, h * ML_DV:(h + 1) * ML_DV]
        it_col = gc[:, h:h + 1]
        it_row = gr[h:h + 1, :]
        b_col = bc_all[:, ML_H + h:ML_H + h + 1]
        b_row = br_all[ML_H + h:ML_H + h + 1, :]
        m_prev = m_ref[h][:, 0:1]

        dmat = b_col + (it_row - b_row) + neg_ref[...]
        inter_log = b_col + m_prev
        m_t = jnp.maximum(inter_log, jnp.max(dmat, axis=1, keepdims=True))
        wts = jnp.exp(dmat - m_t)
        s = lax.dot_general(qh, kh, (((1,), (1,)), ((), ())), preferred_element_type=F32)
        sc = s * wts
        inter_scale = jnp.exp(inter_log - m_t)
        ct = ct_ref[h]
        num = (jnp.dot(sc.astype(BF16), vh, preferred_element_type=F32)
               + inter_scale * jnp.dot(qh, ct.astype(BF16), preferred_element_type=F32))
        n_row = n_ref[h]
        den = (jnp.sum(sc, axis=1, keepdims=True)
               + inter_scale * jnp.sum(qh.astype(F32) * n_row, axis=1, keepdims=True))
        h_out = num / jnp.maximum(jnp.abs(den), jnp.exp(-m_t))

        b_last = b_col[L - 1:L, :]
        lw_col = b_last - b_col + it_col
        lw_row = b_last - b_row + it_row
        m_new = jnp.maximum(b_last + m_prev, jnp.max(lw_row, axis=1, keepdims=True))
        ws_col = jnp.exp(lw_col - m_new)
        decay = jnp.exp(b_last + m_prev - m_new)
        kw = kh_f * ws_col
        ct_ref[h] = decay * ct + lax.dot_general(kw.astype(BF16), vh, (((0,), (0,)), ((), ())),
                                                  preferred_element_type=F32)
        n_ref[h] = decay * n_row + jnp.sum(kw, axis=0, keepdims=True)
        m_ref[h] = jnp.broadcast_to(m_new, (1, LANES))

        ms = jnp.mean(h_out * h_out, axis=1, keepdims=True)
        hn = h_out * lax.rsqrt(ms + EPS) * ng_ref[:, h * ML_DV:(h + 1) * ML_DV]
        og = o_ref[:, h * ML_DV:(h + 1) * ML_DV].astype(F32)
        y_ref[:, h * ML_DV:(h + 1) * ML_DV] = (hn * jax.nn.sigmoid(og)).astype(y_ref.dtype)


def _mlstm_core(p, gcol, grow, conv_w, norm_g, *, batch, seq):
    L = ML_CHUNK
    nc = seq // L
    p3 = p.reshape(batch, seq, 2 * ML_QK + 2 * ML_V)
    gc3 = gcol.reshape(batch, seq, LANES)
    gr3 = grow.reshape(2 * ML_H, batch, seq).transpose(1, 0, 2)
    y = pl.pallas_call(
        _mlstm_kernel,
        out_shape=jax.ShapeDtypeStruct((batch, seq, ML_V), BF16),
        grid=(batch, nc),
        in_specs=[
            pl.BlockSpec((None, L, ML_QK), lambda b, c: (b, c, 0)),
            pl.BlockSpec((None, L, ML_QK), lambda b, c: (b, c, 1)),
            pl.BlockSpec((None, L, ML_V), lambda b, c: (b, c, 1)),
            pl.BlockSpec((None, L, ML_V), lambda b, c: (b, c, 2)),
            pl.BlockSpec((None, L, LANES), lambda b, c: (b, c, 0)),
            pl.BlockSpec((None, 2 * ML_H, L), lambda b, c: (b, 0, c)),
            pl.BlockSpec((CONV_K, 2 * ML_QK), lambda b, c: (0, 0)),
            pl.BlockSpec((1, ML_V), lambda b, c: (0, 0)),
        ],
        out_specs=pl.BlockSpec((None, L, ML_V), lambda b, c: (b, c, 0)),
        scratch_shapes=[
            pltpu.VMEM((2 * CONV_TAIL, 2 * ML_QK), F32),
            pltpu.VMEM((ML_H, ML_DQK, ML_DV), F32),
            pltpu.VMEM((ML_H, 1, ML_DQK), F32),
            pltpu.VMEM((ML_H, 1, LANES), F32),
            pltpu.VMEM(((CONV_K - 1) * L, L), BF16),
            pltpu.VMEM((L, L), BF16),
            pltpu.VMEM((L, L), BF16),
            pltpu.VMEM((L, L), F32),
        ],
        compiler_params=_cparams(("parallel", "arbitrary")),
        name="mlstm_core",
    )(p3, p3, p3, p3, gc3, gr3, conv_w, norm_g)
    return y.reshape(batch * seq, ML_V)


def _diff_attn_kernel(q_ref, k_ref, vt_ref, lp_ref, ng_ref, o_ref, q2_ref, r_ref, m_ref, l_ref,
                      acc_ref, *, lambda_init):
    tq, tk, cw = DA_TQ, DA_TK, DA_CW
    qi = pl.program_id(2)
    q = q_ref[...].astype(F32) * (DA_DH ** -0.5 * LOG2E)
    lane = lax.broadcasted_iota(jnp.int32, q.shape, 1)
    q2_ref[0:tq, :] = jnp.where(lane < DA_DH, q, 0.0).astype(BF16)
    q2_ref[tq:2 * tq, :] = jnp.where(lane >= DA_DH, q, 0.0).astype(BF16)

    r_ref[...] = jnp.zeros_like(r_ref)
    m_ref[...] = jnp.full(m_ref.shape, -jnp.inf, F32)
    l_ref[...] = jnp.zeros_like(l_ref)
    acc_ref[...] = jnp.zeros_like(acc_ref)

    nch = 2 * tq // cw

    def keys_needed(c, diag):
        if diag is None:
            return tk
        visible = (c * cw) % tq + cw - diag * tk
        return max(0, min(tk, -(-visible // DA_KEY_ALIGN) * DA_KEY_ALIGN))

    def mask(s, c, diag):
        nk = s.shape[0]
        key = lax.broadcasted_iota(jnp.int32, (nk, cw), 0) + diag * tk
        qry = lax.broadcasted_iota(jnp.int32, (nk, cw), 1) + (c * cw) % tq
        return jnp.where(key <= qry, s, -jnp.inf)

    def scores(j, c, diag=None):
        nk = keys_needed(c, diag)
        if nk == 0:
            return None
        kb = k_ref[pl.ds(pl.multiple_of(j * tk, tk), nk), :]
        s = lax.dot_general(kb, q2_ref[c * cw:(c + 1) * cw, :], (((1,), (1,)), ((), ())),
                            preferred_element_type=F32)
        if diag is not None:
            s = mask(s, c, diag)
        return s, jnp.max(s, axis=0, keepdims=True)


    def exact_block(j, diag=None):
        start = pl.multiple_of(j * tk, tk)
        nxt = scores(j, 0, diag)
        for c in range(nch):
            cs = slice(c * cw, (c + 1) * cw)
            s, bmax = nxt
            if c + 1 < nch:
                nxt = scores(j, c + 1, diag)
            vbt = vt_ref[:, pl.ds(start, s.shape[0])]
            m_old = m_ref[:, cs]
            m_new = jnp.maximum(m_old, bmax)
            alpha = jnp.where(m_old == -jnp.inf, 0.0, jnp.exp2(r_ref[:, cs] - m_new))
            p = jnp.exp2(s - m_new)
            l_ref[:, cs] = alpha * l_ref[:, cs] + jnp.sum(p, axis=0, keepdims=True)
            acc_ref[:, cs] = alpha * acc_ref[:, cs] + jnp.dot(vbt, p.astype(BF16),
                                                               preferred_element_type=F32)
            m_ref[:, cs] = m_new
            r_ref[:, cs] = m_new

    def fast_block(j, diag=None):
        start = pl.multiple_of(j * tk, tk)
        nxt = scores(j, 0, diag)
        tent = []
        for c in range(nch):
            cs = slice(c * cw, (c + 1) * cw)
            s, bmax = nxt
            if c + 1 < nch:
                nxt = scores(j, c + 1, diag)
            vbt = vt_ref[:, pl.ds(start, s.shape[0])]
            r_old = r_ref[:, cs]
            p = jnp.exp2(s - r_old)
            tent.append((cs, bmax, r_old, m_ref[:, cs], jnp.sum(p, axis=0, keepdims=True),
                         jnp.dot(vbt, p.astype(BF16), preferred_element_type=F32)))
        worst = functools.reduce(jnp.maximum, [
            jnp.max(jnp.maximum(bmax - r_old,
                                jnp.where(m_old == -jnp.inf, r_old - bmax, -jnp.inf)),
                    axis=1, keepdims=True)
            for _, bmax, r_old, m_old, _, _ in tent])
        safe = worst <= DA_LAZY_LIMIT
        for cs, bmax, r_old, m_old, lt, acct in tent:
            m_new = jnp.maximum(m_old, bmax)
            r_new = jnp.where(jnp.abs(m_new - r_old) > DA_REBASE, m_new, r_old)
            scale = jnp.exp2(r_old - r_new)
            l_old = l_ref[:, cs]
            acc_old = acc_ref[:, cs]
            l_ref[:, cs] = jnp.where(safe, (l_old + lt) * scale, l_old)
            acc_ref[:, cs] = jnp.where(safe, (acc_old + acct) * scale, acc_old)
            m_ref[:, cs] = jnp.where(safe, m_new, m_old)
            r_ref[:, cs] = jnp.where(safe, r_new, r_old)

        @pl.when(jnp.logical_not(jnp.max(worst) <= DA_LAZY_LIMIT))
        def _():
            exact_block(j, diag)

    def fast_body(j, carry):
        fast_block(j)
        return carry

    assert tq == tk
    lax.fori_loop(0, qi, fast_body, 0)
    fast_block(qi, diag=0)


    lp = lp_ref[...]
    lam = (jnp.exp(jnp.sum(lp[0:1, :] * lp[1:2, :], axis=1, keepdims=True))
           - jnp.exp(jnp.sum(lp[2:3, :] * lp[3:4, :], axis=1, keepdims=True)) + lambda_init)
    out = acc_ref[...] / l_ref[...]
    o = out[:, :tq] - lam * out[:, tq:]
    ms = jnp.mean(o * o, axis=0, keepdims=True)
    on = o * lax.rsqrt(ms + DA_EPS) * ng_ref[...] * (1.0 - lambda_init)
    o_ref[...] = on.T.astype(o_ref.dtype)


def _diff_attn(p, vt, lam_params, norm_g, lambda_init, *, batch, seq):
    tq = DA_TQ
    p3 = p.reshape(batch, seq, 2 * DA_QK + DA_H * DA_DV)
    kern = functools.partial(_diff_attn_kernel, lambda_init=lambda_init)
    o = pl.pallas_call(
        kern,
        out_shape=jax.ShapeDtypeStruct((batch, seq, DA_H * DA_DV), BF16),
        grid=(batch, DA_H, seq // tq),
        in_specs=[
            pl.BlockSpec((None, tq, 2 * DA_DH), lambda b, h, i: (b, i, h)),
            pl.BlockSpec((None, seq, 2 * DA_DH), lambda b, h, i: (b, 0, DA_H + h)),
            pl.BlockSpec((None, DA_DV, seq), lambda b, h, i: (b, h, 0)),
            pl.BlockSpec((4, DA_DH), lambda b, h, i: (0, 0)),
            pl.BlockSpec((DA_DV, 1), lambda b, h, i: (0, 0)),
        ],
        out_specs=pl.BlockSpec((None, tq, DA_DV), lambda b, h, i: (b, i, h)),
        scratch_shapes=[
            pltpu.VMEM((2 * tq, 2 * DA_DH), BF16),
            pltpu.VMEM((1, 2 * tq), F32),
            pltpu.VMEM((1, 2 * tq), F32),
            pltpu.VMEM((1, 2 * tq), F32),
            pltpu.VMEM((DA_DV, 2 * tq), F32),
        ],
        compiler_params=_cparams(("parallel", "parallel", "arbitrary")),
        name="diff_attn",
    )(p3, p3, vt, lam_params, norm_g)
    return o.reshape(batch * seq, DA_H * DA_DV)


def _pack_halves(y):
    halves = []
    for h in range(2):
        base = h * 2 * HALF_W
        lo = y[:, base:base + HALF_W].astype(BF16).astype(F32)
        hi = y[:, base + HALF_W:base + 2 * HALF_W].astype(BF16).astype(F32)
        lo_bits = lax.bitcast_convert_type(lo, U32) >> 16
        hi_bits = lax.bitcast_convert_type(hi, U32)
        halves.append(hi_bits | lo_bits)
    return halves


def _unpack_halves(w0, w1):
    parts = []
    for w in (w0, w1):
        parts.append(lax.bitcast_convert_type(w << 16, F32))
        parts.append(lax.bitcast_convert_type(w & jnp.uint32(0xFFFF0000), F32))
    return jnp.concatenate(parts, axis=1)


def _router_kernel(x_ref, g_ref, w_ref, b_ref, hp_ref, meta_ref, metat_ref, cnt_ref,
                   run_ref, ls_ref):
    tm = x_ref.shape[0]

    @pl.when(pl.program_id(0) == 0)
    def _():
        run_ref[...] = jnp.zeros_like(run_ref)
        r_i = lax.broadcasted_iota(jnp.int32, (tm, tm), 0)
        c_i = lax.broadcasted_iota(jnp.int32, (tm, tm), 1)
        ls_ref[...] = (c_i < r_i).astype(BF16)

    x = x_ref[...]
    ms = jnp.mean(x * x, axis=-1, keepdims=True)
    hn32 = x * lax.rsqrt(ms + EPS) * g_ref[...]
    hn = hn32.astype(BF16)
    halves = _pack_halves(hn32)
    hp_ref[0] = halves[0]
    hp_ref[1] = halves[1]
    lg = jnp.dot(hn, w_ref[...], preferred_element_type=F32) + b_ref[...]
    lane = lax.broadcasted_iota(jnp.int32, lg.shape, 1).astype(F32)
    neg = -jnp.inf

    gmask = (lane >= N_EXPERTS) & (lane < N_EXPERTS + N_GROUPS)
    gl = jnp.where(gmask, lg, neg)
    gmax = jnp.max(gl, axis=1, keepdims=True)
    gidx = jnp.min(jnp.where(gl == gmax, lane, float(LANES)), axis=1, keepdims=True) - N_EXPERTS
    gsum = jnp.sum(jnp.where(gmask, jnp.exp(gl - gmax), 0.0), axis=1, keepdims=True)
    g_w = 1.0 / gsum

    emask = (lane >= gidx * EPG) & (lane < gidx * EPG + EPG)
    el = jnp.where(emask, lg, neg)
    emax = jnp.max(el, axis=1, keepdims=True)
    eexp = jnp.where(emask, jnp.exp(el - emax), 0.0)
    ep = eexp / jnp.sum(eexp, axis=1, keepdims=True)
    ep = jnp.where(emask, ep, -1.0)
    p1 = jnp.max(ep, axis=1, keepdims=True)
    i1 = jnp.min(jnp.where(ep == p1, lane, float(LANES)), axis=1, keepdims=True)
    ep2 = jnp.where(lane == i1, -1.0, ep)
    p2 = jnp.max(ep2, axis=1, keepdims=True)
    i2 = jnp.min(jnp.where(ep2 == p2, lane, float(LANES)), axis=1, keepdims=True)
    wsum = p1 + p2
    w1 = g_w * (p1 / wsum)
    w2 = g_w * (p2 / wsum)

    a1 = lane == i1
    a2 = lane == i2
    onehot = (a1 | a2).astype(BF16)
    before = jnp.dot(ls_ref[...], onehot, preferred_element_type=F32) + run_ref[...]
    rank1 = jnp.sum(jnp.where(a1, before, 0.0), axis=1, keepdims=True)
    rank2 = jnp.sum(jnp.where(a2, before, 0.0), axis=1, keepdims=True)
    run_ref[...] += jnp.sum(onehot.astype(F32), axis=0, keepdims=True)
    cnt_ref[...] = run_ref[...]

    meta = (jnp.where(lane == 0, i1, 0.0) + jnp.where(lane == 1, i2, 0.0)
            + jnp.where(lane == 2, rank1, 0.0) + jnp.where(lane == 3, rank2, 0.0)
            + jnp.where(lane == 4, w1, 0.0) + jnp.where(lane == 5, w2, 0.0))
    meta_ref[...] = meta
    metat_ref[...] = meta.T[0:8, :]


def _router(x, g, w_group, b_group, w_expert, b_expert, *, tm):
    t, d = x.shape
    w = jnp.zeros((d, LANES), BF16)
    w = w.at[:, :N_EXPERTS].set(w_expert.astype(BF16))
    w = w.at[:, N_EXPERTS:N_EXPERTS + N_GROUPS].set(w_group.astype(BF16))
    b = jnp.zeros((1, LANES), F32)
    b = b.at[0, :N_EXPERTS].set(b_expert.astype(F32))
    b = b.at[0, N_EXPERTS:N_EXPERTS + N_GROUPS].set(b_group.astype(F32))
    return pl.pallas_call(
        _router_kernel,
        out_shape=(jax.ShapeDtypeStruct((2, t, HALF_W), U32),
                   jax.ShapeDtypeStruct((t, LANES), F32),
                   jax.ShapeDtypeStruct((8, t), F32),
                   jax.ShapeDtypeStruct((1, LANES), F32)),
        grid=(t // tm,),
        in_specs=[
            pl.BlockSpec((tm, d), lambda i: (i, 0)),
            pl.BlockSpec((1, d), lambda i: (0, 0)),
            pl.BlockSpec((d, LANES), lambda i: (0, 0)),
            pl.BlockSpec((1, LANES), lambda i: (0, 0)),
        ],
        out_specs=(pl.BlockSpec((2, tm, HALF_W), lambda i: (0, i, 0)),
                   pl.BlockSpec((tm, LANES), lambda i: (i, 0)),
                   pl.BlockSpec((8, tm), lambda i: (0, i)),
                   pl.BlockSpec((1, LANES), lambda i: (0, 0))),
        scratch_shapes=[pltpu.VMEM((1, LANES), F32), pltpu.VMEM((tm, tm), BF16)],
        compiler_params=_cparams(("arbitrary",)),
        name="moe_router",
    )(x, g, w, b)


def _sc_mesh():
    return plsc.VectorSubcoreMesh(core_axis_name="c", subcore_axis_name="s")


def _sc_scatter2(x, i0, i1, n_out):
    n, d = x.shape

    @pl.kernel(out_type=jax.ShapeDtypeStruct((n_out, d), x.dtype), mesh=_sc_mesh())
    def k(x_hbm, i0_hbm, i1_hbm, o_hbm):
        def body(x_vmem, i0_vmem, i1_vmem):
            pltpu.sync_copy(x_vmem, o_hbm.at[i0_vmem.at[0]])
            pltpu.sync_copy(x_vmem, o_hbm.at[i1_vmem.at[0]])

        pltpu.emit_pipeline(
            body,
            grid=(n // SC_WINDOW,),
            in_specs=[pl.BlockSpec((SC_WINDOW, d), lambda i: (i, 0)),
                      pl.BlockSpec((1, SC_WINDOW), lambda i: (0, i)),
                      pl.BlockSpec((1, SC_WINDOW), lambda i: (0, i))],
            out_specs=[],
            core_axis_name=("c", "s"),
            dimension_semantics=(pltpu.PARALLEL,),
        )(x_hbm, i0_hbm, i1_hbm)

    return k(x, i0.reshape(1, n), i1.reshape(1, n))


def _sc_gather(x, idx):
    n = idx.shape[0]
    d = x.shape[1]

    @pl.kernel(out_type=jax.ShapeDtypeStruct((n, d), x.dtype), mesh=_sc_mesh())
    def k(x_hbm, i_hbm, o_hbm):
        def body(i_vmem, o_vmem):
            pltpu.sync_copy(x_hbm.at[i_vmem.at[0]], o_vmem)

        pltpu.emit_pipeline(
            body,
            grid=(n // SC_WINDOW,),
            in_specs=[pl.BlockSpec((1, SC_WINDOW), lambda i: (0, i))],
            out_specs=[pl.BlockSpec((SC_WINDOW, d), lambda i: (i, 0))],
            core_axis_name=("c", "s"),
            dimension_semantics=(pltpu.PARALLEL,),
        )(i_hbm, o_hbm)

    return k(x, idx.reshape(1, n))


def _experts_kernel(te_ref, nu_ref, xs_ref, wgu_ref, wd_ref, ys_ref, wgu_bf_ref, wd_bf_ref):
    j = pl.program_id(0)

    @pl.when(j < nu_ref[0])
    def _():
        @pl.when((j == 0) | (te_ref[j] != te_ref[jnp.maximum(j - 1, 0)]))
        def _():
            wgu_bf_ref[...] = wgu_ref[...].astype(BF16)
            wd_bf_ref[...] = wd_ref[...].astype(BF16)

        subs = [slice(i * MOE_SUB, (i + 1) * MOE_SUB) for i in range(MOE_TR // MOE_SUB)]
        xs = [_unpack_halves(xs_ref[0, r, :], xs_ref[1, r, :]).astype(BF16) for r in subs]
        gus = [jnp.dot(x, wgu_bf_ref[...], preferred_element_type=F32) for x in xs]
        acts = [(gu[:, :D_EXPERT] * jax.nn.sigmoid(gu[:, :D_EXPERT]) * gu[:, D_EXPERT:]).astype(BF16)
                for gu in gus]
        ys = [jnp.dot(a, wd_bf_ref[...], preferred_element_type=F32) for a in acts]
        for r, y in zip(subs, ys):
            halves = _pack_halves(y)
            ys_ref[0, r, :] = halves[0]
            ys_ref[1, r, :] = halves[1]


def _experts(xs, tile_expert, n_used, w_gu, w_down, layer):
    _, rows, _ = xs.shape
    d = w_gu.shape[2]
    return pl.pallas_call(
        _experts_kernel,
        out_shape=jax.ShapeDtypeStruct(xs.shape, U32),
        grid_spec=pltpu.PrefetchScalarGridSpec(
            num_scalar_prefetch=2,
            grid=(rows // MOE_TR,),
            in_specs=[
                pl.BlockSpec((2, MOE_TR, HALF_W), lambda j, te, nu: (0, j, 0)),
                pl.BlockSpec((None, None, d, 2 * D_EXPERT),
                             lambda j, te, nu: (layer, te[j], 0, 0)),
                pl.BlockSpec((None, None, D_EXPERT, d),
                             lambda j, te, nu: (layer, te[j], 0, 0)),
            ],
            out_specs=pl.BlockSpec((2, MOE_TR, HALF_W), lambda j, te, nu: (0, j, 0)),
            scratch_shapes=[pltpu.VMEM((d, 2 * D_EXPERT), BF16), pltpu.VMEM((D_EXPERT, d), BF16)],
        ),
        compiler_params=_cparams(("arbitrary",)),
        name="moe_experts",
    )(tile_expert, n_used, xs, w_gu, w_down)


def _combine_kernel(x_ref, z_ref, meta_ref, g_ref, o_ref, *, final_norm):
    meta = meta_ref[...]
    y_a = _unpack_halves(z_ref[0], z_ref[2])
    y_b = _unpack_halves(z_ref[1], z_ref[3])
    out = x_ref[...] + meta[:, 4:5] * y_a + meta[:, 5:6] * y_b
    if final_norm:
        ms = jnp.mean(out * out, axis=-1, keepdims=True)
        out = out * lax.rsqrt(ms + EPS) * g_ref[...]
    o_ref[...] = out


def _combine(x, z, meta, g, *, tm, final_norm):
    t, d = x.shape
    return pl.pallas_call(
        functools.partial(_combine_kernel, final_norm=final_norm),
        out_shape=jax.ShapeDtypeStruct((t, d), F32),
        grid=(t // tm,),
        in_specs=[
            pl.BlockSpec((tm, d), lambda i: (i, 0)),
            pl.BlockSpec((4, tm, HALF_W), lambda i: (0, i, 0)),
            pl.BlockSpec((tm, LANES), lambda i: (i, 0)),
            pl.BlockSpec((1, d), lambda i: (0, 0)),
        ],
        out_specs=pl.BlockSpec((tm, d), lambda i: (i, 0)),
        compiler_params=_cparams(("parallel",)),
        name="moe_combine",
    )(x, z, meta, g)


def _moe_layer(x, norm_g, w_group, b_group, w_expert, b_expert, w_gu, w_down, layer,
               final_g=None):
    t, d = x.shape
    hp, meta, metat, cnt = _router(x, norm_g.reshape(1, d), w_group, b_group, w_expert, b_expert,
                                   tm=1024)
    n_tiles = 2 * t // MOE_TR + N_EXPERTS
    rows = n_tiles * MOE_TR
    counts = cnt[0, :N_EXPERTS].astype(jnp.int32)
    tiles_e = (counts + MOE_TR - 1) // MOE_TR
    tiles_end = jnp.cumsum(tiles_e)
    row_off = (tiles_end - tiles_e) * MOE_TR
    experts = jnp.arange(N_EXPERTS, dtype=jnp.int32)[:, None]

    def region_start(e_row):
        return jnp.sum(jnp.where(e_row[None, :] == experts, row_off[:, None], 0), axis=0)

    e_a, e_b = metat[0].astype(jnp.int32), metat[1].astype(jnp.int32)
    pos_a = region_start(e_a) + metat[2].astype(jnp.int32)
    pos_b = region_start(e_b) + metat[3].astype(jnp.int32)
    tile_ids = jnp.arange(n_tiles, dtype=jnp.int32)
    tile_expert = jnp.minimum(
        jnp.sum((tile_ids[:, None] >= tiles_end[None, :]).astype(jnp.int32), axis=1),
        N_EXPERTS - 1)
    n_used = tiles_end[-1:].astype(jnp.int32)

    xs = _sc_scatter2(hp.reshape(2 * t, HALF_W),
                      jnp.concatenate([pos_a, pos_a + rows]),
                      jnp.concatenate([pos_b, pos_b + rows]), 2 * rows)
    ys = _experts(xs.reshape(2, rows, HALF_W), tile_expert, n_used,
                  w_gu, w_down, layer)
    z = _sc_gather(ys.reshape(2 * rows, HALF_W),
                   jnp.concatenate([pos_a, pos_b, pos_a + rows, pos_b + rows]))
    g = jnp.ones((1, d), F32) if final_g is None else final_g.reshape(1, d)
    return _combine(x, z.reshape(4, t, HALF_W), meta, g, tm=1024, final_norm=final_g is not None)


def kernel(x, norm_mix, norm_ffn, ml_w_in, ml_conv, ml_b_i, ml_b_f, ml_norm, ml_w_out, da_w_in, da_lq1, da_lk1, da_lq2, da_lk2, da_norm, da_w_out, moe_w_group, moe_b_group, moe_w_expert, moe_b_expert, moe_w_gu, moe_w_down, final_norm):
    batch, seq, d = x.shape
    xt = x.reshape(batch * seq, d)

    p, gcol, grow = _ml_proj(xt, norm_mix[0].reshape(1, d), ml_w_in[0], ml_b_i[0], ml_b_f[0],
                             tm=1024, tn=1024, out_dtype=BF16)
    assert p.shape[1] == 2 * ML_QK + 2 * ML_V
    y = _mlstm_core(p, gcol, grow, ml_conv[0], ml_norm[0].reshape(1, ML_V), batch=batch, seq=seq)
    xt = _matmul_res(y, ml_w_out[0].astype(BF16), xt, tm=512)
    xt = _moe_layer(xt, norm_ffn[0], moe_w_group[0], moe_b_group[0], moe_w_expert[0],
                    moe_b_expert[0], moe_w_gu, moe_w_down, 0)

    lambda_init = 0.8 - 0.6 * math.exp(-0.3 * 1)
    p, vt = _norm_matmul_vt(xt, norm_mix[1].reshape(1, d), da_w_in[0].astype(BF16), tm=1024,
                            tn=DA_H * DA_DV, vt_block=2 * DA_QK // (DA_H * DA_DV), batch=batch,
                            seq=seq, out_dtype=BF16)
    lam_params = jnp.stack([da_lq1[0], da_lk1[0], da_lq2[0], da_lk2[0]]).astype(F32)
    a = _diff_attn(p, vt, lam_params, da_norm[0].reshape(DA_DV, 1), lambda_init, batch=batch,
                   seq=seq)
    xt = _matmul_res(a, da_w_out[0].astype(BF16), xt, tm=512)
    out = _moe_layer(xt, norm_ffn[1], moe_w_group[1], moe_b_group[1], moe_w_expert[1],
                     moe_b_expert[1], moe_w_gu, moe_w_down, 1, final_g=final_norm)
    return out.reshape(batch, seq, d)
```

```python
import functools
import math

import jax
import jax.numpy as jnp
from jax import lax
from jax.experimental import pallas as pl
from jax.experimental.pallas import tpu as pltpu
from jax.experimental.pallas import tpu_sc as plsc

F32 = jnp.float32
BF16 = jnp.bfloat16
U32 = jnp.uint32

D_MODEL = 1024
EPS = 1e-6
ML_H = 4
ML_DV = 512
ML_DQK = 256
ML_QK = ML_H * ML_DQK
ML_V = ML_H * ML_DV
CONV_K = 4
ML_CHUNK = 256
CONV_TAIL = 8
DA_H = 8
DA_DH = 64
DA_DV = 128
DA_QK = DA_H * 2 * DA_DH
DA_EPS = 1e-5
DA_TQ = 1024
DA_TK = 1024
DA_CW = 512
DA_KEY_ALIGN = 256
DA_LAZY_LIMIT = 64.0
DA_REBASE = 8.0
LOG2E = 1.4426950408889634
N_GROUPS = 4
EPG = 8
N_EXPERTS = 32
D_EXPERT = 256
RR_TM = 512
RR_SUB = 256
MOE_TR = 512
MOE_SUB = 256
HALF_W = D_MODEL // 4
SC_WINDOW = 128
LANES = 128

VMEM_LIMIT = 48 * 1024 * 1024


def _cparams(sem):
    return pltpu.CompilerParams(dimension_semantics=sem, vmem_limit_bytes=VMEM_LIMIT)


def _norm_matmul_vt_kernel(x_ref, g_ref, w_ref, o_ref, vt_ref, xn_ref, *, vt_block):
    j = pl.program_id(1)

    @pl.when(j == 0)
    def _():
        x = x_ref[...]
        ms = jnp.mean(x * x, axis=-1, keepdims=True)
        xn_ref[...] = (x * lax.rsqrt(ms + EPS) * g_ref[...]).astype(BF16)

    res = jnp.dot(xn_ref[...], w_ref[...], preferred_element_type=F32)
    o_ref[...] = res.astype(o_ref.dtype)

    @pl.when(j == vt_block)
    def _():
        vt_ref[...] = res.T.astype(vt_ref.dtype)


def _norm_matmul_vt(x, g, w, *, tm, tn, vt_block, batch, seq, out_dtype):
    t, d = x.shape
    n = w.shape[1]
    per_batch = seq // tm
    return pl.pallas_call(
        functools.partial(_norm_matmul_vt_kernel, vt_block=vt_block),
        out_shape=(jax.ShapeDtypeStruct((t, n), out_dtype),
                   jax.ShapeDtypeStruct((batch, tn, seq), out_dtype)),
        grid=(t // tm, n // tn),
        in_specs=[
            pl.BlockSpec((tm, d), lambda i, j: (i, 0)),
            pl.BlockSpec((1, d), lambda i, j: (0, 0)),
            pl.BlockSpec((d, tn), lambda i, j: (0, j)),
        ],
        out_specs=(pl.BlockSpec((tm, tn), lambda i, j: (i, j)),
                   pl.BlockSpec((None, tn, tm), lambda i, j: (i // per_batch, 0, i % per_batch))),
        scratch_shapes=[pltpu.VMEM((tm, d), BF16)],
        compiler_params=_cparams(("parallel", "arbitrary")),
        name="norm_matmul_vt",
    )(x, g, w)


def _log_sigmoid(x):
    return jnp.minimum(x, 0.0) - jnp.log1p(jnp.exp(-jnp.abs(x)))


def _ml_proj_kernel(x_ref, g_ref, w_ref, wc_ref, wr_ref, bc_ref, br_ref, o_ref, oc_ref, or_ref,
                    xn_ref):
    @pl.when(pl.program_id(1) == 0)
    def _():
        x = x_ref[...]
        ms = jnp.mean(x * x, axis=-1, keepdims=True)
        xn = (x * lax.rsqrt(ms + EPS) * g_ref[...]).astype(BF16)
        xn_ref[...] = xn
        gc = jnp.dot(xn, wc_ref[...], preferred_element_type=F32) + bc_ref[...]
        lane = lax.broadcasted_iota(jnp.int32, gc.shape, 1)
        oc_ref[...] = jnp.where(lane < ML_H, gc, _log_sigmoid(gc))
        gr = lax.dot_general(wr_ref[...], xn, (((1,), (1,)), ((), ())),
                             preferred_element_type=F32) + br_ref[...]
        row = lax.broadcasted_iota(jnp.int32, gr.shape, 0)
        or_ref[...] = jnp.where(row < ML_H, gr, _log_sigmoid(gr))

    o_ref[...] = jnp.dot(xn_ref[...], w_ref[...], preferred_element_type=F32).astype(o_ref.dtype)


def _ml_proj(x, g, w_in, b_i, b_f, *, tm, tn, out_dtype):
    t, d = x.shape
    ng = 2 * ML_H
    n_main = w_in.shape[1] - ng
    w_gates = w_in[:, n_main:]
    wc = jnp.zeros((d, LANES), BF16).at[:, :ng].set(w_gates.astype(BF16))
    wr = w_gates.T.astype(BF16)
    bias = jnp.concatenate([b_i, b_f]).astype(F32)
    bc = jnp.zeros((1, LANES), F32).at[0, :ng].set(bias)
    br = bias.reshape(ng, 1)
    return pl.pallas_call(
        _ml_proj_kernel,
        out_shape=(jax.ShapeDtypeStruct((t, n_main), out_dtype),
                   jax.ShapeDtypeStruct((t, LANES), F32), jax.ShapeDtypeStruct((ng, t), F32)),
        grid=(t // tm, n_main // tn),
        in_specs=[
            pl.BlockSpec((tm, d), lambda i, j: (i, 0)),
            pl.BlockSpec((1, d), lambda i, j: (0, 0)),
            pl.BlockSpec((d, tn), lambda i, j: (0, j)),
            pl.BlockSpec((d, LANES), lambda i, j: (0, 0)),
            pl.BlockSpec((ng, d), lambda i, j: (0, 0)),
            pl.BlockSpec((1, LANES), lambda i, j: (0, 0)),
            pl.BlockSpec((ng, 1), lambda i, j: (0, 0)),
        ],
        out_specs=(pl.BlockSpec((tm, tn), lambda i, j: (i, j)),
                   pl.BlockSpec((tm, LANES), lambda i, j: (i, 0)),
                   pl.BlockSpec((ng, tm), lambda i, j: (0, i))),
        scratch_shapes=[pltpu.VMEM((tm, d), BF16)],
        compiler_params=_cparams(("parallel", "arbitrary")),
        name="ml_proj",
    )(x, g, w_in.astype(BF16), wc, wr, bc, br)


def _split3(x):
    hi = x.astype(BF16)
    r = x - hi.astype(F32)
    mid = r.astype(BF16)
    lo = (r - mid.astype(F32)).astype(BF16)
    return hi, mid, lo


def _mlstm_kernel(q_ref, k_ref, v_ref, o_ref, gc_ref, gr_ref, cw_ref, ng_ref, y_ref,
                  tail_ref, ct_ref, n_ref, m_ref, shift_ref, tril_ref, triu_ref, neg_ref):
    L = ML_CHUNK
    c = pl.program_id(1)

    @pl.when(c == 0)
    def _():
        tail_ref[...] = jnp.zeros_like(tail_ref)
        ct_ref[...] = jnp.zeros_like(ct_ref)
        n_ref[...] = jnp.zeros_like(n_ref)
        m_ref[...] = jnp.zeros_like(m_ref)
        r_i = lax.broadcasted_iota(jnp.int32, (L, L), 0)
        c_i = lax.broadcasted_iota(jnp.int32, (L, L), 1)
        tril_ref[...] = (c_i <= r_i).astype(BF16)
        triu_ref[...] = (r_i <= c_i).astype(BF16)
        neg_ref[...] = jnp.where(c_i <= r_i, 0.0, -jnp.inf)
        for j in range(CONV_K - 1):
            shift_ref[j * L:(j + 1) * L, :] = (c_i == r_i - (CONV_K - 1 - j)).astype(BF16)

    u = jnp.concatenate([q_ref[...], k_ref[...]], axis=1)
    uf = u.astype(F32)
    shifted = jnp.dot(shift_ref[...], u, preferred_element_type=F32)
    conv = uf * cw_ref[CONV_K - 1:CONV_K, :]
    head = None
    for j in range(CONV_K - 1):
        conv = conv + shifted[j * L:(j + 1) * L, :] * cw_ref[j:j + 1, :]
        part = tail_ref[pl.ds(CONV_TAIL - (CONV_K - 1) + j, CONV_TAIL), :] * cw_ref[j:j + 1, :]
        head = part if head is None else head + part
    conv = jnp.concatenate([conv[:CONV_TAIL] + head, conv[CONV_TAIL:]], axis=0)
    tail_ref[0:CONV_TAIL, :] = uf[L - CONV_TAIL:, :]
    qk = conv * jax.nn.sigmoid(conv)
    q_all = (qk[:, :ML_QK] * (ML_DQK ** -0.5)).astype(BF16)
    k_all = qk[:, ML_QK:]

    tril = tril_ref[...]
    triu = triu_ref[...]
    gc = gc_ref[...]
    gr = gr_ref[...]
    bc_all = sum(jnp.dot(tril, p, preferred_element_type=F32) for p in _split3(gc))
    br_all = sum(jnp.dot(p, triu, preferred_element_type=F32) for p in _split3(gr))

    for h in range(ML_H):
        qh = q_all[:, h * ML_DQK:(h + 1) * ML_DQK]
        kh_f = k_all[:, h * ML_DQK:(h + 1) * ML_DQK]
        kh = kh_f.astype(BF16)
        vh = v_ref[:, h * ML_DV:(h + 1) * ML_DV]
        it_col = gc[:, h:h + 1]
        it_row = gr[h:h + 1, :]
        b_col = bc_all[:, ML_H + h:ML_H + h + 1]
        b_row = br_all[ML_H + h:ML_H + h + 1, :]
        m_prev = m_ref[h][:, 0:1]

        dmat = b_col + (it_row - b_row) + neg_ref[...]
        inter_log = b_col + m_prev
        m_t = jnp.maximum(inter_log, jnp.max(dmat, axis=1, keepdims=True))
        wts = jnp.exp(dmat - m_t)
        s = lax.dot_general(qh, kh, (((1,), (1,)), ((), ())), preferred_element_type=F32)
        sc = s * wts
        inter_scale = jnp.exp(inter_log - m_t)
        ct = ct_ref[h]
        num = (jnp.dot(sc.astype(BF16), vh, preferred_element_type=F32)
               + inter_scale * jnp.dot(qh, ct.astype(BF16), preferred_element_type=F32))
        n_row = n_ref[h]
        den = (jnp.sum(sc, axis=1, keepdims=True)
               + inter_scale * jnp.sum(qh.astype(F32) * n_row, axis=1, keepdims=True))
        h_out = num / jnp.maximum(jnp.abs(den), jnp.exp(-m_t))

        b_last = b_col[L - 1:L, :]
        lw_col = b_last - b_col + it_col
        lw_row = b_last - b_row + it_row
        m_new = jnp.maximum(b_last + m_prev, jnp.max(lw_row, axis=1, keepdims=True))
        ws_col = jnp.exp(lw_col - m_new)
        decay = jnp.exp(b_last + m_prev - m_new)
        kw = kh_f * ws_col
        ct_ref[h] = decay * ct + lax.dot_general(kw.astype(BF16), vh, (((0,), (0,)), ((), ())),
                                                  preferred_element_type=F32)
        n_ref[h] = decay * n_row + jnp.sum(kw, axis=0, keepdims=True)
        m_ref[h] = jnp.broadcast_to(m_new, (1, LANES))

        ms = jnp.mean(h_out * h_out, axis=1, keepdims=True)
        hn = h_out * lax.rsqrt(ms + EPS) * ng_ref[:, h * ML_DV:(h + 1) * ML_DV]
        og = o_ref[:, h * ML_DV:(h + 1) * ML_DV].astype(F32)
        y_ref[:, h * ML_DV:(h + 1) * ML_DV] = (hn * jax.nn.sigmoid(og)).astype(y_ref.dtype)


def _mlstm_core(p, gcol, grow, conv_w, norm_g, *, batch, seq):
    L = ML_CHUNK
    nc = seq // L
    p3 = p.reshape(batch, seq, 2 * ML_QK + 2 * ML_V)
    gc3 = gcol.reshape(batch, seq, LANES)
    gr3 = grow.reshape(2 * ML_H, batch, seq).transpose(1, 0, 2)
    y = pl.pallas_call(
        _mlstm_kernel,
        out_shape=jax.ShapeDtypeStruct((batch, seq, ML_V), BF16),
        grid=(batch, nc),
        in_specs=[
            pl.BlockSpec((None, L, ML_QK), lambda b, c: (b, c, 0)),
            pl.BlockSpec((None, L, ML_QK), lambda b, c: (b, c, 1)),
            pl.BlockSpec((None, L, ML_V), lambda b, c: (b, c, 1)),
            pl.BlockSpec((None, L, ML_V), lambda b, c: (b, c, 2)),
            pl.BlockSpec((None, L, LANES), lambda b, c: (b, c, 0)),
            pl.BlockSpec((None, 2 * ML_H, L), lambda b, c: (b, 0, c)),
            pl.BlockSpec((CONV_K, 2 * ML_QK), lambda b, c: (0, 0)),
            pl.BlockSpec((1, ML_V), lambda b, c: (0, 0)),
        ],
        out_specs=pl.BlockSpec((None, L, ML_V), lambda b, c: (b, c, 0)),
        scratch_shapes=[
            pltpu.VMEM((2 * CONV_TAIL, 2 * ML_QK), F32),
            pltpu.VMEM((ML_H, ML_DQK, ML_DV), F32),
            pltpu.VMEM((ML_H, 1, ML_DQK), F32),
            pltpu.VMEM((ML_H, 1, LANES), F32),
            pltpu.VMEM(((CONV_K - 1) * L, L), BF16),
            pltpu.VMEM((L, L), BF16),
            pltpu.VMEM((L, L), BF16),
            pltpu.VMEM((L, L), F32),
        ],
        compiler_params=_cparams(("parallel", "arbitrary")),
        name="mlstm_core",
    )(p3, p3, p3, p3, gc3, gr3, conv_w, norm_g)
    return y.reshape(batch * seq, ML_V)


def _diff_attn_kernel(q_ref, k_ref, vt_ref, lp_ref, ng_ref, o_ref, q2_ref, r_ref, m_ref, l_ref,
                      acc_ref, *, lambda_init):
    tq, tk, cw = DA_TQ, DA_TK, DA_CW
    qi = pl.program_id(2)
    q = q_ref[...].astype(F32) * (DA_DH ** -0.5 * LOG2E)
    lane = lax.broadcasted_iota(jnp.int32, q.shape, 1)
    q2_ref[0:tq, :] = jnp.where(lane < DA_DH, q, 0.0).astype(BF16)
    q2_ref[tq:2 * tq, :] = jnp.where(lane >= DA_DH, q, 0.0).astype(BF16)

    r_ref[...] = jnp.zeros_like(r_ref)
    m_ref[...] = jnp.full(m_ref.shape, -jnp.inf, F32)
    l_ref[...] = jnp.zeros_like(l_ref)
    acc_ref[...] = jnp.zeros_like(acc_ref)

    nch = 2 * tq // cw

    def keys_needed(c, diag):
        if diag is None:
            return tk
        visible = (c * cw) % tq + cw - diag * tk
        return max(0, min(tk, -(-visible // DA_KEY_ALIGN) * DA_KEY_ALIGN))

    def mask(s, c, diag):
        nk = s.shape[0]
        key = lax.broadcasted_iota(jnp.int32, (nk, cw), 0) + diag * tk
        qry = lax.broadcasted_iota(jnp.int32, (nk, cw), 1) + (c * cw) % tq
        return jnp.where(key <= qry, s, -jnp.inf)

    def scores(j, c, diag=None):
        nk = keys_needed(c, diag)
        if nk == 0:
            return None
        kb = k_ref[pl.ds(pl.multiple_of(j * tk, tk), nk), :]
        s = lax.dot_general(kb, q2_ref[c * cw:(c + 1) * cw, :], (((1,), (1,)), ((), ())),
                            preferred_element_type=F32)
        if diag is not None:
            s = mask(s, c, diag)
        return s, jnp.max(s, axis=0, keepdims=True)


    def exact_block(j, diag=None):
        start = pl.multiple_of(j * tk, tk)
        nxt = scores(j, 0, diag)
        for c in range(nch):
            cs = slice(c * cw, (c + 1) * cw)
            s, bmax = nxt
            if c + 1 < nch:
                nxt = scores(j, c + 1, diag)
            vbt = vt_ref[:, pl.ds(start, s.shape[0])]
            m_old = m_ref[:, cs]
            m_new = jnp.maximum(m_old, bmax)
            alpha = jnp.where(m_old == -jnp.inf, 0.0, jnp.exp2(r_ref[:, cs] - m_new))
            p = jnp.exp2(s - m_new)
            l_ref[:, cs] = alpha * l_ref[:, cs] + jnp.sum(p, axis=0, keepdims=True)
            acc_ref[:, cs] = alpha * acc_ref[:, cs] + jnp.dot(vbt, p.astype(BF16),
                                                               preferred_element_type=F32)
            m_ref[:, cs] = m_new
            r_ref[:, cs] = m_new

    def fast_block(j, diag=None):
        start = pl.multiple_of(j * tk, tk)
        nxt = scores(j, 0, diag)
        tent = []
        for c in range(nch):
            cs = slice(c * cw, (c + 1) * cw)
            s, bmax = nxt
            if c + 1 < nch:
                nxt = scores(j, c + 1, diag)
            vbt = vt_ref[:, pl.ds(start, s.shape[0])]
            r_old = r_ref[:, cs]
            p = jnp.exp2(s - r_old)
            tent.append((cs, bmax, r_old, m_ref[:, cs], jnp.sum(p, axis=0, keepdims=True),
                         jnp.dot(vbt, p.astype(BF16), preferred_element_type=F32)))
        worst = functools.reduce(jnp.maximum, [
            jnp.max(jnp.maximum(bmax - r_old,
                                jnp.where(m_old == -jnp.inf, r_old - bmax, -jnp.inf)),
                    axis=1, keepdims=True)
            for _, bmax, r_old, m_old, _, _ in tent])
        safe = worst <= DA_LAZY_LIMIT
        for cs, bmax, r_old, m_old, lt, acct in tent:
            m_new = jnp.maximum(m_old, bmax)
            r_new = jnp.where(jnp.abs(m_new - r_old) > DA_REBASE, m_new, r_old)
            scale = jnp.exp2(r_old - r_new)
            l_old = l_ref[:, cs]
            acc_old = acc_ref[:, cs]
            l_ref[:, cs] = jnp.where(safe, (l_old + lt) * scale, l_old)
            acc_ref[:, cs] = jnp.where(safe, (acc_old + acct) * scale, acc_old)
            m_ref[:, cs] = jnp.where(safe, m_new, m_old)
            r_ref[:, cs] = jnp.where(safe, r_new, r_old)

        @pl.when(jnp.logical_not(jnp.max(worst) <= DA_LAZY_LIMIT))
        def _():
            exact_block(j, diag)

    def fast_body(j, carry):
        fast_block(j)
        return carry

    assert tq == tk
    lax.fori_loop(0, qi, fast_body, 0)
    fast_block(qi, diag=0)


    lp = lp_ref[...]
    lam = (jnp.exp(jnp.sum(lp[0:1, :] * lp[1:2, :], axis=1, keepdims=True))
           - jnp.exp(jnp.sum(lp[2:3, :] * lp[3:4, :], axis=1, keepdims=True)) + lambda_init)
    out = acc_ref[...] / l_ref[...]
    o = out[:, :tq] - lam * out[:, tq:]
    ms = jnp.mean(o * o, axis=0, keepdims=True)
    on = o * lax.rsqrt(ms + DA_EPS) * ng_ref[...] * (1.0 - lambda_init)
    o_ref[...] = on.T.astype(o_ref.dtype)


def _diff_attn(p, vt, lam_params, norm_g, lambda_init, *, batch, seq):
    tq = DA_TQ
    p3 = p.reshape(batch, seq, 2 * DA_QK + DA_H * DA_DV)
    kern = functools.partial(_diff_attn_kernel, lambda_init=lambda_init)
    o = pl.pallas_call(
        kern,
        out_shape=jax.ShapeDtypeStruct((batch, seq, DA_H * DA_DV), BF16),
        grid=(batch, DA_H, seq // tq),
        in_specs=[
            pl.BlockSpec((None, tq, 2 * DA_DH), lambda b, h, i: (b, i, h)),
            pl.BlockSpec((None, seq, 2 * DA_DH), lambda b, h, i: (b, 0, DA_H + h)),
            pl.BlockSpec((None, DA_DV, seq), lambda b, h, i: (b, h, 0)),
            pl.BlockSpec((4, DA_DH), lambda b, h, i: (0, 0)),
            pl.BlockSpec((DA_DV, 1), lambda b, h, i: (0, 0)),
        ],
        out_specs=pl.BlockSpec((None, tq, DA_DV), lambda b, h, i: (b, i, h)),
        scratch_shapes=[
            pltpu.VMEM((2 * tq, 2 * DA_DH), BF16),
            pltpu.VMEM((1, 2 * tq), F32),
            pltpu.VMEM((1, 2 * tq), F32),
            pltpu.VMEM((1, 2 * tq), F32),
            pltpu.VMEM((DA_DV, 2 * tq), F32),
        ],
        compiler_params=_cparams(("parallel", "parallel", "arbitrary")),
        name="diff_attn",
    )(p3, p3, vt, lam_params, norm_g)
    return o.reshape(batch * seq, DA_H * DA_DV)


def _pack_halves(y):
    halves = []
    for h in range(2):
        base = h * 2 * HALF_W
        lo = y[:, base:base + HALF_W].astype(BF16).astype(F32)
        hi = y[:, base + HALF_W:base + 2 * HALF_W].astype(BF16).astype(F32)
        lo_bits = lax.bitcast_convert_type(lo, U32) >> 16
        hi_bits = lax.bitcast_convert_type(hi, U32)
        halves.append(hi_bits | lo_bits)
    return halves


def _unpack_halves(w0, w1):
    parts = []
    for w in (w0, w1):
        parts.append(lax.bitcast_convert_type(w << 16, F32))
        parts.append(lax.bitcast_convert_type(w & jnp.uint32(0xFFFF0000), F32))
    return jnp.concatenate(parts, axis=1)


def _route_rows(x, g_ref, w_ref, b_ref, run_ref, ls_ref):
    ms = jnp.mean(x * x, axis=-1, keepdims=True)
    hn32 = x * lax.rsqrt(ms + EPS) * g_ref[...]
    hn = hn32.astype(BF16)
    halves = _pack_halves(hn32)
    lg = jnp.dot(hn, w_ref[...], preferred_element_type=F32) + b_ref[...]
    lane = lax.broadcasted_iota(jnp.int32, lg.shape, 1).astype(F32)
    neg = -jnp.inf

    gmask = (lane >= N_EXPERTS) & (lane < N_EXPERTS + N_GROUPS)
    gl = jnp.where(gmask, lg, neg)
    gmax = jnp.max(gl, axis=1, keepdims=True)
    gidx = jnp.min(jnp.where(gl == gmax, lane, float(LANES)), axis=1, keepdims=True) - N_EXPERTS
    gsum = jnp.sum(jnp.where(gmask, jnp.exp(gl - gmax), 0.0), axis=1, keepdims=True)
    g_w = 1.0 / gsum

    emask = (lane >= gidx * EPG) & (lane < gidx * EPG + EPG)
    el = jnp.where(emask, lg, neg)
    emax = jnp.max(el, axis=1, keepdims=True)
    eexp = jnp.where(emask, jnp.exp(el - emax), 0.0)
    ep = eexp / jnp.sum(eexp, axis=1, keepdims=True)
    ep = jnp.where(emask, ep, -1.0)
    p1 = jnp.max(ep, axis=1, keepdims=True)
    i1 = jnp.min(jnp.where(ep == p1, lane, float(LANES)), axis=1, keepdims=True)
    ep2 = jnp.where(lane == i1, -1.0, ep)
    p2 = jnp.max(ep2, axis=1, keepdims=True)
    i2 = jnp.min(jnp.where(ep2 == p2, lane, float(LANES)), axis=1, keepdims=True)
    wsum = p1 + p2
    w1 = g_w * (p1 / wsum)
    w2 = g_w * (p2 / wsum)

    a1 = lane == i1
    a2 = lane == i2
    onehot = (a1 | a2).astype(BF16)
    before = jnp.dot(ls_ref[...], onehot, preferred_element_type=F32) + run_ref[...]
    rank1 = jnp.sum(jnp.where(a1, before, 0.0), axis=1, keepdims=True)
    rank2 = jnp.sum(jnp.where(a2, before, 0.0), axis=1, keepdims=True)
    run_ref[...] += jnp.sum(onehot.astype(F32), axis=0, keepdims=True)

    meta = (jnp.where(lane == 0, i1, 0.0) + jnp.where(lane == 1, i2, 0.0)
            + jnp.where(lane == 2, rank1, 0.0) + jnp.where(lane == 3, rank2, 0.0)
            + jnp.where(lane == 4, w1, 0.0) + jnp.where(lane == 5, w2, 0.0))
    return halves, meta


def _res_router_kernel(a_ref, w_ref, r_ref, g_ref, wr_ref, br_ref, o_ref, hp_ref, meta_ref,
                       metat_ref, cnt_ref, run_ref, ls_ref):
    sub = ls_ref.shape[0]

    @pl.when(pl.program_id(0) == 0)
    def _():
        run_ref[...] = jnp.zeros_like(run_ref)
        r_i = lax.broadcasted_iota(jnp.int32, (sub, sub), 0)
        c_i = lax.broadcasted_iota(jnp.int32, (sub, sub), 1)
        ls_ref[...] = (c_i < r_i).astype(BF16)

    subs = [slice(i * sub, (i + 1) * sub) for i in range(a_ref.shape[0] // sub)]

    def project(r):
        return r_ref[r, :] + jnp.dot(a_ref[r, :], w_ref[...], preferred_element_type=F32)

    nxt = project(subs[0])
    for i, r in enumerate(subs):
        x = nxt
        if i + 1 < len(subs):
            nxt = project(subs[i + 1])
        o_ref[r, :] = x
        halves, meta = _route_rows(x, g_ref, wr_ref, br_ref, run_ref, ls_ref)
        hp_ref[0, r, :] = halves[0]
        hp_ref[1, r, :] = halves[1]
        meta_ref[r, :] = meta
        metat_ref[:, r] = meta.T[0:8, :]
    cnt_ref[...] = run_ref[...]


def _res_router(a, w, res, g, w_group, b_group, w_expert, b_expert, *, tm, sub):
    t, k = a.shape
    d = w.shape[1]
    wr = jnp.zeros((d, LANES), BF16)
    wr = wr.at[:, :N_EXPERTS].set(w_expert.astype(BF16))
    wr = wr.at[:, N_EXPERTS:N_EXPERTS + N_GROUPS].set(w_group.astype(BF16))
    br = jnp.zeros((1, LANES), F32)
    br = br.at[0, :N_EXPERTS].set(b_expert.astype(F32))
    br = br.at[0, N_EXPERTS:N_EXPERTS + N_GROUPS].set(b_group.astype(F32))
    return pl.pallas_call(
        _res_router_kernel,
        out_shape=(jax.ShapeDtypeStruct((t, d), F32),
                   jax.ShapeDtypeStruct((2, t, HALF_W), U32),
                   jax.ShapeDtypeStruct((t, LANES), F32),
                   jax.ShapeDtypeStruct((8, t), F32),
                   jax.ShapeDtypeStruct((1, LANES), F32)),
        grid=(t // tm,),
        in_specs=[
            pl.BlockSpec((tm, k), lambda i: (i, 0)),
            pl.BlockSpec((k, d), lambda i: (0, 0)),
            pl.BlockSpec((tm, d), lambda i: (i, 0)),
            pl.BlockSpec((1, d), lambda i: (0, 0)),
            pl.BlockSpec((d, LANES), lambda i: (0, 0)),
            pl.BlockSpec((1, LANES), lambda i: (0, 0)),
        ],
        out_specs=(pl.BlockSpec((tm, d), lambda i: (i, 0)),
                   pl.BlockSpec((2, tm, HALF_W), lambda i: (0, i, 0)),
                   pl.BlockSpec((tm, LANES), lambda i: (i, 0)),
                   pl.BlockSpec((8, tm), lambda i: (0, i)),
                   pl.BlockSpec((1, LANES), lambda i: (0, 0))),
        scratch_shapes=[pltpu.VMEM((1, LANES), F32), pltpu.VMEM((sub, sub), BF16)],
        compiler_params=_cparams(("arbitrary",)),
        name="res_router",
    )(a, w, res, g, wr, br)


def _sc_mesh():
    return plsc.VectorSubcoreMesh(core_axis_name="c", subcore_axis_name="s")


def _sc_scatter2(x, i0, i1, n_out):
    n, d = x.shape

    @pl.kernel(out_type=jax.ShapeDtypeStruct((n_out, d), x.dtype), mesh=_sc_mesh())
    def k(x_hbm, i0_hbm, i1_hbm, o_hbm):
        def body(x_vmem, i0_vmem, i1_vmem):
            pltpu.sync_copy(x_vmem, o_hbm.at[i0_vmem.at[0]])
            pltpu.sync_copy(x_vmem, o_hbm.at[i1_vmem.at[0]])

        pltpu.emit_pipeline(
            body,
            grid=(n // SC_WINDOW,),
            in_specs=[pl.BlockSpec((SC_WINDOW, d), lambda i: (i, 0)),
                      pl.BlockSpec((1, SC_WINDOW), lambda i: (0, i)),
                      pl.BlockSpec((1, SC_WINDOW), lambda i: (0, i))],
            out_specs=[],
            core_axis_name=("c", "s"),
            dimension_semantics=(pltpu.PARALLEL,),
        )(x_hbm, i0_hbm, i1_hbm)

    return k(x, i0.reshape(1, n), i1.reshape(1, n))


def _sc_gather(x, idx):
    n = idx.shape[0]
    d = x.shape[1]

    @pl.kernel(out_type=jax.ShapeDtypeStruct((n, d), x.dtype), mesh=_sc_mesh())
    def k(x_hbm, i_hbm, o_hbm):
        def body(i_vmem, o_vmem):
            pltpu.sync_copy(x_hbm.at[i_vmem.at[0]], o_vmem)

        pltpu.emit_pipeline(
            body,
            grid=(n // SC_WINDOW,),
            in_specs=[pl.BlockSpec((1, SC_WINDOW), lambda i: (0, i))],
            out_specs=[pl.BlockSpec((SC_WINDOW, d), lambda i: (i, 0))],
            core_axis_name=("c", "s"),
            dimension_semantics=(pltpu.PARALLEL,),
        )(i_hbm, o_hbm)

    return k(x, idx.reshape(1, n))


def _experts_kernel(te_ref, nu_ref, xs_ref, wgu_ref, wd_ref, ys_ref, wgu_bf_ref, wd_bf_ref):
    j = pl.program_id(0)

    @pl.when(j < nu_ref[0])
    def _():
        @pl.when((j == 0) | (te_ref[j] != te_ref[jnp.maximum(j - 1, 0)]))
        def _():
            wgu_bf_ref[...] = wgu_ref[...].astype(BF16)
            wd_bf_ref[...] = wd_ref[...].astype(BF16)

        subs = [slice(i * MOE_SUB, (i + 1) * MOE_SUB) for i in range(MOE_TR // MOE_SUB)]
        xs = [_unpack_halves(xs_ref[0, r, :], xs_ref[1, r, :]).astype(BF16) for r in subs]
        gus = [jnp.dot(x, wgu_bf_ref[...], preferred_element_type=F32) for x in xs]
        acts = [(gu[:, :D_EXPERT] * jax.nn.sigmoid(gu[:, :D_EXPERT]) * gu[:, D_EXPERT:]).astype(BF16)
                for gu in gus]
        ys = [jnp.dot(a, wd_bf_ref[...], preferred_element_type=F32) for a in acts]
        for r, y in zip(subs, ys):
            halves = _pack_halves(y)
            ys_ref[0, r, :] = halves[0]
            ys_ref[1, r, :] = halves[1]


def _experts(xs, tile_expert, n_used, w_gu, w_down, layer):
    _, rows, _ = xs.shape
    d = w_gu.shape[2]
    return pl.pallas_call(
        _experts_kernel,
        out_shape=jax.ShapeDtypeStruct(xs.shape, U32),
        grid_spec=pltpu.PrefetchScalarGridSpec(
            num_scalar_prefetch=2,
            grid=(rows // MOE_TR,),
            in_specs=[
                pl.BlockSpec((2, MOE_TR, HALF_W), lambda j, te, nu: (0, j, 0)),
                pl.BlockSpec((None, None, d, 2 * D_EXPERT),
                             lambda j, te, nu: (layer, te[j], 0, 0)),
                pl.BlockSpec((None, None, D_EXPERT, d),
                             lambda j, te, nu: (layer, te[j], 0, 0)),
            ],
            out_specs=pl.BlockSpec((2, MOE_TR, HALF_W), lambda j, te, nu: (0, j, 0)),
            scratch_shapes=[pltpu.VMEM((d, 2 * D_EXPERT), BF16), pltpu.VMEM((D_EXPERT, d), BF16)],
        ),
        compiler_params=_cparams(("arbitrary",)),
        name="moe_experts",
    )(tile_expert, n_used, xs, w_gu, w_down)


def _combine_kernel(x_ref, z_ref, meta_ref, g_ref, o_ref, *, final_norm):
    meta = meta_ref[...]
    y_a = _unpack_halves(z_ref[0], z_ref[2])
    y_b = _unpack_halves(z_ref[1], z_ref[3])
    out = x_ref[...] + meta[:, 4:5] * y_a + meta[:, 5:6] * y_b
    if final_norm:
        ms = jnp.mean(out * out, axis=-1, keepdims=True)
        out = out * lax.rsqrt(ms + EPS) * g_ref[...]
    o_ref[...] = out


def _combine(x, z, meta, g, *, tm, final_norm):
    t, d = x.shape
    return pl.pallas_call(
        functools.partial(_combine_kernel, final_norm=final_norm),
        out_shape=jax.ShapeDtypeStruct((t, d), F32),
        grid=(t // tm,),
        in_specs=[
            pl.BlockSpec((tm, d), lambda i: (i, 0)),
            pl.BlockSpec((4, tm, HALF_W), lambda i: (0, i, 0)),
            pl.BlockSpec((tm, LANES), lambda i: (i, 0)),
            pl.BlockSpec((1, d), lambda i: (0, 0)),
        ],
        out_specs=pl.BlockSpec((tm, d), lambda i: (i, 0)),
        compiler_params=_cparams(("parallel",)),
        name="moe_combine",
    )(x, z, meta, g)


def _moe_layer(x, routing, w_gu, w_down, layer, final_g=None):
    t, d = x.shape
    hp, meta, metat, cnt = routing
    n_tiles = 2 * t // MOE_TR + N_EXPERTS
    rows = n_tiles * MOE_TR
    counts = cnt[0, :N_EXPERTS].astype(jnp.int32)
    tiles_e = (counts + MOE_TR - 1) // MOE_TR
    tiles_end = jnp.cumsum(tiles_e)
    row_off = (tiles_end - tiles_e) * MOE_TR
    experts = jnp.arange(N_EXPERTS, dtype=jnp.int32)[:, None]

    def region_start(e_row):
        return jnp.sum(jnp.where(e_row[None, :] == experts, row_off[:, None], 0), axis=0)

    e_a, e_b = metat[0].astype(jnp.int32), metat[1].astype(jnp.int32)
    pos_a = region_start(e_a) + metat[2].astype(jnp.int32)
    pos_b = region_start(e_b) + metat[3].astype(jnp.int32)
    tile_ids = jnp.arange(n_tiles, dtype=jnp.int32)
    tile_expert = jnp.minimum(
        jnp.sum((tile_ids[:, None] >= tiles_end[None, :]).astype(jnp.int32), axis=1),
        N_EXPERTS - 1)
    n_used = tiles_end[-1:].astype(jnp.int32)

    xs = _sc_scatter2(hp.reshape(2 * t, HALF_W),
                      jnp.concatenate([pos_a, pos_a + rows]),
                      jnp.concatenate([pos_b, pos_b + rows]), 2 * rows)
    ys = _experts(xs.reshape(2, rows, HALF_W), tile_expert, n_used,
                  w_gu, w_down, layer)
    z = _sc_gather(ys.reshape(2 * rows, HALF_W),
                   jnp.concatenate([pos_a, pos_b, pos_a + rows, pos_b + rows]))
    g = jnp.ones((1, d), F32) if final_g is None else final_g.reshape(1, d)
    return _combine(x, z.reshape(4, t, HALF_W), meta, g, tm=1024, final_norm=final_g is not None)


def kernel(x, norm_mix, norm_ffn, ml_w_in, ml_conv, ml_b_i, ml_b_f, ml_norm, ml_w_out, da_w_in, da_lq1, da_lk1, da_lq2, da_lk2, da_norm, da_w_out, moe_w_group, moe_b_group, moe_w_expert, moe_b_expert, moe_w_gu, moe_w_down, final_norm):
    batch, seq, d = x.shape
    xt = x.reshape(batch * seq, d)

    p, gcol, grow = _ml_proj(xt, norm_mix[0].reshape(1, d), ml_w_in[0], ml_b_i[0], ml_b_f[0],
                             tm=1024, tn=1024, out_dtype=BF16)
    assert p.shape[1] == 2 * ML_QK + 2 * ML_V
    y = _mlstm_core(p, gcol, grow, ml_conv[0], ml_norm[0].reshape(1, ML_V), batch=batch, seq=seq)
    xt, *routing = _res_router(y, ml_w_out[0].astype(BF16), xt, norm_ffn[0].reshape(1, d),
                               moe_w_group[0], moe_b_group[0], moe_w_expert[0], moe_b_expert[0],
                               tm=RR_TM, sub=RR_SUB)
    xt = _moe_layer(xt, routing, moe_w_gu, moe_w_down, 0)

    lambda_init = 0.8 - 0.6 * math.exp(-0.3 * 1)
    p, vt = _norm_matmul_vt(xt, norm_mix[1].reshape(1, d), da_w_in[0].astype(BF16), tm=1024,
                            tn=DA_H * DA_DV, vt_block=2 * DA_QK // (DA_H * DA_DV), batch=batch,
                            seq=seq, out_dtype=BF16)
    lam_params = jnp.stack([da_lq1[0], da_lk1[0], da_lq2[0], da_lk2[0]]).astype(F32)
    a = _diff_attn(p, vt, lam_params, da_norm[0].reshape(DA_DV, 1), lambda_init, batch=batch,
                   seq=seq)
    xt, *routing = _res_router(a, da_w_out[0].astype(BF16), xt, norm_ffn[1].reshape(1, d),
                               moe_w_group[1], moe_b_group[1], moe_w_expert[1], moe_b_expert[1],
                               tm=RR_TM, sub=RR_SUB)
    out = _moe_layer(xt, routing, moe_w_gu, moe_w_down, 1, final_g=final_norm)
    return out.reshape(batch, seq, d)
```

```python
import functools
import math

import jax
import jax.numpy as jnp
from jax import lax
from jax.experimental import pallas as pl
from jax.experimental.pallas import tpu as pltpu
from jax.experimental.pallas import tpu_sc as plsc

F32 = jnp.float32
BF16 = jnp.bfloat16
U32 = jnp.uint32

D_MODEL = 1024
EPS = 1e-6
ML_H = 4
ML_DV = 512
ML_DQK = 256
ML_QK = ML_H * ML_DQK
ML_V = ML_H * ML_DV
CONV_K = 4
ML_CHUNK = 256
CONV_TAIL = 8
DA_H = 8
DA_DH = 64
DA_DV = 128
DA_QK = DA_H * 2 * DA_DH
DA_EPS = 1e-5
DA_TQ = 1024
DA_TK = 1024
DA_CW = 512
DA_KEY_ALIGN = 256
DA_LAZY_LIMIT = 64.0
DA_REBASE = 8.0
LOG2E = 1.4426950408889634
N_GROUPS = 4
EPG = 8
N_EXPERTS = 32
D_EXPERT = 256
RR_TM = 512
RR_SUB = 256
MOE_TR = 512
MOE_SUB = 256
HALF_W = D_MODEL // 4
SC_WINDOW = 128
LANES = 128

VMEM_LIMIT = 48 * 1024 * 1024


def _cparams(sem):
    return pltpu.CompilerParams(dimension_semantics=sem, vmem_limit_bytes=VMEM_LIMIT)


def _norm_matmul_vt_kernel(x_ref, z_ref, meta_ref, g_ref, w_ref, x2_ref, o_ref, vt_ref, xn_ref,
                           *, vt_block):
    j = pl.program_id(1)

    @pl.when(j == 0)
    def _():
        x = _moe_combined(x_ref, z_ref, meta_ref)
        x2_ref[...] = x
        ms = jnp.mean(x * x, axis=-1, keepdims=True)
        xn_ref[...] = (x * lax.rsqrt(ms + EPS) * g_ref[...]).astype(BF16)

    res = jnp.dot(xn_ref[...], w_ref[...], preferred_element_type=F32)
    o_ref[...] = res.astype(o_ref.dtype)

    @pl.when(j == vt_block)
    def _():
        vt_ref[...] = res.T.astype(vt_ref.dtype)


def _norm_matmul_vt(x, z, meta, g, w, *, tm, tn, vt_block, batch, seq, out_dtype):
    t, d = x.shape
    n = w.shape[1]
    per_batch = seq // tm
    return pl.pallas_call(
        functools.partial(_norm_matmul_vt_kernel, vt_block=vt_block),
        out_shape=(jax.ShapeDtypeStruct((t, d), F32),
                   jax.ShapeDtypeStruct((t, n), out_dtype),
                   jax.ShapeDtypeStruct((batch, tn, seq), out_dtype)),
        grid=(t // tm, n // tn),
        in_specs=[
            pl.BlockSpec((tm, d), lambda i, j: (i, 0)),
            pl.BlockSpec((4, tm, HALF_W), lambda i, j: (0, i, 0)),
            pl.BlockSpec((tm, LANES), lambda i, j: (i, 0)),
            pl.BlockSpec((1, d), lambda i, j: (0, 0)),
            pl.BlockSpec((d, tn), lambda i, j: (0, j)),
        ],
        out_specs=(pl.BlockSpec((tm, d), lambda i, j: (i, 0)),
                   pl.BlockSpec((tm, tn), lambda i, j: (i, j)),
                   pl.BlockSpec((None, tn, tm), lambda i, j: (i // per_batch, 0, i % per_batch))),
        scratch_shapes=[pltpu.VMEM((tm, d), BF16)],
        compiler_params=_cparams(("parallel", "arbitrary")),
        name="norm_matmul_vt",
    )(x, z, meta, g, w)


def _log_sigmoid(x):
    return jnp.minimum(x, 0.0) - jnp.log1p(jnp.exp(-jnp.abs(x)))


def _ml_proj_kernel(x_ref, g_ref, w_ref, wc_ref, wr_ref, bc_ref, br_ref, o_ref, oc_ref, or_ref,
                    xn_ref):
    @pl.when(pl.program_id(1) == 0)
    def _():
        x = x_ref[...]
        ms = jnp.mean(x * x, axis=-1, keepdims=True)
        xn = (x * lax.rsqrt(ms + EPS) * g_ref[...]).astype(BF16)
        xn_ref[...] = xn
        gc = jnp.dot(xn, wc_ref[...], preferred_element_type=F32) + bc_ref[...]
        lane = lax.broadcasted_iota(jnp.int32, gc.shape, 1)
        oc_ref[...] = jnp.where(lane < ML_H, gc, _log_sigmoid(gc))
        gr = lax.dot_general(wr_ref[...], xn, (((1,), (1,)), ((), ())),
                             preferred_element_type=F32) + br_ref[...]
        row = lax.broadcasted_iota(jnp.int32, gr.shape, 0)
        or_ref[...] = jnp.where(row < ML_H, gr, _log_sigmoid(gr))

    o_ref[...] = jnp.dot(xn_ref[...], w_ref[...], preferred_element_type=F32).astype(o_ref.dtype)


def _ml_proj(x, g, w_in, b_i, b_f, *, tm, tn, out_dtype):
    t, d = x.shape
    ng = 2 * ML_H
    n_main = w_in.shape[1] - ng
    w_gates = w_in[:, n_main:]
    wc = jnp.zeros((d, LANES), BF16).at[:, :ng].set(w_gates.astype(BF16))
    wr = w_gates.T.astype(BF16)
    bias = jnp.concatenate([b_i, b_f]).astype(F32)
    bc = jnp.zeros((1, LANES), F32).at[0, :ng].set(bias)
    br = bias.reshape(ng, 1)
    return pl.pallas_call(
        _ml_proj_kernel,
        out_shape=(jax.ShapeDtypeStruct((t, n_main), out_dtype),
                   jax.ShapeDtypeStruct((t, LANES), F32), jax.ShapeDtypeStruct((ng, t), F32)),
        grid=(t // tm, n_main // tn),
        in_specs=[
            pl.BlockSpec((tm, d), lambda i, j: (i, 0)),
            pl.BlockSpec((1, d), lambda i, j: (0, 0)),
            pl.BlockSpec((d, tn), lambda i, j: (0, j)),
            pl.BlockSpec((d, LANES), lambda i, j: (0, 0)),
            pl.BlockSpec((ng, d), lambda i, j: (0, 0)),
            pl.BlockSpec((1, LANES), lambda i, j: (0, 0)),
            pl.BlockSpec((ng, 1), lambda i, j: (0, 0)),
        ],
        out_specs=(pl.BlockSpec((tm, tn), lambda i, j: (i, j)),
                   pl.BlockSpec((tm, LANES), lambda i, j: (i, 0)),
                   pl.BlockSpec((ng, tm), lambda i, j: (0, i))),
        scratch_shapes=[pltpu.VMEM((tm, d), BF16)],
        compiler_params=_cparams(("parallel", "arbitrary")),
        name="ml_proj",
    )(x, g, w_in.astype(BF16), wc, wr, bc, br)


def _split3(x):
    hi = x.astype(BF16)
    r = x - hi.astype(F32)
    mid = r.astype(BF16)
    lo = (r - mid.astype(F32)).astype(BF16)
    return hi, mid, lo


def _mlstm_kernel(q_ref, k_ref, v_ref, o_ref, gc_ref, gr_ref, cw_ref, ng_ref, y_ref,
                  tail_ref, ct_ref, n_ref, m_ref, shift_ref, tril_ref, triu_ref, neg_ref):
    L = ML_CHUNK
    c = pl.program_id(1)

    @pl.when(c == 0)
    def _():
        tail_ref[...] = jnp.zeros_like(tail_ref)
        ct_ref[...] = jnp.zeros_like(ct_ref)
        n_ref[...] = jnp.zeros_like(n_ref)
        m_ref[...] = jnp.zeros_like(m_ref)
        r_i = lax.broadcasted_iota(jnp.int32, (L, L), 0)
        c_i = lax.broadcasted_iota(jnp.int32, (L, L), 1)
        tril_ref[...] = (c_i <= r_i).astype(BF16)
        triu_ref[...] = (r_i <= c_i).astype(BF16)
        neg_ref[...] = jnp.where(c_i <= r_i, 0.0, -jnp.inf)
        for j in range(CONV_K - 1):
            shift_ref[j * L:(j + 1) * L, :] = (c_i == r_i - (CONV_K - 1 - j)).astype(BF16)

    u = jnp.concatenate([q_ref[...], k_ref[...]], axis=1)
    uf = u.astype(F32)
    shifted = jnp.dot(shift_ref[...], u, preferred_element_type=F32)
    conv = uf * cw_ref[CONV_K - 1:CONV_K, :]
    head = None
    for j in range(CONV_K - 1):
        conv = conv + shifted[j * L:(j + 1) * L, :] * cw_ref[j:j + 1, :]
        part = tail_ref[pl.ds(CONV_TAIL - (CONV_K - 1) + j, CONV_TAIL), :] * cw_ref[j:j + 1, :]
        head = part if head is None else head + part
    conv = jnp.concatenate([conv[:CONV_TAIL] + head, conv[CONV_TAIL:]], axis=0)
    tail_ref[0:CONV_TAIL, :] = uf[L - CONV_TAIL:, :]
    qk = conv * jax.nn.sigmoid(conv)
    q_all = (qk[:, :ML_QK] * (ML_DQK ** -0.5)).astype(BF16)
    k_all = qk[:, ML_QK:]

    tril = tril_ref[...]
    triu = triu_ref[...]
    gc = gc_ref[...]
    gr = gr_ref[...]
    bc_all = sum(jnp.dot(tril, p, preferred_element_type=F32) for p in _split3(gc))
    br_all = sum(jnp.dot(p, triu, preferred_element_type=F32) for p in _split3(gr))

    for h in range(ML_H):
        qh = q_all[:, h * ML_DQK:(h + 1) * ML_DQK]
        kh_f = k_all[:, h * ML_DQK:(h + 1) * ML_DQK]
        kh = kh_f.astype(BF16)
        vh = v_ref[:, h * ML_DV:(h + 1) * ML_DV]
        it_col = gc[:, h:h + 1]
        it_row = gr[h:h + 1, :]
        b_col = bc_all[:, ML_H + h:ML_H + h + 1]
        b_row = br_all[ML_H + h:ML_H + h + 1, :]
        m_prev = m_ref[h][:, 0:1]

        dmat = b_col + (it_row - b_row) + neg_ref[...]
        inter_log = b_col + m_prev
        m_t = jnp.maximum(inter_log, jnp.max(dmat, axis=1, keepdims=True))
        wts = jnp.exp(dmat - m_t)
        s = lax.dot_general(qh, kh, (((1,), (1,)), ((), ())), preferred_element_type=F32)
        sc = s * wts
        inter_scale = jnp.exp(inter_log - m_t)
        ct = ct_ref[h]
        num = (jnp.dot(sc.astype(BF16), vh, preferred_element_type=F32)
               + inter_scale * jnp.dot(qh, ct.astype(BF16), preferred_element_type=F32))
        n_row = n_ref[h]
        den = (jnp.sum(sc, axis=1, keepdims=True)
               + inter_scale * jnp.sum(qh.astype(F32) * n_row, axis=1, keepdims=True))
        h_out = num / jnp.maximum(jnp.abs(den), jnp.exp(-m_t))

        b_last = b_col[L - 1:L, :]
        lw_col = b_last - b_col + it_col
        lw_row = b_last - b_row + it_row
        m_new = jnp.maximum(b_last + m_prev, jnp.max(lw_row, axis=1, keepdims=True))
        ws_col = jnp.exp(lw_col - m_new)
        decay = jnp.exp(b_last + m_prev - m_new)
        kw = kh_f * ws_col
        ct_ref[h] = decay * ct + lax.dot_general(kw.astype(BF16), vh, (((0,), (0,)), ((), ())),
                                                  preferred_element_type=F32)
        n_ref[h] = decay * n_row + jnp.sum(kw, axis=0, keepdims=True)
        m_ref[h] = jnp.broadcast_to(m_new, (1, LANES))

        ms = jnp.mean(h_out * h_out, axis=1, keepdims=True)
        hn = h_out * lax.rsqrt(ms + EPS) * ng_ref[:, h * ML_DV:(h + 1) * ML_DV]
        og = o_ref[:, h * ML_DV:(h + 1) * ML_DV].astype(F32)
        y_ref[:, h * ML_DV:(h + 1) * ML_DV] = (hn * jax.nn.sigmoid(og)).astype(y_ref.dtype)


def _mlstm_core(p, gcol, grow, conv_w, norm_g, *, batch, seq):
    L = ML_CHUNK
    nc = seq // L
    p3 = p.reshape(batch, seq, 2 * ML_QK + 2 * ML_V)
    gc3 = gcol.reshape(batch, seq, LANES)
    gr3 = grow.reshape(2 * ML_H, batch, seq).transpose(1, 0, 2)
    y = pl.pallas_call(
        _mlstm_kernel,
        out_shape=jax.ShapeDtypeStruct((batch, seq, ML_V), BF16),
        grid=(batch, nc),
        in_specs=[
            pl.BlockSpec((None, L, ML_QK), lambda b, c: (b, c, 0)),
            pl.BlockSpec((None, L, ML_QK), lambda b, c: (b, c, 1)),
            pl.BlockSpec((None, L, ML_V), lambda b, c: (b, c, 1)),
            pl.BlockSpec((None, L, ML_V), lambda b, c: (b, c, 2)),
            pl.BlockSpec((None, L, LANES), lambda b, c: (b, c, 0)),
            pl.BlockSpec((None, 2 * ML_H, L), lambda b, c: (b, 0, c)),
            pl.BlockSpec((CONV_K, 2 * ML_QK), lambda b, c: (0, 0)),
            pl.BlockSpec((1, ML_V), lambda b, c: (0, 0)),
        ],
        out_specs=pl.BlockSpec((None, L, ML_V), lambda b, c: (b, c, 0)),
        scratch_shapes=[
            pltpu.VMEM((2 * CONV_TAIL, 2 * ML_QK), F32),
            pltpu.VMEM((ML_H, ML_DQK, ML_DV), F32),
            pltpu.VMEM((ML_H, 1, ML_DQK), F32),
            pltpu.VMEM((ML_H, 1, LANES), F32),
            pltpu.VMEM(((CONV_K - 1) * L, L), BF16),
            pltpu.VMEM((L, L), BF16),
            pltpu.VMEM((L, L), BF16),
            pltpu.VMEM((L, L), F32),
        ],
        compiler_params=_cparams(("parallel", "arbitrary")),
        name="mlstm_core",
    )(p3, p3, p3, p3, gc3, gr3, conv_w, norm_g)
    return y.reshape(batch * seq, ML_V)


def _diff_attn_kernel(q_ref, k_ref, vt_ref, lp_ref, ng_ref, o_ref, q2_ref, r_ref, m_ref, l_ref,
                      acc_ref, *, lambda_init):
    tq, tk, cw = DA_TQ, DA_TK, DA_CW
    qi = pl.program_id(2)
    q = q_ref[...].astype(F32) * (DA_DH ** -0.5 * LOG2E)
    lane = lax.broadcasted_iota(jnp.int32, q.shape, 1)
    q2_ref[0:tq, :] = jnp.where(lane < DA_DH, q, 0.0).astype(BF16)
    q2_ref[tq:2 * tq, :] = jnp.where(lane >= DA_DH, q, 0.0).astype(BF16)

    r_ref[...] = jnp.zeros_like(r_ref)
    m_ref[...] = jnp.full(m_ref.shape, -jnp.inf, F32)
    l_ref[...] = jnp.zeros_like(l_ref)
    acc_ref[...] = jnp.zeros_like(acc_ref)

    nch = 2 * tq // cw

    def keys_needed(c, diag):
        if diag is None:
            return tk
        visible = (c * cw) % tq + cw - diag * tk
        return max(0, min(tk, -(-visible // DA_KEY_ALIGN) * DA_KEY_ALIGN))

    def mask(s, c, diag):
        nk = s.shape[0]
        key = lax.broadcasted_iota(jnp.int32, (nk, cw), 0) + diag * tk
        qry = lax.broadcasted_iota(jnp.int32, (nk, cw), 1) + (c * cw) % tq
        return jnp.where(key <= qry, s, -jnp.inf)

    def scores(j, c, diag=None):
        nk = keys_needed(c, diag)
        if nk == 0:
            return None
        kb = k_ref[pl.ds(pl.multiple_of(j * tk, tk), nk), :]
        s = lax.dot_general(kb, q2_ref[c * cw:(c + 1) * cw, :], (((1,), (1,)), ((), ())),
                            preferred_element_type=F32)
        if diag is not None:
            s = mask(s, c, diag)
        return s, jnp.max(s, axis=0, keepdims=True)


    def exact_block(j, diag=None):
        start = pl.multiple_of(j * tk, tk)
        nxt = scores(j, 0, diag)
        for c in range(nch):
            cs = slice(c * cw, (c + 1) * cw)
            s, bmax = nxt
            if c + 1 < nch:
                nxt = scores(j, c + 1, diag)
            vbt = vt_ref[:, pl.ds(start, s.shape[0])]
            m_old = m_ref[:, cs]
            m_new = jnp.maximum(m_old, bmax)
            alpha = jnp.where(m_old == -jnp.inf, 0.0, jnp.exp2(r_ref[:, cs] - m_new))
            p = jnp.exp2(s - m_new)
            l_ref[:, cs] = alpha * l_ref[:, cs] + jnp.sum(p, axis=0, keepdims=True)
            acc_ref[:, cs] = alpha * acc_ref[:, cs] + jnp.dot(vbt, p.astype(BF16),
                                                               preferred_element_type=F32)
            m_ref[:, cs] = m_new
            r_ref[:, cs] = m_new

    def fast_block(j, diag=None):
        start = pl.multiple_of(j * tk, tk)
        nxt = scores(j, 0, diag)
        tent = []
        for c in range(nch):
            cs = slice(c * cw, (c + 1) * cw)
            s, bmax = nxt
            if c + 1 < nch:
                nxt = scores(j, c + 1, diag)
            vbt = vt_ref[:, pl.ds(start, s.shape[0])]
            r_old = r_ref[:, cs]
            p = jnp.exp2(s - r_old)
            tent.append((cs, bmax, r_old, m_ref[:, cs], jnp.sum(p, axis=0, keepdims=True),
                         jnp.dot(vbt, p.astype(BF16), preferred_element_type=F32)))
        worst = functools.reduce(jnp.maximum, [
            jnp.max(jnp.maximum(bmax - r_old,
                                jnp.where(m_old == -jnp.inf, r_old - bmax, -jnp.inf)),
                    axis=1, keepdims=True)
            for _, bmax, r_old, m_old, _, _ in tent])
        safe = worst <= DA_LAZY_LIMIT
        for cs, bmax, r_old, m_old, lt, acct in tent:
            m_new = jnp.maximum(m_old, bmax)
            r_new = jnp.where(jnp.abs(m_new - r_old) > DA_REBASE, m_new, r_old)
            scale = jnp.exp2(r_old - r_new)
            l_old = l_ref[:, cs]
            acc_old = acc_ref[:, cs]
            l_ref[:, cs] = jnp.where(safe, (l_old + lt) * scale, l_old)
            acc_ref[:, cs] = jnp.where(safe, (acc_old + acct) * scale, acc_old)
            m_ref[:, cs] = jnp.where(safe, m_new, m_old)
            r_ref[:, cs] = jnp.where(safe, r_new, r_old)

        @pl.when(jnp.logical_not(jnp.max(worst) <= DA_LAZY_LIMIT))
        def _():
            exact_block(j, diag)

    def fast_body(j, carry):
        fast_block(j)
        return carry

    assert tq == tk
    lax.fori_loop(0, qi, fast_body, 0)
    fast_block(qi, diag=0)


    lp = lp_ref[...]
    lam = (jnp.exp(jnp.sum(lp[0:1, :] * lp[1:2, :], axis=1, keepdims=True))
           - jnp.exp(jnp.sum(lp[2:3, :] * lp[3:4, :], axis=1, keepdims=True)) + lambda_init)
    out = acc_ref[...] / l_ref[...]
    o = out[:, :tq] - lam * out[:, tq:]
    ms = jnp.mean(o * o, axis=0, keepdims=True)
    on = o * lax.rsqrt(ms + DA_EPS) * ng_ref[...] * (1.0 - lambda_init)
    o_ref[...] = on.T.astype(o_ref.dtype)


def _diff_attn(p, vt, lam_params, norm_g, lambda_init, *, batch, seq):
    tq = DA_TQ
    p3 = p.reshape(batch, seq, 2 * DA_QK + DA_H * DA_DV)
    kern = functools.partial(_diff_attn_kernel, lambda_init=lambda_init)
    o = pl.pallas_call(
        kern,
        out_shape=jax.ShapeDtypeStruct((batch, seq, DA_H * DA_DV), BF16),
        grid=(batch, DA_H, seq // tq),
        in_specs=[
            pl.BlockSpec((None, tq, 2 * DA_DH), lambda b, h, i: (b, i, h)),
            pl.BlockSpec((None, seq, 2 * DA_DH), lambda b, h, i: (b, 0, DA_H + h)),
            pl.BlockSpec((None, DA_DV, seq), lambda b, h, i: (b, h, 0)),
            pl.BlockSpec((4, DA_DH), lambda b, h, i: (0, 0)),
            pl.BlockSpec((DA_DV, 1), lambda b, h, i: (0, 0)),
        ],
        out_specs=pl.BlockSpec((None, tq, DA_DV), lambda b, h, i: (b, i, h)),
        scratch_shapes=[
            pltpu.VMEM((2 * tq, 2 * DA_DH), BF16),
            pltpu.VMEM((1, 2 * tq), F32),
            pltpu.VMEM((1, 2 * tq), F32),
            pltpu.VMEM((1, 2 * tq), F32),
            pltpu.VMEM((DA_DV, 2 * tq), F32),
        ],
        compiler_params=_cparams(("parallel", "parallel", "arbitrary")),
        name="diff_attn",
    )(p3, p3, vt, lam_params, norm_g)
    return o.reshape(batch * seq, DA_H * DA_DV)


def _pack_halves(y):
    halves = []
    for h in range(2):
        base = h * 2 * HALF_W
        lo = y[:, base:base + HALF_W].astype(BF16).astype(F32)
        hi = y[:, base + HALF_W:base + 2 * HALF_W].astype(BF16).astype(F32)
        lo_bits = lax.bitcast_convert_type(lo, U32) >> 16
        hi_bits = lax.bitcast_convert_type(hi, U32)
        halves.append(hi_bits | lo_bits)
    return halves


def _unpack_halves(w0, w1):
    parts = []
    for w in (w0, w1):
        parts.append(lax.bitcast_convert_type(w << 16, F32))
        parts.append(lax.bitcast_convert_type(w & jnp.uint32(0xFFFF0000), F32))
    return jnp.concatenate(parts, axis=1)


def _route_logits(x, g_ref, w_ref, b_ref):
    ms = jnp.mean(x * x, axis=-1, keepdims=True)
    hn32 = x * lax.rsqrt(ms + EPS) * g_ref[...]
    lg = jnp.dot(hn32.astype(BF16), w_ref[...], preferred_element_type=F32) + b_ref[...]
    return _pack_halves(hn32), lg


def _route_decide(lg, run_ref, ls_ref):
    lane = lax.broadcasted_iota(jnp.int32, lg.shape, 1).astype(F32)
    neg = -jnp.inf

    gmask = (lane >= N_EXPERTS) & (lane < N_EXPERTS + N_GROUPS)
    gl = jnp.where(gmask, lg, neg)
    gmax = jnp.max(gl, axis=1, keepdims=True)
    gidx = jnp.min(jnp.where(gl == gmax, lane, float(LANES)), axis=1, keepdims=True) - N_EXPERTS
    gsum = jnp.sum(jnp.where(gmask, jnp.exp(gl - gmax), 0.0), axis=1, keepdims=True)
    g_w = 1.0 / gsum

    emask = (lane >= gidx * EPG) & (lane < gidx * EPG + EPG)
    el = jnp.where(emask, lg, neg)
    emax = jnp.max(el, axis=1, keepdims=True)
    eexp = jnp.where(emask, jnp.exp(el - emax), 0.0)
    ep = eexp / jnp.sum(eexp, axis=1, keepdims=True)
    ep = jnp.where(emask, ep, -1.0)
    p1 = jnp.max(ep, axis=1, keepdims=True)
    i1 = jnp.min(jnp.where(ep == p1, lane, float(LANES)), axis=1, keepdims=True)
    ep2 = jnp.where(lane == i1, -1.0, ep)
    p2 = jnp.max(ep2, axis=1, keepdims=True)
    i2 = jnp.min(jnp.where(ep2 == p2, lane, float(LANES)), axis=1, keepdims=True)
    wsum = p1 + p2
    w1 = g_w * (p1 / wsum)
    w2 = g_w * (p2 / wsum)

    a1 = lane == i1
    a2 = lane == i2
    onehot = (a1 | a2).astype(BF16)
    before = jnp.dot(ls_ref[...], onehot, preferred_element_type=F32) + run_ref[...]
    rank1 = jnp.sum(jnp.where(a1, before, 0.0), axis=1, keepdims=True)
    rank2 = jnp.sum(jnp.where(a2, before, 0.0), axis=1, keepdims=True)
    run_ref[...] += jnp.sum(onehot.astype(F32), axis=0, keepdims=True)

    meta = (jnp.where(lane == 0, i1, 0.0) + jnp.where(lane == 1, i2, 0.0)
            + jnp.where(lane == 2, rank1, 0.0) + jnp.where(lane == 3, rank2, 0.0)
            + jnp.where(lane == 4, w1, 0.0) + jnp.where(lane == 5, w2, 0.0))
    return meta


def _res_router_kernel(a_ref, w_ref, r_ref, g_ref, wr_ref, br_ref, o_ref, hp_ref, meta_ref,
                       metat_ref, cnt_ref, run_ref, ls_ref):
    sub = ls_ref.shape[0]

    @pl.when(pl.program_id(0) == 0)
    def _():
        run_ref[...] = jnp.zeros_like(run_ref)
        r_i = lax.broadcasted_iota(jnp.int32, (sub, sub), 0)
        c_i = lax.broadcasted_iota(jnp.int32, (sub, sub), 1)
        ls_ref[...] = (c_i < r_i).astype(BF16)

    subs = [slice(i * sub, (i + 1) * sub) for i in range(a_ref.shape[0] // sub)]

    def project(r):
        x = r_ref[r, :] + jnp.dot(a_ref[r, :], w_ref[...], preferred_element_type=F32)
        o_ref[r, :] = x
        halves, lg = _route_logits(x, g_ref, wr_ref, br_ref)
        hp_ref[0, r, :] = halves[0]
        hp_ref[1, r, :] = halves[1]
        return lg

    nxt = project(subs[0])
    for i, r in enumerate(subs):
        lg = nxt
        if i + 1 < len(subs):
            nxt = project(subs[i + 1])
        meta = _route_decide(lg, run_ref, ls_ref)
        meta_ref[r, :] = meta
        metat_ref[:, r] = meta.T[0:8, :]
    cnt_ref[...] = run_ref[...]


def _res_router(a, w, res, g, w_group, b_group, w_expert, b_expert, *, tm, sub):
    t, k = a.shape
    d = w.shape[1]
    wr = jnp.zeros((d, LANES), BF16)
    wr = wr.at[:, :N_EXPERTS].set(w_expert.astype(BF16))
    wr = wr.at[:, N_EXPERTS:N_EXPERTS + N_GROUPS].set(w_group.astype(BF16))
    br = jnp.zeros((1, LANES), F32)
    br = br.at[0, :N_EXPERTS].set(b_expert.astype(F32))
    br = br.at[0, N_EXPERTS:N_EXPERTS + N_GROUPS].set(b_group.astype(F32))
    return pl.pallas_call(
        _res_router_kernel,
        out_shape=(jax.ShapeDtypeStruct((t, d), F32),
                   jax.ShapeDtypeStruct((2, t, HALF_W), U32),
                   jax.ShapeDtypeStruct((t, LANES), F32),
                   jax.ShapeDtypeStruct((8, t), F32),
                   jax.ShapeDtypeStruct((1, LANES), F32)),
        grid=(t // tm,),
        in_specs=[
            pl.BlockSpec((tm, k), lambda i: (i, 0)),
            pl.BlockSpec((k, d), lambda i: (0, 0)),
            pl.BlockSpec((tm, d), lambda i: (i, 0)),
            pl.BlockSpec((1, d), lambda i: (0, 0)),
            pl.BlockSpec((d, LANES), lambda i: (0, 0)),
            pl.BlockSpec((1, LANES), lambda i: (0, 0)),
        ],
        out_specs=(pl.BlockSpec((tm, d), lambda i: (i, 0)),
                   pl.BlockSpec((2, tm, HALF_W), lambda i: (0, i, 0)),
                   pl.BlockSpec((tm, LANES), lambda i: (i, 0)),
                   pl.BlockSpec((8, tm), lambda i: (0, i)),
                   pl.BlockSpec((1, LANES), lambda i: (0, 0))),
        scratch_shapes=[pltpu.VMEM((1, LANES), F32), pltpu.VMEM((sub, sub), BF16)],
        compiler_params=_cparams(("arbitrary",)),
        name="res_router",
    )(a, w, res, g, wr, br)


def _sc_mesh():
    return plsc.VectorSubcoreMesh(core_axis_name="c", subcore_axis_name="s")


def _sc_scatter2(x, i0, i1, n_out):
    n, d = x.shape

    @pl.kernel(out_type=jax.ShapeDtypeStruct((n_out, d), x.dtype), mesh=_sc_mesh())
    def k(x_hbm, i0_hbm, i1_hbm, o_hbm):
        def body(x_vmem, i0_vmem, i1_vmem):
            pltpu.sync_copy(x_vmem, o_hbm.at[i0_vmem.at[0]])
            pltpu.sync_copy(x_vmem, o_hbm.at[i1_vmem.at[0]])

        pltpu.emit_pipeline(
            body,
            grid=(n // SC_WINDOW,),
            in_specs=[pl.BlockSpec((SC_WINDOW, d), lambda i: (i, 0)),
                      pl.BlockSpec((1, SC_WINDOW), lambda i: (0, i)),
                      pl.BlockSpec((1, SC_WINDOW), lambda i: (0, i))],
            out_specs=[],
            core_axis_name=("c", "s"),
            dimension_semantics=(pltpu.PARALLEL,),
        )(x_hbm, i0_hbm, i1_hbm)

    return k(x, i0.reshape(1, n), i1.reshape(1, n))


def _sc_gather(x, idx):
    n = idx.shape[0]
    d = x.shape[1]

    @pl.kernel(out_type=jax.ShapeDtypeStruct((n, d), x.dtype), mesh=_sc_mesh())
    def k(x_hbm, i_hbm, o_hbm):
        def body(i_vmem, o_vmem):
            pltpu.sync_copy(x_hbm.at[i_vmem.at[0]], o_vmem)

        pltpu.emit_pipeline(
            body,
            grid=(n // SC_WINDOW,),
            in_specs=[pl.BlockSpec((1, SC_WINDOW), lambda i: (0, i))],
            out_specs=[pl.BlockSpec((SC_WINDOW, d), lambda i: (i, 0))],
            core_axis_name=("c", "s"),
            dimension_semantics=(pltpu.PARALLEL,),
        )(i_hbm, o_hbm)

    return k(x, idx.reshape(1, n))


def _experts_kernel(te_ref, nu_ref, xs_ref, wgu_ref, wd_ref, ys_ref, wgu_bf_ref, wd_bf_ref):
    j = pl.program_id(0)

    @pl.when(j < nu_ref[0])
    def _():
        @pl.when((j == 0) | (te_ref[j] != te_ref[jnp.maximum(j - 1, 0)]))
        def _():
            wgu_bf_ref[...] = wgu_ref[...].astype(BF16)
            wd_bf_ref[...] = wd_ref[...].astype(BF16)

        subs = [slice(i * MOE_SUB, (i + 1) * MOE_SUB) for i in range(MOE_TR // MOE_SUB)]
        xs = [_unpack_halves(xs_ref[0, r, :], xs_ref[1, r, :]).astype(BF16) for r in subs]
        gus = [jnp.dot(x, wgu_bf_ref[...], preferred_element_type=F32) for x in xs]
        acts = [(gu[:, :D_EXPERT] * jax.nn.sigmoid(gu[:, :D_EXPERT]) * gu[:, D_EXPERT:]).astype(BF16)
                for gu in gus]
        ys = [jnp.dot(a, wd_bf_ref[...], preferred_element_type=F32) for a in acts]
        for r, y in zip(subs, ys):
            halves = _pack_halves(y)
            ys_ref[0, r, :] = halves[0]
            ys_ref[1, r, :] = halves[1]


def _experts(xs, tile_expert, n_used, w_gu, w_down, layer):
    _, rows, _ = xs.shape
    d = w_gu.shape[2]
    return pl.pallas_call(
        _experts_kernel,
        out_shape=jax.ShapeDtypeStruct(xs.shape, U32),
        grid_spec=pltpu.PrefetchScalarGridSpec(
            num_scalar_prefetch=2,
            grid=(rows // MOE_TR,),
            in_specs=[
                pl.BlockSpec((2, MOE_TR, HALF_W), lambda j, te, nu: (0, j, 0)),
                pl.BlockSpec((None, None, d, 2 * D_EXPERT),
                             lambda j, te, nu: (layer, te[j], 0, 0)),
                pl.BlockSpec((None, None, D_EXPERT, d),
                             lambda j, te, nu: (layer, te[j], 0, 0)),
            ],
            out_specs=pl.BlockSpec((2, MOE_TR, HALF_W), lambda j, te, nu: (0, j, 0)),
            scratch_shapes=[pltpu.VMEM((d, 2 * D_EXPERT), BF16), pltpu.VMEM((D_EXPERT, d), BF16)],
        ),
        compiler_params=_cparams(("arbitrary",)),
        name="moe_experts",
    )(tile_expert, n_used, xs, w_gu, w_down)


def _moe_combined(x_ref, z_ref, meta_ref):
    meta = meta_ref[...]
    y_a = _unpack_halves(z_ref[0], z_ref[2])
    y_b = _unpack_halves(z_ref[1], z_ref[3])
    return x_ref[...] + meta[:, 4:5] * y_a + meta[:, 5:6] * y_b


def _combine_norm_kernel(x_ref, z_ref, meta_ref, g_ref, o_ref):
    out = _moe_combined(x_ref, z_ref, meta_ref)
    ms = jnp.mean(out * out, axis=-1, keepdims=True)
    o_ref[...] = out * lax.rsqrt(ms + EPS) * g_ref[...]


def _combine_norm(x, z, meta, g, *, tm):
    t, d = x.shape
    return pl.pallas_call(
        _combine_norm_kernel,
        out_shape=jax.ShapeDtypeStruct((t, d), F32),
        grid=(t // tm,),
        in_specs=[
            pl.BlockSpec((tm, d), lambda i: (i, 0)),
            pl.BlockSpec((4, tm, HALF_W), lambda i: (0, i, 0)),
            pl.BlockSpec((tm, LANES), lambda i: (i, 0)),
            pl.BlockSpec((1, d), lambda i: (0, 0)),
        ],
        out_specs=pl.BlockSpec((tm, d), lambda i: (i, 0)),
        compiler_params=_cparams(("parallel",)),
        name="moe_combine",
    )(x, z, meta, g)


def _moe_layer(routing, w_gu, w_down, layer):
    hp, meta, metat, cnt = routing
    t = meta.shape[0]
    n_tiles = 2 * t // MOE_TR + N_EXPERTS
    rows = n_tiles * MOE_TR
    counts = cnt[0, :N_EXPERTS].astype(jnp.int32)
    tiles_e = (counts + MOE_TR - 1) // MOE_TR
    tiles_end = jnp.cumsum(tiles_e)
    row_off = (tiles_end - tiles_e) * MOE_TR
    experts = jnp.arange(N_EXPERTS, dtype=jnp.int32)[:, None]

    def region_start(e_row):
        return jnp.sum(jnp.where(e_row[None, :] == experts, row_off[:, None], 0), axis=0)

    e_a, e_b = metat[0].astype(jnp.int32), metat[1].astype(jnp.int32)
    pos_a = region_start(e_a) + metat[2].astype(jnp.int32)
    pos_b = region_start(e_b) + metat[3].astype(jnp.int32)
    tile_ids = jnp.arange(n_tiles, dtype=jnp.int32)
    tile_expert = jnp.minimum(
        jnp.sum((tile_ids[:, None] >= tiles_end[None, :]).astype(jnp.int32), axis=1),
        N_EXPERTS - 1)
    n_used = tiles_end[-1:].astype(jnp.int32)

    xs = _sc_scatter2(hp.reshape(2 * t, HALF_W),
                      jnp.concatenate([pos_a, pos_a + rows]),
                      jnp.concatenate([pos_b, pos_b + rows]), 2 * rows)
    ys = _experts(xs.reshape(2, rows, HALF_W), tile_expert, n_used,
                  w_gu, w_down, layer)
    z = _sc_gather(ys.reshape(2 * rows, HALF_W),
                   jnp.concatenate([pos_a, pos_b, pos_a + rows, pos_b + rows]))
    return z.reshape(4, t, HALF_W)


def kernel(x, norm_mix, norm_ffn, ml_w_in, ml_conv, ml_b_i, ml_b_f, ml_norm, ml_w_out, da_w_in, da_lq1, da_lk1, da_lq2, da_lk2, da_norm, da_w_out, moe_w_group, moe_b_group, moe_w_expert, moe_b_expert, moe_w_gu, moe_w_down, final_norm):
    batch, seq, d = x.shape
    xt = x.reshape(batch * seq, d)

    p, gcol, grow = _ml_proj(xt, norm_mix[0].reshape(1, d), ml_w_in[0], ml_b_i[0], ml_b_f[0],
                             tm=1024, tn=1024, out_dtype=BF16)
    assert p.shape[1] == 2 * ML_QK + 2 * ML_V
    y = _mlstm_core(p, gcol, grow, ml_conv[0], ml_norm[0].reshape(1, ML_V), batch=batch, seq=seq)
    xt, *routing = _res_router(y, ml_w_out[0].astype(BF16), xt, norm_ffn[0].reshape(1, d),
                               moe_w_group[0], moe_b_group[0], moe_w_expert[0], moe_b_expert[0],
                               tm=RR_TM, sub=RR_SUB)
    z = _moe_layer(routing, moe_w_gu, moe_w_down, 0)

    lambda_init = 0.8 - 0.6 * math.exp(-0.3 * 1)
    xt, p, vt = _norm_matmul_vt(xt, z, routing[1], norm_mix[1].reshape(1, d),
                                da_w_in[0].astype(BF16), tm=1024, tn=DA_H * DA_DV,
                                vt_block=2 * DA_QK // (DA_H * DA_DV), batch=batch, seq=seq,
                                out_dtype=BF16)
    lam_params = jnp.stack([da_lq1[0], da_lk1[0], da_lq2[0], da_lk2[0]]).astype(F32)
    a = _diff_attn(p, vt, lam_params, da_norm[0].reshape(DA_DV, 1), lambda_init, batch=batch,
                   seq=seq)
    xt, *routing = _res_router(a, da_w_out[0].astype(BF16), xt, norm_ffn[1].reshape(1, d),
                               moe_w_group[1], moe_b_group[1], moe_w_expert[1], moe_b_expert[1],
                               tm=RR_TM, sub=RR_SUB)
    z = _moe_layer(routing, moe_w_gu, moe_w_down, 1)
    out = _combine_norm(xt, z, routing[1], final_norm.reshape(1, d), tm=1024)
    return out.reshape(batch, seq, d)
```

```python
import functools
import math

import jax
import jax.numpy as jnp
from jax import lax
from jax.experimental import pallas as pl
from jax.experimental.pallas import tpu as pltpu
from jax.experimental.pallas import tpu_sc as plsc

F32 = jnp.float32
BF16 = jnp.bfloat16
U32 = jnp.uint32

D_MODEL = 1024
EPS = 1e-6
ML_H = 4
ML_DV = 512
ML_DQK = 256
ML_QK = ML_H * ML_DQK
ML_V = ML_H * ML_DV
CONV_K = 4
ML_CHUNK = 256
CONV_TAIL = 8
DA_H = 8
DA_DH = 64
DA_DV = 128
DA_QK = DA_H * 2 * DA_DH
DA_EPS = 1e-5
DA_TQ = 1024
DA_TK = 1024
DA_CW = 512
DA_KEY_ALIGN = 256
DA_LAZY_LIMIT = 64.0
DA_REBASE = 8.0
LOG2E = 1.4426950408889634
N_GROUPS = 4
EPG = 8
N_EXPERTS = 32
D_EXPERT = 256
RR_TM = 1024
RR_SUB = 256
MOE_TR = 512
MOE_SUB = 256
HALF_W = D_MODEL // 4
SC_WINDOW = 128
LANES = 128

VMEM_LIMIT = 48 * 1024 * 1024


def _cparams(sem):
    return pltpu.CompilerParams(dimension_semantics=sem, vmem_limit_bytes=VMEM_LIMIT)


def _norm_matmul_vt_kernel(x_ref, z_ref, meta_ref, g_ref, w_ref, x2_ref, o_ref, vt_ref, xn_ref,
                           *, vt_block):
    j = pl.program_id(1)

    @pl.when(j == 0)
    def _():
        x = _moe_combined(x_ref, z_ref, meta_ref)
        x2_ref[...] = x
        ms = jnp.mean(x * x, axis=-1, keepdims=True)
        xn_ref[...] = (x * lax.rsqrt(ms + EPS) * g_ref[...]).astype(BF16)

    res = jnp.dot(xn_ref[...], w_ref[...], preferred_element_type=F32)
    o_ref[...] = res.astype(o_ref.dtype)

    @pl.when(j == vt_block)
    def _():
        vt_ref[...] = res.T.astype(vt_ref.dtype)


def _norm_matmul_vt(x, z, meta, g, w, *, tm, tn, vt_block, batch, seq, out_dtype):
    t, d = x.shape
    n = w.shape[1]
    per_batch = seq // tm
    return pl.pallas_call(
        functools.partial(_norm_matmul_vt_kernel, vt_block=vt_block),
        out_shape=(jax.ShapeDtypeStruct((t, d), F32),
                   jax.ShapeDtypeStruct((t, n), out_dtype),
                   jax.ShapeDtypeStruct((batch, tn, seq), out_dtype)),
        grid=(t // tm, n // tn),
        in_specs=[
            pl.BlockSpec((tm, d), lambda i, j: (i, 0)),
            pl.BlockSpec((4, tm, HALF_W), lambda i, j: (0, i, 0)),
            pl.BlockSpec((tm, LANES), lambda i, j: (i, 0)),
            pl.BlockSpec((1, d), lambda i, j: (0, 0)),
            pl.BlockSpec((d, tn), lambda i, j: (0, j)),
        ],
        out_specs=(pl.BlockSpec((tm, d), lambda i, j: (i, 0)),
                   pl.BlockSpec((tm, tn), lambda i, j: (i, j)),
                   pl.BlockSpec((None, tn, tm), lambda i, j: (i // per_batch, 0, i % per_batch))),
        scratch_shapes=[pltpu.VMEM((tm, d), BF16)],
        compiler_params=_cparams(("parallel", "arbitrary")),
        name="norm_matmul_vt",
    )(x, z, meta, g, w)


def _log_sigmoid(x):
    return jnp.minimum(x, 0.0) - jnp.log1p(jnp.exp(-jnp.abs(x)))


def _ml_proj_kernel(x_ref, g_ref, w_ref, wc_ref, wr_ref, bc_ref, br_ref, o_ref, oc_ref, or_ref,
                    xn_ref):
    @pl.when(pl.program_id(1) == 0)
    def _():
        x = x_ref[...]
        ms = jnp.mean(x * x, axis=-1, keepdims=True)
        xn = (x * lax.rsqrt(ms + EPS) * g_ref[...]).astype(BF16)
        xn_ref[...] = xn
        gc = jnp.dot(xn, wc_ref[...], preferred_element_type=F32) + bc_ref[...]
        lane = lax.broadcasted_iota(jnp.int32, gc.shape, 1)
        oc_ref[...] = jnp.where(lane < ML_H, gc, _log_sigmoid(gc))
        gr = lax.dot_general(wr_ref[...], xn, (((1,), (1,)), ((), ())),
                             preferred_element_type=F32) + br_ref[...]
        row = lax.broadcasted_iota(jnp.int32, gr.shape, 0)
        or_ref[...] = jnp.where(row < ML_H, gr, _log_sigmoid(gr))

    o_ref[...] = jnp.dot(xn_ref[...], w_ref[...], preferred_element_type=F32).astype(o_ref.dtype)


def _ml_proj(x, g, w_in, b_i, b_f, *, tm, tn, out_dtype):
    t, d = x.shape
    ng = 2 * ML_H
    n_main = w_in.shape[1] - ng
    w_gates = w_in[:, n_main:]
    wc = jnp.zeros((d, LANES), BF16).at[:, :ng].set(w_gates.astype(BF16))
    wr = w_gates.T.astype(BF16)
    bias = jnp.concatenate([b_i, b_f]).astype(F32)
    bc = jnp.zeros((1, LANES), F32).at[0, :ng].set(bias)
    br = bias.reshape(ng, 1)
    return pl.pallas_call(
        _ml_proj_kernel,
        out_shape=(jax.ShapeDtypeStruct((t, n_main), out_dtype),
                   jax.ShapeDtypeStruct((t, LANES), F32), jax.ShapeDtypeStruct((ng, t), F32)),
        grid=(t // tm, n_main // tn),
        in_specs=[
            pl.BlockSpec((tm, d), lambda i, j: (i, 0)),
            pl.BlockSpec((1, d), lambda i, j: (0, 0)),
            pl.BlockSpec((d, tn), lambda i, j: (0, j)),
            pl.BlockSpec((d, LANES), lambda i, j: (0, 0)),
            pl.BlockSpec((ng, d), lambda i, j: (0, 0)),
            pl.BlockSpec((1, LANES), lambda i, j: (0, 0)),
            pl.BlockSpec((ng, 1), lambda i, j: (0, 0)),
        ],
        out_specs=(pl.BlockSpec((tm, tn), lambda i, j: (i, j)),
                   pl.BlockSpec((tm, LANES), lambda i, j: (i, 0)),
                   pl.BlockSpec((ng, tm), lambda i, j: (0, i))),
        scratch_shapes=[pltpu.VMEM((tm, d), BF16)],
        compiler_params=_cparams(("parallel", "arbitrary")),
        name="ml_proj",
    )(x, g, w_in.astype(BF16), wc, wr, bc, br)


def _split3(x):
    hi = x.astype(BF16)
    r = x - hi.astype(F32)
    mid = r.astype(BF16)
    lo = (r - mid.astype(F32)).astype(BF16)
    return hi, mid, lo


def _mlstm_kernel(q_ref, k_ref, v_ref, o_ref, gc_ref, gr_ref, cw_ref, ng_ref, y_ref,
                  tail_ref, ct_ref, n_ref, m_ref, shift_ref, tril_ref, triu_ref, neg_ref):
    L = ML_CHUNK
    c = pl.program_id(1)

    @pl.when(c == 0)
    def _():
        tail_ref[...] = jnp.zeros_like(tail_ref)
        ct_ref[...] = jnp.zeros_like(ct_ref)
        n_ref[...] = jnp.zeros_like(n_ref)
        m_ref[...] = jnp.zeros_like(m_ref)
        r_i = lax.broadcasted_iota(jnp.int32, (L, L), 0)
        c_i = lax.broadcasted_iota(jnp.int32, (L, L), 1)
        tril_ref[...] = (c_i <= r_i).astype(BF16)
        triu_ref[...] = (r_i <= c_i).astype(BF16)
        neg_ref[...] = jnp.where(c_i <= r_i, 0.0, -jnp.inf)
        for j in range(CONV_K - 1):
            shift_ref[j * L:(j + 1) * L, :] = (c_i == r_i - (CONV_K - 1 - j)).astype(BF16)

    u = jnp.concatenate([q_ref[...], k_ref[...]], axis=1)
    uf = u.astype(F32)
    shifted = jnp.dot(shift_ref[...], u, preferred_element_type=F32)
    conv = uf * cw_ref[CONV_K - 1:CONV_K, :]
    head = None
    for j in range(CONV_K - 1):
        conv = conv + shifted[j * L:(j + 1) * L, :] * cw_ref[j:j + 1, :]
        part = tail_ref[pl.ds(CONV_TAIL - (CONV_K - 1) + j, CONV_TAIL), :] * cw_ref[j:j + 1, :]
        head = part if head is None else head + part
    conv = jnp.concatenate([conv[:CONV_TAIL] + head, conv[CONV_TAIL:]], axis=0)
    tail_ref[0:CONV_TAIL, :] = uf[L - CONV_TAIL:, :]
    qk = conv * jax.nn.sigmoid(conv)
    q_all = (qk[:, :ML_QK] * (ML_DQK ** -0.5)).astype(BF16)
    k_all = qk[:, ML_QK:]

    tril = tril_ref[...]
    triu = triu_ref[...]
    gc = gc_ref[...]
    gr = gr_ref[...]
    bc_all = sum(jnp.dot(tril, p, preferred_element_type=F32) for p in _split3(gc))
    br_all = sum(jnp.dot(p, triu, preferred_element_type=F32) for p in _split3(gr))

    for h in range(ML_H):
        qh = q_all[:, h * ML_DQK:(h + 1) * ML_DQK]
        kh_f = k_all[:, h * ML_DQK:(h + 1) * ML_DQK]
        kh = kh_f.astype(BF16)
        vh = v_ref[:, h * ML_DV:(h + 1) * ML_DV]
        it_col = gc[:, h:h + 1]
        it_row = gr[h:h + 1, :]
        b_col = bc_all[:, ML_H + h:ML_H + h + 1]
        b_row = br_all[ML_H + h:ML_H + h + 1, :]
        m_prev = m_ref[h][:, 0:1]

        dmat = b_col + (it_row - b_row) + neg_ref[...]
        inter_log = b_col + m_prev
        m_t = jnp.maximum(inter_log, jnp.max(dmat, axis=1, keepdims=True))
        wts = jnp.exp(dmat - m_t)
        s = lax.dot_general(qh, kh, (((1,), (1,)), ((), ())), preferred_element_type=F32)
        sc = s * wts
        inter_scale = jnp.exp(inter_log - m_t)
        ct = ct_ref[h]
        num = (jnp.dot(sc.astype(BF16), vh, preferred_element_type=F32)
               + inter_scale * jnp.dot(qh, ct.astype(BF16), preferred_element_type=F32))
        n_row = n_ref[h]
        den = (jnp.sum(sc, axis=1, keepdims=True)
               + inter_scale * jnp.sum(qh.astype(F32) * n_row, axis=1, keepdims=True))
        h_out = num / jnp.maximum(jnp.abs(den), jnp.exp(-m_t))

        b_last = b_col[L - 1:L, :]
        lw_col = b_last - b_col + it_col
        lw_row = b_last - b_row + it_row
        m_new = jnp.maximum(b_last + m_prev, jnp.max(lw_row, axis=1, keepdims=True))
        ws_col = jnp.exp(lw_col - m_new)
        decay = jnp.exp(b_last + m_prev - m_new)
        kw = kh_f * ws_col
        ct_ref[h] = decay * ct + lax.dot_general(kw.astype(BF16), vh, (((0,), (0,)), ((), ())),
                                                  preferred_element_type=F32)
        n_ref[h] = decay * n_row + jnp.sum(kw, axis=0, keepdims=True)
        m_ref[h] = jnp.broadcast_to(m_new, (1, LANES))

        ms = jnp.mean(h_out * h_out, axis=1, keepdims=True)
        hn = h_out * lax.rsqrt(ms + EPS) * ng_ref[:, h * ML_DV:(h + 1) * ML_DV]
        og = o_ref[:, h * ML_DV:(h + 1) * ML_DV].astype(F32)
        y_ref[:, h * ML_DV:(h + 1) * ML_DV] = (hn * jax.nn.sigmoid(og)).astype(y_ref.dtype)


def _mlstm_core(p, gcol, grow, conv_w, norm_g, *, batch, seq):
    L = ML_CHUNK
    nc = seq // L
    p3 = p.reshape(batch, seq, 2 * ML_QK + 2 * ML_V)
    gc3 = gcol.reshape(batch, seq, LANES)
    gr3 = grow.reshape(2 * ML_H, batch, seq).transpose(1, 0, 2)
    y = pl.pallas_call(
        _mlstm_kernel,
        out_shape=jax.ShapeDtypeStruct((batch, seq, ML_V), BF16),
        grid=(batch, nc),
        in_specs=[
            pl.BlockSpec((None, L, ML_QK), lambda b, c: (b, c, 0)),
            pl.BlockSpec((None, L, ML_QK), lambda b, c: (b, c, 1)),
            pl.BlockSpec((None, L, ML_V), lambda b, c: (b, c, 1)),
            pl.BlockSpec((None, L, ML_V), lambda b, c: (b, c, 2)),
            pl.BlockSpec((None, L, LANES), lambda b, c: (b, c, 0)),
            pl.BlockSpec((None, 2 * ML_H, L), lambda b, c: (b, 0, c)),
            pl.BlockSpec((CONV_K, 2 * ML_QK), lambda b, c: (0, 0)),
            pl.BlockSpec((1, ML_V), lambda b, c: (0, 0)),
        ],
        out_specs=pl.BlockSpec((None, L, ML_V), lambda b, c: (b, c, 0)),
        scratch_shapes=[
            pltpu.VMEM((2 * CONV_TAIL, 2 * ML_QK), F32),
            pltpu.VMEM((ML_H, ML_DQK, ML_DV), F32),
            pltpu.VMEM((ML_H, 1, ML_DQK), F32),
            pltpu.VMEM((ML_H, 1, LANES), F32),
            pltpu.VMEM(((CONV_K - 1) * L, L), BF16),
            pltpu.VMEM((L, L), BF16),
            pltpu.VMEM((L, L), BF16),
            pltpu.VMEM((L, L), F32),
        ],
        compiler_params=_cparams(("parallel", "arbitrary")),
        name="mlstm_core",
    )(p3, p3, p3, p3, gc3, gr3, conv_w, norm_g)
    return y.reshape(batch * seq, ML_V)


def _diff_attn_kernel(q_ref, k_ref, vt_ref, lp_ref, ng_ref, o_ref, q2_ref, r_ref, m_ref, l_ref,
                      acc_ref, *, lambda_init):
    tq, tk, cw = DA_TQ, DA_TK, DA_CW
    qi = pl.program_id(2)
    q = q_ref[...].astype(F32) * (DA_DH ** -0.5 * LOG2E)
    lane = lax.broadcasted_iota(jnp.int32, q.shape, 1)
    q2_ref[0:tq, :] = jnp.where(lane < DA_DH, q, 0.0).astype(BF16)
    q2_ref[tq:2 * tq, :] = jnp.where(lane >= DA_DH, q, 0.0).astype(BF16)

    r_ref[...] = jnp.zeros_like(r_ref)
    m_ref[...] = jnp.full(m_ref.shape, -jnp.inf, F32)
    l_ref[...] = jnp.zeros_like(l_ref)
    acc_ref[...] = jnp.zeros_like(acc_ref)

    nch = 2 * tq // cw

    def keys_needed(c, diag):
        if diag is None:
            return tk
        visible = (c * cw) % tq + cw - diag * tk
        return max(0, min(tk, -(-visible // DA_KEY_ALIGN) * DA_KEY_ALIGN))

    def mask(s, c, diag):
        nk = s.shape[0]
        key = lax.broadcasted_iota(jnp.int32, (nk, cw), 0) + diag * tk
        qry = lax.broadcasted_iota(jnp.int32, (nk, cw), 1) + (c * cw) % tq
        return jnp.where(key <= qry, s, -jnp.inf)

    def scores(j, c, diag=None):
        nk = keys_needed(c, diag)
        if nk == 0:
            return None
        kb = k_ref[pl.ds(pl.multiple_of(j * tk, tk), nk), :]
        s = lax.dot_general(kb, q2_ref[c * cw:(c + 1) * cw, :], (((1,), (1,)), ((), ())),
                            preferred_element_type=F32)
        if diag is not None:
            s = mask(s, c, diag)
        return s, jnp.max(s, axis=0, keepdims=True)


    def exact_block(j, diag=None):
        start = pl.multiple_of(j * tk, tk)
        nxt = scores(j, 0, diag)
        for c in range(nch):
            cs = slice(c * cw, (c + 1) * cw)
            s, bmax = nxt
            if c + 1 < nch:
                nxt = scores(j, c + 1, diag)
            vbt = vt_ref[:, pl.ds(start, s.shape[0])]
            m_old = m_ref[:, cs]
            m_new = jnp.maximum(m_old, bmax)
            alpha = jnp.where(m_old == -jnp.inf, 0.0, jnp.exp2(r_ref[:, cs] - m_new))
            p = jnp.exp2(s - m_new)
            l_ref[:, cs] = alpha * l_ref[:, cs] + jnp.sum(p, axis=0, keepdims=True)
            acc_ref[:, cs] = alpha * acc_ref[:, cs] + jnp.dot(vbt, p.astype(BF16),
                                                               preferred_element_type=F32)
            m_ref[:, cs] = m_new
            r_ref[:, cs] = m_new

    def fast_block(j, diag=None):
        start = pl.multiple_of(j * tk, tk)
        nxt = scores(j, 0, diag)
        tent = []
        for c in range(nch):
            cs = slice(c * cw, (c + 1) * cw)
            s, bmax = nxt
            if c + 1 < nch:
                nxt = scores(j, c + 1, diag)
            vbt = vt_ref[:, pl.ds(start, s.shape[0])]
            r_old = r_ref[:, cs]
            p = jnp.exp2(s - r_old)
            tent.append((cs, bmax, r_old, m_ref[:, cs], jnp.sum(p, axis=0, keepdims=True),
                         jnp.dot(vbt, p.astype(BF16), preferred_element_type=F32)))
        worst = functools.reduce(jnp.maximum, [
            jnp.max(jnp.maximum(bmax - r_old,
                                jnp.where(m_old == -jnp.inf, r_old - bmax, -jnp.inf)),
                    axis=1, keepdims=True)
            for _, bmax, r_old, m_old, _, _ in tent])
        safe = worst <= DA_LAZY_LIMIT
        for cs, bmax, r_old, m_old, lt, acct in tent:
            m_new = jnp.maximum(m_old, bmax)
            r_new = jnp.where(jnp.abs(m_new - r_old) > DA_REBASE, m_new, r_old)
            scale = jnp.exp2(r_old - r_new)
            l_old = l_ref[:, cs]
            acc_old = acc_ref[:, cs]
            l_ref[:, cs] = jnp.where(safe, (l_old + lt) * scale, l_old)
            acc_ref[:, cs] = jnp.where(safe, (acc_old + acct) * scale, acc_old)
            m_ref[:, cs] = jnp.where(safe, m_new, m_old)
            r_ref[:, cs] = jnp.where(safe, r_new, r_old)

        @pl.when(jnp.logical_not(jnp.max(worst) <= DA_LAZY_LIMIT))
        def _():
            exact_block(j, diag)

    def fast_body(j, carry):
        fast_block(j)
        return carry

    assert tq == tk
    lax.fori_loop(0, qi, fast_body, 0)
    fast_block(qi, diag=0)


    lp = lp_ref[...]
    lam = (jnp.exp(jnp.sum(lp[0:1, :] * lp[1:2, :], axis=1, keepdims=True))
           - jnp.exp(jnp.sum(lp[2:3, :] * lp[3:4, :], axis=1, keepdims=True)) + lambda_init)
    out = acc_ref[...] / l_ref[...]
    o = out[:, :tq] - lam * out[:, tq:]
    ms = jnp.mean(o * o, axis=0, keepdims=True)
    on = o * lax.rsqrt(ms + DA_EPS) * ng_ref[...] * (1.0 - lambda_init)
    o_ref[...] = on.T.astype(o_ref.dtype)


def _diff_attn(p, vt, lam_params, norm_g, lambda_init, *, batch, seq):
    tq = DA_TQ
    p3 = p.reshape(batch, seq, 2 * DA_QK + DA_H * DA_DV)
    kern = functools.partial(_diff_attn_kernel, lambda_init=lambda_init)
    o = pl.pallas_call(
        kern,
        out_shape=jax.ShapeDtypeStruct((batch, seq, DA_H * DA_DV), BF16),
        grid=(batch, DA_H, seq // tq),
        in_specs=[
            pl.BlockSpec((None, tq, 2 * DA_DH), lambda b, h, i: (b, i, h)),
            pl.BlockSpec((None, seq, 2 * DA_DH), lambda b, h, i: (b, 0, DA_H + h)),
            pl.BlockSpec((None, DA_DV, seq), lambda b, h, i: (b, h, 0)),
            pl.BlockSpec((4, DA_DH), lambda b, h, i: (0, 0)),
            pl.BlockSpec((DA_DV, 1), lambda b, h, i: (0, 0)),
        ],
        out_specs=pl.BlockSpec((None, tq, DA_DV), lambda b, h, i: (b, i, h)),
        scratch_shapes=[
            pltpu.VMEM((2 * tq, 2 * DA_DH), BF16),
            pltpu.VMEM((1, 2 * tq), F32),
            pltpu.VMEM((1, 2 * tq), F32),
            pltpu.VMEM((1, 2 * tq), F32),
            pltpu.VMEM((DA_DV, 2 * tq), F32),
        ],
        compiler_params=_cparams(("parallel", "parallel", "arbitrary")),
        name="diff_attn",
    )(p3, p3, vt, lam_params, norm_g)
    return o.reshape(batch * seq, DA_H * DA_DV)


def _pack_halves(y):
    halves = []
    for h in range(2):
        base = h * 2 * HALF_W
        lo = y[:, base:base + HALF_W].astype(BF16).astype(F32)
        hi = y[:, base + HALF_W:base + 2 * HALF_W].astype(BF16).astype(F32)
        lo_bits = lax.bitcast_convert_type(lo, U32) >> 16
        hi_bits = lax.bitcast_convert_type(hi, U32)
        halves.append(hi_bits | lo_bits)
    return halves


def _unpack_halves(w0, w1):
    parts = []
    for w in (w0, w1):
        parts.append(lax.bitcast_convert_type(w << 16, F32))
        parts.append(lax.bitcast_convert_type(w & jnp.uint32(0xFFFF0000), F32))
    return jnp.concatenate(parts, axis=1)


def _route_logits(x, g_ref, w_ref, b_ref):
    ms = jnp.mean(x * x, axis=-1, keepdims=True)
    hn32 = x * lax.rsqrt(ms + EPS) * g_ref[...]
    lg = jnp.dot(hn32.astype(BF16), w_ref[...], preferred_element_type=F32) + b_ref[...]
    return _pack_halves(hn32), lg


def _route_decide(lg, run_ref, ls_ref):
    lane = lax.broadcasted_iota(jnp.int32, lg.shape, 1).astype(F32)
    neg = -jnp.inf

    gmask = (lane >= N_EXPERTS) & (lane < N_EXPERTS + N_GROUPS)
    gl = jnp.where(gmask, lg, neg)
    gmax = jnp.max(gl, axis=1, keepdims=True)
    gidx = jnp.min(jnp.where(gl == gmax, lane, float(LANES)), axis=1, keepdims=True) - N_EXPERTS
    gsum = jnp.sum(jnp.where(gmask, jnp.exp(gl - gmax), 0.0), axis=1, keepdims=True)
    g_w = 1.0 / gsum

    emask = (lane >= gidx * EPG) & (lane < gidx * EPG + EPG)
    el = jnp.where(emask, lg, neg)
    emax = jnp.max(el, axis=1, keepdims=True)
    eexp = jnp.where(emask, jnp.exp(el - emax), 0.0)
    ep = eexp / jnp.sum(eexp, axis=1, keepdims=True)
    ep = jnp.where(emask, ep, -1.0)
    p1 = jnp.max(ep, axis=1, keepdims=True)
    i1 = jnp.min(jnp.where(ep == p1, lane, float(LANES)), axis=1, keepdims=True)
    ep2 = jnp.where(lane == i1, -1.0, ep)
    p2 = jnp.max(ep2, axis=1, keepdims=True)
    i2 = jnp.min(jnp.where(ep2 == p2, lane, float(LANES)), axis=1, keepdims=True)
    wsum = p1 + p2
    w1 = g_w * (p1 / wsum)
    w2 = g_w * (p2 / wsum)

    a1 = lane == i1
    a2 = lane == i2
    onehot = (a1 | a2).astype(BF16)
    before = jnp.dot(ls_ref[...], onehot, preferred_element_type=F32) + run_ref[...]
    rank1 = jnp.sum(jnp.where(a1, before, 0.0), axis=1, keepdims=True)
    rank2 = jnp.sum(jnp.where(a2, before, 0.0), axis=1, keepdims=True)
    run_ref[...] += jnp.sum(onehot.astype(F32), axis=0, keepdims=True)

    meta = (jnp.where(lane == 0, i1, 0.0) + jnp.where(lane == 1, i2, 0.0)
            + jnp.where(lane == 2, rank1, 0.0) + jnp.where(lane == 3, rank2, 0.0)
            + jnp.where(lane == 4, w1, 0.0) + jnp.where(lane == 5, w2, 0.0))
    return meta


def _res_router_kernel(a_ref, w_ref, r_ref, g_ref, wr_ref, br_ref, o_ref, hp_ref, meta_ref,
                       metat_ref, cnt_ref, run_ref, ls_ref):
    sub = ls_ref.shape[0]

    @pl.when(pl.program_id(0) == 0)
    def _():
        run_ref[...] = jnp.zeros_like(run_ref)
        r_i = lax.broadcasted_iota(jnp.int32, (sub, sub), 0)
        c_i = lax.broadcasted_iota(jnp.int32, (sub, sub), 1)
        ls_ref[...] = (c_i < r_i).astype(BF16)

    subs = [slice(i * sub, (i + 1) * sub) for i in range(a_ref.shape[0] // sub)]

    def project(r):
        x = r_ref[r, :] + jnp.dot(a_ref[r, :], w_ref[...], preferred_element_type=F32)
        o_ref[r, :] = x
        halves, lg = _route_logits(x, g_ref, wr_ref, br_ref)
        hp_ref[0, r, :] = halves[0]
        hp_ref[1, r, :] = halves[1]
        return lg

    nxt = project(subs[0])
    for i, r in enumerate(subs):
        lg = nxt
        if i + 1 < len(subs):
            nxt = project(subs[i + 1])
        meta = _route_decide(lg, run_ref, ls_ref)
        meta_ref[r, :] = meta
        metat_ref[:, r] = meta.T[0:8, :]
    cnt_ref[...] = run_ref[...]


def _res_router(a, w, res, g, w_group, b_group, w_expert, b_expert, *, tm, sub):
    t, k = a.shape
    d = w.shape[1]
    wr = jnp.zeros((d, LANES), BF16)
    wr = wr.at[:, :N_EXPERTS].set(w_expert.astype(BF16))
    wr = wr.at[:, N_EXPERTS:N_EXPERTS + N_GROUPS].set(w_group.astype(BF16))
    br = jnp.zeros((1, LANES), F32)
    br = br.at[0, :N_EXPERTS].set(b_expert.astype(F32))
    br = br.at[0, N_EXPERTS:N_EXPERTS + N_GROUPS].set(b_group.astype(F32))
    return pl.pallas_call(
        _res_router_kernel,
        out_shape=(jax.ShapeDtypeStruct((t, d), F32),
                   jax.ShapeDtypeStruct((2, t, HALF_W), U32),
                   jax.ShapeDtypeStruct((t, LANES), F32),
                   jax.ShapeDtypeStruct((8, t), F32),
                   jax.ShapeDtypeStruct((1, LANES), F32)),
        grid=(t // tm,),
        in_specs=[
            pl.BlockSpec((tm, k), lambda i: (i, 0)),
            pl.BlockSpec((k, d), lambda i: (0, 0)),
            pl.BlockSpec((tm, d), lambda i: (i, 0)),
            pl.BlockSpec((1, d), lambda i: (0, 0)),
            pl.BlockSpec((d, LANES), lambda i: (0, 0)),
            pl.BlockSpec((1, LANES), lambda i: (0, 0)),
        ],
        out_specs=(pl.BlockSpec((tm, d), lambda i: (i, 0)),
                   pl.BlockSpec((2, tm, HALF_W), lambda i: (0, i, 0)),
                   pl.BlockSpec((tm, LANES), lambda i: (i, 0)),
                   pl.BlockSpec((8, tm), lambda i: (0, i)),
                   pl.BlockSpec((1, LANES), lambda i: (0, 0))),
        scratch_shapes=[pltpu.VMEM((1, LANES), F32), pltpu.VMEM((sub, sub), BF16)],
        compiler_params=_cparams(("arbitrary",)),
        name="res_router",
    )(a, w, res, g, wr, br)


def _sc_mesh():
    return plsc.VectorSubcoreMesh(core_axis_name="c", subcore_axis_name="s")


def _sc_scatter2(x, i0, i1, n_out):
    n, d = x.shape

    @pl.kernel(out_type=jax.ShapeDtypeStruct((n_out, d), x.dtype), mesh=_sc_mesh())
    def k(x_hbm, i0_hbm, i1_hbm, o_hbm):
        def body(x_vmem, i0_vmem, i1_vmem):
            pltpu.sync_copy(x_vmem, o_hbm.at[i0_vmem.at[0]])
            pltpu.sync_copy(x_vmem, o_hbm.at[i1_vmem.at[0]])

        pltpu.emit_pipeline(
            body,
            grid=(n // SC_WINDOW,),
            in_specs=[pl.BlockSpec((SC_WINDOW, d), lambda i: (i, 0)),
                      pl.BlockSpec((1, SC_WINDOW), lambda i: (0, i)),
                      pl.BlockSpec((1, SC_WINDOW), lambda i: (0, i))],
            out_specs=[],
            core_axis_name=("c", "s"),
            dimension_semantics=(pltpu.PARALLEL,),
        )(x_hbm, i0_hbm, i1_hbm)

    return k(x, i0.reshape(1, n), i1.reshape(1, n))


def _sc_gather(x, idx):
    n = idx.shape[0]
    d = x.shape[1]

    @pl.kernel(out_type=jax.ShapeDtypeStruct((n, d), x.dtype), mesh=_sc_mesh())
    def k(x_hbm, i_hbm, o_hbm):
        def body(i_vmem, o_vmem):
            pltpu.sync_copy(x_hbm.at[i_vmem.at[0]], o_vmem)

        pltpu.emit_pipeline(
            body,
            grid=(n // SC_WINDOW,),
            in_specs=[pl.BlockSpec((1, SC_WINDOW), lambda i: (0, i))],
            out_specs=[pl.BlockSpec((SC_WINDOW, d), lambda i: (i, 0))],
            core_axis_name=("c", "s"),
            dimension_semantics=(pltpu.PARALLEL,),
        )(i_hbm, o_hbm)

    return k(x, idx.reshape(1, n))


def _experts_kernel(te_ref, nu_ref, xs_ref, wgu_ref, wd_ref, ys_ref, wgu_bf_ref, wd_bf_ref):
    j = pl.program_id(0)

    @pl.when(j < nu_ref[0])
    def _():
        @pl.when((j == 0) | (te_ref[j] != te_ref[jnp.maximum(j - 1, 0)]))
        def _():
            wgu_bf_ref[...] = wgu_ref[...].astype(BF16)
            wd_bf_ref[...] = wd_ref[...].astype(BF16)

        subs = [slice(i * MOE_SUB, (i + 1) * MOE_SUB) for i in range(MOE_TR // MOE_SUB)]
        xs = [_unpack_halves(xs_ref[0, r, :], xs_ref[1, r, :]).astype(BF16) for r in subs]
        gus = [jnp.dot(x, wgu_bf_ref[...], preferred_element_type=F32) for x in xs]
        acts = [(gu[:, :D_EXPERT] * jax.nn.sigmoid(gu[:, :D_EXPERT]) * gu[:, D_EXPERT:]).astype(BF16)
                for gu in gus]
        ys = [jnp.dot(a, wd_bf_ref[...], preferred_element_type=F32) for a in acts]
        for r, y in zip(subs, ys):
            halves = _pack_halves(y)
            ys_ref[0, r, :] = halves[0]
            ys_ref[1, r, :] = halves[1]


def _experts(xs, tile_expert, n_used, w_gu, w_down, layer):
    _, rows, _ = xs.shape
    d = w_gu.shape[2]
    return pl.pallas_call(
        _experts_kernel,
        out_shape=jax.ShapeDtypeStruct(xs.shape, U32),
        grid_spec=pltpu.PrefetchScalarGridSpec(
            num_scalar_prefetch=2,
            grid=(rows // MOE_TR,),
            in_specs=[
                pl.BlockSpec((2, MOE_TR, HALF_W), lambda j, te, nu: (0, j, 0)),
                pl.BlockSpec((None, None, d, 2 * D_EXPERT),
                             lambda j, te, nu: (layer, te[j], 0, 0)),
                pl.BlockSpec((None, None, D_EXPERT, d),
                             lambda j, te, nu: (layer, te[j], 0, 0)),
            ],
            out_specs=pl.BlockSpec((2, MOE_TR, HALF_W), lambda j, te, nu: (0, j, 0)),
            scratch_shapes=[pltpu.VMEM((d, 2 * D_EXPERT), BF16), pltpu.VMEM((D_EXPERT, d), BF16)],
        ),
        compiler_params=_cparams(("arbitrary",)),
        name="moe_experts",
    )(tile_expert, n_used, xs, w_gu, w_down)


def _moe_combined(x_ref, z_ref, meta_ref):
    meta = meta_ref[...]
    y_a = _unpack_halves(z_ref[0], z_ref[2])
    y_b = _unpack_halves(z_ref[1], z_ref[3])
    return x_ref[...] + meta[:, 4:5] * y_a + meta[:, 5:6] * y_b


def _combine_norm_kernel(x_ref, z_ref, meta_ref, g_ref, o_ref):
    out = _moe_combined(x_ref, z_ref, meta_ref)
    ms = jnp.mean(out * out, axis=-1, keepdims=True)
    o_ref[...] = out * lax.rsqrt(ms + EPS) * g_ref[...]


def _combine_norm(x, z, meta, g, *, tm):
    t, d = x.shape
    return pl.pallas_call(
        _combine_norm_kernel,
        out_shape=jax.ShapeDtypeStruct((t, d), F32),
        grid=(t // tm,),
        in_specs=[
            pl.BlockSpec((tm, d), lambda i: (i, 0)),
            pl.BlockSpec((4, tm, HALF_W), lambda i: (0, i, 0)),
            pl.BlockSpec((tm, LANES), lambda i: (i, 0)),
            pl.BlockSpec((1, d), lambda i: (0, 0)),
        ],
        out_specs=pl.BlockSpec((tm, d), lambda i: (i, 0)),
        compiler_params=_cparams(("parallel",)),
        name="moe_combine",
    )(x, z, meta, g)


def _moe_layer(routing, w_gu, w_down, layer):
    hp, meta, metat, cnt = routing
    t = meta.shape[0]
    n_tiles = 2 * t // MOE_TR + N_EXPERTS
    rows = n_tiles * MOE_TR
    counts = cnt[0, :N_EXPERTS].astype(jnp.int32)
    tiles_e = (counts + MOE_TR - 1) // MOE_TR
    tiles_end = jnp.cumsum(tiles_e)
    row_off = (tiles_end - tiles_e) * MOE_TR
    experts = jnp.arange(N_EXPERTS, dtype=jnp.int32)[:, None]

    def region_start(e_row):
        return jnp.sum(jnp.where(e_row[None, :] == experts, row_off[:, None], 0), axis=0)

    e_a, e_b = metat[0].astype(jnp.int32), metat[1].astype(jnp.int32)
    pos_a = region_start(e_a) + metat[2].astype(jnp.int32)
    pos_b = region_start(e_b) + metat[3].astype(jnp.int32)
    tile_ids = jnp.arange(n_tiles, dtype=jnp.int32)
    tile_expert = jnp.minimum(
        jnp.sum((tile_ids[:, None] >= tiles_end[None, :]).astype(jnp.int32), axis=1),
        N_EXPERTS - 1)
    n_used = tiles_end[-1:].astype(jnp.int32)

    xs = _sc_scatter2(hp.reshape(2 * t, HALF_W),
                      jnp.concatenate([pos_a, pos_a + rows]),
                      jnp.concatenate([pos_b, pos_b + rows]), 2 * rows)
    ys = _experts(xs.reshape(2, rows, HALF_W), tile_expert, n_used,
                  w_gu, w_down, layer)
    z = _sc_gather(ys.reshape(2 * rows, HALF_W),
                   jnp.concatenate([pos_a, pos_b, pos_a + rows, pos_b + rows]))
    return z.reshape(4, t, HALF_W)


def kernel(x, norm_mix, norm_ffn, ml_w_in, ml_conv, ml_b_i, ml_b_f, ml_norm, ml_w_out, da_w_in, da_lq1, da_lk1, da_lq2, da_lk2, da_norm, da_w_out, moe_w_group, moe_b_group, moe_w_expert, moe_b_expert, moe_w_gu, moe_w_down, final_norm):
    batch, seq, d = x.shape
    xt = x.reshape(batch * seq, d)

    p, gcol, grow = _ml_proj(xt, norm_mix[0].reshape(1, d), ml_w_in[0], ml_b_i[0], ml_b_f[0],
                             tm=1024, tn=2048, out_dtype=BF16)
    assert p.shape[1] == 2 * ML_QK + 2 * ML_V
    y = _mlstm_core(p, gcol, grow, ml_conv[0], ml_norm[0].reshape(1, ML_V), batch=batch, seq=seq)
    xt, *routing = _res_router(y, ml_w_out[0].astype(BF16), xt, norm_ffn[0].reshape(1, d),
                               moe_w_group[0], moe_b_group[0], moe_w_expert[0], moe_b_expert[0],
                               tm=RR_TM, sub=RR_SUB)
    z = _moe_layer(routing, moe_w_gu, moe_w_down, 0)

    lambda_init = 0.8 - 0.6 * math.exp(-0.3 * 1)
    xt, p, vt = _norm_matmul_vt(xt, z, routing[1], norm_mix[1].reshape(1, d),
                                da_w_in[0].astype(BF16), tm=1024, tn=DA_H * DA_DV,
                                vt_block=2 * DA_QK // (DA_H * DA_DV), batch=batch, seq=seq,
                                out_dtype=BF16)
    lam_params = jnp.stack([da_lq1[0], da_lk1[0], da_lq2[0], da_lk2[0]]).astype(F32)
    a = _diff_attn(p, vt, lam_params, da_norm[0].reshape(DA_DV, 1), lambda_init, batch=batch,
                   seq=seq)
    xt, *routing = _res_router(a, da_w_out[0].astype(BF16), xt, norm_ffn[1].reshape(1, d),
                               moe_w_group[1], moe_b_group[1], moe_w_expert[1], moe_b_expert[1],
                               tm=RR_TM, sub=RR_SUB)
    z = _moe_layer(routing, moe_w_gu, moe_w_down, 1)
    out = _combine_norm(xt, z, routing[1], final_norm.reshape(1, d), tm=1024)
    return out.reshape(batch, seq, d)
```

```python
import functools
import math

import jax
import jax.numpy as jnp
from jax import lax
from jax.experimental import pallas as pl
from jax.experimental.pallas import tpu as pltpu
from jax.experimental.pallas import tpu_sc as plsc

F32 = jnp.float32
BF16 = jnp.bfloat16
U32 = jnp.uint32

D_MODEL = 1024
EPS = 1e-6
ML_H = 4
ML_DV = 512
ML_DQK = 256
ML_QK = ML_H * ML_DQK
ML_V = ML_H * ML_DV
CONV_K = 4
ML_CHUNK = 256
CONV_TAIL = 8
DA_H = 8
DA_DH = 64
DA_DV = 128
DA_QK = DA_H * 2 * DA_DH
DA_EPS = 1e-5
DA_TQ = 1024
DA_TK = 1024
DA_CW = 512
DA_KEY_ALIGN = 256
DA_LAZY_LIMIT = 64.0
DA_REBASE = 8.0
LOG2E = 1.4426950408889634
N_GROUPS = 4
EPG = 8
N_EXPERTS = 32
D_EXPERT = 256
RR_TM = 1024
RR_SUB = 256
MOE_TR = 512
MOE_SUB = 256
HALF_W = D_MODEL // 4
SC_WINDOW = 128
LANES = 128

VMEM_LIMIT = 48 * 1024 * 1024


def _cparams(sem):
    return pltpu.CompilerParams(dimension_semantics=sem, vmem_limit_bytes=VMEM_LIMIT)


def _norm_matmul_vt_kernel(x_ref, z_ref, meta_ref, g_ref, w_ref, x2_ref, o_ref, vt_ref,
                           *, vt_start, sub):
    for i in range(x_ref.shape[0] // sub):
        r = slice(i * sub, (i + 1) * sub)
        x = _moe_combined(x_ref.at[r, :], z_ref.at[:, r, :], meta_ref.at[r, :])
        x2_ref[r, :] = x
        ms = jnp.mean(x * x, axis=-1, keepdims=True)
        xn = (x * lax.rsqrt(ms + EPS) * g_ref[...]).astype(BF16)
        res = jnp.dot(xn, w_ref[...], preferred_element_type=F32)
        o_ref[r, :] = res.astype(o_ref.dtype)
        vt_ref[:, r] = res[:, vt_start:].T.astype(vt_ref.dtype)


def _norm_matmul_vt(x, z, meta, g, w, *, tm, sub, vt_start, batch, seq, out_dtype):
    t, d = x.shape
    n = w.shape[1]
    per_batch = seq // tm
    return pl.pallas_call(
        functools.partial(_norm_matmul_vt_kernel, vt_start=vt_start, sub=sub),
        out_shape=(jax.ShapeDtypeStruct((t, d), F32),
                   jax.ShapeDtypeStruct((t, n), out_dtype),
                   jax.ShapeDtypeStruct((batch, n - vt_start, seq), out_dtype)),
        grid=(t // tm,),
        in_specs=[
            pl.BlockSpec((tm, d), lambda i: (i, 0)),
            pl.BlockSpec((4, tm, HALF_W), lambda i: (0, i, 0)),
            pl.BlockSpec((tm, LANES), lambda i: (i, 0)),
            pl.BlockSpec((1, d), lambda i: (0, 0)),
            pl.BlockSpec((d, n), lambda i: (0, 0)),
        ],
        out_specs=(pl.BlockSpec((tm, d), lambda i: (i, 0)),
                   pl.BlockSpec((tm, n), lambda i: (i, 0)),
                   pl.BlockSpec((None, n - vt_start, tm),
                                lambda i: (i // per_batch, 0, i % per_batch))),
        compiler_params=_cparams(("parallel",)),
        name="norm_matmul_vt",
    )(x, z, meta, g, w)


def _log_sigmoid(x):
    return jnp.minimum(x, 0.0) - jnp.log1p(jnp.exp(-jnp.abs(x)))


def _ml_proj_kernel(x_ref, g_ref, w_ref, wc_ref, wr_ref, bc_ref, br_ref, o_ref, oc_ref, or_ref,
                    xn_ref):
    @pl.when(pl.program_id(1) == 0)
    def _():
        x = x_ref[...]
        ms = jnp.mean(x * x, axis=-1, keepdims=True)
        xn = (x * lax.rsqrt(ms + EPS) * g_ref[...]).astype(BF16)
        xn_ref[...] = xn
        gc = jnp.dot(xn, wc_ref[...], preferred_element_type=F32) + bc_ref[...]
        lane = lax.broadcasted_iota(jnp.int32, gc.shape, 1)
        oc_ref[...] = jnp.where(lane < ML_H, gc, _log_sigmoid(gc))
        gr = lax.dot_general(wr_ref[...], xn, (((1,), (1,)), ((), ())),
                             preferred_element_type=F32) + br_ref[...]
        row = lax.broadcasted_iota(jnp.int32, gr.shape, 0)
        or_ref[...] = jnp.where(row < ML_H, gr, _log_sigmoid(gr))

    o_ref[...] = jnp.dot(xn_ref[...], w_ref[...], preferred_element_type=F32).astype(o_ref.dtype)


def _ml_proj(x, g, w_in, b_i, b_f, *, tm, tn, out_dtype):
    t, d = x.shape
    ng = 2 * ML_H
    n_main = w_in.shape[1] - ng
    w_gates = w_in[:, n_main:]
    wc = jnp.zeros((d, LANES), BF16).at[:, :ng].set(w_gates.astype(BF16))
    wr = w_gates.T.astype(BF16)
    bias = jnp.concatenate([b_i, b_f]).astype(F32)
    bc = jnp.zeros((1, LANES), F32).at[0, :ng].set(bias)
    br = bias.reshape(ng, 1)
    return pl.pallas_call(
        _ml_proj_kernel,
        out_shape=(jax.ShapeDtypeStruct((t, n_main), out_dtype),
                   jax.ShapeDtypeStruct((t, LANES), F32), jax.ShapeDtypeStruct((ng, t), F32)),
        grid=(t // tm, n_main // tn),
        in_specs=[
            pl.BlockSpec((tm, d), lambda i, j: (i, 0)),
            pl.BlockSpec((1, d), lambda i, j: (0, 0)),
            pl.BlockSpec((d, tn), lambda i, j: (0, j)),
            pl.BlockSpec((d, LANES), lambda i, j: (0, 0)),
            pl.BlockSpec((ng, d), lambda i, j: (0, 0)),
            pl.BlockSpec((1, LANES), lambda i, j: (0, 0)),
            pl.BlockSpec((ng, 1), lambda i, j: (0, 0)),
        ],
        out_specs=(pl.BlockSpec((tm, tn), lambda i, j: (i, j)),
                   pl.BlockSpec((tm, LANES), lambda i, j: (i, 0)),
                   pl.BlockSpec((ng, tm), lambda i, j: (0, i))),
        scratch_shapes=[pltpu.VMEM((tm, d), BF16)],
        compiler_params=_cparams(("parallel", "arbitrary")),
        name="ml_proj",
    )(x, g, w_in.astype(BF16), wc, wr, bc, br)


def _split3(x):
    hi = x.astype(BF16)
    r = x - hi.astype(F32)
    mid = r.astype(BF16)
    lo = (r - mid.astype(F32)).astype(BF16)
    return hi, mid, lo


def _mlstm_kernel(q_ref, k_ref, v_ref, o_ref, gc_ref, gr_ref, cw_ref, ng_ref, y_ref,
                  tail_ref, ct_ref, n_ref, m_ref, shift_ref, tril_ref, triu_ref, neg_ref):
    L = ML_CHUNK
    c = pl.program_id(1)

    @pl.when(c == 0)
    def _():
        tail_ref[...] = jnp.zeros_like(tail_ref)
        ct_ref[...] = jnp.zeros_like(ct_ref)
        n_ref[...] = jnp.zeros_like(n_ref)
        m_ref[...] = jnp.zeros_like(m_ref)
        r_i = lax.broadcasted_iota(jnp.int32, (L, L), 0)
        c_i = lax.broadcasted_iota(jnp.int32, (L, L), 1)
        tril_ref[...] = (c_i <= r_i).astype(BF16)
        triu_ref[...] = (r_i <= c_i).astype(BF16)
        neg_ref[...] = jnp.where(c_i <= r_i, 0.0, -jnp.inf)
        for j in range(CONV_K - 1):
            shift_ref[j * L:(j + 1) * L, :] = (c_i == r_i - (CONV_K - 1 - j)).astype(BF16)

    u = jnp.concatenate([q_ref[...], k_ref[...]], axis=1)
    uf = u.astype(F32)
    shifted = jnp.dot(shift_ref[...], u, preferred_element_type=F32)
    conv = uf * cw_ref[CONV_K - 1:CONV_K, :]
    head = None
    for j in range(CONV_K - 1):
        conv = conv + shifted[j * L:(j + 1) * L, :] * cw_ref[j:j + 1, :]
        part = tail_ref[pl.ds(CONV_TAIL - (CONV_K - 1) + j, CONV_TAIL), :] * cw_ref[j:j + 1, :]
        head = part if head is None else head + part
    conv = jnp.concatenate([conv[:CONV_TAIL] + head, conv[CONV_TAIL:]], axis=0)
    tail_ref[0:CONV_TAIL, :] = uf[L - CONV_TAIL:, :]
    qk = conv * jax.nn.sigmoid(conv)
    q_all = (qk[:, :ML_QK] * (ML_DQK ** -0.5)).astype(BF16)
    k_all = qk[:, ML_QK:]

    tril = tril_ref[...]
    triu = triu_ref[...]
    gc = gc_ref[...]
    gr = gr_ref[...]
    bc_all = sum(jnp.dot(tril, p, preferred_element_type=F32) for p in _split3(gc))
    br_all = sum(jnp.dot(p, triu, preferred_element_type=F32) for p in _split3(gr))

    for h in range(ML_H):
        qh = q_all[:, h * ML_DQK:(h + 1) * ML_DQK]
        kh_f = k_all[:, h * ML_DQK:(h + 1) * ML_DQK]
        kh = kh_f.astype(BF16)
        vh = v_ref[:, h * ML_DV:(h + 1) * ML_DV]
        it_col = gc[:, h:h + 1]
        it_row = gr[h:h + 1, :]
        b_col = bc_all[:, ML_H + h:ML_H + h + 1]
        b_row = br_all[ML_H + h:ML_H + h + 1, :]
        m_prev = m_ref[h][:, 0:1]

        dmat = b_col + (it_row - b_row) + neg_ref[...]
        inter_log = b_col + m_prev
        m_t = jnp.maximum(inter_log, jnp.max(dmat, axis=1, keepdims=True))
        wts = jnp.exp(dmat - m_t)
        s = lax.dot_general(qh, kh, (((1,), (1,)), ((), ())), preferred_element_type=F32)
        sc = s * wts
        inter_scale = jnp.exp(inter_log - m_t)
        ct = ct_ref[h]
        num = (jnp.dot(sc.astype(BF16), vh, preferred_element_type=F32)
               + inter_scale * jnp.dot(qh, ct.astype(BF16), preferred_element_type=F32))
        n_row = n_ref[h]
        den = (jnp.sum(sc, axis=1, keepdims=True)
               + inter_scale * jnp.sum(qh.astype(F32) * n_row, axis=1, keepdims=True))
        h_out = num / jnp.maximum(jnp.abs(den), jnp.exp(-m_t))

        b_last = b_col[L - 1:L, :]
        lw_col = b_last - b_col + it_col
        lw_row = b_last - b_row + it_row
        m_new = jnp.maximum(b_last + m_prev, jnp.max(lw_row, axis=1, keepdims=True))
        ws_col = jnp.exp(lw_col - m_new)
        decay = jnp.exp(b_last + m_prev - m_new)
        kw = kh_f * ws_col
        ct_ref[h] = decay * ct + lax.dot_general(kw.astype(BF16), vh, (((0,), (0,)), ((), ())),
                                                  preferred_element_type=F32)
        n_ref[h] = decay * n_row + jnp.sum(kw, axis=0, keepdims=True)
        m_ref[h] = jnp.broadcast_to(m_new, (1, LANES))

        ms = jnp.mean(h_out * h_out, axis=1, keepdims=True)
        hn = h_out * lax.rsqrt(ms + EPS) * ng_ref[:, h * ML_DV:(h + 1) * ML_DV]
        og = o_ref[:, h * ML_DV:(h + 1) * ML_DV].astype(F32)
        y_ref[:, h * ML_DV:(h + 1) * ML_DV] = (hn * jax.nn.sigmoid(og)).astype(y_ref.dtype)


def _mlstm_core(p, gcol, grow, conv_w, norm_g, *, batch, seq):
    L = ML_CHUNK
    nc = seq // L
    p3 = p.reshape(batch, seq, 2 * ML_QK + 2 * ML_V)
    gc3 = gcol.reshape(batch, seq, LANES)
    gr3 = grow.reshape(2 * ML_H, batch, seq).transpose(1, 0, 2)
    y = pl.pallas_call(
        _mlstm_kernel,
        out_shape=jax.ShapeDtypeStruct((batch, seq, ML_V), BF16),
        grid=(batch, nc),
        in_specs=[
            pl.BlockSpec((None, L, ML_QK), lambda b, c: (b, c, 0)),
            pl.BlockSpec((None, L, ML_QK), lambda b, c: (b, c, 1)),
            pl.BlockSpec((None, L, ML_V), lambda b, c: (b, c, 1)),
            pl.BlockSpec((None, L, ML_V), lambda b, c: (b, c, 2)),
            pl.BlockSpec((None, L, LANES), lambda b, c: (b, c, 0)),
            pl.BlockSpec((None, 2 * ML_H, L), lambda b, c: (b, 0, c)),
            pl.BlockSpec((CONV_K, 2 * ML_QK), lambda b, c: (0, 0)),
            pl.BlockSpec((1, ML_V), lambda b, c: (0, 0)),
        ],
        out_specs=pl.BlockSpec((None, L, ML_V), lambda b, c: (b, c, 0)),
        scratch_shapes=[
            pltpu.VMEM((2 * CONV_TAIL, 2 * ML_QK), F32),
            pltpu.VMEM((ML_H, ML_DQK, ML_DV), F32),
            pltpu.VMEM((ML_H, 1, ML_DQK), F32),
            pltpu.VMEM((ML_H, 1, LANES), F32),
            pltpu.VMEM(((CONV_K - 1) * L, L), BF16),
            pltpu.VMEM((L, L), BF16),
            pltpu.VMEM((L, L), BF16),
            pltpu.VMEM((L, L), F32),
        ],
        compiler_params=_cparams(("parallel", "arbitrary")),
        name="mlstm_core",
    )(p3, p3, p3, p3, gc3, gr3, conv_w, norm_g)
    return y.reshape(batch * seq, ML_V)


def _diff_attn_kernel(q_ref, k_ref, vt_ref, lp_ref, ng_ref, o_ref, q2_ref, r_ref, m_ref, l_ref,
                      acc_ref, *, lambda_init):
    tq, tk, cw = DA_TQ, DA_TK, DA_CW
    qi = pl.program_id(2)
    q = q_ref[...].astype(F32) * (DA_DH ** -0.5 * LOG2E)
    lane = lax.broadcasted_iota(jnp.int32, q.shape, 1)
    q2_ref[0:tq, :] = jnp.where(lane < DA_DH, q, 0.0).astype(BF16)
    q2_ref[tq:2 * tq, :] = jnp.where(lane >= DA_DH, q, 0.0).astype(BF16)

    r_ref[...] = jnp.zeros_like(r_ref)
    m_ref[...] = jnp.full(m_ref.shape, -jnp.inf, F32)
    l_ref[...] = jnp.zeros_like(l_ref)
    acc_ref[...] = jnp.zeros_like(acc_ref)

    nch = 2 * tq // cw

    def keys_needed(c, diag):
        if diag is None:
            return tk
        visible = (c * cw) % tq + cw - diag * tk
        return max(0, min(tk, -(-visible // DA_KEY_ALIGN) * DA_KEY_ALIGN))

    def mask(s, c, diag):
        nk = s.shape[0]
        key = lax.broadcasted_iota(jnp.int32, (nk, cw), 0) + diag * tk
        qry = lax.broadcasted_iota(jnp.int32, (nk, cw), 1) + (c * cw) % tq
        return jnp.where(key <= qry, s, -jnp.inf)

    def scores(j, c, diag=None):
        nk = keys_needed(c, diag)
        if nk == 0:
            return None
        kb = k_ref[pl.ds(pl.multiple_of(j * tk, tk), nk), :]
        s = lax.dot_general(kb, q2_ref[c * cw:(c + 1) * cw, :], (((1,), (1,)), ((), ())),
                            preferred_element_type=F32)
        if diag is not None:
            s = mask(s, c, diag)
        return s, jnp.max(s, axis=0, keepdims=True)


    def exact_block(j, diag=None):
        start = pl.multiple_of(j * tk, tk)
        nxt = scores(j, 0, diag)
        for c in range(nch):
            cs = slice(c * cw, (c + 1) * cw)
            s, bmax = nxt
            if c + 1 < nch:
                nxt = scores(j, c + 1, diag)
            vbt = vt_ref[:, pl.ds(start, s.shape[0])]
            m_old = m_ref[:, cs]
            m_new = jnp.maximum(m_old, bmax)
            alpha = jnp.where(m_old == -jnp.inf, 0.0, jnp.exp2(r_ref[:, cs] - m_new))
            p = jnp.exp2(s - m_new)
            l_ref[:, cs] = alpha * l_ref[:, cs] + jnp.sum(p, axis=0, keepdims=True)
            acc_ref[:, cs] = alpha * acc_ref[:, cs] + jnp.dot(vbt, p.astype(BF16),
                                                               preferred_element_type=F32)
            m_ref[:, cs] = m_new
            r_ref[:, cs] = m_new

    def fast_block(j, diag=None):
        start = pl.multiple_of(j * tk, tk)
        nxt = scores(j, 0, diag)
        tent = []
        for c in range(nch):
            cs = slice(c * cw, (c + 1) * cw)
            s, bmax = nxt
            if c + 1 < nch:
                nxt = scores(j, c + 1, diag)
            vbt = vt_ref[:, pl.ds(start, s.shape[0])]
            r_old = r_ref[:, cs]
            p = jnp.exp2(s - r_old)
            tent.append((cs, bmax, r_old, m_ref[:, cs], jnp.sum(p, axis=0, keepdims=True),
                         jnp.dot(vbt, p.astype(BF16), preferred_element_type=F32)))
        worst = functools.reduce(jnp.maximum, [
            jnp.max(jnp.maximum(bmax - r_old,
                                jnp.where(m_old == -jnp.inf, r_old - bmax, -jnp.inf)),
                    axis=1, keepdims=True)
            for _, bmax, r_old, m_old, _, _ in tent])
        safe = worst <= DA_LAZY_LIMIT
        for cs, bmax, r_old, m_old, lt, acct in tent:
            m_new = jnp.maximum(m_old, bmax)
            r_new = jnp.where(jnp.abs(m_new - r_old) > DA_REBASE, m_new, r_old)
            scale = jnp.exp2(r_old - r_new)
            l_old = l_ref[:, cs]
            acc_old = acc_ref[:, cs]
            l_ref[:, cs] = jnp.where(safe, (l_old + lt) * scale, l_old)
            acc_ref[:, cs] = jnp.where(safe, (acc_old + acct) * scale, acc_old)
            m_ref[:, cs] = jnp.where(safe, m_new, m_old)
            r_ref[:, cs] = jnp.where(safe, r_new, r_old)

        @pl.when(jnp.logical_not(jnp.max(worst) <= DA_LAZY_LIMIT))
        def _():
            exact_block(j, diag)

    def fast_body(j, carry):
        fast_block(j)
        return carry

    assert tq == tk
    lax.fori_loop(0, qi, fast_body, 0)
    fast_block(qi, diag=0)


    lp = lp_ref[...]
    lam = (jnp.exp(jnp.sum(lp[0:1, :] * lp[1:2, :], axis=1, keepdims=True))
           - jnp.exp(jnp.sum(lp[2:3, :] * lp[3:4, :], axis=1, keepdims=True)) + lambda_init)
    out = acc_ref[...] / l_ref[...]
    o = out[:, :tq] - lam * out[:, tq:]
    ms = jnp.mean(o * o, axis=0, keepdims=True)
    on = o * lax.rsqrt(ms + DA_EPS) * ng_ref[...] * (1.0 - lambda_init)
    o_ref[...] = on.T.astype(o_ref.dtype)


def _diff_attn(p, vt, lam_params, norm_g, lambda_init, *, batch, seq):
    tq = DA_TQ
    p3 = p.reshape(batch, seq, 2 * DA_QK + DA_H * DA_DV)
    kern = functools.partial(_diff_attn_kernel, lambda_init=lambda_init)
    o = pl.pallas_call(
        kern,
        out_shape=jax.ShapeDtypeStruct((batch, seq, DA_H * DA_DV), BF16),
        grid=(batch, DA_H, seq // tq),
        in_specs=[
            pl.BlockSpec((None, tq, 2 * DA_DH), lambda b, h, i: (b, i, h)),
            pl.BlockSpec((None, seq, 2 * DA_DH), lambda b, h, i: (b, 0, DA_H + h)),
            pl.BlockSpec((None, DA_DV, seq), lambda b, h, i: (b, h, 0)),
            pl.BlockSpec((4, DA_DH), lambda b, h, i: (0, 0)),
            pl.BlockSpec((DA_DV, 1), lambda b, h, i: (0, 0)),
        ],
        out_specs=pl.BlockSpec((None, tq, DA_DV), lambda b, h, i: (b, i, h)),
        scratch_shapes=[
            pltpu.VMEM((2 * tq, 2 * DA_DH), BF16),
            pltpu.VMEM((1, 2 * tq), F32),
            pltpu.VMEM((1, 2 * tq), F32),
            pltpu.VMEM((1, 2 * tq), F32),
            pltpu.VMEM((DA_DV, 2 * tq), F32),
        ],
        compiler_params=_cparams(("parallel", "parallel", "arbitrary")),
        name="diff_attn",
    )(p3, p3, vt, lam_params, norm_g)
    return o.reshape(batch * seq, DA_H * DA_DV)


def _pack_halves(y):
    halves = []
    for h in range(2):
        base = h * 2 * HALF_W
        lo = y[:, base:base + HALF_W].astype(BF16).astype(F32)
        hi = y[:, base + HALF_W:base + 2 * HALF_W].astype(BF16).astype(F32)
        lo_bits = lax.bitcast_convert_type(lo, U32) >> 16
        hi_bits = lax.bitcast_convert_type(hi, U32)
        halves.append(hi_bits | lo_bits)
    return halves


def _unpack_halves(w0, w1):
    parts = []
    for w in (w0, w1):
        parts.append(lax.bitcast_convert_type(w << 16, F32))
        parts.append(lax.bitcast_convert_type(w & jnp.uint32(0xFFFF0000), F32))
    return jnp.concatenate(parts, axis=1)


def _route_logits(x, g_ref, w_ref, b_ref):
    ms = jnp.mean(x * x, axis=-1, keepdims=True)
    hn32 = x * lax.rsqrt(ms + EPS) * g_ref[...]
    lg = jnp.dot(hn32.astype(BF16), w_ref[...], preferred_element_type=F32) + b_ref[...]
    return _pack_halves(hn32), lg


def _route_decide(lg, run_ref, ls_ref):
    lane = lax.broadcasted_iota(jnp.int32, lg.shape, 1).astype(F32)
    neg = -jnp.inf

    gmask = (lane >= N_EXPERTS) & (lane < N_EXPERTS + N_GROUPS)
    gl = jnp.where(gmask, lg, neg)
    gmax = jnp.max(gl, axis=1, keepdims=True)
    gidx = jnp.min(jnp.where(gl == gmax, lane, float(LANES)), axis=1, keepdims=True) - N_EXPERTS
    gsum = jnp.sum(jnp.where(gmask, jnp.exp(gl - gmax), 0.0), axis=1, keepdims=True)
    g_w = 1.0 / gsum

    emask = (lane >= gidx * EPG) & (lane < gidx * EPG + EPG)
    el = jnp.where(emask, lg, neg)
    emax = jnp.max(el, axis=1, keepdims=True)
    eexp = jnp.where(emask, jnp.exp(el - emax), 0.0)
    ep = eexp / jnp.sum(eexp, axis=1, keepdims=True)
    ep = jnp.where(emask, ep, -1.0)
    p1 = jnp.max(ep, axis=1, keepdims=True)
    i1 = jnp.min(jnp.where(ep == p1, lane, float(LANES)), axis=1, keepdims=True)
    ep2 = jnp.where(lane == i1, -1.0, ep)
    p2 = jnp.max(ep2, axis=1, keepdims=True)
    i2 = jnp.min(jnp.where(ep2 == p2, lane, float(LANES)), axis=1, keepdims=True)
    wsum = p1 + p2
    w1 = g_w * (p1 / wsum)
    w2 = g_w * (p2 / wsum)

    a1 = lane == i1
    a2 = lane == i2
    onehot = (a1 | a2).astype(BF16)
    before = jnp.dot(ls_ref[...], onehot, preferred_element_type=F32) + run_ref[...]
    rank1 = jnp.sum(jnp.where(a1, before, 0.0), axis=1, keepdims=True)
    rank2 = jnp.sum(jnp.where(a2, before, 0.0), axis=1, keepdims=True)
    run_ref[...] += jnp.sum(onehot.astype(F32), axis=0, keepdims=True)

    meta = (jnp.where(lane == 0, i1, 0.0) + jnp.where(lane == 1, i2, 0.0)
            + jnp.where(lane == 2, rank1, 0.0) + jnp.where(lane == 3, rank2, 0.0)
            + jnp.where(lane == 4, w1, 0.0) + jnp.where(lane == 5, w2, 0.0))
    return meta


def _res_router_kernel(a_ref, w_ref, r_ref, g_ref, wr_ref, br_ref, o_ref, hp_ref, meta_ref,
                       metat_ref, cnt_ref, run_ref, ls_ref):
    sub = ls_ref.shape[0]

    @pl.when(pl.program_id(0) == 0)
    def _():
        run_ref[...] = jnp.zeros_like(run_ref)
        r_i = lax.broadcasted_iota(jnp.int32, (sub, sub), 0)
        c_i = lax.broadcasted_iota(jnp.int32, (sub, sub), 1)
        ls_ref[...] = (c_i < r_i).astype(BF16)

    subs = [slice(i * sub, (i + 1) * sub) for i in range(a_ref.shape[0] // sub)]

    def project(r):
        x = r_ref[r, :] + jnp.dot(a_ref[r, :], w_ref[...], preferred_element_type=F32)
        o_ref[r, :] = x
        halves, lg = _route_logits(x, g_ref, wr_ref, br_ref)
        hp_ref[0, r, :] = halves[0]
        hp_ref[1, r, :] = halves[1]
        return lg

    nxt = project(subs[0])
    for i, r in enumerate(subs):
        lg = nxt
        if i + 1 < len(subs):
            nxt = project(subs[i + 1])
        meta = _route_decide(lg, run_ref, ls_ref)
        meta_ref[r, :] = meta
        metat_ref[:, r] = meta.T[0:8, :]
    cnt_ref[...] = run_ref[...]


def _res_router(a, w, res, g, w_group, b_group, w_expert, b_expert, *, tm, sub):
    t, k = a.shape
    d = w.shape[1]
    wr = jnp.zeros((d, LANES), BF16)
    wr = wr.at[:, :N_EXPERTS].set(w_expert.astype(BF16))
    wr = wr.at[:, N_EXPERTS:N_EXPERTS + N_GROUPS].set(w_group.astype(BF16))
    br = jnp.zeros((1, LANES), F32)
    br = br.at[0, :N_EXPERTS].set(b_expert.astype(F32))
    br = br.at[0, N_EXPERTS:N_EXPERTS + N_GROUPS].set(b_group.astype(F32))
    return pl.pallas_call(
        _res_router_kernel,
        out_shape=(jax.ShapeDtypeStruct((t, d), F32),
                   jax.ShapeDtypeStruct((2, t, HALF_W), U32),
                   jax.ShapeDtypeStruct((t, LANES), F32),
                   jax.ShapeDtypeStruct((8, t), F32),
                   jax.ShapeDtypeStruct((1, LANES), F32)),
        grid=(t // tm,),
        in_specs=[
            pl.BlockSpec((tm, k), lambda i: (i, 0)),
            pl.BlockSpec((k, d), lambda i: (0, 0)),
            pl.BlockSpec((tm, d), lambda i: (i, 0)),
            pl.BlockSpec((1, d), lambda i: (0, 0)),
            pl.BlockSpec((d, LANES), lambda i: (0, 0)),
            pl.BlockSpec((1, LANES), lambda i: (0, 0)),
        ],
        out_specs=(pl.BlockSpec((tm, d), lambda i: (i, 0)),
                   pl.BlockSpec((2, tm, HALF_W), lambda i: (0, i, 0)),
                   pl.BlockSpec((tm, LANES), lambda i: (i, 0)),
                   pl.BlockSpec((8, tm), lambda i: (0, i)),
                   pl.BlockSpec((1, LANES), lambda i: (0, 0))),
        scratch_shapes=[pltpu.VMEM((1, LANES), F32), pltpu.VMEM((sub, sub), BF16)],
        compiler_params=_cparams(("arbitrary",)),
        name="res_router",
    )(a, w, res, g, wr, br)


def _sc_mesh():
    return plsc.VectorSubcoreMesh(core_axis_name="c", subcore_axis_name="s")


def _sc_scatter2(x, i0, i1, n_out):
    n, d = x.shape

    @pl.kernel(out_type=jax.ShapeDtypeStruct((n_out, d), x.dtype), mesh=_sc_mesh())
    def k(x_hbm, i0_hbm, i1_hbm, o_hbm):
        def body(x_vmem, i0_vmem, i1_vmem):
            pltpu.sync_copy(x_vmem, o_hbm.at[i0_vmem.at[0]])
            pltpu.sync_copy(x_vmem, o_hbm.at[i1_vmem.at[0]])

        pltpu.emit_pipeline(
            body,
            grid=(n // SC_WINDOW,),
            in_specs=[pl.BlockSpec((SC_WINDOW, d), lambda i: (i, 0)),
                      pl.BlockSpec((1, SC_WINDOW), lambda i: (0, i)),
                      pl.BlockSpec((1, SC_WINDOW), lambda i: (0, i))],
            out_specs=[],
            core_axis_name=("c", "s"),
            dimension_semantics=(pltpu.PARALLEL,),
        )(x_hbm, i0_hbm, i1_hbm)

    return k(x, i0.reshape(1, n), i1.reshape(1, n))


def _sc_gather(x, idx):
    n = idx.shape[0]
    d = x.shape[1]

    @pl.kernel(out_type=jax.ShapeDtypeStruct((n, d), x.dtype), mesh=_sc_mesh())
    def k(x_hbm, i_hbm, o_hbm):
        def body(i_vmem, o_vmem):
            pltpu.sync_copy(x_hbm.at[i_vmem.at[0]], o_vmem)

        pltpu.emit_pipeline(
            body,
            grid=(n // SC_WINDOW,),
            in_specs=[pl.BlockSpec((1, SC_WINDOW), lambda i: (0, i))],
            out_specs=[pl.BlockSpec((SC_WINDOW, d), lambda i: (i, 0))],
            core_axis_name=("c", "s"),
            dimension_semantics=(pltpu.PARALLEL,),
        )(i_hbm, o_hbm)

    return k(x, idx.reshape(1, n))


def _experts_kernel(te_ref, nu_ref, xs_ref, wgu_ref, wd_ref, ys_ref, wgu_bf_ref, wd_bf_ref):
    j = pl.program_id(0)

    @pl.when(j < nu_ref[0])
    def _():
        @pl.when((j == 0) | (te_ref[j] != te_ref[jnp.maximum(j - 1, 0)]))
        def _():
            wgu_bf_ref[...] = wgu_ref[...].astype(BF16)
            wd_bf_ref[...] = wd_ref[...].astype(BF16)

        subs = [slice(i * MOE_SUB, (i + 1) * MOE_SUB) for i in range(MOE_TR // MOE_SUB)]
        xs = [_unpack_halves(xs_ref[0, r, :], xs_ref[1, r, :]).astype(BF16) for r in subs]
        gus = [jnp.dot(x, wgu_bf_ref[...], preferred_element_type=F32) for x in xs]
        acts = [(gu[:, :D_EXPERT] * jax.nn.sigmoid(gu[:, :D_EXPERT]) * gu[:, D_EXPERT:]).astype(BF16)
                for gu in gus]
        ys = [jnp.dot(a, wd_bf_ref[...], preferred_element_type=F32) for a in acts]
        for r, y in zip(subs, ys):
            halves = _pack_halves(y)
            ys_ref[0, r, :] = halves[0]
            ys_ref[1, r, :] = halves[1]


def _experts(xs, tile_expert, n_used, w_gu, w_down, layer):
    _, rows, _ = xs.shape
    d = w_gu.shape[2]
    return pl.pallas_call(
        _experts_kernel,
        out_shape=jax.ShapeDtypeStruct(xs.shape, U32),
        grid_spec=pltpu.PrefetchScalarGridSpec(
            num_scalar_prefetch=2,
            grid=(rows // MOE_TR,),
            in_specs=[
                pl.BlockSpec((2, MOE_TR, HALF_W), lambda j, te, nu: (0, j, 0)),
                pl.BlockSpec((None, None, d, 2 * D_EXPERT),
                             lambda j, te, nu: (layer, te[j], 0, 0)),
                pl.BlockSpec((None, None, D_EXPERT, d),
                             lambda j, te, nu: (layer, te[j], 0, 0)),
            ],
            out_specs=pl.BlockSpec((2, MOE_TR, HALF_W), lambda j, te, nu: (0, j, 0)),
            scratch_shapes=[pltpu.VMEM((d, 2 * D_EXPERT), BF16), pltpu.VMEM((D_EXPERT, d), BF16)],
        ),
        compiler_params=_cparams(("arbitrary",)),
        name="moe_experts",
    )(tile_expert, n_used, xs, w_gu, w_down)


def _moe_combined(x_ref, z_ref, meta_ref):
    meta = meta_ref[...]
    y_a = _unpack_halves(z_ref[0], z_ref[2])
    y_b = _unpack_halves(z_ref[1], z_ref[3])
    return x_ref[...] + meta[:, 4:5] * y_a + meta[:, 5:6] * y_b


def _combine_norm_kernel(x_ref, z_ref, meta_ref, g_ref, o_ref):
    out = _moe_combined(x_ref, z_ref, meta_ref)
    ms = jnp.mean(out * out, axis=-1, keepdims=True)
    o_ref[...] = out * lax.rsqrt(ms + EPS) * g_ref[...]


def _combine_norm(x, z, meta, g, *, tm):
    t, d = x.shape
    return pl.pallas_call(
        _combine_norm_kernel,
        out_shape=jax.ShapeDtypeStruct((t, d), F32),
        grid=(t // tm,),
        in_specs=[
            pl.BlockSpec((tm, d), lambda i: (i, 0)),
            pl.BlockSpec((4, tm, HALF_W), lambda i: (0, i, 0)),
            pl.BlockSpec((tm, LANES), lambda i: (i, 0)),
            pl.BlockSpec((1, d), lambda i: (0, 0)),
        ],
        out_specs=pl.BlockSpec((tm, d), lambda i: (i, 0)),
        compiler_params=_cparams(("parallel",)),
        name="moe_combine",
    )(x, z, meta, g)


def _moe_layer(routing, w_gu, w_down, layer):
    hp, meta, metat, cnt = routing
    t = meta.shape[0]
    n_tiles = 2 * t // MOE_TR + N_EXPERTS
    rows = n_tiles * MOE_TR
    counts = cnt[0, :N_EXPERTS].astype(jnp.int32)
    tiles_e = (counts + MOE_TR - 1) // MOE_TR
    tiles_end = jnp.cumsum(tiles_e)
    row_off = (tiles_end - tiles_e) * MOE_TR
    experts = jnp.arange(N_EXPERTS, dtype=jnp.int32)[:, None]

    def region_start(e_row):
        return jnp.sum(jnp.where(e_row[None, :] == experts, row_off[:, None], 0), axis=0)

    e_a, e_b = metat[0].astype(jnp.int32), metat[1].astype(jnp.int32)
    pos_a = region_start(e_a) + metat[2].astype(jnp.int32)
    pos_b = region_start(e_b) + metat[3].astype(jnp.int32)
    tile_ids = jnp.arange(n_tiles, dtype=jnp.int32)
    tile_expert = jnp.minimum(
        jnp.sum((tile_ids[:, None] >= tiles_end[None, :]).astype(jnp.int32), axis=1),
        N_EXPERTS - 1)
    n_used = tiles_end[-1:].astype(jnp.int32)

    xs = _sc_scatter2(hp.reshape(2 * t, HALF_W),
                      jnp.concatenate([pos_a, pos_a + rows]),
                      jnp.concatenate([pos_b, pos_b + rows]), 2 * rows)
    ys = _experts(xs.reshape(2, rows, HALF_W), tile_expert, n_used,
                  w_gu, w_down, layer)
    z = _sc_gather(ys.reshape(2 * rows, HALF_W),
                   jnp.concatenate([pos_a, pos_b, pos_a + rows, pos_b + rows]))
    return z.reshape(4, t, HALF_W)


def kernel(x, norm_mix, norm_ffn, ml_w_in, ml_conv, ml_b_i, ml_b_f, ml_norm, ml_w_out, da_w_in, da_lq1, da_lk1, da_lq2, da_lk2, da_norm, da_w_out, moe_w_group, moe_b_group, moe_w_expert, moe_b_expert, moe_w_gu, moe_w_down, final_norm):
    batch, seq, d = x.shape
    xt = x.reshape(batch * seq, d)

    p, gcol, grow = _ml_proj(xt, norm_mix[0].reshape(1, d), ml_w_in[0], ml_b_i[0], ml_b_f[0],
                             tm=1024, tn=3072, out_dtype=BF16)
    assert p.shape[1] == 2 * ML_QK + 2 * ML_V
    y = _mlstm_core(p, gcol, grow, ml_conv[0], ml_norm[0].reshape(1, ML_V), batch=batch, seq=seq)
    xt, *routing = _res_router(y, ml_w_out[0].astype(BF16), xt, norm_ffn[0].reshape(1, d),
                               moe_w_group[0], moe_b_group[0], moe_w_expert[0], moe_b_expert[0],
                               tm=RR_TM, sub=RR_SUB)
    z = _moe_layer(routing, moe_w_gu, moe_w_down, 0)

    lambda_init = 0.8 - 0.6 * math.exp(-0.3 * 1)
    xt, p, vt = _norm_matmul_vt(xt, z, routing[1], norm_mix[1].reshape(1, d),
                                da_w_in[0].astype(BF16), tm=512, sub=256, vt_start=2 * DA_QK,
                                batch=batch, seq=seq, out_dtype=BF16)
    lam_params = jnp.stack([da_lq1[0], da_lk1[0], da_lq2[0], da_lk2[0]]).astype(F32)
    a = _diff_attn(p, vt, lam_params, da_norm[0].reshape(DA_DV, 1), lambda_init, batch=batch,
                   seq=seq)
    xt, *routing = _res_router(a, da_w_out[0].astype(BF16), xt, norm_ffn[1].reshape(1, d),
                               moe_w_group[1], moe_b_group[1], moe_w_expert[1], moe_b_expert[1],
                               tm=RR_TM, sub=RR_SUB)
    z = _moe_layer(routing, moe_w_gu, moe_w_down, 1)
    out = _combine_norm(xt, z, routing[1], final_norm.reshape(1, d), tm=1024)
    return out.reshape(batch, seq, d)
```

```python
import functools
import math

import jax
import jax.numpy as jnp
from jax import lax
from jax.experimental import pallas as pl
from jax.experimental.pallas import tpu as pltpu
from jax.experimental.pallas import tpu_sc as plsc

F32 = jnp.float32
BF16 = jnp.bfloat16
U32 = jnp.uint32

D_MODEL = 1024
EPS = 1e-6
ML_H = 4
ML_DV = 512
ML_DQK = 256
ML_QK = ML_H * ML_DQK
ML_V = ML_H * ML_DV
CONV_K = 4
ML_CHUNK = 256
CONV_TAIL = 8
ML_PROJ_SUB = 256
DA_H = 8
DA_DH = 64
DA_DV = 128
DA_QK = DA_H * 2 * DA_DH
DA_EPS = 1e-5
DA_TQ = 1024
DA_TK = 1024
DA_CW = 512
DA_KEY_ALIGN = 256
DA_LAZY_LIMIT = 64.0
DA_REBASE = 8.0
LOG2E = 1.4426950408889634
N_GROUPS = 4
EPG = 8
N_EXPERTS = 32
D_EXPERT = 256
RR_TM = 1024
RR_SUB = 256
MOE_TR = 1024
MOE_SUB = 256
HALF_W = D_MODEL // 4
SC_WINDOW = 128
LANES = 128

VMEM_LIMIT = 48 * 1024 * 1024


def _cparams(sem):
    return pltpu.CompilerParams(dimension_semantics=sem, vmem_limit_bytes=VMEM_LIMIT)


def _norm_matmul_vt_kernel(x_ref, z_ref, meta_ref, g_ref, w_ref, x2_ref, o_ref, vt_ref,
                           *, vt_start, sub):
    for i in range(x_ref.shape[0] // sub):
        r = slice(i * sub, (i + 1) * sub)
        x = _moe_combined(x_ref.at[r, :], z_ref.at[:, r, :], meta_ref.at[r, :])
        x2_ref[r, :] = x
        ms = jnp.mean(x * x, axis=-1, keepdims=True)
        xn = (x * lax.rsqrt(ms + EPS) * g_ref[...]).astype(BF16)
        res = jnp.dot(xn, w_ref[...], preferred_element_type=F32)
        o_ref[r, :] = res.astype(o_ref.dtype)
        vt_ref[:, r] = res[:, vt_start:].T.astype(vt_ref.dtype)


def _norm_matmul_vt(x, z, meta, g, w, *, tm, sub, vt_start, batch, seq, out_dtype):
    t, d = x.shape
    n = w.shape[1]
    per_batch = seq // tm
    return pl.pallas_call(
        functools.partial(_norm_matmul_vt_kernel, vt_start=vt_start, sub=sub),
        out_shape=(jax.ShapeDtypeStruct((t, d), F32),
                   jax.ShapeDtypeStruct((t, n), out_dtype),
                   jax.ShapeDtypeStruct((batch, n - vt_start, seq), out_dtype)),
        grid=(t // tm,),
        in_specs=[
            pl.BlockSpec((tm, d), lambda i: (i, 0)),
            pl.BlockSpec((4, tm, HALF_W), lambda i: (0, i, 0)),
            pl.BlockSpec((tm, LANES), lambda i: (i, 0)),
            pl.BlockSpec((1, d), lambda i: (0, 0)),
            pl.BlockSpec((d, n), lambda i: (0, 0)),
        ],
        out_specs=(pl.BlockSpec((tm, d), lambda i: (i, 0)),
                   pl.BlockSpec((tm, n), lambda i: (i, 0)),
                   pl.BlockSpec((None, n - vt_start, tm),
                                lambda i: (i // per_batch, 0, i % per_batch))),
        compiler_params=_cparams(("parallel",)),
        name="norm_matmul_vt",
    )(x, z, meta, g, w)


def _log_sigmoid(x):
    return jnp.minimum(x, 0.0) - jnp.log1p(jnp.exp(-jnp.abs(x)))


def _ml_proj_kernel(x_ref, g_ref, w_ref, wc_ref, wr_ref, bc_ref, br_ref, o_ref, oc_ref, or_ref,
                    xn_ref):
    j = pl.program_id(1)

    @pl.when(j == 0)
    def _():
        for i in range(x_ref.shape[0] // ML_PROJ_SUB):
            r = slice(i * ML_PROJ_SUB, (i + 1) * ML_PROJ_SUB)
            x = x_ref[r, :]
            ms = jnp.mean(x * x, axis=-1, keepdims=True)
            xn = (x * lax.rsqrt(ms + EPS) * g_ref[...]).astype(BF16)
            xn_ref[r, :] = xn
            o_ref[r, :] = jnp.dot(xn, w_ref[...], preferred_element_type=F32).astype(o_ref.dtype)
            gc = jnp.dot(xn, wc_ref[...], preferred_element_type=F32) + bc_ref[...]
            lane = lax.broadcasted_iota(jnp.int32, gc.shape, 1)
            oc_ref[r, :] = jnp.where(lane < ML_H, gc, _log_sigmoid(gc))
            gr = lax.dot_general(wr_ref[...], xn, (((1,), (1,)), ((), ())),
                                 preferred_element_type=F32) + br_ref[...]
            row = lax.broadcasted_iota(jnp.int32, gr.shape, 0)
            or_ref[:, r] = jnp.where(row < ML_H, gr, _log_sigmoid(gr))

    @pl.when(j > 0)
    def _():
        o_ref[...] = jnp.dot(xn_ref[...], w_ref[...],
                             preferred_element_type=F32).astype(o_ref.dtype)


def _ml_proj(x, g, w_in, b_i, b_f, *, tm, tn, out_dtype):
    t, d = x.shape
    ng = 2 * ML_H
    n_main = w_in.shape[1] - ng
    w_gates = w_in[:, n_main:]
    wc = jnp.zeros((d, LANES), BF16).at[:, :ng].set(w_gates.astype(BF16))
    wr = w_gates.T.astype(BF16)
    bias = jnp.concatenate([b_i, b_f]).astype(F32)
    bc = jnp.zeros((1, LANES), F32).at[0, :ng].set(bias)
    br = bias.reshape(ng, 1)
    return pl.pallas_call(
        _ml_proj_kernel,
        out_shape=(jax.ShapeDtypeStruct((t, n_main), out_dtype),
                   jax.ShapeDtypeStruct((t, LANES), F32), jax.ShapeDtypeStruct((ng, t), F32)),
        grid=(t // tm, n_main // tn),
        in_specs=[
            pl.BlockSpec((tm, d), lambda i, j: (i, 0)),
            pl.BlockSpec((1, d), lambda i, j: (0, 0)),
            pl.BlockSpec((d, tn), lambda i, j: (0, j)),
            pl.BlockSpec((d, LANES), lambda i, j: (0, 0)),
            pl.BlockSpec((ng, d), lambda i, j: (0, 0)),
            pl.BlockSpec((1, LANES), lambda i, j: (0, 0)),
            pl.BlockSpec((ng, 1), lambda i, j: (0, 0)),
        ],
        out_specs=(pl.BlockSpec((tm, tn), lambda i, j: (i, j)),
                   pl.BlockSpec((tm, LANES), lambda i, j: (i, 0)),
                   pl.BlockSpec((ng, tm), lambda i, j: (0, i))),
        scratch_shapes=[pltpu.VMEM((tm, d), BF16)],
        compiler_params=_cparams(("parallel", "arbitrary")),
        name="ml_proj",
    )(x, g, w_in.astype(BF16), wc, wr, bc, br)


def _split3(x):
    hi = x.astype(BF16)
    r = x - hi.astype(F32)
    mid = r.astype(BF16)
    lo = (r - mid.astype(F32)).astype(BF16)
    return hi, mid, lo


def _mlstm_kernel(q_ref, k_ref, v_ref, o_ref, gc_ref, gr_ref, cw_ref, ng_ref, y_ref,
                  tail_ref, ct_ref, n_ref, m_ref, shift_ref, tril_ref, triu_ref, neg_ref):
    L = ML_CHUNK
    c = pl.program_id(1)

    @pl.when(c == 0)
    def _():
        tail_ref[...] = jnp.zeros_like(tail_ref)
        ct_ref[...] = jnp.zeros_like(ct_ref)
        n_ref[...] = jnp.zeros_like(n_ref)
        m_ref[...] = jnp.zeros_like(m_ref)
        r_i = lax.broadcasted_iota(jnp.int32, (L, L), 0)
        c_i = lax.broadcasted_iota(jnp.int32, (L, L), 1)
        tril_ref[...] = (c_i <= r_i).astype(BF16)
        triu_ref[...] = (r_i <= c_i).astype(BF16)
        neg_ref[...] = jnp.where(c_i <= r_i, 0.0, -jnp.inf)
        for j in range(CONV_K - 1):
            shift_ref[j * L:(j + 1) * L, :] = (c_i == r_i - (CONV_K - 1 - j)).astype(BF16)

    u = jnp.concatenate([q_ref[...], k_ref[...]], axis=1)
    uf = u.astype(F32)
    shifted = jnp.dot(shift_ref[...], u, preferred_element_type=F32)
    conv = uf * cw_ref[CONV_K - 1:CONV_K, :]
    head = None
    for j in range(CONV_K - 1):
        conv = conv + shifted[j * L:(j + 1) * L, :] * cw_ref[j:j + 1, :]
        part = tail_ref[pl.ds(CONV_TAIL - (CONV_K - 1) + j, CONV_TAIL), :] * cw_ref[j:j + 1, :]
        head = part if head is None else head + part
    conv = jnp.concatenate([conv[:CONV_TAIL] + head, conv[CONV_TAIL:]], axis=0)
    tail_ref[0:CONV_TAIL, :] = uf[L - CONV_TAIL:, :]
    qk = conv * jax.nn.sigmoid(conv)
    q_all = (qk[:, :ML_QK] * (ML_DQK ** -0.5)).astype(BF16)
    k_all = qk[:, ML_QK:]

    tril = tril_ref[...]
    triu = triu_ref[...]
    gc = gc_ref[...]
    gr = gr_ref[...]
    bc_all = sum(jnp.dot(tril, p, preferred_element_type=F32) for p in _split3(gc))
    br_all = sum(jnp.dot(p, triu, preferred_element_type=F32) for p in _split3(gr))

    for h in range(ML_H):
        qh = q_all[:, h * ML_DQK:(h + 1) * ML_DQK]
        kh_f = k_all[:, h * ML_DQK:(h + 1) * ML_DQK]
        kh = kh_f.astype(BF16)
        vh = v_ref[:, h * ML_DV:(h + 1) * ML_DV]
        it_col = gc[:, h:h + 1]
        it_row = gr[h:h + 1, :]
        b_col = bc_all[:, ML_H + h:ML_H + h + 1]
        b_row = br_all[ML_H + h:ML_H + h + 1, :]
        m_prev = m_ref[h][:, 0:1]

        dmat = b_col + (it_row - b_row) + neg_ref[...]
        inter_log = b_col + m_prev
        m_t = jnp.maximum(inter_log, jnp.max(dmat, axis=1, keepdims=True))
        wts = jnp.exp(dmat - m_t)
        s = lax.dot_general(qh, kh, (((1,), (1,)), ((), ())), preferred_element_type=F32)
        sc = s * wts
        inter_scale = jnp.exp(inter_log - m_t)
        ct = ct_ref[h]
        num = (jnp.dot(sc.astype(BF16), vh, preferred_element_type=F32)
               + inter_scale * jnp.dot(qh, ct.astype(BF16), preferred_element_type=F32))
        n_row = n_ref[h]
        den = (jnp.sum(sc, axis=1, keepdims=True)
               + inter_scale * jnp.sum(qh.astype(F32) * n_row, axis=1, keepdims=True))
        h_out = num / jnp.maximum(jnp.abs(den), jnp.exp(-m_t))

        b_last = b_col[L - 1:L, :]
        lw_col = b_last - b_col + it_col
        lw_row = b_last - b_row + it_row
        m_new = jnp.maximum(b_last + m_prev, jnp.max(lw_row, axis=1, keepdims=True))
        ws_col = jnp.exp(lw_col - m_new)
        decay = jnp.exp(b_last + m_prev - m_new)
        kw = kh_f * ws_col
        ct_ref[h] = decay * ct + lax.dot_general(kw.astype(BF16), vh, (((0,), (0,)), ((), ())),
                                                  preferred_element_type=F32)
        n_ref[h] = decay * n_row + jnp.sum(kw, axis=0, keepdims=True)
        m_ref[h] = jnp.broadcast_to(m_new, (1, LANES))

        ms = jnp.mean(h_out * h_out, axis=1, keepdims=True)
        hn = h_out * lax.rsqrt(ms + EPS) * ng_ref[:, h * ML_DV:(h + 1) * ML_DV]
        og = o_ref[:, h * ML_DV:(h + 1) * ML_DV].astype(F32)
        y_ref[:, h * ML_DV:(h + 1) * ML_DV] = (hn * jax.nn.sigmoid(og)).astype(y_ref.dtype)


def _mlstm_core(p, gcol, grow, conv_w, norm_g, *, batch, seq):
    L = ML_CHUNK
    nc = seq // L
    p3 = p.reshape(batch, seq, 2 * ML_QK + 2 * ML_V)
    gc3 = gcol.reshape(batch, seq, LANES)
    gr3 = grow.reshape(2 * ML_H, batch, seq).transpose(1, 0, 2)
    y = pl.pallas_call(
        _mlstm_kernel,
        out_shape=jax.ShapeDtypeStruct((batch, seq, ML_V), BF16),
        grid=(batch, nc),
        in_specs=[
            pl.BlockSpec((None, L, ML_QK), lambda b, c: (b, c, 0)),
            pl.BlockSpec((None, L, ML_QK), lambda b, c: (b, c, 1)),
            pl.BlockSpec((None, L, ML_V), lambda b, c: (b, c, 1)),
            pl.BlockSpec((None, L, ML_V), lambda b, c: (b, c, 2)),
            pl.BlockSpec((None, L, LANES), lambda b, c: (b, c, 0)),
            pl.BlockSpec((None, 2 * ML_H, L), lambda b, c: (b, 0, c)),
            pl.BlockSpec((CONV_K, 2 * ML_QK), lambda b, c: (0, 0)),
            pl.BlockSpec((1, ML_V), lambda b, c: (0, 0)),
        ],
        out_specs=pl.BlockSpec((None, L, ML_V), lambda b, c: (b, c, 0)),
        scratch_shapes=[
            pltpu.VMEM((2 * CONV_TAIL, 2 * ML_QK), F32),
            pltpu.VMEM((ML_H, ML_DQK, ML_DV), F32),
            pltpu.VMEM((ML_H, 1, ML_DQK), F32),
            pltpu.VMEM((ML_H, 1, LANES), F32),
            pltpu.VMEM(((CONV_K - 1) * L, L), BF16),
            pltpu.VMEM((L, L), BF16),
            pltpu.VMEM((L, L), BF16),
            pltpu.VMEM((L, L), F32),
        ],
        compiler_params=_cparams(("parallel", "arbitrary")),
        name="mlstm_core",
    )(p3, p3, p3, p3, gc3, gr3, conv_w, norm_g)
    return y.reshape(batch * seq, ML_V)


def _diff_attn_kernel(q_ref, k_ref, vt_ref, lp_ref, ng_ref, o_ref, q2_ref, r_ref, m_ref, l_ref,
                      acc_ref, *, lambda_init):
    tq, tk, cw = DA_TQ, DA_TK, DA_CW
    qi = pl.program_id(2)
    q = q_ref[...].astype(F32) * (DA_DH ** -0.5 * LOG2E)
    lane = lax.broadcasted_iota(jnp.int32, q.shape, 1)
    q2_ref[0:tq, :] = jnp.where(lane < DA_DH, q, 0.0).astype(BF16)
    q2_ref[tq:2 * tq, :] = jnp.where(lane >= DA_DH, q, 0.0).astype(BF16)

    r_ref[...] = jnp.zeros_like(r_ref)
    m_ref[...] = jnp.full(m_ref.shape, -jnp.inf, F32)
    l_ref[...] = jnp.zeros_like(l_ref)
    acc_ref[...] = jnp.zeros_like(acc_ref)

    nch = 2 * tq // cw

    def keys_needed(c, diag):
        if diag is None:
            return tk
        visible = (c * cw) % tq + cw - diag * tk
        return max(0, min(tk, -(-visible // DA_KEY_ALIGN) * DA_KEY_ALIGN))

    def mask(s, c, diag):
        nk = s.shape[0]
        key = lax.broadcasted_iota(jnp.int32, (nk, cw), 0) + diag * tk
        qry = lax.broadcasted_iota(jnp.int32, (nk, cw), 1) + (c * cw) % tq
        return jnp.where(key <= qry, s, -jnp.inf)

    def scores(j, c, diag=None):
        nk = keys_needed(c, diag)
        if nk == 0:
            return None
        kb = k_ref[pl.ds(pl.multiple_of(j * tk, tk), nk), :]
        s = lax.dot_general(kb, q2_ref[c * cw:(c + 1) * cw, :], (((1,), (1,)), ((), ())),
                            preferred_element_type=F32)
        if diag is not None:
            s = mask(s, c, diag)
        return s, jnp.max(s, axis=0, keepdims=True)


    def exact_block(j, diag=None):
        start = pl.multiple_of(j * tk, tk)
        nxt = scores(j, 0, diag)
        for c in range(nch):
            cs = slice(c * cw, (c + 1) * cw)
            s, bmax = nxt
            if c + 1 < nch:
                nxt = scores(j, c + 1, diag)
            vbt = vt_ref[:, pl.ds(start, s.shape[0])]
            m_old = m_ref[:, cs]
            m_new = jnp.maximum(m_old, bmax)
            alpha = jnp.where(m_old == -jnp.inf, 0.0, jnp.exp2(r_ref[:, cs] - m_new))
            p = jnp.exp2(s - m_new)
            l_ref[:, cs] = alpha * l_ref[:, cs] + jnp.sum(p, axis=0, keepdims=True)
            acc_ref[:, cs] = alpha * acc_ref[:, cs] + jnp.dot(vbt, p.astype(BF16),
                                                               preferred_element_type=F32)
            m_ref[:, cs] = m_new
            r_ref[:, cs] = m_new

    def fast_block(j, diag=None):
        start = pl.multiple_of(j * tk, tk)
        nxt = scores(j, 0, diag)
        tent = []
        for c in range(nch):
            cs = slice(c * cw, (c + 1) * cw)
            s, bmax = nxt
            if c + 1 < nch:
                nxt = scores(j, c + 1, diag)
            vbt = vt_ref[:, pl.ds(start, s.shape[0])]
            r_old = r_ref[:, cs]
            p = jnp.exp2(s - r_old)
            tent.append((cs, bmax, r_old, m_ref[:, cs], jnp.sum(p, axis=0, keepdims=True),
                         jnp.dot(vbt, p.astype(BF16), preferred_element_type=F32)))
        worst = functools.reduce(jnp.maximum, [
            jnp.max(jnp.maximum(bmax - r_old,
                                jnp.where(m_old == -jnp.inf, r_old - bmax, -jnp.inf)),
                    axis=1, keepdims=True)
            for _, bmax, r_old, m_old, _, _ in tent])
        safe = worst <= DA_LAZY_LIMIT
        for cs, bmax, r_old, m_old, lt, acct in tent:
            m_new = jnp.maximum(m_old, bmax)
            r_new = jnp.where(jnp.abs(m_new - r_old) > DA_REBASE, m_new, r_old)
            scale = jnp.exp2(r_old - r_new)
            l_old = l_ref[:, cs]
            acc_old = acc_ref[:, cs]
            l_ref[:, cs] = jnp.where(safe, (l_old + lt) * scale, l_old)
            acc_ref[:, cs] = jnp.where(safe, (acc_old + acct) * scale, acc_old)
            m_ref[:, cs] = jnp.where(safe, m_new, m_old)
            r_ref[:, cs] = jnp.where(safe, r_new, r_old)

        @pl.when(jnp.logical_not(jnp.max(worst) <= DA_LAZY_LIMIT))
        def _():
            exact_block(j, diag)

    def fast_body(j, carry):
        fast_block(j)
        return carry

    assert tq == tk
    lax.fori_loop(0, qi, fast_body, 0)
    fast_block(qi, diag=0)


    lp = lp_ref[...]
    lam = (jnp.exp(jnp.sum(lp[0:1, :] * lp[1:2, :], axis=1, keepdims=True))
           - jnp.exp(jnp.sum(lp[2:3, :] * lp[3:4, :], axis=1, keepdims=True)) + lambda_init)
    out = acc_ref[...] / l_ref[...]
    o = out[:, :tq] - lam * out[:, tq:]
    ms = jnp.mean(o * o, axis=0, keepdims=True)
    on = o * lax.rsqrt(ms + DA_EPS) * ng_ref[...] * (1.0 - lambda_init)
    o_ref[...] = on.T.astype(o_ref.dtype)


def _diff_attn(p, vt, lam_params, norm_g, lambda_init, *, batch, seq):
    tq = DA_TQ
    p3 = p.reshape(batch, seq, 2 * DA_QK + DA_H * DA_DV)
    kern = functools.partial(_diff_attn_kernel, lambda_init=lambda_init)
    o = pl.pallas_call(
        kern,
        out_shape=jax.ShapeDtypeStruct((batch, seq, DA_H * DA_DV), BF16),
        grid=(batch, DA_H, seq // tq),
        in_specs=[
            pl.BlockSpec((None, tq, 2 * DA_DH), lambda b, h, i: (b, i, h)),
            pl.BlockSpec((None, seq, 2 * DA_DH), lambda b, h, i: (b, 0, DA_H + h)),
            pl.BlockSpec((None, DA_DV, seq), lambda b, h, i: (b, h, 0)),
            pl.BlockSpec((4, DA_DH), lambda b, h, i: (0, 0)),
            pl.BlockSpec((DA_DV, 1), lambda b, h, i: (0, 0)),
        ],
        out_specs=pl.BlockSpec((None, tq, DA_DV), lambda b, h, i: (b, i, h)),
        scratch_shapes=[
            pltpu.VMEM((2 * tq, 2 * DA_DH), BF16),
            pltpu.VMEM((1, 2 * tq), F32),
            pltpu.VMEM((1, 2 * tq), F32),
            pltpu.VMEM((1, 2 * tq), F32),
            pltpu.VMEM((DA_DV, 2 * tq), F32),
        ],
        compiler_params=_cparams(("parallel", "parallel", "arbitrary")),
        name="diff_attn",
    )(p3, p3, vt, lam_params, norm_g)
    return o.reshape(batch * seq, DA_H * DA_DV)


def _pack_halves(y):
    halves = []
    for h in range(2):
        base = h * 2 * HALF_W
        lo = y[:, base:base + HALF_W].astype(BF16).astype(F32)
        hi = y[:, base + HALF_W:base + 2 * HALF_W].astype(BF16).astype(F32)
        lo_bits = lax.bitcast_convert_type(lo, U32) >> 16
        hi_bits = lax.bitcast_convert_type(hi, U32)
        halves.append(hi_bits | lo_bits)
    return halves


def _unpack_halves(w0, w1):
    parts = []
    for w in (w0, w1):
        parts.append(lax.bitcast_convert_type(w << 16, F32))
        parts.append(lax.bitcast_convert_type(w & jnp.uint32(0xFFFF0000), F32))
    return jnp.concatenate(parts, axis=1)


def _route_logits(x, g_ref, w_ref, b_ref):
    ms = jnp.mean(x * x, axis=-1, keepdims=True)
    hn32 = x * lax.rsqrt(ms + EPS) * g_ref[...]
    lg = jnp.dot(hn32.astype(BF16), w_ref[...], preferred_element_type=F32) + b_ref[...]
    return _pack_halves(hn32), lg


def _route_decide(lg, run_ref, ls_ref):
    lane = lax.broadcasted_iota(jnp.int32, lg.shape, 1).astype(F32)
    neg = -jnp.inf

    gmask = (lane >= N_EXPERTS) & (lane < N_EXPERTS + N_GROUPS)
    gl = jnp.where(gmask, lg, neg)
    gmax = jnp.max(gl, axis=1, keepdims=True)
    gidx = jnp.min(jnp.where(gl == gmax, lane, float(LANES)), axis=1, keepdims=True) - N_EXPERTS
    gsum = jnp.sum(jnp.where(gmask, jnp.exp(gl - gmax), 0.0), axis=1, keepdims=True)
    g_w = 1.0 / gsum

    emask = (lane >= gidx * EPG) & (lane < gidx * EPG + EPG)
    el = jnp.where(emask, lg, neg)
    emax = jnp.max(el, axis=1, keepdims=True)
    eexp = jnp.where(emask, jnp.exp(el - emax), 0.0)
    ep = eexp / jnp.sum(eexp, axis=1, keepdims=True)
    ep = jnp.where(emask, ep, -1.0)
    p1 = jnp.max(ep, axis=1, keepdims=True)
    i1 = jnp.min(jnp.where(ep == p1, lane, float(LANES)), axis=1, keepdims=True)
    ep2 = jnp.where(lane == i1, -1.0, ep)
    p2 = jnp.max(ep2, axis=1, keepdims=True)
    i2 = jnp.min(jnp.where(ep2 == p2, lane, float(LANES)), axis=1, keepdims=True)
    wsum = p1 + p2
    w1 = g_w * (p1 / wsum)
    w2 = g_w * (p2 / wsum)

    a1 = lane == i1
    a2 = lane == i2
    onehot = (a1 | a2).astype(BF16)
    before = jnp.dot(ls_ref[...], onehot, preferred_element_type=F32) + run_ref[...]
    rank1 = jnp.sum(jnp.where(a1, before, 0.0), axis=1, keepdims=True)
    rank2 = jnp.sum(jnp.where(a2, before, 0.0), axis=1, keepdims=True)
    run_ref[...] += jnp.sum(onehot.astype(F32), axis=0, keepdims=True)

    meta = (jnp.where(lane == 0, i1, 0.0) + jnp.where(lane == 1, i2, 0.0)
            + jnp.where(lane == 2, rank1, 0.0) + jnp.where(lane == 3, rank2, 0.0)
            + jnp.where(lane == 4, w1, 0.0) + jnp.where(lane == 5, w2, 0.0))
    return meta


def _res_router_kernel(a_ref, w_ref, r_ref, g_ref, wr_ref, br_ref, o_ref, hp_ref, meta_ref,
                       metat_ref, cnt_ref, run_ref, ls_ref):
    sub = ls_ref.shape[0]

    @pl.when(pl.program_id(0) == 0)
    def _():
        run_ref[...] = jnp.zeros_like(run_ref)
        r_i = lax.broadcasted_iota(jnp.int32, (sub, sub), 0)
        c_i = lax.broadcasted_iota(jnp.int32, (sub, sub), 1)
        ls_ref[...] = (c_i < r_i).astype(BF16)

    subs = [slice(i * sub, (i + 1) * sub) for i in range(a_ref.shape[0] // sub)]

    def project(r):
        x = r_ref[r, :] + jnp.dot(a_ref[r, :], w_ref[...], preferred_element_type=F32)
        o_ref[r, :] = x
        halves, lg = _route_logits(x, g_ref, wr_ref, br_ref)
        hp_ref[0, r, :] = halves[0]
        hp_ref[1, r, :] = halves[1]
        return lg

    nxt = project(subs[0])
    for i, r in enumerate(subs):
        lg = nxt
        if i + 1 < len(subs):
            nxt = project(subs[i + 1])
        meta = _route_decide(lg, run_ref, ls_ref)
        meta_ref[r, :] = meta
        metat_ref[:, r] = meta.T[0:8, :]
    cnt_ref[...] = run_ref[...]


def _res_router(a, w, res, g, w_group, b_group, w_expert, b_expert, *, tm, sub):
    t, k = a.shape
    d = w.shape[1]
    wr = jnp.zeros((d, LANES), BF16)
    wr = wr.at[:, :N_EXPERTS].set(w_expert.astype(BF16))
    wr = wr.at[:, N_EXPERTS:N_EXPERTS + N_GROUPS].set(w_group.astype(BF16))
    br = jnp.zeros((1, LANES), F32)
    br = br.at[0, :N_EXPERTS].set(b_expert.astype(F32))
    br = br.at[0, N_EXPERTS:N_EXPERTS + N_GROUPS].set(b_group.astype(F32))
    return pl.pallas_call(
        _res_router_kernel,
        out_shape=(jax.ShapeDtypeStruct((t, d), F32),
                   jax.ShapeDtypeStruct((2, t, HALF_W), U32),
                   jax.ShapeDtypeStruct((t, LANES), F32),
                   jax.ShapeDtypeStruct((8, t), F32),
                   jax.ShapeDtypeStruct((1, LANES), F32)),
        grid=(t // tm,),
        in_specs=[
            pl.BlockSpec((tm, k), lambda i: (i, 0)),
            pl.BlockSpec((k, d), lambda i: (0, 0)),
            pl.BlockSpec((tm, d), lambda i: (i, 0)),
            pl.BlockSpec((1, d), lambda i: (0, 0)),
            pl.BlockSpec((d, LANES), lambda i: (0, 0)),
            pl.BlockSpec((1, LANES), lambda i: (0, 0)),
        ],
        out_specs=(pl.BlockSpec((tm, d), lambda i: (i, 0)),
                   pl.BlockSpec((2, tm, HALF_W), lambda i: (0, i, 0)),
                   pl.BlockSpec((tm, LANES), lambda i: (i, 0)),
                   pl.BlockSpec((8, tm), lambda i: (0, i)),
                   pl.BlockSpec((1, LANES), lambda i: (0, 0))),
        scratch_shapes=[pltpu.VMEM((1, LANES), F32), pltpu.VMEM((sub, sub), BF16)],
        compiler_params=_cparams(("arbitrary",)),
        name="res_router",
    )(a, w, res, g, wr, br)


def _sc_mesh():
    return plsc.VectorSubcoreMesh(core_axis_name="c", subcore_axis_name="s")


def _sc_scatter2(x, i0, i1, n_out):
    n, d = x.shape

    @pl.kernel(out_type=jax.ShapeDtypeStruct((n_out, d), x.dtype), mesh=_sc_mesh())
    def k(x_hbm, i0_hbm, i1_hbm, o_hbm):
        def body(x_vmem, i0_vmem, i1_vmem):
            pltpu.sync_copy(x_vmem, o_hbm.at[i0_vmem.at[0]])
            pltpu.sync_copy(x_vmem, o_hbm.at[i1_vmem.at[0]])

        pltpu.emit_pipeline(
            body,
            grid=(n // SC_WINDOW,),
            in_specs=[pl.BlockSpec((SC_WINDOW, d), lambda i: (i, 0)),
                      pl.BlockSpec((1, SC_WINDOW), lambda i: (0, i)),
                      pl.BlockSpec((1, SC_WINDOW), lambda i: (0, i))],
            out_specs=[],
            core_axis_name=("c", "s"),
            dimension_semantics=(pltpu.PARALLEL,),
        )(x_hbm, i0_hbm, i1_hbm)

    return k(x, i0.reshape(1, n), i1.reshape(1, n))


def _sc_gather(x, idx):
    n = idx.shape[0]
    d = x.shape[1]

    @pl.kernel(out_type=jax.ShapeDtypeStruct((n, d), x.dtype), mesh=_sc_mesh())
    def k(x_hbm, i_hbm, o_hbm):
        def body(i_vmem, o_vmem):
            pltpu.sync_copy(x_hbm.at[i_vmem.at[0]], o_vmem)

        pltpu.emit_pipeline(
            body,
            grid=(n // SC_WINDOW,),
            in_specs=[pl.BlockSpec((1, SC_WINDOW), lambda i: (0, i))],
            out_specs=[pl.BlockSpec((SC_WINDOW, d), lambda i: (i, 0))],
            core_axis_name=("c", "s"),
            dimension_semantics=(pltpu.PARALLEL,),
        )(i_hbm, o_hbm)

    return k(x, idx.reshape(1, n))


def _experts_kernel(te_ref, nu_ref, xs_ref, wgu_ref, wd_ref, ys_ref, wgu_bf_ref, wd_bf_ref):
    j = pl.program_id(0)

    @pl.when(j < nu_ref[0])
    def _():
        @pl.when((j == 0) | (te_ref[j] != te_ref[jnp.maximum(j - 1, 0)]))
        def _():
            wgu_bf_ref[...] = wgu_ref[...].astype(BF16)
            wd_bf_ref[...] = wd_ref[...].astype(BF16)

        subs = [slice(i * MOE_SUB, (i + 1) * MOE_SUB) for i in range(MOE_TR // MOE_SUB)]
        xs = [_unpack_halves(xs_ref[0, r, :], xs_ref[1, r, :]).astype(BF16) for r in subs]
        gus = [jnp.dot(x, wgu_bf_ref[...], preferred_element_type=F32) for x in xs]
        acts = [(gu[:, :D_EXPERT] * jax.nn.sigmoid(gu[:, :D_EXPERT]) * gu[:, D_EXPERT:]).astype(BF16)
                for gu in gus]
        ys = [jnp.dot(a, wd_bf_ref[...], preferred_element_type=F32) for a in acts]
        for r, y in zip(subs, ys):
            halves = _pack_halves(y)
            ys_ref[0, r, :] = halves[0]
            ys_ref[1, r, :] = halves[1]


def _experts(xs, tile_expert, n_used, w_gu, w_down, layer):
    _, rows, _ = xs.shape
    d = w_gu.shape[2]
    return pl.pallas_call(
        _experts_kernel,
        out_shape=jax.ShapeDtypeStruct(xs.shape, U32),
        grid_spec=pltpu.PrefetchScalarGridSpec(
            num_scalar_prefetch=2,
            grid=(rows // MOE_TR,),
            in_specs=[
                pl.BlockSpec((2, MOE_TR, HALF_W), lambda j, te, nu: (0, j, 0)),
                pl.BlockSpec((None, None, d, 2 * D_EXPERT),
                             lambda j, te, nu: (layer, te[j], 0, 0)),
                pl.BlockSpec((None, None, D_EXPERT, d),
                             lambda j, te, nu: (layer, te[j], 0, 0)),
            ],
            out_specs=pl.BlockSpec((2, MOE_TR, HALF_W), lambda j, te, nu: (0, j, 0)),
            scratch_shapes=[pltpu.VMEM((d, 2 * D_EXPERT), BF16), pltpu.VMEM((D_EXPERT, d), BF16)],
        ),
        compiler_params=_cparams(("arbitrary",)),
        name="moe_experts",
    )(tile_expert, n_used, xs, w_gu, w_down)


def _moe_combined(x_ref, z_ref, meta_ref):
    meta = meta_ref[...]
    y_a = _unpack_halves(z_ref[0], z_ref[2])
    y_b = _unpack_halves(z_ref[1], z_ref[3])
    return x_ref[...] + meta[:, 4:5] * y_a + meta[:, 5:6] * y_b


def _combine_norm_kernel(x_ref, z_ref, meta_ref, g_ref, o_ref):
    out = _moe_combined(x_ref, z_ref, meta_ref)
    ms = jnp.mean(out * out, axis=-1, keepdims=True)
    o_ref[...] = out * lax.rsqrt(ms + EPS) * g_ref[...]


def _combine_norm(x, z, meta, g, *, tm):
    t, d = x.shape
    return pl.pallas_call(
        _combine_norm_kernel,
        out_shape=jax.ShapeDtypeStruct((t, d), F32),
        grid=(t // tm,),
        in_specs=[
            pl.BlockSpec((tm, d), lambda i: (i, 0)),
            pl.BlockSpec((4, tm, HALF_W), lambda i: (0, i, 0)),
            pl.BlockSpec((tm, LANES), lambda i: (i, 0)),
            pl.BlockSpec((1, d), lambda i: (0, 0)),
        ],
        out_specs=pl.BlockSpec((tm, d), lambda i: (i, 0)),
        compiler_params=_cparams(("parallel",)),
        name="moe_combine",
    )(x, z, meta, g)


def _moe_layer(routing, w_gu, w_down, layer):
    hp, meta, metat, cnt = routing
    t = meta.shape[0]
    n_tiles = 2 * t // MOE_TR + N_EXPERTS
    rows = n_tiles * MOE_TR
    counts = cnt[0, :N_EXPERTS].astype(jnp.int32)
    tiles_e = (counts + MOE_TR - 1) // MOE_TR
    tiles_end = jnp.cumsum(tiles_e)
    row_off = (tiles_end - tiles_e) * MOE_TR
    experts = jnp.arange(N_EXPERTS, dtype=jnp.int32)[:, None]

    def region_start(e_row):
        return jnp.sum(jnp.where(e_row[None, :] == experts, row_off[:, None], 0), axis=0)

    e_a, e_b = metat[0].astype(jnp.int32), metat[1].astype(jnp.int32)
    pos_a = region_start(e_a) + metat[2].astype(jnp.int32)
    pos_b = region_start(e_b) + metat[3].astype(jnp.int32)
    tile_ids = jnp.arange(n_tiles, dtype=jnp.int32)
    tile_expert = jnp.minimum(
        jnp.sum((tile_ids[:, None] >= tiles_end[None, :]).astype(jnp.int32), axis=1),
        N_EXPERTS - 1)
    n_used = tiles_end[-1:].astype(jnp.int32)

    xs = _sc_scatter2(hp.reshape(2 * t, HALF_W),
                      jnp.concatenate([pos_a, pos_a + rows]),
                      jnp.concatenate([pos_b, pos_b + rows]), 2 * rows)
    ys = _experts(xs.reshape(2, rows, HALF_W), tile_expert, n_used,
                  w_gu, w_down, layer)
    z = _sc_gather(ys.reshape(2 * rows, HALF_W),
                   jnp.concatenate([pos_a, pos_b, pos_a + rows, pos_b + rows]))
    return z.reshape(4, t, HALF_W)


def kernel(x, norm_mix, norm_ffn, ml_w_in, ml_conv, ml_b_i, ml_b_f, ml_norm, ml_w_out, da_w_in, da_lq1, da_lk1, da_lq2, da_lk2, da_norm, da_w_out, moe_w_group, moe_b_group, moe_w_expert, moe_b_expert, moe_w_gu, moe_w_down, final_norm):
    batch, seq, d = x.shape
    xt = x.reshape(batch * seq, d)

    p, gcol, grow = _ml_proj(xt, norm_mix[0].reshape(1, d), ml_w_in[0], ml_b_i[0], ml_b_f[0],
                             tm=1024, tn=3072, out_dtype=BF16)
    assert p.shape[1] == 2 * ML_QK + 2 * ML_V
    y = _mlstm_core(p, gcol, grow, ml_conv[0], ml_norm[0].reshape(1, ML_V), batch=batch, seq=seq)
    xt, *routing = _res_router(y, ml_w_out[0].astype(BF16), xt, norm_ffn[0].reshape(1, d),
                               moe_w_group[0], moe_b_group[0], moe_w_expert[0], moe_b_expert[0],
                               tm=RR_TM, sub=RR_SUB)
    z = _moe_layer(routing, moe_w_gu, moe_w_down, 0)

    lambda_init = 0.8 - 0.6 * math.exp(-0.3 * 1)
    xt, p, vt = _norm_matmul_vt(xt, z, routing[1], norm_mix[1].reshape(1, d),
                                da_w_in[0].astype(BF16), tm=512, sub=256, vt_start=2 * DA_QK,
                                batch=batch, seq=seq, out_dtype=BF16)
    lam_params = jnp.stack([da_lq1[0], da_lk1[0], da_lq2[0], da_lk2[0]]).astype(F32)
    a = _diff_attn(p, vt, lam_params, da_norm[0].reshape(DA_DV, 1), lambda_init, batch=batch,
                   seq=seq)
    xt, *routing = _res_router(a, da_w_out[0].astype(BF16), xt, norm_ffn[1].reshape(1, d),
                               moe_w_group[1], moe_b_group[1], moe_w_expert[1], moe_b_expert[1],
                               tm=RR_TM, sub=RR_SUB)
    z = _moe_layer(routing, moe_w_gu, moe_w_down, 1)
    out = _combine_norm(xt, z, routing[1], final_norm.reshape(1, d), tm=1024)
    return out.reshape(batch, seq, d)
```

```python
import functools
import math

import jax
import jax.numpy as jnp
from jax import lax
from jax.experimental import pallas as pl
from jax.experimental.pallas import tpu as pltpu
from jax.experimental.pallas import tpu_sc as plsc

F32 = jnp.float32
BF16 = jnp.bfloat16
U32 = jnp.uint32

D_MODEL = 1024
EPS = 1e-6
ML_H = 4
ML_DV = 512
ML_DQK = 256
ML_QK = ML_H * ML_DQK
ML_V = ML_H * ML_DV
CONV_K = 4
ML_CHUNK = 256
CONV_TAIL = 8
ML_PROJ_SUB = 256
DA_H = 8
DA_DH = 64
DA_DV = 128
DA_QK = DA_H * 2 * DA_DH
DA_EPS = 1e-5
DA_TQ = 1024
DA_TK = 1024
DA_CW = 512
DA_KEY_ALIGN = 256
DA_LAZY_LIMIT = 64.0
DA_REBASE = 8.0
LOG2E = 1.4426950408889634
N_GROUPS = 4
EPG = 8
N_EXPERTS = 32
D_EXPERT = 256
RR_TM = 1024
RR_SUB = 256
ROUTE_ROWS = 64
MOE_TR = 1024
MOE_SUB = 256
HALF_W = D_MODEL // 4
SC_WINDOW = 128
LANES = 128

VMEM_LIMIT = 48 * 1024 * 1024


def _cparams(sem):
    return pltpu.CompilerParams(dimension_semantics=sem, vmem_limit_bytes=VMEM_LIMIT)


def _norm_matmul_vt_kernel(x_ref, z_ref, meta_ref, g_ref, w_ref, x2_ref, o_ref, vt_ref,
                           *, vt_start, sub):
    for i in range(x_ref.shape[0] // sub):
        r = slice(i * sub, (i + 1) * sub)
        x = _moe_combined(x_ref.at[r, :], z_ref.at[:, r, :], meta_ref.at[r, :])
        x2_ref[r, :] = x
        ms = jnp.mean(x * x, axis=-1, keepdims=True)
        xn = (x * lax.rsqrt(ms + EPS) * g_ref[...]).astype(BF16)
        res = jnp.dot(xn, w_ref[...], preferred_element_type=F32)
        o_ref[r, :] = res.astype(o_ref.dtype)
        vt_ref[:, r] = res[:, vt_start:].T.astype(vt_ref.dtype)


def _norm_matmul_vt(x, z, meta, g, w, *, tm, sub, vt_start, batch, seq, out_dtype):
    t, d = x.shape
    n = w.shape[1]
    per_batch = seq // tm
    return pl.pallas_call(
        functools.partial(_norm_matmul_vt_kernel, vt_start=vt_start, sub=sub),
        out_shape=(jax.ShapeDtypeStruct((t, d), F32),
                   jax.ShapeDtypeStruct((t, n), out_dtype),
                   jax.ShapeDtypeStruct((batch, n - vt_start, seq), out_dtype)),
        grid=(t // tm,),
        in_specs=[
            pl.BlockSpec((tm, d), lambda i: (i, 0)),
            pl.BlockSpec((4, tm, HALF_W), lambda i: (0, i, 0)),
            pl.BlockSpec((tm, LANES), lambda i: (i, 0)),
            pl.BlockSpec((1, d), lambda i: (0, 0)),
            pl.BlockSpec((d, n), lambda i: (0, 0)),
        ],
        out_specs=(pl.BlockSpec((tm, d), lambda i: (i, 0)),
                   pl.BlockSpec((tm, n), lambda i: (i, 0)),
                   pl.BlockSpec((None, n - vt_start, tm),
                                lambda i: (i // per_batch, 0, i % per_batch))),
        compiler_params=_cparams(("parallel",)),
        name="norm_matmul_vt",
    )(x, z, meta, g, w)


def _log_sigmoid(x):
    return jnp.minimum(x, 0.0) - jnp.log1p(jnp.exp(-jnp.abs(x)))


def _ml_proj_kernel(x_ref, g_ref, w_ref, wc_ref, wr_ref, bc_ref, br_ref, o_ref, oc_ref, or_ref,
                    xn_ref):
    j = pl.program_id(1)

    @pl.when(j == 0)
    def _():
        for i in range(x_ref.shape[0] // ML_PROJ_SUB):
            r = slice(i * ML_PROJ_SUB, (i + 1) * ML_PROJ_SUB)
            x = x_ref[r, :]
            ms = jnp.mean(x * x, axis=-1, keepdims=True)
            xn = (x * lax.rsqrt(ms + EPS) * g_ref[...]).astype(BF16)
            xn_ref[r, :] = xn
            o_ref[r, :] = jnp.dot(xn, w_ref[...], preferred_element_type=F32).astype(o_ref.dtype)
            gc = jnp.dot(xn, wc_ref[...], preferred_element_type=F32) + bc_ref[...]
            lane = lax.broadcasted_iota(jnp.int32, gc.shape, 1)
            oc_ref[r, :] = jnp.where(lane < ML_H, gc, _log_sigmoid(gc))
            gr = lax.dot_general(wr_ref[...], xn, (((1,), (1,)), ((), ())),
                                 preferred_element_type=F32) + br_ref[...]
            row = lax.broadcasted_iota(jnp.int32, gr.shape, 0)
            or_ref[:, r] = jnp.where(row < ML_H, gr, _log_sigmoid(gr))

    @pl.when(j > 0)
    def _():
        o_ref[...] = jnp.dot(xn_ref[...], w_ref[...],
                             preferred_element_type=F32).astype(o_ref.dtype)


def _ml_proj(x, g, w_in, b_i, b_f, *, tm, tn, out_dtype):
    t, d = x.shape
    ng = 2 * ML_H
    n_main = w_in.shape[1] - ng
    w_gates = w_in[:, n_main:]
    wc = jnp.zeros((d, LANES), BF16).at[:, :ng].set(w_gates.astype(BF16))
    wr = w_gates.T.astype(BF16)
    bias = jnp.concatenate([b_i, b_f]).astype(F32)
    bc = jnp.zeros((1, LANES), F32).at[0, :ng].set(bias)
    br = bias.reshape(ng, 1)
    return pl.pallas_call(
        _ml_proj_kernel,
        out_shape=(jax.ShapeDtypeStruct((t, n_main), out_dtype),
                   jax.ShapeDtypeStruct((t, LANES), F32), jax.ShapeDtypeStruct((ng, t), F32)),
        grid=(t // tm, n_main // tn),
        in_specs=[
            pl.BlockSpec((tm, d), lambda i, j: (i, 0)),
            pl.BlockSpec((1, d), lambda i, j: (0, 0)),
            pl.BlockSpec((d, tn), lambda i, j: (0, j)),
            pl.BlockSpec((d, LANES), lambda i, j: (0, 0)),
            pl.BlockSpec((ng, d), lambda i, j: (0, 0)),
            pl.BlockSpec((1, LANES), lambda i, j: (0, 0)),
            pl.BlockSpec((ng, 1), lambda i, j: (0, 0)),
        ],
        out_specs=(pl.BlockSpec((tm, tn), lambda i, j: (i, j)),
                   pl.BlockSpec((tm, LANES), lambda i, j: (i, 0)),
                   pl.BlockSpec((ng, tm), lambda i, j: (0, i))),
        scratch_shapes=[pltpu.VMEM((tm, d), BF16)],
        compiler_params=_cparams(("parallel", "arbitrary")),
        name="ml_proj",
    )(x, g, w_in.astype(BF16), wc, wr, bc, br)


def _split3(x):
    hi = x.astype(BF16)
    r = x - hi.astype(F32)
    mid = r.astype(BF16)
    lo = (r - mid.astype(F32)).astype(BF16)
    return hi, mid, lo


def _mlstm_kernel(q_ref, k_ref, v_ref, o_ref, gc_ref, gr_ref, cw_ref, ng_ref, y_ref,
                  tail_ref, ct_ref, n_ref, m_ref, shift_ref, tril_ref, triu_ref, neg_ref):
    L = ML_CHUNK
    c = pl.program_id(1)

    @pl.when(c == 0)
    def _():
        tail_ref[...] = jnp.zeros_like(tail_ref)
        ct_ref[...] = jnp.zeros_like(ct_ref)
        n_ref[...] = jnp.zeros_like(n_ref)
        m_ref[...] = jnp.zeros_like(m_ref)
        r_i = lax.broadcasted_iota(jnp.int32, (L, L), 0)
        c_i = lax.broadcasted_iota(jnp.int32, (L, L), 1)
        tril_ref[...] = (c_i <= r_i).astype(BF16)
        triu_ref[...] = (r_i <= c_i).astype(BF16)
        neg_ref[...] = jnp.where(c_i <= r_i, 0.0, -jnp.inf)
        for j in range(CONV_K - 1):
            shift_ref[j * L:(j + 1) * L, :] = (c_i == r_i - (CONV_K - 1 - j)).astype(BF16)

    u = jnp.concatenate([q_ref[...], k_ref[...]], axis=1)
    uf = u.astype(F32)
    shifted = jnp.dot(shift_ref[...], u, preferred_element_type=F32)
    conv = uf * cw_ref[CONV_K - 1:CONV_K, :]
    head = None
    for j in range(CONV_K - 1):
        conv = conv + shifted[j * L:(j + 1) * L, :] * cw_ref[j:j + 1, :]
        part = tail_ref[pl.ds(CONV_TAIL - (CONV_K - 1) + j, CONV_TAIL), :] * cw_ref[j:j + 1, :]
        head = part if head is None else head + part
    conv = jnp.concatenate([conv[:CONV_TAIL] + head, conv[CONV_TAIL:]], axis=0)
    tail_ref[0:CONV_TAIL, :] = uf[L - CONV_TAIL:, :]
    qk = conv * jax.nn.sigmoid(conv)
    q_all = (qk[:, :ML_QK] * (ML_DQK ** -0.5)).astype(BF16)
    k_all = qk[:, ML_QK:]

    tril = tril_ref[...]
    triu = triu_ref[...]
    gc = gc_ref[...]
    gr = gr_ref[...]
    bc_all = sum(jnp.dot(tril, p, preferred_element_type=F32) for p in _split3(gc))
    br_all = sum(jnp.dot(p, triu, preferred_element_type=F32) for p in _split3(gr))

    for h in range(ML_H):
        qh = q_all[:, h * ML_DQK:(h + 1) * ML_DQK]
        kh_f = k_all[:, h * ML_DQK:(h + 1) * ML_DQK]
        kh = kh_f.astype(BF16)
        vh = v_ref[:, h * ML_DV:(h + 1) * ML_DV]
        it_col = gc[:, h:h + 1]
        it_row = gr[h:h + 1, :]
        b_col = bc_all[:, ML_H + h:ML_H + h + 1]
        b_row = br_all[ML_H + h:ML_H + h + 1, :]
        m_prev = m_ref[h][:, 0:1]

        dmat = b_col + (it_row - b_row) + neg_ref[...]
        inter_log = b_col + m_prev
        m_t = jnp.maximum(inter_log, jnp.max(dmat, axis=1, keepdims=True))
        wts = jnp.exp(dmat - m_t)
        s = lax.dot_general(qh, kh, (((1,), (1,)), ((), ())), preferred_element_type=F32)
        sc = s * wts
        inter_scale = jnp.exp(inter_log - m_t)
        ct = ct_ref[h]
        num = (jnp.dot(sc.astype(BF16), vh, preferred_element_type=F32)
               + inter_scale * jnp.dot(qh, ct.astype(BF16), preferred_element_type=F32))
        n_row = n_ref[h]
        den = (jnp.sum(sc, axis=1, keepdims=True)
               + inter_scale * jnp.sum(qh.astype(F32) * n_row, axis=1, keepdims=True))
        h_out = num / jnp.maximum(jnp.abs(den), jnp.exp(-m_t))

        b_last = b_col[L - 1:L, :]
        lw_col = b_last - b_col + it_col
        lw_row = b_last - b_row + it_row
        m_new = jnp.maximum(b_last + m_prev, jnp.max(lw_row, axis=1, keepdims=True))
        ws_col = jnp.exp(lw_col - m_new)
        decay = jnp.exp(b_last + m_prev - m_new)
        kw = kh_f * ws_col
        ct_ref[h] = decay * ct + lax.dot_general(kw.astype(BF16), vh, (((0,), (0,)), ((), ())),
                                                  preferred_element_type=F32)
        n_ref[h] = decay * n_row + jnp.sum(kw, axis=0, keepdims=True)
        m_ref[h] = jnp.broadcast_to(m_new, (1, LANES))

        ms = jnp.mean(h_out * h_out, axis=1, keepdims=True)
        hn = h_out * lax.rsqrt(ms + EPS) * ng_ref[:, h * ML_DV:(h + 1) * ML_DV]
        og = o_ref[:, h * ML_DV:(h + 1) * ML_DV].astype(F32)
        y_ref[:, h * ML_DV:(h + 1) * ML_DV] = (hn * jax.nn.sigmoid(og)).astype(y_ref.dtype)


def _mlstm_core(p, gcol, grow, conv_w, norm_g, *, batch, seq):
    L = ML_CHUNK
    nc = seq // L
    p3 = p.reshape(batch, seq, 2 * ML_QK + 2 * ML_V)
    gc3 = gcol.reshape(batch, seq, LANES)
    gr3 = grow.reshape(2 * ML_H, batch, seq).transpose(1, 0, 2)
    y = pl.pallas_call(
        _mlstm_kernel,
        out_shape=jax.ShapeDtypeStruct((batch, seq, ML_V), BF16),
        grid=(batch, nc),
        in_specs=[
            pl.BlockSpec((None, L, ML_QK), lambda b, c: (b, c, 0)),
            pl.BlockSpec((None, L, ML_QK), lambda b, c: (b, c, 1)),
            pl.BlockSpec((None, L, ML_V), lambda b, c: (b, c, 1)),
            pl.BlockSpec((None, L, ML_V), lambda b, c: (b, c, 2)),
            pl.BlockSpec((None, L, LANES), lambda b, c: (b, c, 0)),
            pl.BlockSpec((None, 2 * ML_H, L), lambda b, c: (b, 0, c)),
            pl.BlockSpec((CONV_K, 2 * ML_QK), lambda b, c: (0, 0)),
            pl.BlockSpec((1, ML_V), lambda b, c: (0, 0)),
        ],
        out_specs=pl.BlockSpec((None, L, ML_V), lambda b, c: (b, c, 0)),
        scratch_shapes=[
            pltpu.VMEM((2 * CONV_TAIL, 2 * ML_QK), F32),
            pltpu.VMEM((ML_H, ML_DQK, ML_DV), F32),
            pltpu.VMEM((ML_H, 1, ML_DQK), F32),
            pltpu.VMEM((ML_H, 1, LANES), F32),
            pltpu.VMEM(((CONV_K - 1) * L, L), BF16),
            pltpu.VMEM((L, L), BF16),
            pltpu.VMEM((L, L), BF16),
            pltpu.VMEM((L, L), F32),
        ],
        compiler_params=_cparams(("parallel", "arbitrary")),
        name="mlstm_core",
    )(p3, p3, p3, p3, gc3, gr3, conv_w, norm_g)
    return y.reshape(batch * seq, ML_V)


def _diff_attn_kernel(q_ref, k_ref, vt_ref, lp_ref, ng_ref, o_ref, q2_ref, r_ref, m_ref, l_ref,
                      acc_ref, *, lambda_init):
    tq, tk, cw = DA_TQ, DA_TK, DA_CW
    qi = pl.program_id(2)
    q = q_ref[...].astype(F32) * (DA_DH ** -0.5 * LOG2E)
    lane = lax.broadcasted_iota(jnp.int32, q.shape, 1)
    q2_ref[0:tq, :] = jnp.where(lane < DA_DH, q, 0.0).astype(BF16)
    q2_ref[tq:2 * tq, :] = jnp.where(lane >= DA_DH, q, 0.0).astype(BF16)

    r_ref[...] = jnp.zeros_like(r_ref)
    m_ref[...] = jnp.full(m_ref.shape, -jnp.inf, F32)
    l_ref[...] = jnp.zeros_like(l_ref)
    acc_ref[...] = jnp.zeros_like(acc_ref)

    nch = 2 * tq // cw

    def keys_needed(c, diag):
        if diag is None:
            return tk
        visible = (c * cw) % tq + cw - diag * tk
        return max(0, min(tk, -(-visible // DA_KEY_ALIGN) * DA_KEY_ALIGN))

    def mask(s, c, diag):
        nk = s.shape[0]
        key = lax.broadcasted_iota(jnp.int32, (nk, cw), 0) + diag * tk
        qry = lax.broadcasted_iota(jnp.int32, (nk, cw), 1) + (c * cw) % tq
        return jnp.where(key <= qry, s, -jnp.inf)

    def scores(j, c, diag=None):
        nk = keys_needed(c, diag)
        if nk == 0:
            return None
        kb = k_ref[pl.ds(pl.multiple_of(j * tk, tk), nk), :]
        s = lax.dot_general(kb, q2_ref[c * cw:(c + 1) * cw, :], (((1,), (1,)), ((), ())),
                            preferred_element_type=F32)
        if diag is not None:
            s = mask(s, c, diag)
        return s, jnp.max(s, axis=0, keepdims=True)


    def exact_block(j, diag=None):
        start = pl.multiple_of(j * tk, tk)
        nxt = scores(j, 0, diag)
        for c in range(nch):
            cs = slice(c * cw, (c + 1) * cw)
            s, bmax = nxt
            if c + 1 < nch:
                nxt = scores(j, c + 1, diag)
            vbt = vt_ref[:, pl.ds(start, s.shape[0])]
            m_old = m_ref[:, cs]
            m_new = jnp.maximum(m_old, bmax)
            alpha = jnp.where(m_old == -jnp.inf, 0.0, jnp.exp2(r_ref[:, cs] - m_new))
            p = jnp.exp2(s - m_new)
            l_ref[:, cs] = alpha * l_ref[:, cs] + jnp.sum(p, axis=0, keepdims=True)
            acc_ref[:, cs] = alpha * acc_ref[:, cs] + jnp.dot(vbt, p.astype(BF16),
                                                               preferred_element_type=F32)
            m_ref[:, cs] = m_new
            r_ref[:, cs] = m_new

    def fast_block(j, diag=None):
        start = pl.multiple_of(j * tk, tk)
        nxt = scores(j, 0, diag)
        tent = []
        for c in range(nch):
            cs = slice(c * cw, (c + 1) * cw)
            s, bmax = nxt
            if c + 1 < nch:
                nxt = scores(j, c + 1, diag)
            vbt = vt_ref[:, pl.ds(start, s.shape[0])]
            r_old = r_ref[:, cs]
            p = jnp.exp2(s - r_old)
            tent.append((cs, bmax, r_old, m_ref[:, cs], jnp.sum(p, axis=0, keepdims=True),
                         jnp.dot(vbt, p.astype(BF16), preferred_element_type=F32)))
        worst = functools.reduce(jnp.maximum, [
            jnp.max(jnp.maximum(bmax - r_old,
                                jnp.where(m_old == -jnp.inf, r_old - bmax, -jnp.inf)),
                    axis=1, keepdims=True)
            for _, bmax, r_old, m_old, _, _ in tent])
        safe = worst <= DA_LAZY_LIMIT
        for cs, bmax, r_old, m_old, lt, acct in tent:
            m_new = jnp.maximum(m_old, bmax)
            r_new = jnp.where(jnp.abs(m_new - r_old) > DA_REBASE, m_new, r_old)
            scale = jnp.exp2(r_old - r_new)
            l_old = l_ref[:, cs]
            acc_old = acc_ref[:, cs]
            l_ref[:, cs] = jnp.where(safe, (l_old + lt) * scale, l_old)
            acc_ref[:, cs] = jnp.where(safe, (acc_old + acct) * scale, acc_old)
            m_ref[:, cs] = jnp.where(safe, m_new, m_old)
            r_ref[:, cs] = jnp.where(safe, r_new, r_old)

        @pl.when(jnp.logical_not(jnp.max(worst) <= DA_LAZY_LIMIT))
        def _():
            exact_block(j, diag)

    def fast_body(j, carry):
        fast_block(j)
        return carry

    assert tq == tk
    lax.fori_loop(0, qi, fast_body, 0)
    fast_block(qi, diag=0)


    lp = lp_ref[...]
    lam = (jnp.exp(jnp.sum(lp[0:1, :] * lp[1:2, :], axis=1, keepdims=True))
           - jnp.exp(jnp.sum(lp[2:3, :] * lp[3:4, :], axis=1, keepdims=True)) + lambda_init)
    out = acc_ref[...] / l_ref[...]
    o = out[:, :tq] - lam * out[:, tq:]
    ms = jnp.mean(o * o, axis=0, keepdims=True)
    on = o * lax.rsqrt(ms + DA_EPS) * ng_ref[...] * (1.0 - lambda_init)
    o_ref[...] = on.T.astype(o_ref.dtype)


def _diff_attn(p, vt, lam_params, norm_g, lambda_init, *, batch, seq):
    tq = DA_TQ
    p3 = p.reshape(batch, seq, 2 * DA_QK + DA_H * DA_DV)
    kern = functools.partial(_diff_attn_kernel, lambda_init=lambda_init)
    o = pl.pallas_call(
        kern,
        out_shape=jax.ShapeDtypeStruct((batch, seq, DA_H * DA_DV), BF16),
        grid=(batch, DA_H, seq // tq),
        in_specs=[
            pl.BlockSpec((None, tq, 2 * DA_DH), lambda b, h, i: (b, i, h)),
            pl.BlockSpec((None, seq, 2 * DA_DH), lambda b, h, i: (b, 0, DA_H + h)),
            pl.BlockSpec((None, DA_DV, seq), lambda b, h, i: (b, h, 0)),
            pl.BlockSpec((4, DA_DH), lambda b, h, i: (0, 0)),
            pl.BlockSpec((DA_DV, 1), lambda b, h, i: (0, 0)),
        ],
        out_specs=pl.BlockSpec((None, tq, DA_DV), lambda b, h, i: (b, i, h)),
        scratch_shapes=[
            pltpu.VMEM((2 * tq, 2 * DA_DH), BF16),
            pltpu.VMEM((1, 2 * tq), F32),
            pltpu.VMEM((1, 2 * tq), F32),
            pltpu.VMEM((1, 2 * tq), F32),
            pltpu.VMEM((DA_DV, 2 * tq), F32),
        ],
        compiler_params=_cparams(("parallel", "parallel", "arbitrary")),
        name="diff_attn",
    )(p3, p3, vt, lam_params, norm_g)
    return o.reshape(batch * seq, DA_H * DA_DV)


def _pack_halves(y):
    halves = []
    for h in range(2):
        base = h * 2 * HALF_W
        lo = y[:, base:base + HALF_W].astype(BF16).astype(F32)
        hi = y[:, base + HALF_W:base + 2 * HALF_W].astype(BF16).astype(F32)
        lo_bits = lax.bitcast_convert_type(lo, U32) >> 16
        hi_bits = lax.bitcast_convert_type(hi, U32)
        halves.append(hi_bits | lo_bits)
    return halves


def _unpack_halves(w0, w1):
    parts = []
    for w in (w0, w1):
        parts.append(lax.bitcast_convert_type(w << 16, F32))
        parts.append(lax.bitcast_convert_type(w & jnp.uint32(0xFFFF0000), F32))
    return jnp.concatenate(parts, axis=1)


def _route_logits(x, g_ref, w_ref, b_ref):
    ms = jnp.mean(x * x, axis=-1, keepdims=True)
    hn32 = x * lax.rsqrt(ms + EPS) * g_ref[...]
    lg = jnp.dot(hn32.astype(BF16), w_ref[...], preferred_element_type=F32) + b_ref[...]
    return _pack_halves(hn32), lg


def _route_decide(lg, run_ref, us_ref):
    n = lg.shape[0]
    lgt = lg.T[0:ROUTE_ROWS, :]
    row = lax.broadcasted_iota(jnp.int32, lgt.shape, 0).astype(F32)
    neg = -jnp.inf

    gmask = (row >= N_EXPERTS) & (row < N_EXPERTS + N_GROUPS)
    gl = jnp.where(gmask, lgt, neg)
    gmax = jnp.max(gl, axis=0, keepdims=True)
    gidx = jnp.min(jnp.where(gl == gmax, row, float(LANES)), axis=0, keepdims=True) - N_EXPERTS
    gsum = jnp.sum(jnp.where(gmask, jnp.exp(gl - gmax), 0.0), axis=0, keepdims=True)
    g_w = 1.0 / gsum

    emask = (row >= gidx * EPG) & (row < gidx * EPG + EPG)
    el = jnp.where(emask, lgt, neg)
    emax = jnp.max(el, axis=0, keepdims=True)
    eexp = jnp.where(emask, jnp.exp(el - emax), 0.0)
    ep = eexp / jnp.sum(eexp, axis=0, keepdims=True)
    ep = jnp.where(emask, ep, -1.0)
    p1 = jnp.max(ep, axis=0, keepdims=True)
    i1 = jnp.min(jnp.where(ep == p1, row, float(LANES)), axis=0, keepdims=True)
    ep2 = jnp.where(row == i1, -1.0, ep)
    p2 = jnp.max(ep2, axis=0, keepdims=True)
    i2 = jnp.min(jnp.where(ep2 == p2, row, float(LANES)), axis=0, keepdims=True)
    wsum = p1 + p2
    w1 = g_w * (p1 / wsum)
    w2 = g_w * (p2 / wsum)

    a1 = row == i1
    a2 = row == i2
    onehot = a1 | a2
    before = jnp.dot(onehot.astype(BF16), us_ref[...], preferred_element_type=F32) + run_ref[...]
    rank1 = jnp.sum(jnp.where(a1, before, 0.0), axis=0, keepdims=True)
    rank2 = jnp.sum(jnp.where(a2, before, 0.0), axis=0, keepdims=True)
    run_ref[...] += jnp.sum(onehot.astype(F32), axis=1, keepdims=True)

    return jnp.concatenate([i1, i2, rank1, rank2, w1, w2,
                            jnp.zeros((ROUTE_ROWS - 6, n), F32)], axis=0)


def _res_router_kernel(a_ref, w_ref, r_ref, g_ref, wr_ref, br_ref, o_ref, hp_ref, meta_ref,
                       metat_ref, cnt_ref, run_ref, us_ref):
    sub = us_ref.shape[0]

    @pl.when(pl.program_id(0) == 0)
    def _():
        run_ref[...] = jnp.zeros_like(run_ref)
        r_i = lax.broadcasted_iota(jnp.int32, (sub, sub), 0)
        c_i = lax.broadcasted_iota(jnp.int32, (sub, sub), 1)
        us_ref[...] = (r_i < c_i).astype(BF16)

    subs = [slice(i * sub, (i + 1) * sub) for i in range(a_ref.shape[0] // sub)]

    def project(r):
        x = r_ref[r, :] + jnp.dot(a_ref[r, :], w_ref[...], preferred_element_type=F32)
        o_ref[r, :] = x
        halves, lg = _route_logits(x, g_ref, wr_ref, br_ref)
        hp_ref[0, r, :] = halves[0]
        hp_ref[1, r, :] = halves[1]
        return lg

    nxt = project(subs[0])
    for i, r in enumerate(subs):
        lg = nxt
        if i + 1 < len(subs):
            nxt = project(subs[i + 1])
        metat = _route_decide(lg, run_ref, us_ref)
        metat_ref[:, r] = metat[0:8, :]
        meta_ref[r, :] = jnp.concatenate(
            [metat, jnp.zeros((LANES - ROUTE_ROWS, sub), F32)], axis=0).T
    cnt_ref[...] = jnp.broadcast_to(run_ref[...], cnt_ref.shape)


def _res_router(a, w, res, g, w_group, b_group, w_expert, b_expert, *, tm, sub):
    t, k = a.shape
    d = w.shape[1]
    wr = jnp.zeros((d, LANES), BF16)
    wr = wr.at[:, :N_EXPERTS].set(w_expert.astype(BF16))
    wr = wr.at[:, N_EXPERTS:N_EXPERTS + N_GROUPS].set(w_group.astype(BF16))
    br = jnp.zeros((1, LANES), F32)
    br = br.at[0, :N_EXPERTS].set(b_expert.astype(F32))
    br = br.at[0, N_EXPERTS:N_EXPERTS + N_GROUPS].set(b_group.astype(F32))
    return pl.pallas_call(
        _res_router_kernel,
        out_shape=(jax.ShapeDtypeStruct((t, d), F32),
                   jax.ShapeDtypeStruct((2, t, HALF_W), U32),
                   jax.ShapeDtypeStruct((t, LANES), F32),
                   jax.ShapeDtypeStruct((8, t), F32),
                   jax.ShapeDtypeStruct((ROUTE_ROWS, LANES), F32)),
        grid=(t // tm,),
        in_specs=[
            pl.BlockSpec((tm, k), lambda i: (i, 0)),
            pl.BlockSpec((k, d), lambda i: (0, 0)),
            pl.BlockSpec((tm, d), lambda i: (i, 0)),
            pl.BlockSpec((1, d), lambda i: (0, 0)),
            pl.BlockSpec((d, LANES), lambda i: (0, 0)),
            pl.BlockSpec((1, LANES), lambda i: (0, 0)),
        ],
        out_specs=(pl.BlockSpec((tm, d), lambda i: (i, 0)),
                   pl.BlockSpec((2, tm, HALF_W), lambda i: (0, i, 0)),
                   pl.BlockSpec((tm, LANES), lambda i: (i, 0)),
                   pl.BlockSpec((8, tm), lambda i: (0, i)),
                   pl.BlockSpec((ROUTE_ROWS, LANES), lambda i: (0, 0))),
        scratch_shapes=[pltpu.VMEM((ROUTE_ROWS, 1), F32), pltpu.VMEM((sub, sub), BF16)],
        compiler_params=_cparams(("arbitrary",)),
        name="res_router",
    )(a, w, res, g, wr, br)


def _sc_mesh():
    return plsc.VectorSubcoreMesh(core_axis_name="c", subcore_axis_name="s")


def _sc_scatter2(x, i0, i1, n_out):
    n, d = x.shape

    @pl.kernel(out_type=jax.ShapeDtypeStruct((n_out, d), x.dtype), mesh=_sc_mesh())
    def k(x_hbm, i0_hbm, i1_hbm, o_hbm):
        def body(x_vmem, i0_vmem, i1_vmem):
            pltpu.sync_copy(x_vmem, o_hbm.at[i0_vmem.at[0]])
            pltpu.sync_copy(x_vmem, o_hbm.at[i1_vmem.at[0]])

        pltpu.emit_pipeline(
            body,
            grid=(n // SC_WINDOW,),
            in_specs=[pl.BlockSpec((SC_WINDOW, d), lambda i: (i, 0)),
                      pl.BlockSpec((1, SC_WINDOW), lambda i: (0, i)),
                      pl.BlockSpec((1, SC_WINDOW), lambda i: (0, i))],
            out_specs=[],
            core_axis_name=("c", "s"),
            dimension_semantics=(pltpu.PARALLEL,),
        )(x_hbm, i0_hbm, i1_hbm)

    return k(x, i0.reshape(1, n), i1.reshape(1, n))


def _sc_gather(x, idx):
    n = idx.shape[0]
    d = x.shape[1]

    @pl.kernel(out_type=jax.ShapeDtypeStruct((n, d), x.dtype), mesh=_sc_mesh())
    def k(x_hbm, i_hbm, o_hbm):
        def body(i_vmem, o_vmem):
            pltpu.sync_copy(x_hbm.at[i_vmem.at[0]], o_vmem)

        pltpu.emit_pipeline(
            body,
            grid=(n // SC_WINDOW,),
            in_specs=[pl.BlockSpec((1, SC_WINDOW), lambda i: (0, i))],
            out_specs=[pl.BlockSpec((SC_WINDOW, d), lambda i: (i, 0))],
            core_axis_name=("c", "s"),
            dimension_semantics=(pltpu.PARALLEL,),
        )(i_hbm, o_hbm)

    return k(x, idx.reshape(1, n))


def _experts_kernel(te_ref, nu_ref, xs_ref, wgu_ref, wd_ref, ys_ref, wgu_bf_ref, wd_bf_ref):
    j = pl.program_id(0)

    @pl.when(j < nu_ref[0])
    def _():
        @pl.when((j == 0) | (te_ref[j] != te_ref[jnp.maximum(j - 1, 0)]))
        def _():
            wgu_bf_ref[...] = wgu_ref[...].astype(BF16)
            wd_bf_ref[...] = wd_ref[...].astype(BF16)

        subs = [slice(i * MOE_SUB, (i + 1) * MOE_SUB) for i in range(MOE_TR // MOE_SUB)]
        xs = [_unpack_halves(xs_ref[0, r, :], xs_ref[1, r, :]).astype(BF16) for r in subs]
        gus = [jnp.dot(x, wgu_bf_ref[...], preferred_element_type=F32) for x in xs]
        acts = [(gu[:, :D_EXPERT] * jax.nn.sigmoid(gu[:, :D_EXPERT]) * gu[:, D_EXPERT:]).astype(BF16)
                for gu in gus]
        ys = [jnp.dot(a, wd_bf_ref[...], preferred_element_type=F32) for a in acts]
        for r, y in zip(subs, ys):
            halves = _pack_halves(y)
            ys_ref[0, r, :] = halves[0]
            ys_ref[1, r, :] = halves[1]


def _experts(xs, tile_expert, n_used, w_gu, w_down, layer):
    _, rows, _ = xs.shape
    d = w_gu.shape[2]
    return pl.pallas_call(
        _experts_kernel,
        out_shape=jax.ShapeDtypeStruct(xs.shape, U32),
        grid_spec=pltpu.PrefetchScalarGridSpec(
            num_scalar_prefetch=2,
            grid=(rows // MOE_TR,),
            in_specs=[
                pl.BlockSpec((2, MOE_TR, HALF_W), lambda j, te, nu: (0, j, 0)),
                pl.BlockSpec((None, None, d, 2 * D_EXPERT),
                             lambda j, te, nu: (layer, te[j], 0, 0)),
                pl.BlockSpec((None, None, D_EXPERT, d),
                             lambda j, te, nu: (layer, te[j], 0, 0)),
            ],
            out_specs=pl.BlockSpec((2, MOE_TR, HALF_W), lambda j, te, nu: (0, j, 0)),
            scratch_shapes=[pltpu.VMEM((d, 2 * D_EXPERT), BF16), pltpu.VMEM((D_EXPERT, d), BF16)],
        ),
        compiler_params=_cparams(("arbitrary",)),
        name="moe_experts",
    )(tile_expert, n_used, xs, w_gu, w_down)


def _moe_combined(x_ref, z_ref, meta_ref):
    meta = meta_ref[...]
    y_a = _unpack_halves(z_ref[0], z_ref[2])
    y_b = _unpack_halves(z_ref[1], z_ref[3])
    return x_ref[...] + meta[:, 4:5] * y_a + meta[:, 5:6] * y_b


def _combine_norm_kernel(x_ref, z_ref, meta_ref, g_ref, o_ref):
    out = _moe_combined(x_ref, z_ref, meta_ref)
    ms = jnp.mean(out * out, axis=-1, keepdims=True)
    o_ref[...] = out * lax.rsqrt(ms + EPS) * g_ref[...]


def _combine_norm(x, z, meta, g, *, tm):
    t, d = x.shape
    return pl.pallas_call(
        _combine_norm_kernel,
        out_shape=jax.ShapeDtypeStruct((t, d), F32),
        grid=(t // tm,),
        in_specs=[
            pl.BlockSpec((tm, d), lambda i: (i, 0)),
            pl.BlockSpec((4, tm, HALF_W), lambda i: (0, i, 0)),
            pl.BlockSpec((tm, LANES), lambda i: (i, 0)),
            pl.BlockSpec((1, d), lambda i: (0, 0)),
        ],
        out_specs=pl.BlockSpec((tm, d), lambda i: (i, 0)),
        compiler_params=_cparams(("parallel",)),
        name="moe_combine",
    )(x, z, meta, g)


def _moe_layer(routing, w_gu, w_down, layer):
    hp, meta, metat, cnt = routing
    t = meta.shape[0]
    n_tiles = 2 * t // MOE_TR + N_EXPERTS
    rows = n_tiles * MOE_TR
    counts = cnt[:N_EXPERTS, 0].astype(jnp.int32)
    tiles_e = (counts + MOE_TR - 1) // MOE_TR
    tiles_end = jnp.cumsum(tiles_e)
    row_off = (tiles_end - tiles_e) * MOE_TR
    experts = jnp.arange(N_EXPERTS, dtype=jnp.int32)[:, None]

    def region_start(e_row):
        return jnp.sum(jnp.where(e_row[None, :] == experts, row_off[:, None], 0), axis=0)

    e_a, e_b = metat[0].astype(jnp.int32), metat[1].astype(jnp.int32)
    pos_a = region_start(e_a) + metat[2].astype(jnp.int32)
    pos_b = region_start(e_b) + metat[3].astype(jnp.int32)
    tile_ids = jnp.arange(n_tiles, dtype=jnp.int32)
    tile_expert = jnp.minimum(
        jnp.sum((tile_ids[:, None] >= tiles_end[None, :]).astype(jnp.int32), axis=1),
        N_EXPERTS - 1)
    n_used = tiles_end[-1:].astype(jnp.int32)

    xs = _sc_scatter2(hp.reshape(2 * t, HALF_W),
                      jnp.concatenate([pos_a, pos_a + rows]),
                      jnp.concatenate([pos_b, pos_b + rows]), 2 * rows)
    ys = _experts(xs.reshape(2, rows, HALF_W), tile_expert, n_used,
                  w_gu, w_down, layer)
    z = _sc_gather(ys.reshape(2 * rows, HALF_W),
                   jnp.concatenate([pos_a, pos_b, pos_a + rows, pos_b + rows]))
    return z.reshape(4, t, HALF_W)


def kernel(x, norm_mix, norm_ffn, ml_w_in, ml_conv, ml_b_i, ml_b_f, ml_norm, ml_w_out, da_w_in, da_lq1, da_lk1, da_lq2, da_lk2, da_norm, da_w_out, moe_w_group, moe_b_group, moe_w_expert, moe_b_expert, moe_w_gu, moe_w_down, final_norm):
    batch, seq, d = x.shape
    xt = x.reshape(batch * seq, d)

    p, gcol, grow = _ml_proj(xt, norm_mix[0].reshape(1, d), ml_w_in[0], ml_b_i[0], ml_b_f[0],
                             tm=1024, tn=3072, out_dtype=BF16)
    assert p.shape[1] == 2 * ML_QK + 2 * ML_V
    y = _mlstm_core(p, gcol, grow, ml_conv[0], ml_norm[0].reshape(1, ML_V), batch=batch, seq=seq)
    xt, *routing = _res_router(y, ml_w_out[0].astype(BF16), xt, norm_ffn[0].reshape(1, d),
                               moe_w_group[0], moe_b_group[0], moe_w_expert[0], moe_b_expert[0],
                               tm=RR_TM, sub=RR_SUB)
    z = _moe_layer(routing, moe_w_gu, moe_w_down, 0)

    lambda_init = 0.8 - 0.6 * math.exp(-0.3 * 1)
    xt, p, vt = _norm_matmul_vt(xt, z, routing[1], norm_mix[1].reshape(1, d),
                                da_w_in[0].astype(BF16), tm=512, sub=256, vt_start=2 * DA_QK,
                                batch=batch, seq=seq, out_dtype=BF16)
    lam_params = jnp.stack([da_lq1[0], da_lk1[0], da_lq2[0], da_lk2[0]]).astype(F32)
    a = _diff_attn(p, vt, lam_params, da_norm[0].reshape(DA_DV, 1), lambda_init, batch=batch,
                   seq=seq)
    xt, *routing = _res_router(a, da_w_out[0].astype(BF16), xt, norm_ffn[1].reshape(1, d),
                               moe_w_group[1], moe_b_group[1], moe_w_expert[1], moe_b_expert[1],
                               tm=RR_TM, sub=RR_SUB)
    z = _moe_layer(routing, moe_w_gu, moe_w_down, 1)
    out = _combine_norm(xt, z, routing[1], final_norm.reshape(1, d), tm=1024)
    return out.reshape(batch, seq, d)
```

```python
import functools
import math

import jax
import jax.numpy as jnp
from jax import lax
from jax.experimental import pallas as pl
from jax.experimental.pallas import tpu as pltpu
from jax.experimental.pallas import tpu_sc as plsc

F32 = jnp.float32
BF16 = jnp.bfloat16
U32 = jnp.uint32

D_MODEL = 1024
EPS = 1e-6
ML_H = 4
ML_DV = 512
ML_DQK = 256
ML_QK = ML_H * ML_DQK
ML_V = ML_H * ML_DV
CONV_K = 4
ML_CHUNK = 256
CONV_TAIL = 8
ML_PROJ_SUB = 256
DA_H = 8
DA_DH = 64
DA_DV = 128
DA_QK = DA_H * 2 * DA_DH
DA_EPS = 1e-5
DA_TQ = 2048
DA_TK = 2048
DA_CW = 512
DA_KEY_ALIGN = 256
DA_LAZY_LIMIT = 64.0
DA_REBASE = 8.0
LOG2E = 1.4426950408889634
N_GROUPS = 4
EPG = 8
N_EXPERTS = 32
D_EXPERT = 256
RR_TM = 1024
RR_SUB = 256
ROUTE_ROWS = 40
MOE_TR = 1024
MOE_SUB = 256
HALF_W = D_MODEL // 4
SC_WINDOW = 128
LANES = 128

VMEM_LIMIT = 48 * 1024 * 1024


def _cparams(sem):
    return pltpu.CompilerParams(dimension_semantics=sem, vmem_limit_bytes=VMEM_LIMIT)


def _norm_matmul_vt_kernel(x_ref, z_ref, meta_ref, g_ref, w_ref, x2_ref, o_ref, vt_ref,
                           *, vt_start, sub):
    for i in range(x_ref.shape[0] // sub):
        r = slice(i * sub, (i + 1) * sub)
        x = _moe_combined(x_ref.at[r, :], z_ref.at[:, r, :], meta_ref.at[r, :])
        x2_ref[r, :] = x
        ms = jnp.mean(x * x, axis=-1, keepdims=True)
        xn = (x * lax.rsqrt(ms + EPS) * g_ref[...]).astype(BF16)
        res = jnp.dot(xn, w_ref[...], preferred_element_type=F32)
        o_ref[r, :] = res.astype(o_ref.dtype)
        vt_ref[:, r] = res[:, vt_start:].T.astype(vt_ref.dtype)


def _norm_matmul_vt(x, z, meta, g, w, *, tm, sub, vt_start, batch, seq, out_dtype):
    t, d = x.shape
    n = w.shape[1]
    per_batch = seq // tm
    return pl.pallas_call(
        functools.partial(_norm_matmul_vt_kernel, vt_start=vt_start, sub=sub),
        out_shape=(jax.ShapeDtypeStruct((t, d), F32),
                   jax.ShapeDtypeStruct((t, n), out_dtype),
                   jax.ShapeDtypeStruct((batch, n - vt_start, seq), out_dtype)),
        grid=(t // tm,),
        in_specs=[
            pl.BlockSpec((tm, d), lambda i: (i, 0)),
            pl.BlockSpec((4, tm, HALF_W), lambda i: (0, i, 0)),
            pl.BlockSpec((tm, LANES), lambda i: (i, 0)),
            pl.BlockSpec((1, d), lambda i: (0, 0)),
            pl.BlockSpec((d, n), lambda i: (0, 0)),
        ],
        out_specs=(pl.BlockSpec((tm, d), lambda i: (i, 0)),
                   pl.BlockSpec((tm, n), lambda i: (i, 0)),
                   pl.BlockSpec((None, n - vt_start, tm),
                                lambda i: (i // per_batch, 0, i % per_batch))),
        compiler_params=_cparams(("parallel",)),
        name="norm_matmul_vt",
    )(x, z, meta, g, w)


def _log_sigmoid(x):
    return jnp.minimum(x, 0.0) - jnp.log1p(jnp.exp(-jnp.abs(x)))


def _ml_proj_kernel(x_ref, g_ref, w_ref, wc_ref, wr_ref, bc_ref, br_ref, o_ref, oc_ref, or_ref,
                    xn_ref):
    j = pl.program_id(1)

    @pl.when(j == 0)
    def _():
        for i in range(x_ref.shape[0] // ML_PROJ_SUB):
            r = slice(i * ML_PROJ_SUB, (i + 1) * ML_PROJ_SUB)
            x = x_ref[r, :]
            ms = jnp.mean(x * x, axis=-1, keepdims=True)
            xn = (x * lax.rsqrt(ms + EPS) * g_ref[...]).astype(BF16)
            xn_ref[r, :] = xn
            o_ref[r, :] = jnp.dot(xn, w_ref[...], preferred_element_type=F32).astype(o_ref.dtype)
            gc = jnp.dot(xn, wc_ref[...], preferred_element_type=F32) + bc_ref[...]
            lane = lax.broadcasted_iota(jnp.int32, gc.shape, 1)
            oc_ref[r, :] = jnp.where(lane < ML_H, gc, _log_sigmoid(gc))
            gr = lax.dot_general(wr_ref[...], xn, (((1,), (1,)), ((), ())),
                                 preferred_element_type=F32) + br_ref[...]
            row = lax.broadcasted_iota(jnp.int32, gr.shape, 0)
            or_ref[:, r] = jnp.where(row < ML_H, gr, _log_sigmoid(gr))

    @pl.when(j > 0)
    def _():
        o_ref[...] = jnp.dot(xn_ref[...], w_ref[...],
                             preferred_element_type=F32).astype(o_ref.dtype)


def _ml_proj(x, g, w_in, b_i, b_f, *, tm, tn, out_dtype):
    t, d = x.shape
    ng = 2 * ML_H
    n_main = w_in.shape[1] - ng
    w_gates = w_in[:, n_main:]
    wc = jnp.zeros((d, LANES), BF16).at[:, :ng].set(w_gates.astype(BF16))
    wr = w_gates.T.astype(BF16)
    bias = jnp.concatenate([b_i, b_f]).astype(F32)
    bc = jnp.zeros((1, LANES), F32).at[0, :ng].set(bias)
    br = bias.reshape(ng, 1)
    return pl.pallas_call(
        _ml_proj_kernel,
        out_shape=(jax.ShapeDtypeStruct((t, n_main), out_dtype),
                   jax.ShapeDtypeStruct((t, LANES), F32), jax.ShapeDtypeStruct((ng, t), F32)),
        grid=(t // tm, n_main // tn),
        in_specs=[
            pl.BlockSpec((tm, d), lambda i, j: (i, 0)),
            pl.BlockSpec((1, d), lambda i, j: (0, 0)),
            pl.BlockSpec((d, tn), lambda i, j: (0, j)),
            pl.BlockSpec((d, LANES), lambda i, j: (0, 0)),
            pl.BlockSpec((ng, d), lambda i, j: (0, 0)),
            pl.BlockSpec((1, LANES), lambda i, j: (0, 0)),
            pl.BlockSpec((ng, 1), lambda i, j: (0, 0)),
        ],
        out_specs=(pl.BlockSpec((tm, tn), lambda i, j: (i, j)),
                   pl.BlockSpec((tm, LANES), lambda i, j: (i, 0)),
                   pl.BlockSpec((ng, tm), lambda i, j: (0, i))),
        scratch_shapes=[pltpu.VMEM((tm, d), BF16)],
        compiler_params=_cparams(("parallel", "arbitrary")),
        name="ml_proj",
    )(x, g, w_in.astype(BF16), wc, wr, bc, br)


def _split3(x):
    hi = x.astype(BF16)
    r = x - hi.astype(F32)
    mid = r.astype(BF16)
    lo = (r - mid.astype(F32)).astype(BF16)
    return hi, mid, lo


def _mlstm_kernel(q_ref, k_ref, v_ref, o_ref, gc_ref, gr_ref, cw_ref, ng_ref, y_ref,
                  tail_ref, ct_ref, n_ref, m_ref, shift_ref, tril_ref, triu_ref, neg_ref):
    L = ML_CHUNK
    c = pl.program_id(1)

    @pl.when(c == 0)
    def _():
        tail_ref[...] = jnp.zeros_like(tail_ref)
        ct_ref[...] = jnp.zeros_like(ct_ref)
        n_ref[...] = jnp.zeros_like(n_ref)
        m_ref[...] = jnp.zeros_like(m_ref)
        r_i = lax.broadcasted_iota(jnp.int32, (L, L), 0)
        c_i = lax.broadcasted_iota(jnp.int32, (L, L), 1)
        tril_ref[...] = (c_i <= r_i).astype(BF16)
        triu_ref[...] = (r_i <= c_i).astype(BF16)
        neg_ref[...] = jnp.where(c_i <= r_i, 0.0, -jnp.inf)
        for j in range(CONV_K - 1):
            shift_ref[j * L:(j + 1) * L, :] = (c_i == r_i - (CONV_K - 1 - j)).astype(BF16)

    u = jnp.concatenate([q_ref[...], k_ref[...]], axis=1)
    uf = u.astype(F32)
    shifted = jnp.dot(shift_ref[...], u, preferred_element_type=F32)
    conv = uf * cw_ref[CONV_K - 1:CONV_K, :]
    head = None
    for j in range(CONV_K - 1):
        conv = conv + shifted[j * L:(j + 1) * L, :] * cw_ref[j:j + 1, :]
        part = tail_ref[pl.ds(CONV_TAIL - (CONV_K - 1) + j, CONV_TAIL), :] * cw_ref[j:j + 1, :]
        head = part if head is None else head + part
    conv = jnp.concatenate([conv[:CONV_TAIL] + head, conv[CONV_TAIL:]], axis=0)
    tail_ref[0:CONV_TAIL, :] = uf[L - CONV_TAIL:, :]
    qk = conv * jax.nn.sigmoid(conv)
    q_all = (qk[:, :ML_QK] * (ML_DQK ** -0.5)).astype(BF16)
    k_all = qk[:, ML_QK:]

    tril = tril_ref[...]
    triu = triu_ref[...]
    gc = gc_ref[...]
    gr = gr_ref[...]
    bc_all = sum(jnp.dot(tril, p, preferred_element_type=F32) for p in _split3(gc))
    br_all = sum(jnp.dot(p, triu, preferred_element_type=F32) for p in _split3(gr))

    for h in range(ML_H):
        qh = q_all[:, h * ML_DQK:(h + 1) * ML_DQK]
        kh_f = k_all[:, h * ML_DQK:(h + 1) * ML_DQK]
        kh = kh_f.astype(BF16)
        vh = v_ref[:, h * ML_DV:(h + 1) * ML_DV]
        it_col = gc[:, h:h + 1]
        it_row = gr[h:h + 1, :]
        b_col = bc_all[:, ML_H + h:ML_H + h + 1]
        b_row = br_all[ML_H + h:ML_H + h + 1, :]
        m_prev = m_ref[h][:, 0:1]

        dmat = b_col + (it_row - b_row) + neg_ref[...]
        inter_log = b_col + m_prev
        m_t = jnp.maximum(inter_log, jnp.max(dmat, axis=1, keepdims=True))
        wts = jnp.exp(dmat - m_t)
        s = lax.dot_general(qh, kh, (((1,), (1,)), ((), ())), preferred_element_type=F32)
        sc = s * wts
        inter_scale = jnp.exp(inter_log - m_t)
        ct = ct_ref[h]
        num = (jnp.dot(sc.astype(BF16), vh, preferred_element_type=F32)
               + inter_scale * jnp.dot(qh, ct.astype(BF16), preferred_element_type=F32))
        n_row = n_ref[h]
        den = (jnp.sum(sc, axis=1, keepdims=True)
               + inter_scale * jnp.sum(qh.astype(F32) * n_row, axis=1, keepdims=True))
        h_out = num / jnp.maximum(jnp.abs(den), jnp.exp(-m_t))

        b_last = b_col[L - 1:L, :]
        lw_col = b_last - b_col + it_col
        lw_row = b_last - b_row + it_row
        m_new = jnp.maximum(b_last + m_prev, jnp.max(lw_row, axis=1, keepdims=True))
        ws_col = jnp.exp(lw_col - m_new)
        decay = jnp.exp(b_last + m_prev - m_new)
        kw = kh_f * ws_col
        ct_ref[h] = decay * ct + lax.dot_general(kw.astype(BF16), vh, (((0,), (0,)), ((), ())),
                                                  preferred_element_type=F32)
        n_ref[h] = decay * n_row + jnp.sum(kw, axis=0, keepdims=True)
        m_ref[h] = jnp.broadcast_to(m_new, (1, LANES))

        ms = jnp.mean(h_out * h_out, axis=1, keepdims=True)
        hn = h_out * lax.rsqrt(ms + EPS) * ng_ref[:, h * ML_DV:(h + 1) * ML_DV]
        og = o_ref[:, h * ML_DV:(h + 1) * ML_DV].astype(F32)
        y_ref[:, h * ML_DV:(h + 1) * ML_DV] = (hn * jax.nn.sigmoid(og)).astype(y_ref.dtype)


def _mlstm_core(p, gcol, grow, conv_w, norm_g, *, batch, seq):
    L = ML_CHUNK
    nc = seq // L
    p3 = p.reshape(batch, seq, 2 * ML_QK + 2 * ML_V)
    gc3 = gcol.reshape(batch, seq, LANES)
    gr3 = grow.reshape(2 * ML_H, batch, seq).transpose(1, 0, 2)
    y = pl.pallas_call(
        _mlstm_kernel,
        out_shape=jax.ShapeDtypeStruct((batch, seq, ML_V), BF16),
        grid=(batch, nc),
        in_specs=[
            pl.BlockSpec((None, L, ML_QK), lambda b, c: (b, c, 0)),
            pl.BlockSpec((None, L, ML_QK), lambda b, c: (b, c, 1)),
            pl.BlockSpec((None, L, ML_V), lambda b, c: (b, c, 1)),
            pl.BlockSpec((None, L, ML_V), lambda b, c: (b, c, 2)),
            pl.BlockSpec((None, L, LANES), lambda b, c: (b, c, 0)),
            pl.BlockSpec((None, 2 * ML_H, L), lambda b, c: (b, 0, c)),
            pl.BlockSpec((CONV_K, 2 * ML_QK), lambda b, c: (0, 0)),
            pl.BlockSpec((1, ML_V), lambda b, c: (0, 0)),
        ],
        out_specs=pl.BlockSpec((None, L, ML_V), lambda b, c: (b, c, 0)),
        scratch_shapes=[
            pltpu.VMEM((2 * CONV_TAIL, 2 * ML_QK), F32),
            pltpu.VMEM((ML_H, ML_DQK, ML_DV), F32),
            pltpu.VMEM((ML_H, 1, ML_DQK), F32),
            pltpu.VMEM((ML_H, 1, LANES), F32),
            pltpu.VMEM(((CONV_K - 1) * L, L), BF16),
            pltpu.VMEM((L, L), BF16),
            pltpu.VMEM((L, L), BF16),
            pltpu.VMEM((L, L), F32),
        ],
        compiler_params=_cparams(("parallel", "arbitrary")),
        name="mlstm_core",
    )(p3, p3, p3, p3, gc3, gr3, conv_w, norm_g)
    return y.reshape(batch * seq, ML_V)


def _diff_attn_kernel(q_ref, k_ref, vt_ref, lp_ref, ng_ref, o_ref, q2_ref, r_ref, m_ref, l_ref,
                      acc_ref, *, lambda_init):
    tq, tk, cw = DA_TQ, DA_TK, DA_CW
    qi = pl.program_id(2)
    q = q_ref[...].astype(F32) * (DA_DH ** -0.5 * LOG2E)
    lane = lax.broadcasted_iota(jnp.int32, q.shape, 1)
    q2_ref[0:tq, :] = jnp.where(lane < DA_DH, q, 0.0).astype(BF16)
    q2_ref[tq:2 * tq, :] = jnp.where(lane >= DA_DH, q, 0.0).astype(BF16)

    r_ref[...] = jnp.zeros_like(r_ref)
    m_ref[...] = jnp.full(m_ref.shape, -jnp.inf, F32)
    l_ref[...] = jnp.zeros_like(l_ref)
    acc_ref[...] = jnp.zeros_like(acc_ref)

    nch = 2 * tq // cw

    def keys_needed(c, diag):
        if diag is None:
            return tk
        visible = (c * cw) % tq + cw - diag * tk
        return max(0, min(tk, -(-visible // DA_KEY_ALIGN) * DA_KEY_ALIGN))

    def mask(s, c, diag):
        nk = s.shape[0]
        key = lax.broadcasted_iota(jnp.int32, (nk, cw), 0) + diag * tk
        qry = lax.broadcasted_iota(jnp.int32, (nk, cw), 1) + (c * cw) % tq
        return jnp.where(key <= qry, s, -jnp.inf)

    def scores(j, c, diag=None):
        nk = keys_needed(c, diag)
        if nk == 0:
            return None
        kb = k_ref[pl.ds(pl.multiple_of(j * tk, tk), nk), :]
        s = lax.dot_general(kb, q2_ref[c * cw:(c + 1) * cw, :], (((1,), (1,)), ((), ())),
                            preferred_element_type=F32)
        if diag is not None:
            s = mask(s, c, diag)
        return s, jnp.max(s, axis=0, keepdims=True)


    def exact_block(j, diag=None):
        start = pl.multiple_of(j * tk, tk)
        nxt = scores(j, 0, diag)
        for c in range(nch):
            cs = slice(c * cw, (c + 1) * cw)
            s, bmax = nxt
            if c + 1 < nch:
                nxt = scores(j, c + 1, diag)
            vbt = vt_ref[:, pl.ds(start, s.shape[0])]
            m_old = m_ref[:, cs]
            m_new = jnp.maximum(m_old, bmax)
            alpha = jnp.where(m_old == -jnp.inf, 0.0, jnp.exp2(r_ref[:, cs] - m_new))
            p = jnp.exp2(s - m_new)
            l_ref[:, cs] = alpha * l_ref[:, cs] + jnp.sum(p, axis=0, keepdims=True)
            acc_ref[:, cs] = alpha * acc_ref[:, cs] + jnp.dot(vbt, p.astype(BF16),
                                                               preferred_element_type=F32)
            m_ref[:, cs] = m_new
            r_ref[:, cs] = m_new

    def fast_block(j, diag=None):
        start = pl.multiple_of(j * tk, tk)
        nxt = scores(j, 0, diag)
        tent = []
        for c in range(nch):
            cs = slice(c * cw, (c + 1) * cw)
            s, bmax = nxt
            if c + 1 < nch:
                nxt = scores(j, c + 1, diag)
            vbt = vt_ref[:, pl.ds(start, s.shape[0])]
            r_old = r_ref[:, cs]
            p = jnp.exp2(s - r_old)
            tent.append((cs, bmax, r_old, m_ref[:, cs], jnp.sum(p, axis=0, keepdims=True),
                         jnp.dot(vbt, p.astype(BF16), preferred_element_type=F32)))
        worst = functools.reduce(jnp.maximum, [
            jnp.max(jnp.maximum(bmax - r_old,
                                jnp.where(m_old == -jnp.inf, r_old - bmax, -jnp.inf)),
                    axis=1, keepdims=True)
            for _, bmax, r_old, m_old, _, _ in tent])
        safe = worst <= DA_LAZY_LIMIT
        for cs, bmax, r_old, m_old, lt, acct in tent:
            m_new = jnp.maximum(m_old, bmax)
            r_new = jnp.where(jnp.abs(m_new - r_old) > DA_REBASE, m_new, r_old)
            scale = jnp.exp2(r_old - r_new)
            l_old = l_ref[:, cs]
            acc_old = acc_ref[:, cs]
            l_ref[:, cs] = jnp.where(safe, (l_old + lt) * scale, l_old)
            acc_ref[:, cs] = jnp.where(safe, (acc_old + acct) * scale, acc_old)
            m_ref[:, cs] = jnp.where(safe, m_new, m_old)
            r_ref[:, cs] = jnp.where(safe, r_new, r_old)

        @pl.when(jnp.logical_not(jnp.max(worst) <= DA_LAZY_LIMIT))
        def _():
            exact_block(j, diag)

    def fast_body(j, carry):
        fast_block(j)
        return carry

    assert tq == tk
    lax.fori_loop(0, qi, fast_body, 0)
    fast_block(qi, diag=0)


    lp = lp_ref[...]
    lam = (jnp.exp(jnp.sum(lp[0:1, :] * lp[1:2, :], axis=1, keepdims=True))
           - jnp.exp(jnp.sum(lp[2:3, :] * lp[3:4, :], axis=1, keepdims=True)) + lambda_init)
    out = acc_ref[...] / l_ref[...]
    o = out[:, :tq] - lam * out[:, tq:]
    ms = jnp.mean(o * o, axis=0, keepdims=True)
    on = o * lax.rsqrt(ms + DA_EPS) * ng_ref[...] * (1.0 - lambda_init)
    o_ref[...] = on.T.astype(o_ref.dtype)


def _diff_attn(p, vt, lam_params, norm_g, lambda_init, *, batch, seq):
    tq = DA_TQ
    p3 = p.reshape(batch, seq, 2 * DA_QK + DA_H * DA_DV)
    kern = functools.partial(_diff_attn_kernel, lambda_init=lambda_init)
    o = pl.pallas_call(
        kern,
        out_shape=jax.ShapeDtypeStruct((batch, seq, DA_H * DA_DV), BF16),
        grid=(batch, DA_H, seq // tq),
        in_specs=[
            pl.BlockSpec((None, tq, 2 * DA_DH), lambda b, h, i: (b, i, h)),
            pl.BlockSpec((None, seq, 2 * DA_DH), lambda b, h, i: (b, 0, DA_H + h)),
            pl.BlockSpec((None, DA_DV, seq), lambda b, h, i: (b, h, 0)),
            pl.BlockSpec((4, DA_DH), lambda b, h, i: (0, 0)),
            pl.BlockSpec((DA_DV, 1), lambda b, h, i: (0, 0)),
        ],
        out_specs=pl.BlockSpec((None, tq, DA_DV), lambda b, h, i: (b, i, h)),
        scratch_shapes=[
            pltpu.VMEM((2 * tq, 2 * DA_DH), BF16),
            pltpu.VMEM((1, 2 * tq), F32),
            pltpu.VMEM((1, 2 * tq), F32),
            pltpu.VMEM((1, 2 * tq), F32),
            pltpu.VMEM((DA_DV, 2 * tq), F32),
        ],
        compiler_params=_cparams(("parallel", "parallel", "arbitrary")),
        name="diff_attn",
    )(p3, p3, vt, lam_params, norm_g)
    return o.reshape(batch * seq, DA_H * DA_DV)


def _pack_halves(y):
    halves = []
    for h in range(2):
        base = h * 2 * HALF_W
        lo = y[:, base:base + HALF_W].astype(BF16).astype(F32)
        hi = y[:, base + HALF_W:base + 2 * HALF_W].astype(BF16).astype(F32)
        lo_bits = lax.bitcast_convert_type(lo, U32) >> 16
        hi_bits = lax.bitcast_convert_type(hi, U32)
        halves.append(hi_bits | lo_bits)
    return halves


def _unpack_halves(w0, w1):
    parts = []
    for w in (w0, w1):
        parts.append(lax.bitcast_convert_type(w << 16, F32))
        parts.append(lax.bitcast_convert_type(w & jnp.uint32(0xFFFF0000), F32))
    return jnp.concatenate(parts, axis=1)


def _route_logits(x, g_ref, w_ref, b_ref):
    ms = jnp.mean(x * x, axis=-1, keepdims=True)
    hn32 = x * lax.rsqrt(ms + EPS) * g_ref[...]
    lg = jnp.dot(hn32.astype(BF16), w_ref[...], preferred_element_type=F32) + b_ref[...]
    return _pack_halves(hn32), lg


def _route_decide(lg, run_ref, us_ref):
    n = lg.shape[0]
    lgt = lg.T[0:ROUTE_ROWS, :]
    row = lax.broadcasted_iota(jnp.int32, lgt.shape, 0).astype(F32)
    neg = -jnp.inf

    gmask = (row >= N_EXPERTS) & (row < N_EXPERTS + N_GROUPS)
    gl = jnp.where(gmask, lgt, neg)
    gmax = jnp.max(gl, axis=0, keepdims=True)
    gidx = jnp.min(jnp.where(gl == gmax, row, float(LANES)), axis=0, keepdims=True) - N_EXPERTS
    gsum = jnp.sum(jnp.where(gmask, jnp.exp(gl - gmax), 0.0), axis=0, keepdims=True)
    g_w = 1.0 / gsum

    emask = (row >= gidx * EPG) & (row < gidx * EPG + EPG)
    el = jnp.where(emask, lgt, neg)
    emax = jnp.max(el, axis=0, keepdims=True)
    eexp = jnp.where(emask, jnp.exp(el - emax), 0.0)
    ep = eexp / jnp.sum(eexp, axis=0, keepdims=True)
    ep = jnp.where(emask, ep, -1.0)
    p1 = jnp.max(ep, axis=0, keepdims=True)
    i1 = jnp.min(jnp.where(ep == p1, row, float(LANES)), axis=0, keepdims=True)
    ep2 = jnp.where(row == i1, -1.0, ep)
    p2 = jnp.max(ep2, axis=0, keepdims=True)
    i2 = jnp.min(jnp.where(ep2 == p2, row, float(LANES)), axis=0, keepdims=True)
    wsum = p1 + p2
    w1 = g_w * (p1 / wsum)
    w2 = g_w * (p2 / wsum)

    a1 = row == i1
    a2 = row == i2
    onehot = a1 | a2
    before = jnp.dot(onehot.astype(BF16), us_ref[...], preferred_element_type=F32) + run_ref[...]
    rank1 = jnp.sum(jnp.where(a1, before, 0.0), axis=0, keepdims=True)
    rank2 = jnp.sum(jnp.where(a2, before, 0.0), axis=0, keepdims=True)
    run_ref[...] += jnp.sum(onehot.astype(F32), axis=1, keepdims=True)

    return jnp.concatenate([i1, i2, rank1, rank2, w1, w2,
                            jnp.zeros((ROUTE_ROWS - 6, n), F32)], axis=0)


def _res_router_kernel(a_ref, w_ref, r_ref, g_ref, wr_ref, br_ref, o_ref, hp_ref, meta_ref,
                       metat_ref, cnt_ref, run_ref, us_ref):
    sub = us_ref.shape[0]

    @pl.when(pl.program_id(0) == 0)
    def _():
        run_ref[...] = jnp.zeros_like(run_ref)
        r_i = lax.broadcasted_iota(jnp.int32, (sub, sub), 0)
        c_i = lax.broadcasted_iota(jnp.int32, (sub, sub), 1)
        us_ref[...] = (r_i < c_i).astype(BF16)

    subs = [slice(i * sub, (i + 1) * sub) for i in range(a_ref.shape[0] // sub)]

    def project(r):
        x = r_ref[r, :] + jnp.dot(a_ref[r, :], w_ref[...], preferred_element_type=F32)
        o_ref[r, :] = x
        halves, lg = _route_logits(x, g_ref, wr_ref, br_ref)
        hp_ref[0, r, :] = halves[0]
        hp_ref[1, r, :] = halves[1]
        return lg

    nxt = project(subs[0])
    for i, r in enumerate(subs):
        lg = nxt
        if i + 1 < len(subs):
            nxt = project(subs[i + 1])
        metat = _route_decide(lg, run_ref, us_ref)
        metat_ref[:, r] = metat[0:8, :]
        meta_ref[r, :] = jnp.concatenate(
            [metat, jnp.zeros((LANES - ROUTE_ROWS, sub), F32)], axis=0).T
    cnt_ref[...] = jnp.broadcast_to(run_ref[...], cnt_ref.shape)


def _res_router(a, w, res, g, w_group, b_group, w_expert, b_expert, *, tm, sub):
    t, k = a.shape
    d = w.shape[1]
    wr = jnp.zeros((d, LANES), BF16)
    wr = wr.at[:, :N_EXPERTS].set(w_expert.astype(BF16))
    wr = wr.at[:, N_EXPERTS:N_EXPERTS + N_GROUPS].set(w_group.astype(BF16))
    br = jnp.zeros((1, LANES), F32)
    br = br.at[0, :N_EXPERTS].set(b_expert.astype(F32))
    br = br.at[0, N_EXPERTS:N_EXPERTS + N_GROUPS].set(b_group.astype(F32))
    return pl.pallas_call(
        _res_router_kernel,
        out_shape=(jax.ShapeDtypeStruct((t, d), F32),
                   jax.ShapeDtypeStruct((2, t, HALF_W), U32),
                   jax.ShapeDtypeStruct((t, LANES), F32),
                   jax.ShapeDtypeStruct((8, t), F32),
                   jax.ShapeDtypeStruct((ROUTE_ROWS, LANES), F32)),
        grid=(t // tm,),
        in_specs=[
            pl.BlockSpec((tm, k), lambda i: (i, 0)),
            pl.BlockSpec((k, d), lambda i: (0, 0)),
            pl.BlockSpec((tm, d), lambda i: (i, 0)),
            pl.BlockSpec((1, d), lambda i: (0, 0)),
            pl.BlockSpec((d, LANES), lambda i: (0, 0)),
            pl.BlockSpec((1, LANES), lambda i: (0, 0)),
        ],
        out_specs=(pl.BlockSpec((tm, d), lambda i: (i, 0)),
                   pl.BlockSpec((2, tm, HALF_W), lambda i: (0, i, 0)),
                   pl.BlockSpec((tm, LANES), lambda i: (i, 0)),
                   pl.BlockSpec((8, tm), lambda i: (0, i)),
                   pl.BlockSpec((ROUTE_ROWS, LANES), lambda i: (0, 0))),
        scratch_shapes=[pltpu.VMEM((ROUTE_ROWS, 1), F32), pltpu.VMEM((sub, sub), BF16)],
        compiler_params=_cparams(("arbitrary",)),
        name="res_router",
    )(a, w, res, g, wr, br)


def _sc_mesh():
    return plsc.VectorSubcoreMesh(core_axis_name="c", subcore_axis_name="s")


def _sc_scatter2(x, i0, i1, n_out):
    n, d = x.shape

    @pl.kernel(out_type=jax.ShapeDtypeStruct((n_out, d), x.dtype), mesh=_sc_mesh())
    def k(x_hbm, i0_hbm, i1_hbm, o_hbm):
        def body(x_vmem, i0_vmem, i1_vmem):
            pltpu.sync_copy(x_vmem, o_hbm.at[i0_vmem.at[0]])
            pltpu.sync_copy(x_vmem, o_hbm.at[i1_vmem.at[0]])

        pltpu.emit_pipeline(
            body,
            grid=(n // SC_WINDOW,),
            in_specs=[pl.BlockSpec((SC_WINDOW, d), lambda i: (i, 0)),
                      pl.BlockSpec((1, SC_WINDOW), lambda i: (0, i)),
                      pl.BlockSpec((1, SC_WINDOW), lambda i: (0, i))],
            out_specs=[],
            core_axis_name=("c", "s"),
            dimension_semantics=(pltpu.PARALLEL,),
        )(x_hbm, i0_hbm, i1_hbm)

    return k(x, i0.reshape(1, n), i1.reshape(1, n))


def _sc_gather(x, idx):
    n = idx.shape[0]
    d = x.shape[1]

    @pl.kernel(out_type=jax.ShapeDtypeStruct((n, d), x.dtype), mesh=_sc_mesh())
    def k(x_hbm, i_hbm, o_hbm):
        def body(i_vmem, o_vmem):
            pltpu.sync_copy(x_hbm.at[i_vmem.at[0]], o_vmem)

        pltpu.emit_pipeline(
            body,
            grid=(n // SC_WINDOW,),
            in_specs=[pl.BlockSpec((1, SC_WINDOW), lambda i: (0, i))],
            out_specs=[pl.BlockSpec((SC_WINDOW, d), lambda i: (i, 0))],
            core_axis_name=("c", "s"),
            dimension_semantics=(pltpu.PARALLEL,),
        )(i_hbm, o_hbm)

    return k(x, idx.reshape(1, n))


def _experts_kernel(te_ref, nu_ref, xs_ref, wgu_ref, wd_ref, ys_ref, wgu_bf_ref, wd_bf_ref):
    j = pl.program_id(0)

    @pl.when(j < nu_ref[0])
    def _():
        @pl.when((j == 0) | (te_ref[j] != te_ref[jnp.maximum(j - 1, 0)]))
        def _():
            wgu_bf_ref[...] = wgu_ref[...].astype(BF16)
            wd_bf_ref[...] = wd_ref[...].astype(BF16)

        subs = [slice(i * MOE_SUB, (i + 1) * MOE_SUB) for i in range(MOE_TR // MOE_SUB)]
        xs = [_unpack_halves(xs_ref[0, r, :], xs_ref[1, r, :]).astype(BF16) for r in subs]
        gus = [jnp.dot(x, wgu_bf_ref[...], preferred_element_type=F32) for x in xs]
        acts = [(gu[:, :D_EXPERT] * jax.nn.sigmoid(gu[:, :D_EXPERT]) * gu[:, D_EXPERT:]).astype(BF16)
                for gu in gus]
        ys = [jnp.dot(a, wd_bf_ref[...], preferred_element_type=F32) for a in acts]
        for r, y in zip(subs, ys):
            halves = _pack_halves(y)
            ys_ref[0, r, :] = halves[0]
            ys_ref[1, r, :] = halves[1]


def _experts(xs, tile_expert, n_used, w_gu, w_down, layer):
    _, rows, _ = xs.shape
    d = w_gu.shape[2]
    return pl.pallas_call(
        _experts_kernel,
        out_shape=jax.ShapeDtypeStruct(xs.shape, U32),
        grid_spec=pltpu.PrefetchScalarGridSpec(
            num_scalar_prefetch=2,
            grid=(rows // MOE_TR,),
            in_specs=[
                pl.BlockSpec((2, MOE_TR, HALF_W), lambda j, te, nu: (0, j, 0)),
                pl.BlockSpec((None, None, d, 2 * D_EXPERT),
                             lambda j, te, nu: (layer, te[j], 0, 0)),
                pl.BlockSpec((None, None, D_EXPERT, d),
                             lambda j, te, nu: (layer, te[j], 0, 0)),
            ],
            out_specs=pl.BlockSpec((2, MOE_TR, HALF_W), lambda j, te, nu: (0, j, 0)),
            scratch_shapes=[pltpu.VMEM((d, 2 * D_EXPERT), BF16), pltpu.VMEM((D_EXPERT, d), BF16)],
        ),
        compiler_params=_cparams(("arbitrary",)),
        name="moe_experts",
    )(tile_expert, n_used, xs, w_gu, w_down)


def _moe_combined(x_ref, z_ref, meta_ref):
    meta = meta_ref[...]
    y_a = _unpack_halves(z_ref[0], z_ref[2])
    y_b = _unpack_halves(z_ref[1], z_ref[3])
    return x_ref[...] + meta[:, 4:5] * y_a + meta[:, 5:6] * y_b


def _combine_norm_kernel(x_ref, z_ref, meta_ref, g_ref, o_ref):
    out = _moe_combined(x_ref, z_ref, meta_ref)
    ms = jnp.mean(out * out, axis=-1, keepdims=True)
    o_ref[...] = out * lax.rsqrt(ms + EPS) * g_ref[...]


def _combine_norm(x, z, meta, g, *, tm):
    t, d = x.shape
    return pl.pallas_call(
        _combine_norm_kernel,
        out_shape=jax.ShapeDtypeStruct((t, d), F32),
        grid=(t // tm,),
        in_specs=[
            pl.BlockSpec((tm, d), lambda i: (i, 0)),
            pl.BlockSpec((4, tm, HALF_W), lambda i: (0, i, 0)),
            pl.BlockSpec((tm, LANES), lambda i: (i, 0)),
            pl.BlockSpec((1, d), lambda i: (0, 0)),
        ],
        out_specs=pl.BlockSpec((tm, d), lambda i: (i, 0)),
        compiler_params=_cparams(("parallel",)),
        name="moe_combine",
    )(x, z, meta, g)


def _moe_layer(routing, w_gu, w_down, layer):
    hp, meta, metat, cnt = routing
    t = meta.shape[0]
    n_tiles = 2 * t // MOE_TR + N_EXPERTS
    rows = n_tiles * MOE_TR
    counts = cnt[:N_EXPERTS, 0].astype(jnp.int32)
    tiles_e = (counts + MOE_TR - 1) // MOE_TR
    tiles_end = jnp.cumsum(tiles_e)
    row_off = (tiles_end - tiles_e) * MOE_TR
    experts = jnp.arange(N_EXPERTS, dtype=jnp.int32)[:, None]

    def region_start(e_row):
        return jnp.sum(jnp.where(e_row[None, :] == experts, row_off[:, None], 0), axis=0)

    e_a, e_b = metat[0].astype(jnp.int32), metat[1].astype(jnp.int32)
    pos_a = region_start(e_a) + metat[2].astype(jnp.int32)
    pos_b = region_start(e_b) + metat[3].astype(jnp.int32)
    tile_ids = jnp.arange(n_tiles, dtype=jnp.int32)
    tile_expert = jnp.minimum(
        jnp.sum((tile_ids[:, None] >= tiles_end[None, :]).astype(jnp.int32), axis=1),
        N_EXPERTS - 1)
    n_used = tiles_end[-1:].astype(jnp.int32)

    xs = _sc_scatter2(hp.reshape(2 * t, HALF_W),
                      jnp.concatenate([pos_a, pos_a + rows]),
                      jnp.concatenate([pos_b, pos_b + rows]), 2 * rows)
    ys = _experts(xs.reshape(2, rows, HALF_W), tile_expert, n_used,
                  w_gu, w_down, layer)
    z = _sc_gather(ys.reshape(2 * rows, HALF_W),
                   jnp.concatenate([pos_a, pos_b, pos_a + rows, pos_b + rows]))
    return z.reshape(4, t, HALF_W)


def kernel(x, norm_mix, norm_ffn, ml_w_in, ml_conv, ml_b_i, ml_b_f, ml_norm, ml_w_out, da_w_in, da_lq1, da_lk1, da_lq2, da_lk2, da_norm, da_w_out, moe_w_group, moe_b_group, moe_w_expert, moe_b_expert, moe_w_gu, moe_w_down, final_norm):
    batch, seq, d = x.shape
    xt = x.reshape(batch * seq, d)

    p, gcol, grow = _ml_proj(xt, norm_mix[0].reshape(1, d), ml_w_in[0], ml_b_i[0], ml_b_f[0],
                             tm=1024, tn=3072, out_dtype=BF16)
    assert p.shape[1] == 2 * ML_QK + 2 * ML_V
    y = _mlstm_core(p, gcol, grow, ml_conv[0], ml_norm[0].reshape(1, ML_V), batch=batch, seq=seq)
    xt, *routing = _res_router(y, ml_w_out[0].astype(BF16), xt, norm_ffn[0].reshape(1, d),
                               moe_w_group[0], moe_b_group[0], moe_w_expert[0], moe_b_expert[0],
                               tm=RR_TM, sub=RR_SUB)
    z = _moe_layer(routing, moe_w_gu, moe_w_down, 0)

    lambda_init = 0.8 - 0.6 * math.exp(-0.3 * 1)
    xt, p, vt = _norm_matmul_vt(xt, z, routing[1], norm_mix[1].reshape(1, d),
                                da_w_in[0].astype(BF16), tm=512, sub=256, vt_start=2 * DA_QK,
                                batch=batch, seq=seq, out_dtype=BF16)
    lam_params = jnp.stack([da_lq1[0], da_lk1[0], da_lq2[0], da_lk2[0]]).astype(F32)
    a = _diff_attn(p, vt, lam_params, da_norm[0].reshape(DA_DV, 1), lambda_init, batch=batch,
                   seq=seq)
    xt, *routing = _res_router(a, da_w_out[0].astype(BF16), xt, norm_ffn[1].reshape(1, d),
                               moe_w_group[1], moe_b_group[1], moe_w_expert[1], moe_b_expert[1],
                               tm=RR_TM, sub=RR_SUB)
    z = _moe_layer(routing, moe_w_gu, moe_w_down, 1)
    out = _combine_norm(xt, z, routing[1], final_norm.reshape(1, d), tm=1024)
    return out.reshape(batch, seq, d)
```

```python
import functools
import math

import jax
import jax.numpy as jnp
from jax import lax
from jax.experimental import pallas as pl
from jax.experimental.pallas import tpu as pltpu
from jax.experimental.pallas import tpu_sc as plsc

F32 = jnp.float32
BF16 = jnp.bfloat16
U32 = jnp.uint32

D_MODEL = 1024
EPS = 1e-6
ML_H = 4
ML_DV = 512
ML_DQK = 256
ML_QK = ML_H * ML_DQK
ML_V = ML_H * ML_DV
CONV_K = 4
ML_CHUNK = 256
ML_STEP_CHUNKS = 2
CONV_TAIL = 8
ML_PROJ_SUB = 256
DA_H = 8
DA_DH = 64
DA_DV = 128
DA_QK = DA_H * 2 * DA_DH
DA_EPS = 1e-5
DA_TQ = 2048
DA_TK = 2048
DA_CW = 512
DA_KEY_ALIGN = 256
DA_LAZY_LIMIT = 64.0
DA_REBASE = 8.0
LOG2E = 1.4426950408889634
N_GROUPS = 4
EPG = 8
N_EXPERTS = 32
D_EXPERT = 256
RR_TM = 1024
RR_SUB = 512
ROUTE_ROWS = 40
MOE_TR = 1024
MOE_SUB = 256
HALF_W = D_MODEL // 4
SC_WINDOW = 128
LANES = 128

VMEM_LIMIT = 48 * 1024 * 1024


def _cparams(sem):
    return pltpu.CompilerParams(dimension_semantics=sem, vmem_limit_bytes=VMEM_LIMIT)


def _norm_matmul_vt_kernel(x_ref, z_ref, meta_ref, g_ref, w_ref, x2_ref, o_ref, vt_ref,
                           *, vt_start, sub):
    for i in range(x_ref.shape[0] // sub):
        r = slice(i * sub, (i + 1) * sub)
        x = _moe_combined(x_ref.at[r, :], z_ref.at[:, r, :], meta_ref.at[r, :])
        x2_ref[r, :] = x
        ms = jnp.mean(x * x, axis=-1, keepdims=True)
        xn = (x * lax.rsqrt(ms + EPS) * g_ref[...]).astype(BF16)
        res = jnp.dot(xn, w_ref[...], preferred_element_type=F32)
        o_ref[r, :] = res.astype(o_ref.dtype)
        vt_ref[:, r] = res[:, vt_start:].T.astype(vt_ref.dtype)


def _norm_matmul_vt(x, z, meta, g, w, *, tm, sub, vt_start, batch, seq, out_dtype):
    t, d = x.shape
    n = w.shape[1]
    per_batch = seq // tm
    return pl.pallas_call(
        functools.partial(_norm_matmul_vt_kernel, vt_start=vt_start, sub=sub),
        out_shape=(jax.ShapeDtypeStruct((t, d), F32),
                   jax.ShapeDtypeStruct((t, n), out_dtype),
                   jax.ShapeDtypeStruct((batch, n - vt_start, seq), out_dtype)),
        grid=(t // tm,),
        in_specs=[
            pl.BlockSpec((tm, d), lambda i: (i, 0)),
            pl.BlockSpec((4, tm, HALF_W), lambda i: (0, i, 0)),
            pl.BlockSpec((tm, LANES), lambda i: (i, 0)),
            pl.BlockSpec((1, d), lambda i: (0, 0)),
            pl.BlockSpec((d, n), lambda i: (0, 0)),
        ],
        out_specs=(pl.BlockSpec((tm, d), lambda i: (i, 0)),
                   pl.BlockSpec((tm, n), lambda i: (i, 0)),
                   pl.BlockSpec((None, n - vt_start, tm),
                                lambda i: (i // per_batch, 0, i % per_batch))),
        compiler_params=_cparams(("parallel",)),
        name="norm_matmul_vt",
    )(x, z, meta, g, w)


def _log_sigmoid(x):
    return jnp.minimum(x, 0.0) - jnp.log1p(jnp.exp(-jnp.abs(x)))


def _ml_proj_kernel(x_ref, g_ref, w_ref, wc_ref, wr_ref, bc_ref, br_ref, o_ref, oc_ref, or_ref,
                    xn_ref):
    j = pl.program_id(1)

    @pl.when(j == 0)
    def _():
        for i in range(x_ref.shape[0] // ML_PROJ_SUB):
            r = slice(i * ML_PROJ_SUB, (i + 1) * ML_PROJ_SUB)
            x = x_ref[r, :]
            ms = jnp.mean(x * x, axis=-1, keepdims=True)
            xn = (x * lax.rsqrt(ms + EPS) * g_ref[...]).astype(BF16)
            xn_ref[r, :] = xn
            o_ref[r, :] = jnp.dot(xn, w_ref[...], preferred_element_type=F32).astype(o_ref.dtype)
            gc = jnp.dot(xn, wc_ref[...], preferred_element_type=F32) + bc_ref[...]
            lane = lax.broadcasted_iota(jnp.int32, gc.shape, 1)
            oc_ref[r, :] = jnp.where(lane < ML_H, gc, _log_sigmoid(gc))
            gr = lax.dot_general(wr_ref[...], xn, (((1,), (1,)), ((), ())),
                                 preferred_element_type=F32) + br_ref[...]
            row = lax.broadcasted_iota(jnp.int32, gr.shape, 0)
            or_ref[:, r] = jnp.where(row < ML_H, gr, _log_sigmoid(gr))

    @pl.when(j > 0)
    def _():
        o_ref[...] = jnp.dot(xn_ref[...], w_ref[...],
                             preferred_element_type=F32).astype(o_ref.dtype)


def _ml_proj(x, g, w_in, b_i, b_f, *, tm, tn, out_dtype):
    t, d = x.shape
    ng = 2 * ML_H
    n_main = w_in.shape[1] - ng
    w_gates = w_in[:, n_main:]
    wc = jnp.zeros((d, LANES), BF16).at[:, :ng].set(w_gates.astype(BF16))
    wr = w_gates.T.astype(BF16)
    bias = jnp.concatenate([b_i, b_f]).astype(F32)
    bc = jnp.zeros((1, LANES), F32).at[0, :ng].set(bias)
    br = bias.reshape(ng, 1)
    return pl.pallas_call(
        _ml_proj_kernel,
        out_shape=(jax.ShapeDtypeStruct((t, n_main), out_dtype),
                   jax.ShapeDtypeStruct((t, LANES), F32), jax.ShapeDtypeStruct((ng, t), F32)),
        grid=(t // tm, n_main // tn),
        in_specs=[
            pl.BlockSpec((tm, d), lambda i, j: (i, 0)),
            pl.BlockSpec((1, d), lambda i, j: (0, 0)),
            pl.BlockSpec((d, tn), lambda i, j: (0, j)),
            pl.BlockSpec((d, LANES), lambda i, j: (0, 0)),
            pl.BlockSpec((ng, d), lambda i, j: (0, 0)),
            pl.BlockSpec((1, LANES), lambda i, j: (0, 0)),
            pl.BlockSpec((ng, 1), lambda i, j: (0, 0)),
        ],
        out_specs=(pl.BlockSpec((tm, tn), lambda i, j: (i, j)),
                   pl.BlockSpec((tm, LANES), lambda i, j: (i, 0)),
                   pl.BlockSpec((ng, tm), lambda i, j: (0, i))),
        scratch_shapes=[pltpu.VMEM((tm, d), BF16)],
        compiler_params=_cparams(("parallel", "arbitrary")),
        name="ml_proj",
    )(x, g, w_in.astype(BF16), wc, wr, bc, br)


def _split3(x):
    hi = x.astype(BF16)
    r = x - hi.astype(F32)
    mid = r.astype(BF16)
    lo = (r - mid.astype(F32)).astype(BF16)
    return hi, mid, lo


def _mlstm_kernel(*refs):
    for i in range(ML_STEP_CHUNKS):
        _mlstm_chunk(i, *refs)


def _mlstm_chunk(i, q_ref, k_ref, v_ref, o_ref, gc_ref, gr_ref, cw_ref, ng_ref, y_ref,
                 tail_ref, ct_ref, n_ref, m_ref, shift_ref, tril_ref, triu_ref, neg_ref):
    L = ML_CHUNK
    rows = slice(i * L, (i + 1) * L)

    def reset():
        tail_ref[...] = jnp.zeros_like(tail_ref)
        ct_ref[...] = jnp.zeros_like(ct_ref)
        n_ref[...] = jnp.zeros_like(n_ref)
        m_ref[...] = jnp.zeros_like(m_ref)
        r_i = lax.broadcasted_iota(jnp.int32, (L, L), 0)
        c_i = lax.broadcasted_iota(jnp.int32, (L, L), 1)
        tril_ref[...] = (c_i <= r_i).astype(BF16)
        triu_ref[...] = (r_i <= c_i).astype(BF16)
        neg_ref[...] = jnp.where(c_i <= r_i, 0.0, -jnp.inf)
        for j in range(CONV_K - 1):
            shift_ref[j * L:(j + 1) * L, :] = (c_i == r_i - (CONV_K - 1 - j)).astype(BF16)

    if i == 0:
        pl.when(pl.program_id(1) == 0)(reset)

    u = jnp.concatenate([q_ref[rows, :], k_ref[rows, :]], axis=1)
    uf = u.astype(F32)
    shifted = jnp.dot(shift_ref[...], u, preferred_element_type=F32)
    conv = uf * cw_ref[CONV_K - 1:CONV_K, :]
    head = None
    for j in range(CONV_K - 1):
        conv = conv + shifted[j * L:(j + 1) * L, :] * cw_ref[j:j + 1, :]
        part = tail_ref[pl.ds(CONV_TAIL - (CONV_K - 1) + j, CONV_TAIL), :] * cw_ref[j:j + 1, :]
        head = part if head is None else head + part
    conv = jnp.concatenate([conv[:CONV_TAIL] + head, conv[CONV_TAIL:]], axis=0)
    tail_ref[0:CONV_TAIL, :] = uf[L - CONV_TAIL:, :]
    qk = conv * jax.nn.sigmoid(conv)
    q_all = (qk[:, :ML_QK] * (ML_DQK ** -0.5)).astype(BF16)
    k_all = qk[:, ML_QK:]

    tril = tril_ref[...]
    triu = triu_ref[...]
    gc = gc_ref[rows, :]
    gr = gr_ref[:, rows]
    bc_all = sum(jnp.dot(tril, p, preferred_element_type=F32) for p in _split3(gc))
    br_all = sum(jnp.dot(p, triu, preferred_element_type=F32) for p in _split3(gr))

    for h in range(ML_H):
        qh = q_all[:, h * ML_DQK:(h + 1) * ML_DQK]
        kh_f = k_all[:, h * ML_DQK:(h + 1) * ML_DQK]
        kh = kh_f.astype(BF16)
        vh = v_ref[rows, h * ML_DV:(h + 1) * ML_DV]
        it_col = gc[:, h:h + 1]
        it_row = gr[h:h + 1, :]
        b_col = bc_all[:, ML_H + h:ML_H + h + 1]
        b_row = br_all[ML_H + h:ML_H + h + 1, :]
        m_prev = m_ref[h][:, 0:1]

        dmat = b_col + (it_row - b_row) + neg_ref[...]
        inter_log = b_col + m_prev
        m_t = jnp.maximum(inter_log, jnp.max(dmat, axis=1, keepdims=True))
        wts = jnp.exp(dmat - m_t)
        s = lax.dot_general(qh, kh, (((1,), (1,)), ((), ())), preferred_element_type=F32)
        sc = s * wts
        inter_scale = jnp.exp(inter_log - m_t)
        ct = ct_ref[h]
        num = (jnp.dot(sc.astype(BF16), vh, preferred_element_type=F32)
               + inter_scale * jnp.dot(qh, ct.astype(BF16), preferred_element_type=F32))
        n_row = n_ref[h]
        den = (jnp.sum(sc, axis=1, keepdims=True)
               + inter_scale * jnp.sum(qh.astype(F32) * n_row, axis=1, keepdims=True))
        h_out = num / jnp.maximum(jnp.abs(den), jnp.exp(-m_t))

        b_last = b_col[L - 1:L, :]
        lw_col = b_last - b_col + it_col
        lw_row = b_last - b_row + it_row
        m_new = jnp.maximum(b_last + m_prev, jnp.max(lw_row, axis=1, keepdims=True))
        ws_col = jnp.exp(lw_col - m_new)
        decay = jnp.exp(b_last + m_prev - m_new)
        kw = kh_f * ws_col
        ct_ref[h] = decay * ct + lax.dot_general(kw.astype(BF16), vh, (((0,), (0,)), ((), ())),
                                                  preferred_element_type=F32)
        n_ref[h] = decay * n_row + jnp.sum(kw, axis=0, keepdims=True)
        m_ref[h] = jnp.broadcast_to(m_new, (1, LANES))

        ms = jnp.mean(h_out * h_out, axis=1, keepdims=True)
        hn = h_out * lax.rsqrt(ms + EPS) * ng_ref[:, h * ML_DV:(h + 1) * ML_DV]
        og = o_ref[rows, h * ML_DV:(h + 1) * ML_DV].astype(F32)
        y_ref[rows, h * ML_DV:(h + 1) * ML_DV] = (hn * jax.nn.sigmoid(og)).astype(y_ref.dtype)


def _mlstm_core(p, gcol, grow, conv_w, norm_g, *, batch, seq):
    L = ML_CHUNK
    LB = ML_STEP_CHUNKS * L
    p3 = p.reshape(batch, seq, 2 * ML_QK + 2 * ML_V)
    gc3 = gcol.reshape(batch, seq, LANES)
    gr3 = grow.reshape(2 * ML_H, batch, seq).transpose(1, 0, 2)
    y = pl.pallas_call(
        _mlstm_kernel,
        out_shape=jax.ShapeDtypeStruct((batch, seq, ML_V), BF16),
        grid=(batch, seq // LB),
        in_specs=[
            pl.BlockSpec((None, LB, ML_QK), lambda b, c: (b, c, 0)),
            pl.BlockSpec((None, LB, ML_QK), lambda b, c: (b, c, 1)),
            pl.BlockSpec((None, LB, ML_V), lambda b, c: (b, c, 1)),
            pl.BlockSpec((None, LB, ML_V), lambda b, c: (b, c, 2)),
            pl.BlockSpec((None, LB, LANES), lambda b, c: (b, c, 0)),
            pl.BlockSpec((None, 2 * ML_H, LB), lambda b, c: (b, 0, c)),
            pl.BlockSpec((CONV_K, 2 * ML_QK), lambda b, c: (0, 0)),
            pl.BlockSpec((1, ML_V), lambda b, c: (0, 0)),
        ],
        out_specs=pl.BlockSpec((None, LB, ML_V), lambda b, c: (b, c, 0)),
        scratch_shapes=[
            pltpu.VMEM((2 * CONV_TAIL, 2 * ML_QK), F32),
            pltpu.VMEM((ML_H, ML_DQK, ML_DV), F32),
            pltpu.VMEM((ML_H, 1, ML_DQK), F32),
            pltpu.VMEM((ML_H, 1, LANES), F32),
            pltpu.VMEM(((CONV_K - 1) * L, L), BF16),
            pltpu.VMEM((L, L), BF16),
            pltpu.VMEM((L, L), BF16),
            pltpu.VMEM((L, L), F32),
        ],
        compiler_params=_cparams(("parallel", "arbitrary")),
        name="mlstm_core",
    )(p3, p3, p3, p3, gc3, gr3, conv_w, norm_g)
    return y.reshape(batch * seq, ML_V)


def _diff_attn_kernel(q_ref, k_ref, vt_ref, lp_ref, ng_ref, o_ref, q2_ref, r_ref, m_ref, l_ref,
                      acc_ref, *, lambda_init):
    tq, tk, cw = DA_TQ, DA_TK, DA_CW
    qi = pl.program_id(2)
    q = q_ref[...].astype(F32) * (DA_DH ** -0.5 * LOG2E)
    lane = lax.broadcasted_iota(jnp.int32, q.shape, 1)
    q2_ref[0:tq, :] = jnp.where(lane < DA_DH, q, 0.0).astype(BF16)
    q2_ref[tq:2 * tq, :] = jnp.where(lane >= DA_DH, q, 0.0).astype(BF16)

    r_ref[...] = jnp.zeros_like(r_ref)
    m_ref[...] = jnp.full(m_ref.shape, -jnp.inf, F32)
    l_ref[...] = jnp.zeros_like(l_ref)
    acc_ref[...] = jnp.zeros_like(acc_ref)

    nch = 2 * tq // cw

    def keys_needed(c, diag):
        if diag is None:
            return tk
        visible = (c * cw) % tq + cw - diag * tk
        return max(0, min(tk, -(-visible // DA_KEY_ALIGN) * DA_KEY_ALIGN))

    def mask(s, c, diag):
        nk = s.shape[0]
        key = lax.broadcasted_iota(jnp.int32, (nk, cw), 0) + diag * tk
        qry = lax.broadcasted_iota(jnp.int32, (nk, cw), 1) + (c * cw) % tq
        return jnp.where(key <= qry, s, -jnp.inf)

    def scores(j, c, diag=None):
        nk = keys_needed(c, diag)
        if nk == 0:
            return None
        kb = k_ref[pl.ds(pl.multiple_of(j * tk, tk), nk), :]
        s = lax.dot_general(kb, q2_ref[c * cw:(c + 1) * cw, :], (((1,), (1,)), ((), ())),
                            preferred_element_type=F32)
        if diag is not None:
            s = mask(s, c, diag)
        return s, jnp.max(s, axis=0, keepdims=True)


    def exact_block(j, diag=None):
        start = pl.multiple_of(j * tk, tk)
        nxt = scores(j, 0, diag)
        for c in range(nch):
            cs = slice(c * cw, (c + 1) * cw)
            s, bmax = nxt
            if c + 1 < nch:
                nxt = scores(j, c + 1, diag)
            vbt = vt_ref[:, pl.ds(start, s.shape[0])]
            m_old = m_ref[:, cs]
            m_new = jnp.maximum(m_old, bmax)
            alpha = jnp.where(m_old == -jnp.inf, 0.0, jnp.exp2(r_ref[:, cs] - m_new))
            p = jnp.exp2(s - m_new)
            l_ref[:, cs] = alpha * l_ref[:, cs] + jnp.sum(p, axis=0, keepdims=True)
            acc_ref[:, cs] = alpha * acc_ref[:, cs] + jnp.dot(vbt, p.astype(BF16),
                                                               preferred_element_type=F32)
            m_ref[:, cs] = m_new
            r_ref[:, cs] = m_new

    def fast_block(j, diag=None):
        start = pl.multiple_of(j * tk, tk)
        nxt = scores(j, 0, diag)
        tent = []
        for c in range(nch):
            cs = slice(c * cw, (c + 1) * cw)
            s, bmax = nxt
            if c + 1 < nch:
                nxt = scores(j, c + 1, diag)
            vbt = vt_ref[:, pl.ds(start, s.shape[0])]
            r_old = r_ref[:, cs]
            p = jnp.exp2(s - r_old)
            tent.append((cs, bmax, r_old, m_ref[:, cs], jnp.sum(p, axis=0, keepdims=True),
                         jnp.dot(vbt, p.astype(BF16), preferred_element_type=F32)))
        worst = functools.reduce(jnp.maximum, [
            jnp.max(jnp.maximum(bmax - r_old,
                                jnp.where(m_old == -jnp.inf, r_old - bmax, -jnp.inf)),
                    axis=1, keepdims=True)
            for _, bmax, r_old, m_old, _, _ in tent])
        safe = worst <= DA_LAZY_LIMIT
        for cs, bmax, r_old, m_old, lt, acct in tent:
            m_new = jnp.maximum(m_old, bmax)
            r_new = jnp.where(jnp.abs(m_new - r_old) > DA_REBASE, m_new, r_old)
            scale = jnp.exp2(r_old - r_new)
            l_old = l_ref[:, cs]
            acc_old = acc_ref[:, cs]
            l_ref[:, cs] = jnp.where(safe, (l_old + lt) * scale, l_old)
            acc_ref[:, cs] = jnp.where(safe, (acc_old + acct) * scale, acc_old)
            m_ref[:, cs] = jnp.where(safe, m_new, m_old)
            r_ref[:, cs] = jnp.where(safe, r_new, r_old)

        @pl.when(jnp.logical_not(jnp.max(worst) <= DA_LAZY_LIMIT))
        def _():
            exact_block(j, diag)

    def fast_body(j, carry):
        fast_block(j)
        return carry

    assert tq == tk
    lax.fori_loop(0, qi, fast_body, 0)
    fast_block(qi, diag=0)


    lp = lp_ref[...]
    lam = (jnp.exp(jnp.sum(lp[0:1, :] * lp[1:2, :], axis=1, keepdims=True))
           - jnp.exp(jnp.sum(lp[2:3, :] * lp[3:4, :], axis=1, keepdims=True)) + lambda_init)
    out = acc_ref[...] / l_ref[...]
    o = out[:, :tq] - lam * out[:, tq:]
    ms = jnp.mean(o * o, axis=0, keepdims=True)
    on = o * lax.rsqrt(ms + DA_EPS) * ng_ref[...] * (1.0 - lambda_init)
    o_ref[...] = on.T.astype(o_ref.dtype)


def _diff_attn(p, vt, lam_params, norm_g, lambda_init, *, batch, seq):
    tq = DA_TQ
    p3 = p.reshape(batch, seq, 2 * DA_QK + DA_H * DA_DV)
    kern = functools.partial(_diff_attn_kernel, lambda_init=lambda_init)
    o = pl.pallas_call(
        kern,
        out_shape=jax.ShapeDtypeStruct((batch, seq, DA_H * DA_DV), BF16),
        grid=(batch, DA_H, seq // tq),
        in_specs=[
            pl.BlockSpec((None, tq, 2 * DA_DH), lambda b, h, i: (b, i, h)),
            pl.BlockSpec((None, seq, 2 * DA_DH), lambda b, h, i: (b, 0, DA_H + h)),
            pl.BlockSpec((None, DA_DV, seq), lambda b, h, i: (b, h, 0)),
            pl.BlockSpec((4, DA_DH), lambda b, h, i: (0, 0)),
            pl.BlockSpec((DA_DV, 1), lambda b, h, i: (0, 0)),
        ],
        out_specs=pl.BlockSpec((None, tq, DA_DV), lambda b, h, i: (b, i, h)),
        scratch_shapes=[
            pltpu.VMEM((2 * tq, 2 * DA_DH), BF16),
            pltpu.VMEM((1, 2 * tq), F32),
            pltpu.VMEM((1, 2 * tq), F32),
            pltpu.VMEM((1, 2 * tq), F32),
            pltpu.VMEM((DA_DV, 2 * tq), F32),
        ],
        compiler_params=_cparams(("parallel", "parallel", "arbitrary")),
        name="diff_attn",
    )(p3, p3, vt, lam_params, norm_g)
    return o.reshape(batch * seq, DA_H * DA_DV)


def _pack_halves(y):
    halves = []
    for h in range(2):
        base = h * 2 * HALF_W
        lo = y[:, base:base + HALF_W].astype(BF16).astype(F32)
        hi = y[:, base + HALF_W:base + 2 * HALF_W].astype(BF16).astype(F32)
        lo_bits = lax.bitcast_convert_type(lo, U32) >> 16
        hi_bits = lax.bitcast_convert_type(hi, U32)
        halves.append(hi_bits | lo_bits)
    return halves


def _unpack_halves(w0, w1):
    parts = []
    for w in (w0, w1):
        parts.append(lax.bitcast_convert_type(w << 16, F32))
        parts.append(lax.bitcast_convert_type(w & jnp.uint32(0xFFFF0000), F32))
    return jnp.concatenate(parts, axis=1)


def _route_logits(x, g_ref, w_ref, b_ref):
    ms = jnp.mean(x * x, axis=-1, keepdims=True)
    hn32 = x * lax.rsqrt(ms + EPS) * g_ref[...]
    lg = jnp.dot(hn32.astype(BF16), w_ref[...], preferred_element_type=F32) + b_ref[...]
    return _pack_halves(hn32), lg


def _route_decide(lg, run_ref, us_ref):
    n = lg.shape[0]
    lgt = lg.T[0:ROUTE_ROWS, :]
    row = lax.broadcasted_iota(jnp.int32, lgt.shape, 0).astype(F32)
    neg = -jnp.inf

    gmask = (row >= N_EXPERTS) & (row < N_EXPERTS + N_GROUPS)
    gl = jnp.where(gmask, lgt, neg)
    gmax = jnp.max(gl, axis=0, keepdims=True)
    gidx = jnp.min(jnp.where(gl == gmax, row, float(LANES)), axis=0, keepdims=True) - N_EXPERTS
    gsum = jnp.sum(jnp.where(gmask, jnp.exp(gl - gmax), 0.0), axis=0, keepdims=True)
    g_w = 1.0 / gsum

    emask = (row >= gidx * EPG) & (row < gidx * EPG + EPG)
    el = jnp.where(emask, lgt, neg)
    emax = jnp.max(el, axis=0, keepdims=True)
    eexp = jnp.where(emask, jnp.exp(el - emax), 0.0)
    ep = eexp / jnp.sum(eexp, axis=0, keepdims=True)
    ep = jnp.where(emask, ep, -1.0)
    p1 = jnp.max(ep, axis=0, keepdims=True)
    i1 = jnp.min(jnp.where(ep == p1, row, float(LANES)), axis=0, keepdims=True)
    ep2 = jnp.where(row == i1, -1.0, ep)
    p2 = jnp.max(ep2, axis=0, keepdims=True)
    i2 = jnp.min(jnp.where(ep2 == p2, row, float(LANES)), axis=0, keepdims=True)
    wsum = p1 + p2
    w1 = g_w * (p1 / wsum)
    w2 = g_w * (p2 / wsum)

    a1 = row == i1
    a2 = row == i2
    onehot = a1 | a2
    before = jnp.dot(onehot.astype(BF16), us_ref[...], preferred_element_type=F32) + run_ref[...]
    rank1 = jnp.sum(jnp.where(a1, before, 0.0), axis=0, keepdims=True)
    rank2 = jnp.sum(jnp.where(a2, before, 0.0), axis=0, keepdims=True)
    run_ref[...] += jnp.sum(onehot.astype(F32), axis=1, keepdims=True)

    return jnp.concatenate([i1, i2, rank1, rank2, w1, w2,
                            jnp.zeros((ROUTE_ROWS - 6, n), F32)], axis=0)


def _res_router_kernel(a_ref, w_ref, r_ref, g_ref, wr_ref, br_ref, o_ref, hp_ref, meta_ref,
                       metat_ref, cnt_ref, run_ref, us_ref):
    sub = us_ref.shape[0]

    @pl.when(pl.program_id(0) == 0)
    def _():
        run_ref[...] = jnp.zeros_like(run_ref)
        r_i = lax.broadcasted_iota(jnp.int32, (sub, sub), 0)
        c_i = lax.broadcasted_iota(jnp.int32, (sub, sub), 1)
        us_ref[...] = (r_i < c_i).astype(BF16)

    subs = [slice(i * sub, (i + 1) * sub) for i in range(a_ref.shape[0] // sub)]

    def project(r):
        x = r_ref[r, :] + jnp.dot(a_ref[r, :], w_ref[...], preferred_element_type=F32)
        o_ref[r, :] = x
        halves, lg = _route_logits(x, g_ref, wr_ref, br_ref)
        hp_ref[0, r, :] = halves[0]
        hp_ref[1, r, :] = halves[1]
        return lg

    nxt = project(subs[0])
    for i, r in enumerate(subs):
        lg = nxt
        if i + 1 < len(subs):
            nxt = project(subs[i + 1])
        metat = _route_decide(lg, run_ref, us_ref)
        metat_ref[:, r] = metat[0:8, :]
        meta_ref[r, :] = jnp.concatenate(
            [metat, jnp.zeros((LANES - ROUTE_ROWS, sub), F32)], axis=0).T
    cnt_ref[...] = jnp.broadcast_to(run_ref[...], cnt_ref.shape)


def _res_router(a, w, res, g, w_group, b_group, w_expert, b_expert, *, tm, sub):
    t, k = a.shape
    d = w.shape[1]
    wr = jnp.zeros((d, LANES), BF16)
    wr = wr.at[:, :N_EXPERTS].set(w_expert.astype(BF16))
    wr = wr.at[:, N_EXPERTS:N_EXPERTS + N_GROUPS].set(w_group.astype(BF16))
    br = jnp.zeros((1, LANES), F32)
    br = br.at[0, :N_EXPERTS].set(b_expert.astype(F32))
    br = br.at[0, N_EXPERTS:N_EXPERTS + N_GROUPS].set(b_group.astype(F32))
    return pl.pallas_call(
        _res_router_kernel,
        out_shape=(jax.ShapeDtypeStruct((t, d), F32),
                   jax.ShapeDtypeStruct((2, t, HALF_W), U32),
                   jax.ShapeDtypeStruct((t, LANES), F32),
                   jax.ShapeDtypeStruct((8, t), F32),
                   jax.ShapeDtypeStruct((ROUTE_ROWS, LANES), F32)),
        grid=(t // tm,),
        in_specs=[
            pl.BlockSpec((tm, k), lambda i: (i, 0)),
            pl.BlockSpec((k, d), lambda i: (0, 0)),
            pl.BlockSpec((tm, d), lambda i: (i, 0)),
            pl.BlockSpec((1, d), lambda i: (0, 0)),
            pl.BlockSpec((d, LANES), lambda i: (0, 0)),
            pl.BlockSpec((1, LANES), lambda i: (0, 0)),
        ],
        out_specs=(pl.BlockSpec((tm, d), lambda i: (i, 0)),
                   pl.BlockSpec((2, tm, HALF_W), lambda i: (0, i, 0)),
                   pl.BlockSpec((tm, LANES), lambda i: (i, 0)),
                   pl.BlockSpec((8, tm), lambda i: (0, i)),
                   pl.BlockSpec((ROUTE_ROWS, LANES), lambda i: (0, 0))),
        scratch_shapes=[pltpu.VMEM((ROUTE_ROWS, 1), F32), pltpu.VMEM((sub, sub), BF16)],
        compiler_params=_cparams(("arbitrary",)),
        name="res_router",
    )(a, w, res, g, wr, br)


def _sc_mesh():
    return plsc.VectorSubcoreMesh(core_axis_name="c", subcore_axis_name="s")


def _sc_scatter2(x, i0, i1, n_out):
    n, d = x.shape

    @pl.kernel(out_type=jax.ShapeDtypeStruct((n_out, d), x.dtype), mesh=_sc_mesh())
    def k(x_hbm, i0_hbm, i1_hbm, o_hbm):
        def body(x_vmem, i0_vmem, i1_vmem):
            pltpu.sync_copy(x_vmem, o_hbm.at[i0_vmem.at[0]])
            pltpu.sync_copy(x_vmem, o_hbm.at[i1_vmem.at[0]])

        pltpu.emit_pipeline(
            body,
            grid=(n // SC_WINDOW,),
            in_specs=[pl.BlockSpec((SC_WINDOW, d), lambda i: (i, 0)),
                      pl.BlockSpec((1, SC_WINDOW), lambda i: (0, i)),
                      pl.BlockSpec((1, SC_WINDOW), lambda i: (0, i))],
            out_specs=[],
            core_axis_name=("c", "s"),
            dimension_semantics=(pltpu.PARALLEL,),
        )(x_hbm, i0_hbm, i1_hbm)

    return k(x, i0.reshape(1, n), i1.reshape(1, n))


def _sc_gather(x, idx):
    n = idx.shape[0]
    d = x.shape[1]

    @pl.kernel(out_type=jax.ShapeDtypeStruct((n, d), x.dtype), mesh=_sc_mesh())
    def k(x_hbm, i_hbm, o_hbm):
        def body(i_vmem, o_vmem):
            pltpu.sync_copy(x_hbm.at[i_vmem.at[0]], o_vmem)

        pltpu.emit_pipeline(
            body,
            grid=(n // SC_WINDOW,),
            in_specs=[pl.BlockSpec((1, SC_WINDOW), lambda i: (0, i))],
            out_specs=[pl.BlockSpec((SC_WINDOW, d), lambda i: (i, 0))],
            core_axis_name=("c", "s"),
            dimension_semantics=(pltpu.PARALLEL,),
        )(i_hbm, o_hbm)

    return k(x, idx.reshape(1, n))


def _experts_kernel(te_ref, nu_ref, xs_ref, wgu_ref, wd_ref, ys_ref, wgu_bf_ref, wd_bf_ref):
    j = pl.program_id(0)

    @pl.when(j < nu_ref[0])
    def _():
        @pl.when((j == 0) | (te_ref[j] != te_ref[jnp.maximum(j - 1, 0)]))
        def _():
            wgu_bf_ref[...] = wgu_ref[...].astype(BF16)
            wd_bf_ref[...] = wd_ref[...].astype(BF16)

        subs = [slice(i * MOE_SUB, (i + 1) * MOE_SUB) for i in range(MOE_TR // MOE_SUB)]
        xs = [_unpack_halves(xs_ref[0, r, :], xs_ref[1, r, :]).astype(BF16) for r in subs]
        gus = [jnp.dot(x, wgu_bf_ref[...], preferred_element_type=F32) for x in xs]
        acts = [(gu[:, :D_EXPERT] * jax.nn.sigmoid(gu[:, :D_EXPERT]) * gu[:, D_EXPERT:]).astype(BF16)
                for gu in gus]
        ys = [jnp.dot(a, wd_bf_ref[...], preferred_element_type=F32) for a in acts]
        for r, y in zip(subs, ys):
            halves = _pack_halves(y)
            ys_ref[0, r, :] = halves[0]
            ys_ref[1, r, :] = halves[1]


def _experts(xs, tile_expert, n_used, w_gu, w_down, layer):
    _, rows, _ = xs.shape
    d = w_gu.shape[2]
    return pl.pallas_call(
        _experts_kernel,
        out_shape=jax.ShapeDtypeStruct(xs.shape, U32),
        grid_spec=pltpu.PrefetchScalarGridSpec(
            num_scalar_prefetch=2,
            grid=(rows // MOE_TR,),
            in_specs=[
                pl.BlockSpec((2, MOE_TR, HALF_W), lambda j, te, nu: (0, j, 0)),
                pl.BlockSpec((None, None, d, 2 * D_EXPERT),
                             lambda j, te, nu: (layer, te[j], 0, 0)),
                pl.BlockSpec((None, None, D_EXPERT, d),
                             lambda j, te, nu: (layer, te[j], 0, 0)),
            ],
            out_specs=pl.BlockSpec((2, MOE_TR, HALF_W), lambda j, te, nu: (0, j, 0)),
            scratch_shapes=[pltpu.VMEM((d, 2 * D_EXPERT), BF16), pltpu.VMEM((D_EXPERT, d), BF16)],
        ),
        compiler_params=_cparams(("arbitrary",)),
        name="moe_experts",
    )(tile_expert, n_used, xs, w_gu, w_down)


def _moe_combined(x_ref, z_ref, meta_ref):
    meta = meta_ref[...]
    y_a = _unpack_halves(z_ref[0], z_ref[2])
    y_b = _unpack_halves(z_ref[1], z_ref[3])
    return x_ref[...] + meta[:, 4:5] * y_a + meta[:, 5:6] * y_b


def _combine_norm_kernel(x_ref, z_ref, meta_ref, g_ref, o_ref):
    out = _moe_combined(x_ref, z_ref, meta_ref)
    ms = jnp.mean(out * out, axis=-1, keepdims=True)
    o_ref[...] = out * lax.rsqrt(ms + EPS) * g_ref[...]


def _combine_norm(x, z, meta, g, *, tm):
    t, d = x.shape
    return pl.pallas_call(
        _combine_norm_kernel,
        out_shape=jax.ShapeDtypeStruct((t, d), F32),
        grid=(t // tm,),
        in_specs=[
            pl.BlockSpec((tm, d), lambda i: (i, 0)),
            pl.BlockSpec((4, tm, HALF_W), lambda i: (0, i, 0)),
            pl.BlockSpec((tm, LANES), lambda i: (i, 0)),
            pl.BlockSpec((1, d), lambda i: (0, 0)),
        ],
        out_specs=pl.BlockSpec((tm, d), lambda i: (i, 0)),
        compiler_params=_cparams(("parallel",)),
        name="moe_combine",
    )(x, z, meta, g)


def _moe_layer(routing, w_gu, w_down, layer):
    hp, meta, metat, cnt = routing
    t = meta.shape[0]
    n_tiles = 2 * t // MOE_TR + N_EXPERTS
    rows = n_tiles * MOE_TR
    counts = cnt[:N_EXPERTS, 0].astype(jnp.int32)
    tiles_e = (counts + MOE_TR - 1) // MOE_TR
    tiles_end = jnp.cumsum(tiles_e)
    row_off = (tiles_end - tiles_e) * MOE_TR
    experts = jnp.arange(N_EXPERTS, dtype=jnp.int32)[:, None]

    def region_start(e_row):
        return jnp.sum(jnp.where(e_row[None, :] == experts, row_off[:, None], 0), axis=0)

    e_a, e_b = metat[0].astype(jnp.int32), metat[1].astype(jnp.int32)
    pos_a = region_start(e_a) + metat[2].astype(jnp.int32)
    pos_b = region_start(e_b) + metat[3].astype(jnp.int32)
    tile_ids = jnp.arange(n_tiles, dtype=jnp.int32)
    tile_expert = jnp.minimum(
        jnp.sum((tile_ids[:, None] >= tiles_end[None, :]).astype(jnp.int32), axis=1),
        N_EXPERTS - 1)
    n_used = tiles_end[-1:].astype(jnp.int32)

    xs = _sc_scatter2(hp.reshape(2 * t, HALF_W),
                      jnp.concatenate([pos_a, pos_a + rows]),
                      jnp.concatenate([pos_b, pos_b + rows]), 2 * rows)
    ys = _experts(xs.reshape(2, rows, HALF_W), tile_expert, n_used,
                  w_gu, w_down, layer)
    z = _sc_gather(ys.reshape(2 * rows, HALF_W),
                   jnp.concatenate([pos_a, pos_b, pos_a + rows, pos_b + rows]))
    return z.reshape(4, t, HALF_W)


def kernel(x, norm_mix, norm_ffn, ml_w_in, ml_conv, ml_b_i, ml_b_f, ml_norm, ml_w_out, da_w_in, da_lq1, da_lk1, da_lq2, da_lk2, da_norm, da_w_out, moe_w_group, moe_b_group, moe_w_expert, moe_b_expert, moe_w_gu, moe_w_down, final_norm):
    batch, seq, d = x.shape
    xt = x.reshape(batch * seq, d)

    p, gcol, grow = _ml_proj(xt, norm_mix[0].reshape(1, d), ml_w_in[0], ml_b_i[0], ml_b_f[0],
                             tm=1024, tn=3072, out_dtype=BF16)
    assert p.shape[1] == 2 * ML_QK + 2 * ML_V
    y = _mlstm_core(p, gcol, grow, ml_conv[0], ml_norm[0].reshape(1, ML_V), batch=batch, seq=seq)
    xt, *routing = _res_router(y, ml_w_out[0].astype(BF16), xt, norm_ffn[0].reshape(1, d),
                               moe_w_group[0], moe_b_group[0], moe_w_expert[0], moe_b_expert[0],
                               tm=RR_TM, sub=RR_SUB)
    z = _moe_layer(routing, moe_w_gu, moe_w_down, 0)

    lambda_init = 0.8 - 0.6 * math.exp(-0.3 * 1)
    xt, p, vt = _norm_matmul_vt(xt, z, routing[1], norm_mix[1].reshape(1, d),
                                da_w_in[0].astype(BF16), tm=512, sub=256, vt_start=2 * DA_QK,
                                batch=batch, seq=seq, out_dtype=BF16)
    lam_params = jnp.stack([da_lq1[0], da_lk1[0], da_lq2[0], da_lk2[0]]).astype(F32)
    a = _diff_attn(p, vt, lam_params, da_norm[0].reshape(DA_DV, 1), lambda_init, batch=batch,
                   seq=seq)
    xt, *routing = _res_router(a, da_w_out[0].astype(BF16), xt, norm_ffn[1].reshape(1, d),
                               moe_w_group[1], moe_b_group[1], moe_w_expert[1], moe_b_expert[1],
                               tm=RR_TM, sub=RR_SUB)
    z = _moe_layer(routing, moe_w_gu, moe_w_down, 1)
    out = _combine_norm(xt, z, routing[1], final_norm.reshape(1, d), tm=1024)
    return out.reshape(batch, seq, d)
```

```python
import functools
import math

import jax
import jax.numpy as jnp
from jax import lax
from jax.experimental import pallas as pl
from jax.experimental.pallas import tpu as pltpu
from jax.experimental.pallas import tpu_sc as plsc

F32 = jnp.float32
BF16 = jnp.bfloat16
U32 = jnp.uint32

D_MODEL = 1024
EPS = 1e-6
ML_H = 4
ML_DV = 512
ML_DQK = 256
ML_QK = ML_H * ML_DQK
ML_V = ML_H * ML_DV
CONV_K = 4
ML_CHUNK = 256
ML_STEP_CHUNKS = 4
CONV_TAIL = 8
ML_PROJ_SUB = 256
DA_H = 8
DA_DH = 64
DA_DV = 128
DA_QK = DA_H * 2 * DA_DH
DA_EPS = 1e-5
DA_TQ = 2048
DA_TK = 2048
DA_CW = 512
DA_KEY_ALIGN = 256
DA_LAZY_LIMIT = 64.0
DA_REBASE = 8.0
LOG2E = 1.4426950408889634
N_GROUPS = 4
EPG = 8
N_EXPERTS = 32
D_EXPERT = 256
RR_TM = 1024
RR_SUB = 512
ROUTE_ROWS = 40
MOE_TR = 1024
MOE_SUB = 256
HALF_W = D_MODEL // 4
SC_WINDOW = 128
LANES = 128

VMEM_LIMIT = 48 * 1024 * 1024


def _cparams(sem):
    return pltpu.CompilerParams(dimension_semantics=sem, vmem_limit_bytes=VMEM_LIMIT)


def _norm_matmul_vt_kernel(x_ref, z_ref, meta_ref, g_ref, w_ref, x2_ref, o_ref, vt_ref,
                           *, vt_start, sub):
    for i in range(x_ref.shape[0] // sub):
        r = slice(i * sub, (i + 1) * sub)
        x = _moe_combined(x_ref.at[r, :], z_ref.at[:, r, :], meta_ref.at[r, :])
        x2_ref[r, :] = x
        ms = jnp.mean(x * x, axis=-1, keepdims=True)
        xn = (x * lax.rsqrt(ms + EPS) * g_ref[...]).astype(BF16)
        res = jnp.dot(xn, w_ref[...], preferred_element_type=F32)
        o_ref[r, :] = res.astype(o_ref.dtype)
        vt_ref[:, r] = res[:, vt_start:].T.astype(vt_ref.dtype)


def _norm_matmul_vt(x, z, meta, g, w, *, tm, sub, vt_start, batch, seq, out_dtype):
    t, d = x.shape
    n = w.shape[1]
    per_batch = seq // tm
    return pl.pallas_call(
        functools.partial(_norm_matmul_vt_kernel, vt_start=vt_start, sub=sub),
        out_shape=(jax.ShapeDtypeStruct((t, d), F32),
                   jax.ShapeDtypeStruct((t, n), out_dtype),
                   jax.ShapeDtypeStruct((batch, n - vt_start, seq), out_dtype)),
        grid=(t // tm,),
        in_specs=[
            pl.BlockSpec((tm, d), lambda i: (i, 0)),
            pl.BlockSpec((4, tm, HALF_W), lambda i: (0, i, 0)),
            pl.BlockSpec((tm, LANES), lambda i: (i, 0)),
            pl.BlockSpec((1, d), lambda i: (0, 0)),
            pl.BlockSpec((d, n), lambda i: (0, 0)),
        ],
        out_specs=(pl.BlockSpec((tm, d), lambda i: (i, 0)),
                   pl.BlockSpec((tm, n), lambda i: (i, 0)),
                   pl.BlockSpec((None, n - vt_start, tm),
                                lambda i: (i // per_batch, 0, i % per_batch))),
        compiler_params=_cparams(("parallel",)),
        name="norm_matmul_vt",
    )(x, z, meta, g, w)


def _log_sigmoid(x):
    return jnp.minimum(x, 0.0) - jnp.log1p(jnp.exp(-jnp.abs(x)))


def _ml_proj_kernel(x_ref, g_ref, w_ref, wc_ref, wr_ref, bc_ref, br_ref, o_ref, oc_ref, or_ref,
                    xn_ref):
    j = pl.program_id(1)

    @pl.when(j == 0)
    def _():
        for i in range(x_ref.shape[0] // ML_PROJ_SUB):
            r = slice(i * ML_PROJ_SUB, (i + 1) * ML_PROJ_SUB)
            x = x_ref[r, :]
            ms = jnp.mean(x * x, axis=-1, keepdims=True)
            xn = (x * lax.rsqrt(ms + EPS) * g_ref[...]).astype(BF16)
            xn_ref[r, :] = xn
            o_ref[r, :] = jnp.dot(xn, w_ref[...], preferred_element_type=F32).astype(o_ref.dtype)
            gc = jnp.dot(xn, wc_ref[...], preferred_element_type=F32) + bc_ref[...]
            lane = lax.broadcasted_iota(jnp.int32, gc.shape, 1)
            oc_ref[r, :] = jnp.where(lane < ML_H, gc, _log_sigmoid(gc))
            gr = lax.dot_general(wr_ref[...], xn, (((1,), (1,)), ((), ())),
                                 preferred_element_type=F32) + br_ref[...]
            row = lax.broadcasted_iota(jnp.int32, gr.shape, 0)
            or_ref[:, r] = jnp.where(row < ML_H, gr, _log_sigmoid(gr))

    @pl.when(j > 0)
    def _():
        o_ref[...] = jnp.dot(xn_ref[...], w_ref[...],
                             preferred_element_type=F32).astype(o_ref.dtype)


def _ml_proj(x, g, w_in, b_i, b_f, *, tm, tn, batch, seq, out_dtype):
    t, d = x.shape
    ng = 2 * ML_H
    n_main = w_in.shape[1] - ng
    w_gates = w_in[:, n_main:]
    wc = jnp.zeros((d, LANES), BF16).at[:, :ng].set(w_gates.astype(BF16))
    wr = w_gates.T.astype(BF16)
    bias = jnp.concatenate([b_i, b_f]).astype(F32)
    bc = jnp.zeros((1, LANES), F32).at[0, :ng].set(bias)
    br = bias.reshape(ng, 1)
    return pl.pallas_call(
        _ml_proj_kernel,
        out_shape=(jax.ShapeDtypeStruct((t, n_main), out_dtype),
                   jax.ShapeDtypeStruct((t, LANES), F32),
                   jax.ShapeDtypeStruct((batch, ng, seq), F32)),
        grid=(t // tm, n_main // tn),
        in_specs=[
            pl.BlockSpec((tm, d), lambda i, j: (i, 0)),
            pl.BlockSpec((1, d), lambda i, j: (0, 0)),
            pl.BlockSpec((d, tn), lambda i, j: (0, j)),
            pl.BlockSpec((d, LANES), lambda i, j: (0, 0)),
            pl.BlockSpec((ng, d), lambda i, j: (0, 0)),
            pl.BlockSpec((1, LANES), lambda i, j: (0, 0)),
            pl.BlockSpec((ng, 1), lambda i, j: (0, 0)),
        ],
        out_specs=(pl.BlockSpec((tm, tn), lambda i, j: (i, j)),
                   pl.BlockSpec((tm, LANES), lambda i, j: (i, 0)),
                   pl.BlockSpec((None, ng, tm),
                                lambda i, j: (i // (seq // tm), 0, i % (seq // tm)))),
        scratch_shapes=[pltpu.VMEM((tm, d), BF16)],
        compiler_params=_cparams(("parallel", "arbitrary")),
        name="ml_proj",
    )(x, g, w_in.astype(BF16), wc, wr, bc, br)


def _split3(x):
    hi = x.astype(BF16)
    r = x - hi.astype(F32)
    mid = r.astype(BF16)
    lo = (r - mid.astype(F32)).astype(BF16)
    return hi, mid, lo


def _mlstm_kernel(*refs):
    for i in range(ML_STEP_CHUNKS):
        _mlstm_chunk(i, *refs)


def _mlstm_chunk(i, q_ref, k_ref, v_ref, o_ref, gc_ref, gr_ref, cw_ref, ng_ref, y_ref,
                 tail_ref, ct_ref, n_ref, m_ref, shift_ref, tril_ref, triu_ref, neg_ref):
    L = ML_CHUNK
    rows = slice(i * L, (i + 1) * L)

    def reset():
        tail_ref[...] = jnp.zeros_like(tail_ref)
        ct_ref[...] = jnp.zeros_like(ct_ref)
        n_ref[...] = jnp.zeros_like(n_ref)
        m_ref[...] = jnp.zeros_like(m_ref)
        r_i = lax.broadcasted_iota(jnp.int32, (L, L), 0)
        c_i = lax.broadcasted_iota(jnp.int32, (L, L), 1)
        tril_ref[...] = (c_i <= r_i).astype(BF16)
        triu_ref[...] = (r_i <= c_i).astype(BF16)
        neg_ref[...] = jnp.where(c_i <= r_i, 0.0, -jnp.inf)
        for j in range(CONV_K - 1):
            shift_ref[j * L:(j + 1) * L, :] = (c_i == r_i - (CONV_K - 1 - j)).astype(BF16)

    if i == 0:
        pl.when(pl.program_id(1) == 0)(reset)

    u = jnp.concatenate([q_ref[rows, :], k_ref[rows, :]], axis=1)
    uf = u.astype(F32)
    shifted = jnp.dot(shift_ref[...], u, preferred_element_type=F32)
    conv = uf * cw_ref[CONV_K - 1:CONV_K, :]
    head = None
    for j in range(CONV_K - 1):
        conv = conv + shifted[j * L:(j + 1) * L, :] * cw_ref[j:j + 1, :]
        part = tail_ref[pl.ds(CONV_TAIL - (CONV_K - 1) + j, CONV_TAIL), :] * cw_ref[j:j + 1, :]
        head = part if head is None else head + part
    conv = jnp.concatenate([conv[:CONV_TAIL] + head, conv[CONV_TAIL:]], axis=0)
    tail_ref[0:CONV_TAIL, :] = uf[L - CONV_TAIL:, :]
    qk = conv * jax.nn.sigmoid(conv)
    q_all = (qk[:, :ML_QK] * (ML_DQK ** -0.5)).astype(BF16)
    k_all = qk[:, ML_QK:]

    tril = tril_ref[...]
    triu = triu_ref[...]
    gc = gc_ref[rows, :]
    gr = gr_ref[:, rows]
    bc_all = sum(jnp.dot(tril, p, preferred_element_type=F32) for p in _split3(gc))
    br_all = sum(jnp.dot(p, triu, preferred_element_type=F32) for p in _split3(gr))

    for h in range(ML_H):
        qh = q_all[:, h * ML_DQK:(h + 1) * ML_DQK]
        kh_f = k_all[:, h * ML_DQK:(h + 1) * ML_DQK]
        kh = kh_f.astype(BF16)
        vh = v_ref[rows, h * ML_DV:(h + 1) * ML_DV]
        it_col = gc[:, h:h + 1]
        it_row = gr[h:h + 1, :]
        b_col = bc_all[:, ML_H + h:ML_H + h + 1]
        b_row = br_all[ML_H + h:ML_H + h + 1, :]
        m_prev = m_ref[h][:, 0:1]

        dmat = b_col + (it_row - b_row) + neg_ref[...]
        inter_log = b_col + m_prev
        m_t = jnp.maximum(inter_log, jnp.max(dmat, axis=1, keepdims=True))
        wts = jnp.exp(dmat - m_t)
        s = lax.dot_general(qh, kh, (((1,), (1,)), ((), ())), preferred_element_type=F32)
        sc = s * wts
        inter_scale = jnp.exp(inter_log - m_t)
        ct = ct_ref[h]
        num = (jnp.dot(sc.astype(BF16), vh, preferred_element_type=F32)
               + inter_scale * jnp.dot(qh, ct.astype(BF16), preferred_element_type=F32))
        n_row = n_ref[h]
        den = (jnp.sum(sc, axis=1, keepdims=True)
               + inter_scale * jnp.sum(qh.astype(F32) * n_row, axis=1, keepdims=True))
        h_out = num / jnp.maximum(jnp.abs(den), jnp.exp(-m_t))

        b_last = b_col[L - 1:L, :]
        lw_col = b_last - b_col + it_col
        lw_row = b_last - b_row + it_row
        m_new = jnp.maximum(b_last + m_prev, jnp.max(lw_row, axis=1, keepdims=True))
        ws_col = jnp.exp(lw_col - m_new)
        decay = jnp.exp(b_last + m_prev - m_new)
        kw = kh_f * ws_col
        ct_ref[h] = decay * ct + lax.dot_general(kw.astype(BF16), vh, (((0,), (0,)), ((), ())),
                                                  preferred_element_type=F32)
        n_ref[h] = decay * n_row + jnp.sum(kw, axis=0, keepdims=True)
        m_ref[h] = jnp.broadcast_to(m_new, (1, LANES))

        ms = jnp.mean(h_out * h_out, axis=1, keepdims=True)
        hn = h_out * lax.rsqrt(ms + EPS) * ng_ref[:, h * ML_DV:(h + 1) * ML_DV]
        og = o_ref[rows, h * ML_DV:(h + 1) * ML_DV].astype(F32)
        y_ref[rows, h * ML_DV:(h + 1) * ML_DV] = (hn * jax.nn.sigmoid(og)).astype(y_ref.dtype)


def _mlstm_core(p, gcol, grow, conv_w, norm_g, *, batch, seq):
    L = ML_CHUNK
    LB = ML_STEP_CHUNKS * L
    p3 = p.reshape(batch, seq, 2 * ML_QK + 2 * ML_V)
    gc3 = gcol.reshape(batch, seq, LANES)
    y = pl.pallas_call(
        _mlstm_kernel,
        out_shape=jax.ShapeDtypeStruct((batch, seq, ML_V), BF16),
        grid=(batch, seq // LB),
        in_specs=[
            pl.BlockSpec((None, LB, ML_QK), lambda b, c: (b, c, 0)),
            pl.BlockSpec((None, LB, ML_QK), lambda b, c: (b, c, 1)),
            pl.BlockSpec((None, LB, ML_V), lambda b, c: (b, c, 1)),
            pl.BlockSpec((None, LB, ML_V), lambda b, c: (b, c, 2)),
            pl.BlockSpec((None, LB, LANES), lambda b, c: (b, c, 0)),
            pl.BlockSpec((None, 2 * ML_H, LB), lambda b, c: (b, 0, c)),
            pl.BlockSpec((CONV_K, 2 * ML_QK), lambda b, c: (0, 0)),
            pl.BlockSpec((1, ML_V), lambda b, c: (0, 0)),
        ],
        out_specs=pl.BlockSpec((None, LB, ML_V), lambda b, c: (b, c, 0)),
        scratch_shapes=[
            pltpu.VMEM((2 * CONV_TAIL, 2 * ML_QK), F32),
            pltpu.VMEM((ML_H, ML_DQK, ML_DV), F32),
            pltpu.VMEM((ML_H, 1, ML_DQK), F32),
            pltpu.VMEM((ML_H, 1, LANES), F32),
            pltpu.VMEM(((CONV_K - 1) * L, L), BF16),
            pltpu.VMEM((L, L), BF16),
            pltpu.VMEM((L, L), BF16),
            pltpu.VMEM((L, L), F32),
        ],
        compiler_params=_cparams(("parallel", "arbitrary")),
        name="mlstm_core",
    )(p3, p3, p3, p3, gc3, grow, conv_w, norm_g)
    return y.reshape(batch * seq, ML_V)


def _diff_attn_kernel(q_ref, k_ref, vt_ref, lp_ref, ng_ref, o_ref, q2_ref, r_ref, m_ref, l_ref,
                      acc_ref, *, lambda_init):
    tq, tk, cw = DA_TQ, DA_TK, DA_CW
    qi = pl.program_id(2)
    q = q_ref[...].astype(F32) * (DA_DH ** -0.5 * LOG2E)
    lane = lax.broadcasted_iota(jnp.int32, q.shape, 1)
    q2_ref[0:tq, :] = jnp.where(lane < DA_DH, q, 0.0).astype(BF16)
    q2_ref[tq:2 * tq, :] = jnp.where(lane >= DA_DH, q, 0.0).astype(BF16)

    r_ref[...] = jnp.zeros_like(r_ref)
    m_ref[...] = jnp.full(m_ref.shape, -jnp.inf, F32)
    l_ref[...] = jnp.zeros_like(l_ref)
    acc_ref[...] = jnp.zeros_like(acc_ref)

    nch = 2 * tq // cw

    def keys_needed(c, diag):
        if diag is None:
            return tk
        visible = (c * cw) % tq + cw - diag * tk
        return max(0, min(tk, -(-visible // DA_KEY_ALIGN) * DA_KEY_ALIGN))

    def mask(s, c, diag):
        nk = s.shape[0]
        key = lax.broadcasted_iota(jnp.int32, (nk, cw), 0) + diag * tk
        qry = lax.broadcasted_iota(jnp.int32, (nk, cw), 1) + (c * cw) % tq
        return jnp.where(key <= qry, s, -jnp.inf)

    def scores(j, c, diag=None):
        nk = keys_needed(c, diag)
        if nk == 0:
            return None
        kb = k_ref[pl.ds(pl.multiple_of(j * tk, tk), nk), :]
        s = lax.dot_general(kb, q2_ref[c * cw:(c + 1) * cw, :], (((1,), (1,)), ((), ())),
                            preferred_element_type=F32)
        if diag is not None:
            s = mask(s, c, diag)
        return s, jnp.max(s, axis=0, keepdims=True)


    def exact_block(j, diag=None):
        start = pl.multiple_of(j * tk, tk)
        nxt = scores(j, 0, diag)
        for c in range(nch):
            cs = slice(c * cw, (c + 1) * cw)
            s, bmax = nxt
            if c + 1 < nch:
                nxt = scores(j, c + 1, diag)
            vbt = vt_ref[:, pl.ds(start, s.shape[0])]
            m_old = m_ref[:, cs]
            m_new = jnp.maximum(m_old, bmax)
            alpha = jnp.where(m_old == -jnp.inf, 0.0, jnp.exp2(r_ref[:, cs] - m_new))
            p = jnp.exp2(s - m_new)
            l_ref[:, cs] = alpha * l_ref[:, cs] + jnp.sum(p, axis=0, keepdims=True)
            acc_ref[:, cs] = alpha * acc_ref[:, cs] + jnp.dot(vbt, p.astype(BF16),
                                                               preferred_element_type=F32)
            m_ref[:, cs] = m_new
            r_ref[:, cs] = m_new

    def fast_block(j, diag=None):
        start = pl.multiple_of(j * tk, tk)
        nxt = scores(j, 0, diag)
        tent = []
        for c in range(nch):
            cs = slice(c * cw, (c + 1) * cw)
            s, bmax = nxt
            if c + 1 < nch:
                nxt = scores(j, c + 1, diag)
            vbt = vt_ref[:, pl.ds(start, s.shape[0])]
            r_old = r_ref[:, cs]
            p = jnp.exp2(s - r_old)
            tent.append((cs, bmax, r_old, m_ref[:, cs], jnp.sum(p, axis=0, keepdims=True),
                         jnp.dot(vbt, p.astype(BF16), preferred_element_type=F32)))
        worst = functools.reduce(jnp.maximum, [
            jnp.max(jnp.maximum(bmax - r_old,
                                jnp.where(m_old == -jnp.inf, r_old - bmax, -jnp.inf)),
                    axis=1, keepdims=True)
            for _, bmax, r_old, m_old, _, _ in tent])
        safe = worst <= DA_LAZY_LIMIT
        for cs, bmax, r_old, m_old, lt, acct in tent:
            m_new = jnp.maximum(m_old, bmax)
            r_new = jnp.where(jnp.abs(m_new - r_old) > DA_REBASE, m_new, r_old)
            scale = jnp.exp2(r_old - r_new)
            l_old = l_ref[:, cs]
            acc_old = acc_ref[:, cs]
            l_ref[:, cs] = jnp.where(safe, (l_old + lt) * scale, l_old)
            acc_ref[:, cs] = jnp.where(safe, (acc_old + acct) * scale, acc_old)
            m_ref[:, cs] = jnp.where(safe, m_new, m_old)
            r_ref[:, cs] = jnp.where(safe, r_new, r_old)

        @pl.when(jnp.logical_not(jnp.max(worst) <= DA_LAZY_LIMIT))
        def _():
            exact_block(j, diag)

    def fast_body(j, carry):
        fast_block(j)
        return carry

    assert tq == tk
    lax.fori_loop(0, qi, fast_body, 0)
    fast_block(qi, diag=0)


    lp = lp_ref[...]
    lam = (jnp.exp(jnp.sum(lp[0:1, :] * lp[1:2, :], axis=1, keepdims=True))
           - jnp.exp(jnp.sum(lp[2:3, :] * lp[3:4, :], axis=1, keepdims=True)) + lambda_init)
    out = acc_ref[...] / l_ref[...]
    o = out[:, :tq] - lam * out[:, tq:]
    ms = jnp.mean(o * o, axis=0, keepdims=True)
    on = o * lax.rsqrt(ms + DA_EPS) * ng_ref[...] * (1.0 - lambda_init)
    o_ref[...] = on.T.astype(o_ref.dtype)


def _diff_attn(p, vt, lam_params, norm_g, lambda_init, *, batch, seq):
    tq = DA_TQ
    p3 = p.reshape(batch, seq, 2 * DA_QK + DA_H * DA_DV)
    kern = functools.partial(_diff_attn_kernel, lambda_init=lambda_init)
    o = pl.pallas_call(
        kern,
        out_shape=jax.ShapeDtypeStruct((batch, seq, DA_H * DA_DV), BF16),
        grid=(batch, DA_H, seq // tq),
        in_specs=[
            pl.BlockSpec((None, tq, 2 * DA_DH), lambda b, h, i: (b, i, h)),
            pl.BlockSpec((None, seq, 2 * DA_DH), lambda b, h, i: (b, 0, DA_H + h)),
            pl.BlockSpec((None, DA_DV, seq), lambda b, h, i: (b, h, 0)),
            pl.BlockSpec((4, DA_DH), lambda b, h, i: (0, 0)),
            pl.BlockSpec((DA_DV, 1), lambda b, h, i: (0, 0)),
        ],
        out_specs=pl.BlockSpec((None, tq, DA_DV), lambda b, h, i: (b, i, h)),
        scratch_shapes=[
            pltpu.VMEM((2 * tq, 2 * DA_DH), BF16),
            pltpu.VMEM((1, 2 * tq), F32),
            pltpu.VMEM((1, 2 * tq), F32),
            pltpu.VMEM((1, 2 * tq), F32),
            pltpu.VMEM((DA_DV, 2 * tq), F32),
        ],
        compiler_params=_cparams(("parallel", "parallel", "arbitrary")),
        name="diff_attn",
    )(p3, p3, vt, lam_params, norm_g)
    return o.reshape(batch * seq, DA_H * DA_DV)


def _pack_halves(y):
    halves = []
    for h in range(2):
        base = h * 2 * HALF_W
        lo = y[:, base:base + HALF_W].astype(BF16).astype(F32)
        hi = y[:, base + HALF_W:base + 2 * HALF_W].astype(BF16).astype(F32)
        lo_bits = lax.bitcast_convert_type(lo, U32) >> 16
        hi_bits = lax.bitcast_convert_type(hi, U32)
        halves.append(hi_bits | lo_bits)
    return halves


def _unpack_halves(w0, w1):
    parts = []
    for w in (w0, w1):
        parts.append(lax.bitcast_convert_type(w << 16, F32))
        parts.append(lax.bitcast_convert_type(w & jnp.uint32(0xFFFF0000), F32))
    return jnp.concatenate(parts, axis=1)


def _route_logits(x, g_ref, w_ref, b_ref):
    ms = jnp.mean(x * x, axis=-1, keepdims=True)
    hn32 = x * lax.rsqrt(ms + EPS) * g_ref[...]
    lg = jnp.dot(hn32.astype(BF16), w_ref[...], preferred_element_type=F32) + b_ref[...]
    return _pack_halves(hn32), lg


def _route_decide(lg, run_ref, us_ref):
    n = lg.shape[0]
    lgt = lg.T[0:ROUTE_ROWS, :]
    row = lax.broadcasted_iota(jnp.int32, lgt.shape, 0).astype(F32)
    neg = -jnp.inf

    gmask = (row >= N_EXPERTS) & (row < N_EXPERTS + N_GROUPS)
    gl = jnp.where(gmask, lgt, neg)
    gmax = jnp.max(gl, axis=0, keepdims=True)
    gidx = jnp.min(jnp.where(gl == gmax, row, float(LANES)), axis=0, keepdims=True) - N_EXPERTS
    gsum = jnp.sum(jnp.where(gmask, jnp.exp(gl - gmax), 0.0), axis=0, keepdims=True)
    g_w = 1.0 / gsum

    emask = (row >= gidx * EPG) & (row < gidx * EPG + EPG)
    el = jnp.where(emask, lgt, neg)
    emax = jnp.max(el, axis=0, keepdims=True)
    eexp = jnp.where(emask, jnp.exp(el - emax), 0.0)
    ep = eexp / jnp.sum(eexp, axis=0, keepdims=True)
    ep = jnp.where(emask, ep, -1.0)
    p1 = jnp.max(ep, axis=0, keepdims=True)
    i1 = jnp.min(jnp.where(ep == p1, row, float(LANES)), axis=0, keepdims=True)
    ep2 = jnp.where(row == i1, -1.0, ep)
    p2 = jnp.max(ep2, axis=0, keepdims=True)
    i2 = jnp.min(jnp.where(ep2 == p2, row, float(LANES)), axis=0, keepdims=True)
    wsum = p1 + p2
    w1 = g_w * (p1 / wsum)
    w2 = g_w * (p2 / wsum)

    a1 = row == i1
    a2 = row == i2
    onehot = a1 | a2
    before = jnp.dot(onehot.astype(BF16), us_ref[...], preferred_element_type=F32) + run_ref[...]
    rank1 = jnp.sum(jnp.where(a1, before, 0.0), axis=0, keepdims=True)
    rank2 = jnp.sum(jnp.where(a2, before, 0.0), axis=0, keepdims=True)
    run_ref[...] += jnp.sum(onehot.astype(F32), axis=1, keepdims=True)

    return jnp.concatenate([i1, i2, rank1, rank2, w1, w2,
                            jnp.zeros((ROUTE_ROWS - 6, n), F32)], axis=0)


def _res_router_kernel(a_ref, w_ref, r_ref, g_ref, wr_ref, br_ref, o_ref, hp_ref, meta_ref,
                       metat_ref, cnt_ref, run_ref, us_ref):
    sub = us_ref.shape[0]

    @pl.when(pl.program_id(0) == 0)
    def _():
        run_ref[...] = jnp.zeros_like(run_ref)
        r_i = lax.broadcasted_iota(jnp.int32, (sub, sub), 0)
        c_i = lax.broadcasted_iota(jnp.int32, (sub, sub), 1)
        us_ref[...] = (r_i < c_i).astype(BF16)

    subs = [slice(i * sub, (i + 1) * sub) for i in range(a_ref.shape[0] // sub)]

    def project(r):
        x = r_ref[r, :] + jnp.dot(a_ref[r, :], w_ref[...], preferred_element_type=F32)
        o_ref[r, :] = x
        halves, lg = _route_logits(x, g_ref, wr_ref, br_ref)
        hp_ref[0, r, :] = halves[0]
        hp_ref[1, r, :] = halves[1]
        return lg

    nxt = project(subs[0])
    for i, r in enumerate(subs):
        lg = nxt
        if i + 1 < len(subs):
            nxt = project(subs[i + 1])
        metat = _route_decide(lg, run_ref, us_ref)
        metat_ref[:, r] = metat[0:8, :]
        meta_ref[r, :] = jnp.concatenate(
            [metat, jnp.zeros((LANES - ROUTE_ROWS, sub), F32)], axis=0).T
    cnt_ref[...] = jnp.broadcast_to(run_ref[...], cnt_ref.shape)


def _res_router(a, w, res, g, w_group, b_group, w_expert, b_expert, *, tm, sub):
    t, k = a.shape
    d = w.shape[1]
    wr = jnp.zeros((d, LANES), BF16)
    wr = wr.at[:, :N_EXPERTS].set(w_expert.astype(BF16))
    wr = wr.at[:, N_EXPERTS:N_EXPERTS + N_GROUPS].set(w_group.astype(BF16))
    br = jnp.zeros((1, LANES), F32)
    br = br.at[0, :N_EXPERTS].set(b_expert.astype(F32))
    br = br.at[0, N_EXPERTS:N_EXPERTS + N_GROUPS].set(b_group.astype(F32))
    return pl.pallas_call(
        _res_router_kernel,
        out_shape=(jax.ShapeDtypeStruct((t, d), F32),
                   jax.ShapeDtypeStruct((2, t, HALF_W), U32),
                   jax.ShapeDtypeStruct((t, LANES), F32),
                   jax.ShapeDtypeStruct((8, t), F32),
                   jax.ShapeDtypeStruct((ROUTE_ROWS, LANES), F32)),
        grid=(t // tm,),
        in_specs=[
            pl.BlockSpec((tm, k), lambda i: (i, 0)),
            pl.BlockSpec((k, d), lambda i: (0, 0)),
            pl.BlockSpec((tm, d), lambda i: (i, 0)),
            pl.BlockSpec((1, d), lambda i: (0, 0)),
            pl.BlockSpec((d, LANES), lambda i: (0, 0)),
            pl.BlockSpec((1, LANES), lambda i: (0, 0)),
        ],
        out_specs=(pl.BlockSpec((tm, d), lambda i: (i, 0)),
                   pl.BlockSpec((2, tm, HALF_W), lambda i: (0, i, 0)),
                   pl.BlockSpec((tm, LANES), lambda i: (i, 0)),
                   pl.BlockSpec((8, tm), lambda i: (0, i)),
                   pl.BlockSpec((ROUTE_ROWS, LANES), lambda i: (0, 0))),
        scratch_shapes=[pltpu.VMEM((ROUTE_ROWS, 1), F32), pltpu.VMEM((sub, sub), BF16)],
        compiler_params=_cparams(("arbitrary",)),
        name="res_router",
    )(a, w, res, g, wr, br)


def _sc_mesh():
    return plsc.VectorSubcoreMesh(core_axis_name="c", subcore_axis_name="s")


def _sc_scatter2(x, i0, i1, n_out):
    n, d = x.shape

    @pl.kernel(out_type=jax.ShapeDtypeStruct((n_out, d), x.dtype), mesh=_sc_mesh())
    def k(x_hbm, i0_hbm, i1_hbm, o_hbm):
        def body(x_vmem, i0_vmem, i1_vmem):
            pltpu.sync_copy(x_vmem, o_hbm.at[i0_vmem.at[0]])
            pltpu.sync_copy(x_vmem, o_hbm.at[i1_vmem.at[0]])

        pltpu.emit_pipeline(
            body,
            grid=(n // SC_WINDOW,),
            in_specs=[pl.BlockSpec((SC_WINDOW, d), lambda i: (i, 0)),
                      pl.BlockSpec((1, SC_WINDOW), lambda i: (0, i)),
                      pl.BlockSpec((1, SC_WINDOW), lambda i: (0, i))],
            out_specs=[],
            core_axis_name=("c", "s"),
            dimension_semantics=(pltpu.PARALLEL,),
        )(x_hbm, i0_hbm, i1_hbm)

    return k(x, i0.reshape(1, n), i1.reshape(1, n))


def _sc_gather(x, idx):
    n = idx.shape[0]
    d = x.shape[1]

    @pl.kernel(out_type=jax.ShapeDtypeStruct((n, d), x.dtype), mesh=_sc_mesh())
    def k(x_hbm, i_hbm, o_hbm):
        def body(i_vmem, o_vmem):
            pltpu.sync_copy(x_hbm.at[i_vmem.at[0]], o_vmem)

        pltpu.emit_pipeline(
            body,
            grid=(n // SC_WINDOW,),
            in_specs=[pl.BlockSpec((1, SC_WINDOW), lambda i: (0, i))],
            out_specs=[pl.BlockSpec((SC_WINDOW, d), lambda i: (i, 0))],
            core_axis_name=("c", "s"),
            dimension_semantics=(pltpu.PARALLEL,),
        )(i_hbm, o_hbm)

    return k(x, idx.reshape(1, n))


def _experts_kernel(te_ref, nu_ref, xs_ref, wgu_ref, wd_ref, ys_ref, wgu_bf_ref, wd_bf_ref):
    j = pl.program_id(0)

    @pl.when(j < nu_ref[0])
    def _():
        @pl.when((j == 0) | (te_ref[j] != te_ref[jnp.maximum(j - 1, 0)]))
        def _():
            wgu_bf_ref[...] = wgu_ref[...].astype(BF16)
            wd_bf_ref[...] = wd_ref[...].astype(BF16)

        subs = [slice(i * MOE_SUB, (i + 1) * MOE_SUB) for i in range(MOE_TR // MOE_SUB)]
        xs = [_unpack_halves(xs_ref[0, r, :], xs_ref[1, r, :]).astype(BF16) for r in subs]
        gus = [jnp.dot(x, wgu_bf_ref[...], preferred_element_type=F32) for x in xs]
        acts = [(gu[:, :D_EXPERT] * jax.nn.sigmoid(gu[:, :D_EXPERT]) * gu[:, D_EXPERT:]).astype(BF16)
                for gu in gus]
        ys = [jnp.dot(a, wd_bf_ref[...], preferred_element_type=F32) for a in acts]
        for r, y in zip(subs, ys):
            halves = _pack_halves(y)
            ys_ref[0, r, :] = halves[0]
            ys_ref[1, r, :] = halves[1]


def _experts(xs, tile_expert, n_used, w_gu, w_down, layer):
    _, rows, _ = xs.shape
    d = w_gu.shape[2]
    return pl.pallas_call(
        _experts_kernel,
        out_shape=jax.ShapeDtypeStruct(xs.shape, U32),
        grid_spec=pltpu.PrefetchScalarGridSpec(
            num_scalar_prefetch=2,
            grid=(rows // MOE_TR,),
            in_specs=[
                pl.BlockSpec((2, MOE_TR, HALF_W), lambda j, te, nu: (0, j, 0)),
                pl.BlockSpec((None, None, d, 2 * D_EXPERT),
                             lambda j, te, nu: (layer, te[j], 0, 0)),
                pl.BlockSpec((None, None, D_EXPERT, d),
                             lambda j, te, nu: (layer, te[j], 0, 0)),
            ],
            out_specs=pl.BlockSpec((2, MOE_TR, HALF_W), lambda j, te, nu: (0, j, 0)),
            scratch_shapes=[pltpu.VMEM((d, 2 * D_EXPERT), BF16), pltpu.VMEM((D_EXPERT, d), BF16)],
        ),
        compiler_params=_cparams(("arbitrary",)),
        name="moe_experts",
    )(tile_expert, n_used, xs, w_gu, w_down)


def _moe_combined(x_ref, z_ref, meta_ref):
    meta = meta_ref[...]
    y_a = _unpack_halves(z_ref[0], z_ref[2])
    y_b = _unpack_halves(z_ref[1], z_ref[3])
    return x_ref[...] + meta[:, 4:5] * y_a + meta[:, 5:6] * y_b


def _combine_norm_kernel(x_ref, z_ref, meta_ref, g_ref, o_ref):
    out = _moe_combined(x_ref, z_ref, meta_ref)
    ms = jnp.mean(out * out, axis=-1, keepdims=True)
    o_ref[...] = out * lax.rsqrt(ms + EPS) * g_ref[...]


def _combine_norm(x, z, meta, g, *, tm):
    t, d = x.shape
    return pl.pallas_call(
        _combine_norm_kernel,
        out_shape=jax.ShapeDtypeStruct((t, d), F32),
        grid=(t // tm,),
        in_specs=[
            pl.BlockSpec((tm, d), lambda i: (i, 0)),
            pl.BlockSpec((4, tm, HALF_W), lambda i: (0, i, 0)),
            pl.BlockSpec((tm, LANES), lambda i: (i, 0)),
            pl.BlockSpec((1, d), lambda i: (0, 0)),
        ],
        out_specs=pl.BlockSpec((tm, d), lambda i: (i, 0)),
        compiler_params=_cparams(("parallel",)),
        name="moe_combine",
    )(x, z, meta, g)


def _moe_layer(routing, w_gu, w_down, layer):
    hp, meta, metat, cnt = routing
    t = meta.shape[0]
    n_tiles = 2 * t // MOE_TR + N_EXPERTS
    rows = n_tiles * MOE_TR
    counts = cnt[:N_EXPERTS, 0].astype(jnp.int32)
    tiles_e = (counts + MOE_TR - 1) // MOE_TR
    tiles_end = jnp.cumsum(tiles_e)
    row_off = (tiles_end - tiles_e) * MOE_TR
    experts = jnp.arange(N_EXPERTS, dtype=jnp.int32)[:, None]

    def region_start(e_row):
        return jnp.sum(jnp.where(e_row[None, :] == experts, row_off[:, None], 0), axis=0)

    e_a, e_b = metat[0].astype(jnp.int32), metat[1].astype(jnp.int32)
    pos_a = region_start(e_a) + metat[2].astype(jnp.int32)
    pos_b = region_start(e_b) + metat[3].astype(jnp.int32)
    tile_ids = jnp.arange(n_tiles, dtype=jnp.int32)
    tile_expert = jnp.minimum(
        jnp.sum((tile_ids[:, None] >= tiles_end[None, :]).astype(jnp.int32), axis=1),
        N_EXPERTS - 1)
    n_used = tiles_end[-1:].astype(jnp.int32)

    xs = _sc_scatter2(hp.reshape(2 * t, HALF_W),
                      jnp.concatenate([pos_a, pos_a + rows]),
                      jnp.concatenate([pos_b, pos_b + rows]), 2 * rows)
    ys = _experts(xs.reshape(2, rows, HALF_W), tile_expert, n_used,
                  w_gu, w_down, layer)
    z = _sc_gather(ys.reshape(2 * rows, HALF_W),
                   jnp.concatenate([pos_a, pos_b, pos_a + rows, pos_b + rows]))
    return z.reshape(4, t, HALF_W)


def kernel(x, norm_mix, norm_ffn, ml_w_in, ml_conv, ml_b_i, ml_b_f, ml_norm, ml_w_out, da_w_in, da_lq1, da_lk1, da_lq2, da_lk2, da_norm, da_w_out, moe_w_group, moe_b_group, moe_w_expert, moe_b_expert, moe_w_gu, moe_w_down, final_norm):
    batch, seq, d = x.shape
    xt = x.reshape(batch * seq, d)

    p, gcol, grow = _ml_proj(xt, norm_mix[0].reshape(1, d), ml_w_in[0], ml_b_i[0], ml_b_f[0],
                             tm=1024, tn=3072, batch=batch, seq=seq, out_dtype=BF16)
    assert p.shape[1] == 2 * ML_QK + 2 * ML_V
    y = _mlstm_core(p, gcol, grow, ml_conv[0], ml_norm[0].reshape(1, ML_V), batch=batch, seq=seq)
    xt, *routing = _res_router(y, ml_w_out[0].astype(BF16), xt, norm_ffn[0].reshape(1, d),
                               moe_w_group[0], moe_b_group[0], moe_w_expert[0], moe_b_expert[0],
                               tm=RR_TM, sub=RR_SUB)
    z = _moe_layer(routing, moe_w_gu, moe_w_down, 0)

    lambda_init = 0.8 - 0.6 * math.exp(-0.3 * 1)
    xt, p, vt = _norm_matmul_vt(xt, z, routing[1], norm_mix[1].reshape(1, d),
                                da_w_in[0].astype(BF16), tm=512, sub=256, vt_start=2 * DA_QK,
                                batch=batch, seq=seq, out_dtype=BF16)
    lam_params = jnp.stack([da_lq1[0], da_lk1[0], da_lq2[0], da_lk2[0]]).astype(F32)
    a = _diff_attn(p, vt, lam_params, da_norm[0].reshape(DA_DV, 1), lambda_init, batch=batch,
                   seq=seq)
    xt, *routing = _res_router(a, da_w_out[0].astype(BF16), xt, norm_ffn[1].reshape(1, d),
                               moe_w_group[1], moe_b_group[1], moe_w_expert[1], moe_b_expert[1],
                               tm=RR_TM, sub=RR_SUB)
    z = _moe_layer(routing, moe_w_gu, moe_w_down, 1)
    out = _combine_norm(xt, z, routing[1], final_norm.reshape(1, d), tm=1024)
    return out.reshape(batch, seq, d)
```

```python
import functools
import math

import jax
import jax.numpy as jnp
from jax import lax
from jax.experimental import pallas as pl
from jax.experimental.pallas import tpu as pltpu
from jax.experimental.pallas import tpu_sc as plsc

F32 = jnp.float32
BF16 = jnp.bfloat16
U32 = jnp.uint32

D_MODEL = 1024
EPS = 1e-6
ML_H = 4
ML_DV = 512
ML_DQK = 256
ML_QK = ML_H * ML_DQK
ML_V = ML_H * ML_DV
CONV_K = 4
ML_CHUNK = 256
ML_STEP_CHUNKS = 2
CONV_TAIL = 8
ML_PROJ_SUB = 256
DA_H = 8
DA_DH = 64
DA_DV = 128
DA_QK = DA_H * 2 * DA_DH
DA_EPS = 1e-5
DA_TQ = 2048
DA_TK = 2048
DA_CW = 512
DA_KEY_ALIGN = 256
DA_LAZY_LIMIT = 64.0
DA_REBASE = 8.0
LOG2E = 1.4426950408889634
N_GROUPS = 4
EPG = 8
N_EXPERTS = 32
D_EXPERT = 256
RR_TM = 1024
RR_SUB = 512
ROUTE_ROWS = 40
MOE_TR = 1024
MOE_SUB = 256
HALF_W = D_MODEL // 4
SC_WINDOW = 128
LANES = 128

VMEM_LIMIT = 48 * 1024 * 1024


def _cparams(sem):
    return pltpu.CompilerParams(dimension_semantics=sem, vmem_limit_bytes=VMEM_LIMIT)


def _norm_matmul_vt_kernel(x_ref, z_ref, meta_ref, g_ref, w_ref, x2_ref, o_ref, vt_ref,
                           *, vt_start, sub):
    for i in range(x_ref.shape[0] // sub):
        r = slice(i * sub, (i + 1) * sub)
        x = _moe_combined(x_ref.at[r, :], z_ref.at[:, r, :], meta_ref.at[r, :])
        x2_ref[r, :] = x
        ms = jnp.mean(x * x, axis=-1, keepdims=True)
        xn = (x * lax.rsqrt(ms + EPS) * g_ref[...]).astype(BF16)
        res = jnp.dot(xn, w_ref[...], preferred_element_type=F32)
        o_ref[r, :] = res.astype(o_ref.dtype)
        vt_ref[:, r] = res[:, vt_start:].T.astype(vt_ref.dtype)


def _norm_matmul_vt(x, z, meta, g, w, *, tm, sub, vt_start, batch, seq, out_dtype):
    t, d = x.shape
    n = w.shape[1]
    per_batch = seq // tm
    return pl.pallas_call(
        functools.partial(_norm_matmul_vt_kernel, vt_start=vt_start, sub=sub),
        out_shape=(jax.ShapeDtypeStruct((t, d), F32),
                   jax.ShapeDtypeStruct((t, n), out_dtype),
                   jax.ShapeDtypeStruct((batch, n - vt_start, seq), out_dtype)),
        grid=(t // tm,),
        in_specs=[
            pl.BlockSpec((tm, d), lambda i: (i, 0)),
            pl.BlockSpec((4, tm, HALF_W), lambda i: (0, i, 0)),
            pl.BlockSpec((tm, LANES), lambda i: (i, 0)),
            pl.BlockSpec((1, d), lambda i: (0, 0)),
            pl.BlockSpec((d, n), lambda i: (0, 0)),
        ],
        out_specs=(pl.BlockSpec((tm, d), lambda i: (i, 0)),
                   pl.BlockSpec((tm, n), lambda i: (i, 0)),
                   pl.BlockSpec((None, n - vt_start, tm),
                                lambda i: (i // per_batch, 0, i % per_batch))),
        compiler_params=_cparams(("parallel",)),
        name="norm_matmul_vt",
    )(x, z, meta, g, w)


def _log_sigmoid(x):
    return jnp.minimum(x, 0.0) - jnp.log1p(jnp.exp(-jnp.abs(x)))


def _ml_proj_kernel(x_ref, g_ref, w_ref, wc_ref, wr_ref, bc_ref, br_ref, o_ref, oc_ref, or_ref,
                    xn_ref):
    j = pl.program_id(1)

    @pl.when(j == 0)
    def _():
        for i in range(x_ref.shape[0] // ML_PROJ_SUB):
            r = slice(i * ML_PROJ_SUB, (i + 1) * ML_PROJ_SUB)
            x = x_ref[r, :]
            ms = jnp.mean(x * x, axis=-1, keepdims=True)
            xn = (x * lax.rsqrt(ms + EPS) * g_ref[...]).astype(BF16)
            xn_ref[r, :] = xn
            o_ref[r, :] = jnp.dot(xn, w_ref[...], preferred_element_type=F32).astype(o_ref.dtype)
            gc = jnp.dot(xn, wc_ref[...], preferred_element_type=F32) + bc_ref[...]
            lane = lax.broadcasted_iota(jnp.int32, gc.shape, 1)
            oc_ref[r, :] = jnp.where(lane < ML_H, gc, _log_sigmoid(gc))
            gr = lax.dot_general(wr_ref[...], xn, (((1,), (1,)), ((), ())),
                                 preferred_element_type=F32) + br_ref[...]
            row = lax.broadcasted_iota(jnp.int32, gr.shape, 0)
            or_ref[:, r] = jnp.where(row < ML_H, gr, _log_sigmoid(gr))

    @pl.when(j > 0)
    def _():
        o_ref[...] = jnp.dot(xn_ref[...], w_ref[...],
                             preferred_element_type=F32).astype(o_ref.dtype)


def _ml_proj(x, g, w_in, b_i, b_f, *, tm, tn, batch, seq, out_dtype):
    t, d = x.shape
    ng = 2 * ML_H
    n_main = w_in.shape[1] - ng
    w_gates = w_in[:, n_main:]
    wc = jnp.zeros((d, LANES), BF16).at[:, :ng].set(w_gates.astype(BF16))
    wr = w_gates.T.astype(BF16)
    bias = jnp.concatenate([b_i, b_f]).astype(F32)
    bc = jnp.zeros((1, LANES), F32).at[0, :ng].set(bias)
    br = bias.reshape(ng, 1)
    return pl.pallas_call(
        _ml_proj_kernel,
        out_shape=(jax.ShapeDtypeStruct((t, n_main), out_dtype),
                   jax.ShapeDtypeStruct((t, LANES), F32),
                   jax.ShapeDtypeStruct((batch, ng, seq), F32)),
        grid=(t // tm, n_main // tn),
        in_specs=[
            pl.BlockSpec((tm, d), lambda i, j: (i, 0)),
            pl.BlockSpec((1, d), lambda i, j: (0, 0)),
            pl.BlockSpec((d, tn), lambda i, j: (0, j)),
            pl.BlockSpec((d, LANES), lambda i, j: (0, 0)),
            pl.BlockSpec((ng, d), lambda i, j: (0, 0)),
            pl.BlockSpec((1, LANES), lambda i, j: (0, 0)),
            pl.BlockSpec((ng, 1), lambda i, j: (0, 0)),
        ],
        out_specs=(pl.BlockSpec((tm, tn), lambda i, j: (i, j)),
                   pl.BlockSpec((tm, LANES), lambda i, j: (i, 0)),
                   pl.BlockSpec((None, ng, tm),
                                lambda i, j: (i // (seq // tm), 0, i % (seq // tm)))),
        scratch_shapes=[pltpu.VMEM((tm, d), BF16)],
        compiler_params=_cparams(("parallel", "arbitrary")),
        name="ml_proj",
    )(x, g, w_in.astype(BF16), wc, wr, bc, br)


def _split3(x):
    hi = x.astype(BF16)
    r = x - hi.astype(F32)
    mid = r.astype(BF16)
    lo = (r - mid.astype(F32)).astype(BF16)
    return hi, mid, lo


def _mlstm_kernel(*refs):
    for i in range(ML_STEP_CHUNKS):
        _mlstm_chunk(i, *refs)


def _mlstm_chunk(i, q_ref, k_ref, v_ref, o_ref, gc_ref, gr_ref, cw_ref, ng_ref, y_ref,
                 tail_ref, ct_ref, n_ref, m_ref, shift_ref, tril_ref, triu_ref, neg_ref):
    L = ML_CHUNK
    rows = slice(i * L, (i + 1) * L)

    def reset():
        tail_ref[...] = jnp.zeros_like(tail_ref)
        ct_ref[...] = jnp.zeros_like(ct_ref)
        n_ref[...] = jnp.zeros_like(n_ref)
        m_ref[...] = jnp.zeros_like(m_ref)
        r_i = lax.broadcasted_iota(jnp.int32, (L, L), 0)
        c_i = lax.broadcasted_iota(jnp.int32, (L, L), 1)
        tril_ref[...] = (c_i <= r_i).astype(BF16)
        triu_ref[...] = (r_i <= c_i).astype(BF16)
        neg_ref[...] = jnp.where(c_i <= r_i, 0.0, -jnp.inf)
        for j in range(CONV_K - 1):
            shift_ref[j * L:(j + 1) * L, :] = (c_i == r_i - (CONV_K - 1 - j)).astype(BF16)

    if i == 0:
        pl.when(pl.program_id(1) == 0)(reset)

    u = jnp.concatenate([q_ref[rows, :], k_ref[rows, :]], axis=1)
    uf = u.astype(F32)
    shifted = jnp.dot(shift_ref[...], u, preferred_element_type=F32)
    conv = uf * cw_ref[CONV_K - 1:CONV_K, :]
    head = None
    for j in range(CONV_K - 1):
        conv = conv + shifted[j * L:(j + 1) * L, :] * cw_ref[j:j + 1, :]
        part = tail_ref[pl.ds(CONV_TAIL - (CONV_K - 1) + j, CONV_TAIL), :] * cw_ref[j:j + 1, :]
        head = part if head is None else head + part
    conv = jnp.concatenate([conv[:CONV_TAIL] + head, conv[CONV_TAIL:]], axis=0)
    tail_ref[0:CONV_TAIL, :] = uf[L - CONV_TAIL:, :]
    qk = conv * jax.nn.sigmoid(conv)
    q_all = (qk[:, :ML_QK] * (ML_DQK ** -0.5)).astype(BF16)
    k_all = qk[:, ML_QK:]

    tril = tril_ref[...]
    triu = triu_ref[...]
    gc = gc_ref[rows, :]
    gr = gr_ref[:, rows]
    bc_all = sum(jnp.dot(tril, p, preferred_element_type=F32) for p in _split3(gc))
    br_all = sum(jnp.dot(p, triu, preferred_element_type=F32) for p in _split3(gr))

    for h in range(ML_H):
        qh = q_all[:, h * ML_DQK:(h + 1) * ML_DQK]
        kh_f = k_all[:, h * ML_DQK:(h + 1) * ML_DQK]
        kh = kh_f.astype(BF16)
        vh = v_ref[rows, h * ML_DV:(h + 1) * ML_DV]
        it_col = gc[:, h:h + 1]
        it_row = gr[h:h + 1, :]
        b_col = bc_all[:, ML_H + h:ML_H + h + 1]
        b_row = br_all[ML_H + h:ML_H + h + 1, :]
        m_prev = m_ref[h][:, 0:1]

        dmat = b_col + (it_row - b_row) + neg_ref[...]
        inter_log = b_col + m_prev
        m_t = jnp.maximum(inter_log, jnp.max(dmat, axis=1, keepdims=True))
        wts = jnp.exp(dmat - m_t)
        s = lax.dot_general(qh, kh, (((1,), (1,)), ((), ())), preferred_element_type=F32)
        sc = s * wts
        inter_scale = jnp.exp(inter_log - m_t)
        ct = ct_ref[h]
        num = (jnp.dot(sc.astype(BF16), vh, preferred_element_type=F32)
               + inter_scale * jnp.dot(qh, ct.astype(BF16), preferred_element_type=F32))
        n_row = n_ref[h]
        den = (jnp.sum(sc, axis=1, keepdims=True)
               + inter_scale * jnp.sum(qh.astype(F32) * n_row, axis=1, keepdims=True))
        h_out = num / jnp.maximum(jnp.abs(den), jnp.exp(-m_t))

        b_last = b_col[L - 1:L, :]
        lw_col = b_last - b_col + it_col
        lw_row = b_last - b_row + it_row
        m_new = jnp.maximum(b_last + m_prev, jnp.max(lw_row, axis=1, keepdims=True))
        ws_col = jnp.exp(lw_col - m_new)
        decay = jnp.exp(b_last + m_prev - m_new)
        kw = kh_f * ws_col
        ct_ref[h] = decay * ct + lax.dot_general(kw.astype(BF16), vh, (((0,), (0,)), ((), ())),
                                                  preferred_element_type=F32)
        n_ref[h] = decay * n_row + jnp.sum(kw, axis=0, keepdims=True)
        m_ref[h] = jnp.broadcast_to(m_new, (1, LANES))

        ms = jnp.mean(h_out * h_out, axis=1, keepdims=True)
        hn = h_out * lax.rsqrt(ms + EPS) * ng_ref[:, h * ML_DV:(h + 1) * ML_DV]
        og = o_ref[rows, h * ML_DV:(h + 1) * ML_DV].astype(F32)
        y_ref[rows, h * ML_DV:(h + 1) * ML_DV] = (hn * jax.nn.sigmoid(og)).astype(y_ref.dtype)


def _mlstm_core(p, gcol, grow, conv_w, norm_g, *, batch, seq):
    L = ML_CHUNK
    LB = ML_STEP_CHUNKS * L
    p3 = p.reshape(batch, seq, 2 * ML_QK + 2 * ML_V)
    gc3 = gcol.reshape(batch, seq, LANES)
    y = pl.pallas_call(
        _mlstm_kernel,
        out_shape=jax.ShapeDtypeStruct((batch, seq, ML_V), BF16),
        grid=(batch, seq // LB),
        in_specs=[
            pl.BlockSpec((None, LB, ML_QK), lambda b, c: (b, c, 0)),
            pl.BlockSpec((None, LB, ML_QK), lambda b, c: (b, c, 1)),
            pl.BlockSpec((None, LB, ML_V), lambda b, c: (b, c, 1)),
            pl.BlockSpec((None, LB, ML_V), lambda b, c: (b, c, 2)),
            pl.BlockSpec((None, LB, LANES), lambda b, c: (b, c, 0)),
            pl.BlockSpec((None, 2 * ML_H, LB), lambda b, c: (b, 0, c)),
            pl.BlockSpec((CONV_K, 2 * ML_QK), lambda b, c: (0, 0)),
            pl.BlockSpec((1, ML_V), lambda b, c: (0, 0)),
        ],
        out_specs=pl.BlockSpec((None, LB, ML_V), lambda b, c: (b, c, 0)),
        scratch_shapes=[
            pltpu.VMEM((2 * CONV_TAIL, 2 * ML_QK), F32),
            pltpu.VMEM((ML_H, ML_DQK, ML_DV), F32),
            pltpu.VMEM((ML_H, 1, ML_DQK), F32),
            pltpu.VMEM((ML_H, 1, LANES), F32),
            pltpu.VMEM(((CONV_K - 1) * L, L), BF16),
            pltpu.VMEM((L, L), BF16),
            pltpu.VMEM((L, L), BF16),
            pltpu.VMEM((L, L), F32),
        ],
        compiler_params=_cparams(("parallel", "arbitrary")),
        name="mlstm_core",
    )(p3, p3, p3, p3, gc3, grow, conv_w, norm_g)
    return y.reshape(batch * seq, ML_V)


def _diff_attn_kernel(q_ref, k_ref, vt_ref, lp_ref, ng_ref, o_ref, q2_ref, r_ref, m_ref, l_ref,
                      acc_ref, *, lambda_init):
    tq, tk, cw = DA_TQ, DA_TK, DA_CW
    qi = pl.program_id(2)
    q = q_ref[...].astype(F32) * (DA_DH ** -0.5 * LOG2E)
    lane = lax.broadcasted_iota(jnp.int32, q.shape, 1)
    q2_ref[0:tq, :] = jnp.where(lane < DA_DH, q, 0.0).astype(BF16)
    q2_ref[tq:2 * tq, :] = jnp.where(lane >= DA_DH, q, 0.0).astype(BF16)

    r_ref[...] = jnp.zeros_like(r_ref)
    m_ref[...] = jnp.full(m_ref.shape, -jnp.inf, F32)
    l_ref[...] = jnp.zeros_like(l_ref)
    acc_ref[...] = jnp.zeros_like(acc_ref)

    nch = 2 * tq // cw

    def keys_needed(c, diag):
        if diag is None:
            return tk
        visible = (c * cw) % tq + cw - diag * tk
        return max(0, min(tk, -(-visible // DA_KEY_ALIGN) * DA_KEY_ALIGN))

    def mask(s, c, diag):
        nk = s.shape[0]
        key = lax.broadcasted_iota(jnp.int32, (nk, cw), 0) + diag * tk
        qry = lax.broadcasted_iota(jnp.int32, (nk, cw), 1) + (c * cw) % tq
        return jnp.where(key <= qry, s, -jnp.inf)

    def scores(j, c, diag=None):
        nk = keys_needed(c, diag)
        if nk == 0:
            return None
        kb = k_ref[pl.ds(pl.multiple_of(j * tk, tk), nk), :]
        s = lax.dot_general(kb, q2_ref[c * cw:(c + 1) * cw, :], (((1,), (1,)), ((), ())),
                            preferred_element_type=F32)
        if diag is not None:
            s = mask(s, c, diag)
        return s, jnp.max(s, axis=0, keepdims=True)


    def exact_block(j, diag=None):
        start = pl.multiple_of(j * tk, tk)
        nxt = scores(j, 0, diag)
        for c in range(nch):
            cs = slice(c * cw, (c + 1) * cw)
            s, bmax = nxt
            if c + 1 < nch:
                nxt = scores(j, c + 1, diag)
            vbt = vt_ref[:, pl.ds(start, s.shape[0])]
            m_old = m_ref[:, cs]
            m_new = jnp.maximum(m_old, bmax)
            alpha = jnp.where(m_old == -jnp.inf, 0.0, jnp.exp2(r_ref[:, cs] - m_new))
            p = jnp.exp2(s - m_new)
            l_ref[:, cs] = alpha * l_ref[:, cs] + jnp.sum(p, axis=0, keepdims=True)
            acc_ref[:, cs] = alpha * acc_ref[:, cs] + jnp.dot(vbt, p.astype(BF16),
                                                               preferred_element_type=F32)
            m_ref[:, cs] = m_new
            r_ref[:, cs] = m_new

    def fast_block(j, diag=None):
        start = pl.multiple_of(j * tk, tk)
        nxt = scores(j, 0, diag)
        tent = []
        for c in range(nch):
            cs = slice(c * cw, (c + 1) * cw)
            s, bmax = nxt
            if c + 1 < nch:
                nxt = scores(j, c + 1, diag)
            vbt = vt_ref[:, pl.ds(start, s.shape[0])]
            r_old = r_ref[:, cs]
            p = jnp.exp2(s - r_old)
            tent.append((cs, bmax, r_old, m_ref[:, cs], jnp.sum(p, axis=0, keepdims=True),
                         jnp.dot(vbt, p.astype(BF16), preferred_element_type=F32)))
        worst = functools.reduce(jnp.maximum, [
            jnp.max(jnp.maximum(bmax - r_old,
                                jnp.where(m_old == -jnp.inf, r_old - bmax, -jnp.inf)),
                    axis=1, keepdims=True)
            for _, bmax, r_old, m_old, _, _ in tent])
        safe = worst <= DA_LAZY_LIMIT
        for cs, bmax, r_old, m_old, lt, acct in tent:
            m_new = jnp.maximum(m_old, bmax)
            r_new = jnp.where(jnp.abs(m_new - r_old) > DA_REBASE, m_new, r_old)
            scale = jnp.exp2(r_old - r_new)
            l_old = l_ref[:, cs]
            acc_old = acc_ref[:, cs]
            l_ref[:, cs] = jnp.where(safe, (l_old + lt) * scale, l_old)
            acc_ref[:, cs] = jnp.where(safe, (acc_old + acct) * scale, acc_old)
            m_ref[:, cs] = jnp.where(safe, m_new, m_old)
            r_ref[:, cs] = jnp.where(safe, r_new, r_old)

        @pl.when(jnp.logical_not(jnp.max(worst) <= DA_LAZY_LIMIT))
        def _():
            exact_block(j, diag)

    def fast_body(j, carry):
        fast_block(j)
        return carry

    assert tq == tk
    lax.fori_loop(0, qi, fast_body, 0)
    fast_block(qi, diag=0)


    lp = lp_ref[...]
    lam = (jnp.exp(jnp.sum(lp[0:1, :] * lp[1:2, :], axis=1, keepdims=True))
           - jnp.exp(jnp.sum(lp[2:3, :] * lp[3:4, :], axis=1, keepdims=True)) + lambda_init)
    out = acc_ref[...] / l_ref[...]
    o = out[:, :tq] - lam * out[:, tq:]
    ms = jnp.mean(o * o, axis=0, keepdims=True)
    on = o * lax.rsqrt(ms + DA_EPS) * ng_ref[...] * (1.0 - lambda_init)
    o_ref[...] = on.T.astype(o_ref.dtype)


def _diff_attn(p, vt, lam_params, norm_g, lambda_init, *, batch, seq):
    tq = DA_TQ
    p3 = p.reshape(batch, seq, 2 * DA_QK + DA_H * DA_DV)
    kern = functools.partial(_diff_attn_kernel, lambda_init=lambda_init)
    o = pl.pallas_call(
        kern,
        out_shape=jax.ShapeDtypeStruct((batch, seq, DA_H * DA_DV), BF16),
        grid=(batch, DA_H, seq // tq),
        in_specs=[
            pl.BlockSpec((None, tq, 2 * DA_DH), lambda b, h, i: (b, i, h)),
            pl.BlockSpec((None, seq, 2 * DA_DH), lambda b, h, i: (b, 0, DA_H + h)),
            pl.BlockSpec((None, DA_DV, seq), lambda b, h, i: (b, h, 0)),
            pl.BlockSpec((4, DA_DH), lambda b, h, i: (0, 0)),
            pl.BlockSpec((DA_DV, 1), lambda b, h, i: (0, 0)),
        ],
        out_specs=pl.BlockSpec((None, tq, DA_DV), lambda b, h, i: (b, i, h)),
        scratch_shapes=[
            pltpu.VMEM((2 * tq, 2 * DA_DH), BF16),
            pltpu.VMEM((1, 2 * tq), F32),
            pltpu.VMEM((1, 2 * tq), F32),
            pltpu.VMEM((1, 2 * tq), F32),
            pltpu.VMEM((DA_DV, 2 * tq), F32),
        ],
        compiler_params=_cparams(("parallel", "parallel", "arbitrary")),
        name="diff_attn",
    )(p3, p3, vt, lam_params, norm_g)
    return o.reshape(batch * seq, DA_H * DA_DV)


def _pack_halves(y):
    halves = []
    for h in range(2):
        base = h * 2 * HALF_W
        lo = y[:, base:base + HALF_W].astype(BF16).astype(F32)
        hi = y[:, base + HALF_W:base + 2 * HALF_W].astype(BF16).astype(F32)
        lo_bits = lax.bitcast_convert_type(lo, U32) >> 16
        hi_bits = lax.bitcast_convert_type(hi, U32)
        halves.append(hi_bits | lo_bits)
    return halves


def _unpack_halves(w0, w1):
    parts = []
    for w in (w0, w1):
        parts.append(lax.bitcast_convert_type(w << 16, F32))
        parts.append(lax.bitcast_convert_type(w & jnp.uint32(0xFFFF0000), F32))
    return jnp.concatenate(parts, axis=1)


def _route_logits(x, g_ref, w_ref, b_ref):
    ms = jnp.mean(x * x, axis=-1, keepdims=True)
    hn32 = x * lax.rsqrt(ms + EPS) * g_ref[...]
    lg = jnp.dot(hn32.astype(BF16), w_ref[...], preferred_element_type=F32) + b_ref[...]
    return _pack_halves(hn32), lg


def _route_decide(lg, run_ref, us_ref):
    n = lg.shape[0]
    lgt = lg.T[0:ROUTE_ROWS, :]
    row = lax.broadcasted_iota(jnp.int32, lgt.shape, 0).astype(F32)
    neg = -jnp.inf

    gmask = (row >= N_EXPERTS) & (row < N_EXPERTS + N_GROUPS)
    gl = jnp.where(gmask, lgt, neg)
    gmax = jnp.max(gl, axis=0, keepdims=True)
    gidx = jnp.min(jnp.where(gl == gmax, row, float(LANES)), axis=0, keepdims=True) - N_EXPERTS
    gsum = jnp.sum(jnp.where(gmask, jnp.exp(gl - gmax), 0.0), axis=0, keepdims=True)
    g_w = 1.0 / gsum

    emask = (row >= gidx * EPG) & (row < gidx * EPG + EPG)
    el = jnp.where(emask, lgt, neg)
    emax = jnp.max(el, axis=0, keepdims=True)
    eexp = jnp.where(emask, jnp.exp(el - emax), 0.0)
    ep = eexp / jnp.sum(eexp, axis=0, keepdims=True)
    ep = jnp.where(emask, ep, -1.0)
    p1 = jnp.max(ep, axis=0, keepdims=True)
    i1 = jnp.min(jnp.where(ep == p1, row, float(LANES)), axis=0, keepdims=True)
    ep2 = jnp.where(row == i1, -1.0, ep)
    p2 = jnp.max(ep2, axis=0, keepdims=True)
    i2 = jnp.min(jnp.where(ep2 == p2, row, float(LANES)), axis=0, keepdims=True)
    wsum = p1 + p2
    w1 = g_w * (p1 / wsum)
    w2 = g_w * (p2 / wsum)

    a1 = row == i1
    a2 = row == i2
    onehot = a1 | a2
    before = jnp.dot(onehot.astype(BF16), us_ref[...], preferred_element_type=F32) + run_ref[...]
    rank1 = jnp.sum(jnp.where(a1, before, 0.0), axis=0, keepdims=True)
    rank2 = jnp.sum(jnp.where(a2, before, 0.0), axis=0, keepdims=True)
    run_ref[...] += jnp.sum(onehot.astype(F32), axis=1, keepdims=True)

    return jnp.concatenate([i1, i2, rank1, rank2, w1, w2,
                            jnp.zeros((ROUTE_ROWS - 6, n), F32)], axis=0)


def _res_router_kernel(a_ref, w_ref, r_ref, g_ref, wr_ref, br_ref, o_ref, hp_ref, meta_ref,
                       metat_ref, cnt_ref, run_ref, us_ref):
    sub = us_ref.shape[0]

    @pl.when(pl.program_id(0) == 0)
    def _():
        run_ref[...] = jnp.zeros_like(run_ref)
        r_i = lax.broadcasted_iota(jnp.int32, (sub, sub), 0)
        c_i = lax.broadcasted_iota(jnp.int32, (sub, sub), 1)
        us_ref[...] = (r_i < c_i).astype(BF16)

    subs = [slice(i * sub, (i + 1) * sub) for i in range(a_ref.shape[0] // sub)]

    def project(r):
        x = r_ref[r, :] + jnp.dot(a_ref[r, :], w_ref[...], preferred_element_type=F32)
        o_ref[r, :] = x
        halves, lg = _route_logits(x, g_ref, wr_ref, br_ref)
        hp_ref[0, r, :] = halves[0]
        hp_ref[1, r, :] = halves[1]
        return lg

    nxt = project(subs[0])
    for i, r in enumerate(subs):
        lg = nxt
        if i + 1 < len(subs):
            nxt = project(subs[i + 1])
        metat = _route_decide(lg, run_ref, us_ref)
        metat_ref[:, r] = metat[0:8, :]
        meta_ref[r, :] = jnp.concatenate(
            [metat, jnp.zeros((LANES - ROUTE_ROWS, sub), F32)], axis=0).T
    cnt_ref[...] = jnp.broadcast_to(run_ref[...], cnt_ref.shape)


def _res_router(a, w, res, g, w_group, b_group, w_expert, b_expert, *, tm, sub):
    t, k = a.shape
    d = w.shape[1]
    wr = jnp.zeros((d, LANES), BF16)
    wr = wr.at[:, :N_EXPERTS].set(w_expert.astype(BF16))
    wr = wr.at[:, N_EXPERTS:N_EXPERTS + N_GROUPS].set(w_group.astype(BF16))
    br = jnp.zeros((1, LANES), F32)
    br = br.at[0, :N_EXPERTS].set(b_expert.astype(F32))
    br = br.at[0, N_EXPERTS:N_EXPERTS + N_GROUPS].set(b_group.astype(F32))
    return pl.pallas_call(
        _res_router_kernel,
        out_shape=(jax.ShapeDtypeStruct((t, d), F32),
                   jax.ShapeDtypeStruct((2, t, HALF_W), U32),
                   jax.ShapeDtypeStruct((t, LANES), F32),
                   jax.ShapeDtypeStruct((8, t), F32),
                   jax.ShapeDtypeStruct((ROUTE_ROWS, LANES), F32)),
        grid=(t // tm,),
        in_specs=[
            pl.BlockSpec((tm, k), lambda i: (i, 0)),
            pl.BlockSpec((k, d), lambda i: (0, 0)),
            pl.BlockSpec((tm, d), lambda i: (i, 0)),
            pl.BlockSpec((1, d), lambda i: (0, 0)),
            pl.BlockSpec((d, LANES), lambda i: (0, 0)),
            pl.BlockSpec((1, LANES), lambda i: (0, 0)),
        ],
        out_specs=(pl.BlockSpec((tm, d), lambda i: (i, 0)),
                   pl.BlockSpec((2, tm, HALF_W), lambda i: (0, i, 0)),
                   pl.BlockSpec((tm, LANES), lambda i: (i, 0)),
                   pl.BlockSpec((8, tm), lambda i: (0, i)),
                   pl.BlockSpec((ROUTE_ROWS, LANES), lambda i: (0, 0))),
        scratch_shapes=[pltpu.VMEM((ROUTE_ROWS, 1), F32), pltpu.VMEM((sub, sub), BF16)],
        compiler_params=_cparams(("arbitrary",)),
        name="res_router",
    )(a, w, res, g, wr, br)


def _sc_mesh():
    return plsc.VectorSubcoreMesh(core_axis_name="c", subcore_axis_name="s")


def _sc_scatter2(x, i0, i1, n_out):
    n, d = x.shape

    @pl.kernel(out_type=jax.ShapeDtypeStruct((n_out, d), x.dtype), mesh=_sc_mesh())
    def k(x_hbm, i0_hbm, i1_hbm, o_hbm):
        def body(x_vmem, i0_vmem, i1_vmem):
            pltpu.sync_copy(x_vmem, o_hbm.at[i0_vmem.at[0]])
            pltpu.sync_copy(x_vmem, o_hbm.at[i1_vmem.at[0]])

        pltpu.emit_pipeline(
            body,
            grid=(n // SC_WINDOW,),
            in_specs=[pl.BlockSpec((SC_WINDOW, d), lambda i: (i, 0)),
                      pl.BlockSpec((1, SC_WINDOW), lambda i: (0, i)),
                      pl.BlockSpec((1, SC_WINDOW), lambda i: (0, i))],
            out_specs=[],
            core_axis_name=("c", "s"),
            dimension_semantics=(pltpu.PARALLEL,),
        )(x_hbm, i0_hbm, i1_hbm)

    return k(x, i0.reshape(1, n), i1.reshape(1, n))


def _sc_gather(x, idx):
    n = idx.shape[0]
    d = x.shape[1]

    @pl.kernel(out_type=jax.ShapeDtypeStruct((n, d), x.dtype), mesh=_sc_mesh())
    def k(x_hbm, i_hbm, o_hbm):
        def body(i_vmem, o_vmem):
            pltpu.sync_copy(x_hbm.at[i_vmem.at[0]], o_vmem)

        pltpu.emit_pipeline(
            body,
            grid=(n // SC_WINDOW,),
            in_specs=[pl.BlockSpec((1, SC_WINDOW), lambda i: (0, i))],
            out_specs=[pl.BlockSpec((SC_WINDOW, d), lambda i: (i, 0))],
            core_axis_name=("c", "s"),
            dimension_semantics=(pltpu.PARALLEL,),
        )(i_hbm, o_hbm)

    return k(x, idx.reshape(1, n))


def _experts_kernel(te_ref, nu_ref, xs_ref, wgu_ref, wd_ref, ys_ref, wgu_bf_ref, wd_bf_ref):
    j = pl.program_id(0)

    @pl.when(j < nu_ref[0])
    def _():
        @pl.when((j == 0) | (te_ref[j] != te_ref[jnp.maximum(j - 1, 0)]))
        def _():
            wgu_bf_ref[...] = wgu_ref[...].astype(BF16)
            wd_bf_ref[...] = wd_ref[...].astype(BF16)

        subs = [slice(i * MOE_SUB, (i + 1) * MOE_SUB) for i in range(MOE_TR // MOE_SUB)]
        xs = [_unpack_halves(xs_ref[0, r, :], xs_ref[1, r, :]).astype(BF16) for r in subs]
        gus = [jnp.dot(x, wgu_bf_ref[...], preferred_element_type=F32) for x in xs]
        acts = [(gu[:, :D_EXPERT] * jax.nn.sigmoid(gu[:, :D_EXPERT]) * gu[:, D_EXPERT:]).astype(BF16)
                for gu in gus]
        ys = [jnp.dot(a, wd_bf_ref[...], preferred_element_type=F32) for a in acts]
        for r, y in zip(subs, ys):
            halves = _pack_halves(y)
            ys_ref[0, r, :] = halves[0]
            ys_ref[1, r, :] = halves[1]


def _experts(xs, tile_expert, n_used, w_gu, w_down, layer):
    _, rows, _ = xs.shape
    d = w_gu.shape[2]
    return pl.pallas_call(
        _experts_kernel,
        out_shape=jax.ShapeDtypeStruct(xs.shape, U32),
        grid_spec=pltpu.PrefetchScalarGridSpec(
            num_scalar_prefetch=2,
            grid=(rows // MOE_TR,),
            in_specs=[
                pl.BlockSpec((2, MOE_TR, HALF_W), lambda j, te, nu: (0, j, 0)),
                pl.BlockSpec((None, None, d, 2 * D_EXPERT),
                             lambda j, te, nu: (layer, te[j], 0, 0)),
                pl.BlockSpec((None, None, D_EXPERT, d),
                             lambda j, te, nu: (layer, te[j], 0, 0)),
            ],
            out_specs=pl.BlockSpec((2, MOE_TR, HALF_W), lambda j, te, nu: (0, j, 0)),
            scratch_shapes=[pltpu.VMEM((d, 2 * D_EXPERT), BF16), pltpu.VMEM((D_EXPERT, d), BF16)],
        ),
        compiler_params=_cparams(("arbitrary",)),
        name="moe_experts",
    )(tile_expert, n_used, xs, w_gu, w_down)


def _moe_combined(x_ref, z_ref, meta_ref):
    meta = meta_ref[...]
    y_a = _unpack_halves(z_ref[0], z_ref[2])
    y_b = _unpack_halves(z_ref[1], z_ref[3])
    return x_ref[...] + meta[:, 4:5] * y_a + meta[:, 5:6] * y_b


def _combine_norm_kernel(x_ref, z_ref, meta_ref, g_ref, o_ref):
    out = _moe_combined(x_ref, z_ref, meta_ref)
    ms = jnp.mean(out * out, axis=-1, keepdims=True)
    o_ref[...] = out * lax.rsqrt(ms + EPS) * g_ref[...]


def _combine_norm(x, z, meta, g, *, tm):
    t, d = x.shape
    return pl.pallas_call(
        _combine_norm_kernel,
        out_shape=jax.ShapeDtypeStruct((t, d), F32),
        grid=(t // tm,),
        in_specs=[
            pl.BlockSpec((tm, d), lambda i: (i, 0)),
            pl.BlockSpec((4, tm, HALF_W), lambda i: (0, i, 0)),
            pl.BlockSpec((tm, LANES), lambda i: (i, 0)),
            pl.BlockSpec((1, d), lambda i: (0, 0)),
        ],
        out_specs=pl.BlockSpec((tm, d), lambda i: (i, 0)),
        compiler_params=_cparams(("parallel",)),
        name="moe_combine",
    )(x, z, meta, g)


def _moe_layer(routing, w_gu, w_down, layer):
    hp, meta, metat, cnt = routing
    t = meta.shape[0]
    n_tiles = 2 * t // MOE_TR + N_EXPERTS
    rows = n_tiles * MOE_TR
    counts = cnt[:N_EXPERTS, 0].astype(jnp.int32)
    tiles_e = (counts + MOE_TR - 1) // MOE_TR
    tiles_end = jnp.cumsum(tiles_e)
    row_off = (tiles_end - tiles_e) * MOE_TR
    experts = jnp.arange(N_EXPERTS, dtype=jnp.int32)[:, None]

    def region_start(e_row):
        return jnp.sum(jnp.where(e_row[None, :] == experts, row_off[:, None], 0), axis=0)

    e_a, e_b = metat[0].astype(jnp.int32), metat[1].astype(jnp.int32)
    pos_a = region_start(e_a) + metat[2].astype(jnp.int32)
    pos_b = region_start(e_b) + metat[3].astype(jnp.int32)
    tile_ids = jnp.arange(n_tiles, dtype=jnp.int32)
    tile_expert = jnp.minimum(
        jnp.sum((tile_ids[:, None] >= tiles_end[None, :]).astype(jnp.int32), axis=1),
        N_EXPERTS - 1)
    n_used = tiles_end[-1:].astype(jnp.int32)

    xs = _sc_scatter2(hp.reshape(2 * t, HALF_W),
                      jnp.concatenate([pos_a, pos_a + rows]),
                      jnp.concatenate([pos_b, pos_b + rows]), 2 * rows)
    ys = _experts(xs.reshape(2, rows, HALF_W), tile_expert, n_used,
                  w_gu, w_down, layer)
    z = _sc_gather(ys.reshape(2 * rows, HALF_W),
                   jnp.concatenate([pos_a, pos_b, pos_a + rows, pos_b + rows]))
    return z.reshape(4, t, HALF_W)


def kernel(x, norm_mix, norm_ffn, ml_w_in, ml_conv, ml_b_i, ml_b_f, ml_norm, ml_w_out, da_w_in, da_lq1, da_lk1, da_lq2, da_lk2, da_norm, da_w_out, moe_w_group, moe_b_group, moe_w_expert, moe_b_expert, moe_w_gu, moe_w_down, final_norm):
    batch, seq, d = x.shape
    xt = x.reshape(batch * seq, d)

    p, gcol, grow = _ml_proj(xt, norm_mix[0].reshape(1, d), ml_w_in[0], ml_b_i[0], ml_b_f[0],
                             tm=1024, tn=3072, batch=batch, seq=seq, out_dtype=BF16)
    assert p.shape[1] == 2 * ML_QK + 2 * ML_V
    y = _mlstm_core(p, gcol, grow, ml_conv[0], ml_norm[0].reshape(1, ML_V), batch=batch, seq=seq)
    xt, *routing = _res_router(y, ml_w_out[0].astype(BF16), xt, norm_ffn[0].reshape(1, d),
                               moe_w_group[0], moe_b_group[0], moe_w_expert[0], moe_b_expert[0],
                               tm=RR_TM, sub=RR_SUB)
    z = _moe_layer(routing, moe_w_gu, moe_w_down, 0)

    lambda_init = 0.8 - 0.6 * math.exp(-0.3 * 1)
    xt, p, vt = _norm_matmul_vt(xt, z, routing[1], norm_mix[1].reshape(1, d),
                                da_w_in[0].astype(BF16), tm=512, sub=256, vt_start=2 * DA_QK,
                                batch=batch, seq=seq, out_dtype=BF16)
    lam_params = jnp.stack([da_lq1[0], da_lk1[0], da_lq2[0], da_lk2[0]]).astype(F32)
    a = _diff_attn(p, vt, lam_params, da_norm[0].reshape(DA_DV, 1), lambda_init, batch=batch,
                   seq=seq)
    xt, *routing = _res_router(a, da_w_out[0].astype(BF16), xt, norm_ffn[1].reshape(1, d),
                               moe_w_group[1], moe_b_group[1], moe_w_expert[1], moe_b_expert[1],
                               tm=RR_TM, sub=RR_SUB)
    z = _moe_layer(routing, moe_w_gu, moe_w_down, 1)
    out = _combine_norm(xt, z, routing[1], final_norm.reshape(1, d), tm=1024)
    return out.reshape(batch, seq, d)
```

```python
import functools
import math

import jax
import jax.numpy as jnp
from jax import lax
from jax.experimental import pallas as pl
from jax.experimental.pallas import tpu as pltpu
from jax.experimental.pallas import tpu_sc as plsc

F32 = jnp.float32
BF16 = jnp.bfloat16
U32 = jnp.uint32

D_MODEL = 1024
EPS = 1e-6
ML_H = 4
ML_DV = 512
ML_DQK = 256
ML_QK = ML_H * ML_DQK
ML_V = ML_H * ML_DV
CONV_K = 4
ML_CHUNK = 256
ML_STEP_CHUNKS = 2
CONV_TAIL = 8
ML_PROJ_SUB = 256
DA_H = 8
DA_DH = 64
DA_DV = 128
DA_QK = DA_H * 2 * DA_DH
DA_EPS = 1e-5
DA_TQ = 2048
DA_TK = 2048
DA_CW = 512
DA_KEY_ALIGN = 256
DA_LAZY_LIMIT = 64.0
DA_REBASE = 8.0
LOG2E = 1.4426950408889634
N_GROUPS = 4
EPG = 8
N_EXPERTS = 32
D_EXPERT = 256
RR_TM = 1024
RR_SUB = 512
ROUTE_ROWS = 40
MOE_TR = 1024
MOE_SUB = 256
HALF_W = D_MODEL // 4
SC_WINDOW = 128
LANES = 128

VMEM_LIMIT = 48 * 1024 * 1024


def _cparams(sem):
    return pltpu.CompilerParams(dimension_semantics=sem, vmem_limit_bytes=VMEM_LIMIT)


def _norm_matmul_vt_kernel(x_ref, z_ref, meta_ref, g_ref, w_ref, x2_ref, o_ref, vt_ref,
                           *, vt_start, sub):
    for i in range(x_ref.shape[0] // sub):
        r = slice(i * sub, (i + 1) * sub)
        x = _moe_combined(x_ref.at[r, :], z_ref.at[:, r, :], meta_ref.at[r, :])
        x2_ref[r, :] = x
        ms = jnp.mean(x * x, axis=-1, keepdims=True)
        xn = (x * lax.rsqrt(ms + EPS) * g_ref[...]).astype(BF16)
        res = jnp.dot(xn, w_ref[...], preferred_element_type=F32)
        o_ref[r, :] = res.astype(o_ref.dtype)
        vt_ref[:, r] = res[:, vt_start:].T.astype(vt_ref.dtype)


def _norm_matmul_vt(x, z, meta, g, w, *, tm, sub, vt_start, batch, seq, out_dtype):
    t, d = x.shape
    n = w.shape[1]
    per_batch = seq // tm
    return pl.pallas_call(
        functools.partial(_norm_matmul_vt_kernel, vt_start=vt_start, sub=sub),
        out_shape=(jax.ShapeDtypeStruct((t, d), F32),
                   jax.ShapeDtypeStruct((t, n), out_dtype),
                   jax.ShapeDtypeStruct((batch, n - vt_start, seq), out_dtype)),
        grid=(t // tm,),
        in_specs=[
            pl.BlockSpec((tm, d), lambda i: (i, 0)),
            pl.BlockSpec((4, tm, HALF_W), lambda i: (0, i, 0)),
            pl.BlockSpec((tm, LANES), lambda i: (i, 0)),
            pl.BlockSpec((1, d), lambda i: (0, 0)),
            pl.BlockSpec((d, n), lambda i: (0, 0)),
        ],
        out_specs=(pl.BlockSpec((tm, d), lambda i: (i, 0)),
                   pl.BlockSpec((tm, n), lambda i: (i, 0)),
                   pl.BlockSpec((None, n - vt_start, tm),
                                lambda i: (i // per_batch, 0, i % per_batch))),
        compiler_params=_cparams(("parallel",)),
        name="norm_matmul_vt",
    )(x, z, meta, g, w)


def _log_sigmoid(x):
    return jnp.minimum(x, 0.0) - jnp.log1p(jnp.exp(-jnp.abs(x)))


def _ml_proj_kernel(x_ref, g_ref, w_ref, wc_ref, wr_ref, bc_ref, br_ref, o_ref, oc_ref, or_ref,
                    xn_ref):
    j = pl.program_id(1)

    @pl.when(j == 0)
    def _():
        for i in range(x_ref.shape[0] // ML_PROJ_SUB):
            r = slice(i * ML_PROJ_SUB, (i + 1) * ML_PROJ_SUB)
            x = x_ref[r, :]
            ms = jnp.mean(x * x, axis=-1, keepdims=True)
            xn = (x * lax.rsqrt(ms + EPS) * g_ref[...]).astype(BF16)
            xn_ref[r, :] = xn
            o_ref[r, :] = jnp.dot(xn, w_ref[...], preferred_element_type=F32).astype(o_ref.dtype)
            gc = jnp.dot(xn, wc_ref[...], preferred_element_type=F32) + bc_ref[...]
            lane = lax.broadcasted_iota(jnp.int32, gc.shape, 1)
            oc_ref[r, :] = jnp.where(lane < ML_H, gc, _log_sigmoid(gc))
            gr = lax.dot_general(wr_ref[...], xn, (((1,), (1,)), ((), ())),
                                 preferred_element_type=F32) + br_ref[...]
            row = lax.broadcasted_iota(jnp.int32, gr.shape, 0)
            or_ref[:, r] = jnp.where(row < ML_H, gr, _log_sigmoid(gr))

    @pl.when(j > 0)
    def _():
        o_ref[...] = jnp.dot(xn_ref[...], w_ref[...],
                             preferred_element_type=F32).astype(o_ref.dtype)


def _ml_proj(x, g, w_in, b_i, b_f, *, tm, tn, batch, seq, out_dtype):
    t, d = x.shape
    ng = 2 * ML_H
    n_main = w_in.shape[1] - ng
    w_gates = w_in[:, n_main:]
    wc = jnp.zeros((d, LANES), BF16).at[:, :ng].set(w_gates.astype(BF16))
    wr = w_gates.T.astype(BF16)
    bias = jnp.concatenate([b_i, b_f]).astype(F32)
    bc = jnp.zeros((1, LANES), F32).at[0, :ng].set(bias)
    br = bias.reshape(ng, 1)
    return pl.pallas_call(
        _ml_proj_kernel,
        out_shape=(jax.ShapeDtypeStruct((t, n_main), out_dtype),
                   jax.ShapeDtypeStruct((t, LANES), F32),
                   jax.ShapeDtypeStruct((batch, ng, seq), F32)),
        grid=(t // tm, n_main // tn),
        in_specs=[
            pl.BlockSpec((tm, d), lambda i, j: (i, 0)),
            pl.BlockSpec((1, d), lambda i, j: (0, 0)),
            pl.BlockSpec((d, tn), lambda i, j: (0, j)),
            pl.BlockSpec((d, LANES), lambda i, j: (0, 0)),
            pl.BlockSpec((ng, d), lambda i, j: (0, 0)),
            pl.BlockSpec((1, LANES), lambda i, j: (0, 0)),
            pl.BlockSpec((ng, 1), lambda i, j: (0, 0)),
        ],
        out_specs=(pl.BlockSpec((tm, tn), lambda i, j: (i, j)),
                   pl.BlockSpec((tm, LANES), lambda i, j: (i, 0)),
                   pl.BlockSpec((None, ng, tm),
                                lambda i, j: (i // (seq // tm), 0, i % (seq // tm)))),
        scratch_shapes=[pltpu.VMEM((tm, d), BF16)],
        compiler_params=_cparams(("parallel", "arbitrary")),
        name="ml_proj",
    )(x, g, w_in.astype(BF16), wc, wr, bc, br)


def _split3(x):
    hi = x.astype(BF16)
    r = x - hi.astype(F32)
    mid = r.astype(BF16)
    lo = (r - mid.astype(F32)).astype(BF16)
    return hi, mid, lo


def _mlstm_kernel(*refs):
    for i in range(ML_STEP_CHUNKS):
        _mlstm_chunk(i, *refs)


def _mlstm_chunk(i, q_ref, k_ref, v_ref, o_ref, gc_ref, gr_ref, cw_ref, ng_ref, y_ref,
                 tail_ref, ct_ref, n_ref, m_ref, shift_ref, tril_ref, triu_ref, neg_ref):
    L = ML_CHUNK
    rows = slice(i * L, (i + 1) * L)

    def reset():
        tail_ref[...] = jnp.zeros_like(tail_ref)
        ct_ref[...] = jnp.zeros_like(ct_ref)
        n_ref[...] = jnp.zeros_like(n_ref)
        m_ref[...] = jnp.zeros_like(m_ref)
        r_i = lax.broadcasted_iota(jnp.int32, (L, L), 0)
        c_i = lax.broadcasted_iota(jnp.int32, (L, L), 1)
        tril_ref[...] = (c_i <= r_i).astype(BF16)
        triu_ref[...] = (r_i <= c_i).astype(BF16)
        neg_ref[...] = jnp.where(c_i <= r_i, 0.0, -jnp.inf)
        for j in range(CONV_K - 1):
            shift_ref[j * L:(j + 1) * L, :] = (c_i == r_i - (CONV_K - 1 - j)).astype(BF16)

    if i == 0:
        pl.when(pl.program_id(1) == 0)(reset)

    u = jnp.concatenate([q_ref[rows, :], k_ref[rows, :]], axis=1)
    uf = u.astype(F32)
    shifted = jnp.dot(shift_ref[...], u, preferred_element_type=F32)
    conv = uf * cw_ref[CONV_K - 1:CONV_K, :]
    head = None
    for j in range(CONV_K - 1):
        conv = conv + shifted[j * L:(j + 1) * L, :] * cw_ref[j:j + 1, :]
        part = tail_ref[pl.ds(CONV_TAIL - (CONV_K - 1) + j, CONV_TAIL), :] * cw_ref[j:j + 1, :]
        head = part if head is None else head + part
    conv = jnp.concatenate([conv[:CONV_TAIL] + head, conv[CONV_TAIL:]], axis=0)
    tail_ref[0:CONV_TAIL, :] = uf[L - CONV_TAIL:, :]
    qk = conv * jax.nn.sigmoid(conv)
    q_all = (qk[:, :ML_QK] * (ML_DQK ** -0.5)).astype(BF16)
    k_all = qk[:, ML_QK:]

    tril = tril_ref[...]
    triu = triu_ref[...]
    gc = gc_ref[rows, :]
    gr = gr_ref[:, rows]
    bc_all = sum(jnp.dot(tril, p, preferred_element_type=F32) for p in _split3(gc))
    br_all = sum(jnp.dot(p, triu, preferred_element_type=F32) for p in _split3(gr))

    for h in range(ML_H):
        qh = q_all[:, h * ML_DQK:(h + 1) * ML_DQK]
        kh_f = k_all[:, h * ML_DQK:(h + 1) * ML_DQK]
        kh = kh_f.astype(BF16)
        vh = v_ref[rows, h * ML_DV:(h + 1) * ML_DV]
        it_col = gc[:, h:h + 1]
        it_row = gr[h:h + 1, :]
        b_col = bc_all[:, ML_H + h:ML_H + h + 1]
        b_row = br_all[ML_H + h:ML_H + h + 1, :]
        m_prev = m_ref[h][:, 0:1]

        dmat = b_col + (it_row - b_row) + neg_ref[...]
        inter_log = b_col + m_prev
        m_t = jnp.maximum(inter_log, jnp.max(dmat, axis=1, keepdims=True))
        wts = jnp.exp(dmat - m_t)
        s = lax.dot_general(qh, kh, (((1,), (1,)), ((), ())), preferred_element_type=F32)
        sc = s * wts
        inter_scale = jnp.exp(inter_log - m_t)
        ct = ct_ref[h]
        num = (jnp.dot(sc.astype(BF16), vh, preferred_element_type=F32)
               + inter_scale * jnp.dot(qh, ct.astype(BF16), preferred_element_type=F32))
        n_row = n_ref[h]
        den = (jnp.sum(sc, axis=1, keepdims=True)
               + inter_scale * jnp.sum(qh.astype(F32) * n_row, axis=1, keepdims=True))
        h_out = num / jnp.maximum(jnp.abs(den), jnp.exp(-m_t))

        b_last = b_col[L - 1:L, :]
        lw_col = b_last - b_col + it_col
        lw_row = b_last - b_row + it_row
        m_new = jnp.maximum(b_last + m_prev, jnp.max(lw_row, axis=1, keepdims=True))
        ws_col = jnp.exp(lw_col - m_new)
        decay = jnp.exp(b_last + m_prev - m_new)
        kw = kh_f * ws_col
        ct_ref[h] = decay * ct + lax.dot_general(kw.astype(BF16), vh, (((0,), (0,)), ((), ())),
                                                  preferred_element_type=F32)
        n_ref[h] = decay * n_row + jnp.sum(kw, axis=0, keepdims=True)
        m_ref[h] = jnp.broadcast_to(m_new, (1, LANES))

        ms = jnp.mean(h_out * h_out, axis=1, keepdims=True)
        hn = h_out * lax.rsqrt(ms + EPS) * ng_ref[:, h * ML_DV:(h + 1) * ML_DV]
        og = o_ref[rows, h * ML_DV:(h + 1) * ML_DV].astype(F32)
        y_ref[rows, h * ML_DV:(h + 1) * ML_DV] = (hn * jax.nn.sigmoid(og)).astype(y_ref.dtype)


def _mlstm_core(p, gcol, grow, conv_w, norm_g, *, batch, seq):
    L = ML_CHUNK
    LB = ML_STEP_CHUNKS * L
    p3 = p.reshape(batch, seq, 2 * ML_QK + 2 * ML_V)
    gc3 = gcol.reshape(batch, seq, LANES)
    y = pl.pallas_call(
        _mlstm_kernel,
        out_shape=jax.ShapeDtypeStruct((batch, seq, ML_V), BF16),
        grid=(batch, seq // LB),
        in_specs=[
            pl.BlockSpec((None, LB, ML_QK), lambda b, c: (b, c, 0)),
            pl.BlockSpec((None, LB, ML_QK), lambda b, c: (b, c, 1)),
            pl.BlockSpec((None, LB, ML_V), lambda b, c: (b, c, 1)),
            pl.BlockSpec((None, LB, ML_V), lambda b, c: (b, c, 2)),
            pl.BlockSpec((None, LB, LANES), lambda b, c: (b, c, 0)),
            pl.BlockSpec((None, 2 * ML_H, LB), lambda b, c: (b, 0, c)),
            pl.BlockSpec((CONV_K, 2 * ML_QK), lambda b, c: (0, 0)),
            pl.BlockSpec((1, ML_V), lambda b, c: (0, 0)),
        ],
        out_specs=pl.BlockSpec((None, LB, ML_V), lambda b, c: (b, c, 0)),
        scratch_shapes=[
            pltpu.VMEM((2 * CONV_TAIL, 2 * ML_QK), F32),
            pltpu.VMEM((ML_H, ML_DQK, ML_DV), F32),
            pltpu.VMEM((ML_H, 1, ML_DQK), F32),
            pltpu.VMEM((ML_H, 1, LANES), F32),
            pltpu.VMEM(((CONV_K - 1) * L, L), BF16),
            pltpu.VMEM((L, L), BF16),
            pltpu.VMEM((L, L), BF16),
            pltpu.VMEM((L, L), F32),
        ],
        compiler_params=_cparams(("parallel", "arbitrary")),
        name="mlstm_core",
    )(p3, p3, p3, p3, gc3, grow, conv_w, norm_g)
    return y.reshape(batch * seq, ML_V)


def _diff_attn_kernel(q_ref, k_ref, vt_ref, lp_ref, ng_ref, o_ref, q2_ref, r_ref, m_ref, l_ref,
                      acc_ref, *, lambda_init):
    tq, tk, cw = DA_TQ, DA_TK, DA_CW
    qi = pl.program_id(2)
    q = q_ref[...].astype(F32) * (DA_DH ** -0.5 * LOG2E)
    lane = lax.broadcasted_iota(jnp.int32, q.shape, 1)
    q2_ref[0:tq, :] = jnp.where(lane < DA_DH, q, 0.0).astype(BF16)
    q2_ref[tq:2 * tq, :] = jnp.where(lane >= DA_DH, q, 0.0).astype(BF16)

    r_ref[...] = jnp.zeros_like(r_ref)
    m_ref[...] = jnp.full(m_ref.shape, -jnp.inf, F32)
    l_ref[...] = jnp.zeros_like(l_ref)
    acc_ref[...] = jnp.zeros_like(acc_ref)

    nch = 2 * tq // cw

    def keys_needed(c, diag):
        if diag is None:
            return tk
        visible = (c * cw) % tq + cw - diag * tk
        return max(0, min(tk, -(-visible // DA_KEY_ALIGN) * DA_KEY_ALIGN))

    def mask(s, c, diag):
        nk = s.shape[0]
        key = lax.broadcasted_iota(jnp.int32, (nk, cw), 0) + diag * tk
        qry = lax.broadcasted_iota(jnp.int32, (nk, cw), 1) + (c * cw) % tq
        return jnp.where(key <= qry, s, -jnp.inf)

    def scores(j, c, diag=None):
        nk = keys_needed(c, diag)
        if nk == 0:
            return None
        kb = k_ref[pl.ds(pl.multiple_of(j * tk, tk), nk), :]
        s = lax.dot_general(kb, q2_ref[c * cw:(c + 1) * cw, :], (((1,), (1,)), ((), ())),
                            preferred_element_type=F32)
        if diag is not None:
            s = mask(s, c, diag)
        return s, jnp.max(s, axis=0, keepdims=True)


    def exact_block(j, diag=None):
        start = pl.multiple_of(j * tk, tk)
        nxt = scores(j, 0, diag)
        for c in range(nch):
            cs = slice(c * cw, (c + 1) * cw)
            s, bmax = nxt
            if c + 1 < nch:
                nxt = scores(j, c + 1, diag)
            vbt = vt_ref[:, pl.ds(start, s.shape[0])]
            m_old = m_ref[:, cs]
            m_new = jnp.maximum(m_old, bmax)
            alpha = jnp.where(m_old == -jnp.inf, 0.0, jnp.exp2(r_ref[:, cs] - m_new))
            p = jnp.exp2(s - m_new)
            l_ref[:, cs] = alpha * l_ref[:, cs] + jnp.sum(p, axis=0, keepdims=True)
            acc_ref[:, cs] = alpha * acc_ref[:, cs] + jnp.dot(vbt, p.astype(BF16),
                                                               preferred_element_type=F32)
            m_ref[:, cs] = m_new
            r_ref[:, cs] = m_new

    def fast_block(j, diag=None):
        start = pl.multiple_of(j * tk, tk)
        nxt = scores(j, 0, diag)
        tent = []
        for c in range(nch):
            cs = slice(c * cw, (c + 1) * cw)
            s, bmax = nxt
            if c + 1 < nch:
                nxt = scores(j, c + 1, diag)
            vbt = vt_ref[:, pl.ds(start, s.shape[0])]
            r_old = r_ref[:, cs]
            p = jnp.exp2(s - r_old)
            tent.append((cs, bmax, r_old, m_ref[:, cs], jnp.sum(p, axis=0, keepdims=True),
                         jnp.dot(vbt, p.astype(BF16), preferred_element_type=F32)))
        worst = functools.reduce(jnp.maximum, [
            jnp.max(jnp.maximum(bmax - r_old,
                                jnp.where(m_old == -jnp.inf, r_old - bmax, -jnp.inf)),
                    axis=1, keepdims=True)
            for _, bmax, r_old, m_old, _, _ in tent])
        safe = worst <= DA_LAZY_LIMIT
        for cs, bmax, r_old, m_old, lt, acct in tent:
            m_new = jnp.maximum(m_old, bmax)
            r_new = jnp.where(jnp.abs(m_new - r_old) > DA_REBASE, m_new, r_old)
            scale = jnp.exp2(r_old - r_new)
            l_old = l_ref[:, cs]
            acc_old = acc_ref[:, cs]
            l_ref[:, cs] = jnp.where(safe, (l_old + lt) * scale, l_old)
            acc_ref[:, cs] = jnp.where(safe, (acc_old + acct) * scale, acc_old)
            m_ref[:, cs] = jnp.where(safe, m_new, m_old)
            r_ref[:, cs] = jnp.where(safe, r_new, r_old)

        @pl.when(jnp.logical_not(jnp.max(worst) <= DA_LAZY_LIMIT))
        def _():
            exact_block(j, diag)

    def fast_body(j, carry):
        fast_block(j)
        return carry

    assert tq == tk
    lax.fori_loop(0, qi, fast_body, 0)
    fast_block(qi, diag=0)


    lp = lp_ref[...]
    lam = (jnp.exp(jnp.sum(lp[0:1, :] * lp[1:2, :], axis=1, keepdims=True))
           - jnp.exp(jnp.sum(lp[2:3, :] * lp[3:4, :], axis=1, keepdims=True)) + lambda_init)
    out = acc_ref[...] / l_ref[...]
    o = out[:, :tq] - lam * out[:, tq:]
    ms = jnp.mean(o * o, axis=0, keepdims=True)
    on = o * lax.rsqrt(ms + DA_EPS) * ng_ref[...] * (1.0 - lambda_init)
    o_ref[...] = on.T.astype(o_ref.dtype)


def _diff_attn(p, vt, lam_params, norm_g, lambda_init, *, batch, seq):
    tq = DA_TQ
    p3 = p.reshape(batch, seq, 2 * DA_QK + DA_H * DA_DV)
    kern = functools.partial(_diff_attn_kernel, lambda_init=lambda_init)
    o = pl.pallas_call(
        kern,
        out_shape=jax.ShapeDtypeStruct((batch, seq, DA_H * DA_DV), BF16),
        grid=(batch, DA_H, seq // tq),
        in_specs=[
            pl.BlockSpec((None, tq, 2 * DA_DH), lambda b, h, i: (b, i, h)),
            pl.BlockSpec((None, seq, 2 * DA_DH), lambda b, h, i: (b, 0, DA_H + h)),
            pl.BlockSpec((None, DA_DV, seq), lambda b, h, i: (b, h, 0)),
            pl.BlockSpec((4, DA_DH), lambda b, h, i: (0, 0)),
            pl.BlockSpec((DA_DV, 1), lambda b, h, i: (0, 0)),
        ],
        out_specs=pl.BlockSpec((None, tq, DA_DV), lambda b, h, i: (b, i, h)),
        scratch_shapes=[
            pltpu.VMEM((2 * tq, 2 * DA_DH), BF16),
            pltpu.VMEM((1, 2 * tq), F32),
            pltpu.VMEM((1, 2 * tq), F32),
            pltpu.VMEM((1, 2 * tq), F32),
            pltpu.VMEM((DA_DV, 2 * tq), F32),
        ],
        compiler_params=_cparams(("parallel", "parallel", "arbitrary")),
        name="diff_attn",
    )(p3, p3, vt, lam_params, norm_g)
    return o.reshape(batch * seq, DA_H * DA_DV)


def _pack_halves(y):
    halves = []
    for h in range(2):
        base = h * 2 * HALF_W
        lo = y[:, base:base + HALF_W].astype(BF16).astype(F32)
        hi = y[:, base + HALF_W:base + 2 * HALF_W].astype(BF16).astype(F32)
        lo_bits = lax.bitcast_convert_type(lo, U32) >> 16
        hi_bits = lax.bitcast_convert_type(hi, U32)
        halves.append(hi_bits | lo_bits)
    return halves


def _unpack_halves(w0, w1):
    parts = []
    for w in (w0, w1):
        parts.append(lax.bitcast_convert_type(w << 16, F32))
        parts.append(lax.bitcast_convert_type(w & jnp.uint32(0xFFFF0000), F32))
    return jnp.concatenate(parts, axis=1)


def _route_logits(x, g_ref, w_ref, b_ref):
    ms = jnp.mean(x * x, axis=-1, keepdims=True)
    hn32 = x * lax.rsqrt(ms + EPS) * g_ref[...]
    lg = jnp.dot(hn32.astype(BF16), w_ref[...], preferred_element_type=F32) + b_ref[...]
    return _pack_halves(hn32), lg


def _route_decide(lg, run_ref, us_ref):
    n = lg.shape[0]
    lgt = lg.T[0:ROUTE_ROWS, :]
    row = lax.broadcasted_iota(jnp.int32, lgt.shape, 0).astype(F32)
    neg = -jnp.inf

    gmask = (row >= N_EXPERTS) & (row < N_EXPERTS + N_GROUPS)
    gl = jnp.where(gmask, lgt, neg)
    gmax = jnp.max(gl, axis=0, keepdims=True)
    gidx = jnp.min(jnp.where(gl == gmax, row, float(LANES)), axis=0, keepdims=True) - N_EXPERTS
    gsum = jnp.sum(jnp.where(gmask, jnp.exp(gl - gmax), 0.0), axis=0, keepdims=True)
    g_w = 1.0 / gsum

    emask = (row >= gidx * EPG) & (row < gidx * EPG + EPG)
    el = jnp.where(emask, lgt, neg)
    emax = jnp.max(el, axis=0, keepdims=True)
    eexp = jnp.where(emask, jnp.exp(el - emax), 0.0)
    ep = eexp / jnp.sum(eexp, axis=0, keepdims=True)
    ep = jnp.where(emask, ep, -1.0)
    p1 = jnp.max(ep, axis=0, keepdims=True)
    i1 = jnp.min(jnp.where(ep == p1, row, float(LANES)), axis=0, keepdims=True)
    ep2 = jnp.where(row == i1, -1.0, ep)
    p2 = jnp.max(ep2, axis=0, keepdims=True)
    i2 = jnp.min(jnp.where(ep2 == p2, row, float(LANES)), axis=0, keepdims=True)
    wsum = p1 + p2
    w1 = g_w * (p1 / wsum)
    w2 = g_w * (p2 / wsum)

    a1 = row == i1
    a2 = row == i2
    onehot = a1 | a2
    before = jnp.dot(onehot.astype(BF16), us_ref[...], preferred_element_type=F32) + run_ref[...]
    rank1 = jnp.sum(jnp.where(a1, before, 0.0), axis=0, keepdims=True)
    rank2 = jnp.sum(jnp.where(a2, before, 0.0), axis=0, keepdims=True)
    run_ref[...] += jnp.sum(onehot.astype(F32), axis=1, keepdims=True)

    return jnp.concatenate([i1, i2, rank1, rank2, w1, w2,
                            jnp.zeros((ROUTE_ROWS - 6, n), F32)], axis=0)


def _res_router_kernel(a_ref, w_ref, r_ref, g_ref, wr_ref, br_ref, o_ref, hp_ref, meta_ref,
                       metat_ref, cnt_ref, run_ref, us_ref):
    sub = us_ref.shape[0]

    @pl.when(pl.program_id(0) == 0)
    def _():
        run_ref[...] = jnp.zeros_like(run_ref)
        r_i = lax.broadcasted_iota(jnp.int32, (sub, sub), 0)
        c_i = lax.broadcasted_iota(jnp.int32, (sub, sub), 1)
        us_ref[...] = (r_i < c_i).astype(BF16)

    subs = [slice(i * sub, (i + 1) * sub) for i in range(a_ref.shape[0] // sub)]

    def project(r):
        x = r_ref[r, :] + jnp.dot(a_ref[r, :], w_ref[...], preferred_element_type=F32)
        o_ref[r, :] = x
        halves, lg = _route_logits(x, g_ref, wr_ref, br_ref)
        hp_ref[0, r, :] = halves[0]
        hp_ref[1, r, :] = halves[1]
        return lg

    nxt = project(subs[0])
    for i, r in enumerate(subs):
        lg = nxt
        if i + 1 < len(subs):
            nxt = project(subs[i + 1])
        metat = _route_decide(lg, run_ref, us_ref)
        metat_ref[:, r] = metat[0:8, :]
        meta_ref[r, :] = jnp.concatenate(
            [metat, jnp.zeros((LANES - ROUTE_ROWS, sub), F32)], axis=0).T
    cnt_ref[...] = jnp.broadcast_to(run_ref[...], cnt_ref.shape)


def _res_router(a, w, res, g, w_group, b_group, w_expert, b_expert, *, tm, sub):
    t, k = a.shape
    d = w.shape[1]
    wr = jnp.zeros((d, LANES), BF16)
    wr = wr.at[:, :N_EXPERTS].set(w_expert.astype(BF16))
    wr = wr.at[:, N_EXPERTS:N_EXPERTS + N_GROUPS].set(w_group.astype(BF16))
    br = jnp.zeros((1, LANES), F32)
    br = br.at[0, :N_EXPERTS].set(b_expert.astype(F32))
    br = br.at[0, N_EXPERTS:N_EXPERTS + N_GROUPS].set(b_group.astype(F32))
    return pl.pallas_call(
        _res_router_kernel,
        out_shape=(jax.ShapeDtypeStruct((t, d), F32),
                   jax.ShapeDtypeStruct((2, t, HALF_W), U32),
                   jax.ShapeDtypeStruct((t, LANES), F32),
                   jax.ShapeDtypeStruct((8, t), F32),
                   jax.ShapeDtypeStruct((ROUTE_ROWS, LANES), F32)),
        grid=(t // tm,),
        in_specs=[
            pl.BlockSpec((tm, k), lambda i: (i, 0)),
            pl.BlockSpec((k, d), lambda i: (0, 0)),
            pl.BlockSpec((tm, d), lambda i: (i, 0)),
            pl.BlockSpec((1, d), lambda i: (0, 0)),
            pl.BlockSpec((d, LANES), lambda i: (0, 0)),
            pl.BlockSpec((1, LANES), lambda i: (0, 0)),
        ],
        out_specs=(pl.BlockSpec((tm, d), lambda i: (i, 0)),
                   pl.BlockSpec((2, tm, HALF_W), lambda i: (0, i, 0)),
                   pl.BlockSpec((tm, LANES), lambda i: (i, 0)),
                   pl.BlockSpec((8, tm), lambda i: (0, i)),
                   pl.BlockSpec((ROUTE_ROWS, LANES), lambda i: (0, 0))),
        scratch_shapes=[pltpu.VMEM((ROUTE_ROWS, 1), F32), pltpu.VMEM((sub, sub), BF16)],
        compiler_params=_cparams(("arbitrary",)),
        name="res_router",
    )(a, w, res, g, wr, br)


def _sc_mesh():
    return plsc.VectorSubcoreMesh(core_axis_name="c", subcore_axis_name="s")


def _sc_scatter2(x, i0, i1, n_out):
    n, d = x.shape

    @pl.kernel(out_type=jax.ShapeDtypeStruct((n_out, d), x.dtype), mesh=_sc_mesh())
    def k(x_hbm, i0_hbm, i1_hbm, o_hbm):
        def body(x_vmem, i0_vmem, i1_vmem):
            pltpu.sync_copy(x_vmem, o_hbm.at[i0_vmem.at[0]])
            pltpu.sync_copy(x_vmem, o_hbm.at[i1_vmem.at[0]])

        pltpu.emit_pipeline(
            body,
            grid=(n // SC_WINDOW,),
            in_specs=[pl.BlockSpec((SC_WINDOW, d), lambda i: (i, 0)),
                      pl.BlockSpec((1, SC_WINDOW), lambda i: (0, i)),
                      pl.BlockSpec((1, SC_WINDOW), lambda i: (0, i))],
            out_specs=[],
            core_axis_name=("c", "s"),
            dimension_semantics=(pltpu.PARALLEL,),
        )(x_hbm, i0_hbm, i1_hbm)

    return k(x, i0.reshape(1, n), i1.reshape(1, n))


def _sc_gather(x, idx):
    n = idx.shape[0]
    d = x.shape[1]

    @pl.kernel(out_type=jax.ShapeDtypeStruct((n, d), x.dtype), mesh=_sc_mesh())
    def k(x_hbm, i_hbm, o_hbm):
        def body(i_vmem, o_vmem):
            pltpu.sync_copy(x_hbm.at[i_vmem.at[0]], o_vmem)

        pltpu.emit_pipeline(
            body,
            grid=(n // SC_WINDOW,),
            in_specs=[pl.BlockSpec((1, SC_WINDOW), lambda i: (0, i))],
            out_specs=[pl.BlockSpec((SC_WINDOW, d), lambda i: (i, 0))],
            core_axis_name=("c", "s"),
            dimension_semantics=(pltpu.PARALLEL,),
        )(i_hbm, o_hbm)

    return k(x, idx.reshape(1, n))


def _experts_kernel(te_ref, ns_ref, xs_ref, wgu_ref, wd_ref, ys_ref, wgu_bf_ref, wd_bf_ref):
    j = pl.program_id(0)
    n_valid = ns_ref[j]

    @pl.when((n_valid > 0) & ((j == 0) | (te_ref[j] != te_ref[jnp.maximum(j - 1, 0)])))
    def _():
        wgu_bf_ref[...] = wgu_ref[...].astype(BF16)
        wd_bf_ref[...] = wd_ref[...].astype(BF16)

    for n in range(1, MOE_TR // MOE_SUB + 1):
        pl.when(n_valid == n)(functools.partial(
            _experts_tile, n, xs_ref, ys_ref, wgu_bf_ref, wd_bf_ref))


def _experts_tile(n, xs_ref, ys_ref, wgu_bf_ref, wd_bf_ref):
    subs = [slice(i * MOE_SUB, (i + 1) * MOE_SUB) for i in range(n)]
    xs = [_unpack_halves(xs_ref[0, r, :], xs_ref[1, r, :]).astype(BF16) for r in subs]
    gus = [jnp.dot(x, wgu_bf_ref[...], preferred_element_type=F32) for x in xs]
    acts = [(gu[:, :D_EXPERT] * jax.nn.sigmoid(gu[:, :D_EXPERT]) * gu[:, D_EXPERT:]).astype(BF16)
            for gu in gus]
    ys = [jnp.dot(a, wd_bf_ref[...], preferred_element_type=F32) for a in acts]
    for r, y in zip(subs, ys):
        halves = _pack_halves(y)
        ys_ref[0, r, :] = halves[0]
        ys_ref[1, r, :] = halves[1]


def _experts(xs, tile_expert, tile_subs, w_gu, w_down, layer):
    _, rows, _ = xs.shape
    d = w_gu.shape[2]
    return pl.pallas_call(
        _experts_kernel,
        out_shape=jax.ShapeDtypeStruct(xs.shape, U32),
        grid_spec=pltpu.PrefetchScalarGridSpec(
            num_scalar_prefetch=2,
            grid=(rows // MOE_TR,),
            in_specs=[
                pl.BlockSpec((2, MOE_TR, HALF_W), lambda j, te, ns: (0, j, 0)),
                pl.BlockSpec((None, None, d, 2 * D_EXPERT),
                             lambda j, te, ns: (layer, te[j], 0, 0)),
                pl.BlockSpec((None, None, D_EXPERT, d),
                             lambda j, te, ns: (layer, te[j], 0, 0)),
            ],
            out_specs=pl.BlockSpec((2, MOE_TR, HALF_W), lambda j, te, ns: (0, j, 0)),
            scratch_shapes=[pltpu.VMEM((d, 2 * D_EXPERT), BF16), pltpu.VMEM((D_EXPERT, d), BF16)],
        ),
        compiler_params=_cparams(("arbitrary",)),
        name="moe_experts",
    )(tile_expert, tile_subs, xs, w_gu, w_down)


def _moe_combined(x_ref, z_ref, meta_ref):
    meta = meta_ref[...]
    y_a = _unpack_halves(z_ref[0], z_ref[2])
    y_b = _unpack_halves(z_ref[1], z_ref[3])
    return x_ref[...] + meta[:, 4:5] * y_a + meta[:, 5:6] * y_b


def _combine_norm_kernel(x_ref, z_ref, meta_ref, g_ref, o_ref):
    out = _moe_combined(x_ref, z_ref, meta_ref)
    ms = jnp.mean(out * out, axis=-1, keepdims=True)
    o_ref[...] = out * lax.rsqrt(ms + EPS) * g_ref[...]


def _combine_norm(x, z, meta, g, *, tm):
    t, d = x.shape
    return pl.pallas_call(
        _combine_norm_kernel,
        out_shape=jax.ShapeDtypeStruct((t, d), F32),
        grid=(t // tm,),
        in_specs=[
            pl.BlockSpec((tm, d), lambda i: (i, 0)),
            pl.BlockSpec((4, tm, HALF_W), lambda i: (0, i, 0)),
            pl.BlockSpec((tm, LANES), lambda i: (i, 0)),
            pl.BlockSpec((1, d), lambda i: (0, 0)),
        ],
        out_specs=pl.BlockSpec((tm, d), lambda i: (i, 0)),
        compiler_params=_cparams(("parallel",)),
        name="moe_combine",
    )(x, z, meta, g)


def _moe_layer(routing, w_gu, w_down, layer):
    hp, meta, metat, cnt = routing
    t = meta.shape[0]
    n_tiles = 2 * t // MOE_TR + N_EXPERTS
    rows = n_tiles * MOE_TR
    counts = cnt[:N_EXPERTS, 0].astype(jnp.int32)
    tiles_e = (counts + MOE_TR - 1) // MOE_TR
    tiles_end = jnp.cumsum(tiles_e)
    row_off = (tiles_end - tiles_e) * MOE_TR
    experts = jnp.arange(N_EXPERTS, dtype=jnp.int32)[:, None]

    def region_start(e_row):
        return jnp.sum(jnp.where(e_row[None, :] == experts, row_off[:, None], 0), axis=0)

    e_a, e_b = metat[0].astype(jnp.int32), metat[1].astype(jnp.int32)
    pos_a = region_start(e_a) + metat[2].astype(jnp.int32)
    pos_b = region_start(e_b) + metat[3].astype(jnp.int32)
    tile_ids = jnp.arange(n_tiles, dtype=jnp.int32)
    tile_expert = jnp.minimum(
        jnp.sum((tile_ids[:, None] >= tiles_end[None, :]).astype(jnp.int32), axis=1),
        N_EXPERTS - 1)
    tile_first = tiles_end - tiles_e
    in_tile = (tile_ids[:, None] >= tile_first[None, :]) & (tile_ids[:, None] < tiles_end[None, :])
    rows_left = counts[None, :] - (tile_ids[:, None] - tile_first[None, :]) * MOE_TR
    tile_rows = jnp.sum(jnp.where(in_tile, jnp.clip(rows_left, 0, MOE_TR), 0), axis=1)
    tile_subs = ((tile_rows + MOE_SUB - 1) // MOE_SUB).astype(jnp.int32)

    xs = _sc_scatter2(hp.reshape(2 * t, HALF_W),
                      jnp.concatenate([pos_a, pos_a + rows]),
                      jnp.concatenate([pos_b, pos_b + rows]), 2 * rows)
    ys = _experts(xs.reshape(2, rows, HALF_W), tile_expert, tile_subs, w_gu, w_down, layer)
    z = _sc_gather(ys.reshape(2 * rows, HALF_W),
                   jnp.concatenate([pos_a, pos_b, pos_a + rows, pos_b + rows]))
    return z.reshape(4, t, HALF_W)


def kernel(x, norm_mix, norm_ffn, ml_w_in, ml_conv, ml_b_i, ml_b_f, ml_norm, ml_w_out, da_w_in, da_lq1, da_lk1, da_lq2, da_lk2, da_norm, da_w_out, moe_w_group, moe_b_group, moe_w_expert, moe_b_expert, moe_w_gu, moe_w_down, final_norm):
    batch, seq, d = x.shape
    xt = x.reshape(batch * seq, d)

    p, gcol, grow = _ml_proj(xt, norm_mix[0].reshape(1, d), ml_w_in[0], ml_b_i[0], ml_b_f[0],
                             tm=1024, tn=3072, batch=batch, seq=seq, out_dtype=BF16)
    assert p.shape[1] == 2 * ML_QK + 2 * ML_V
    y = _mlstm_core(p, gcol, grow, ml_conv[0], ml_norm[0].reshape(1, ML_V), batch=batch, seq=seq)
    xt, *routing = _res_router(y, ml_w_out[0].astype(BF16), xt, norm_ffn[0].reshape(1, d),
                               moe_w_group[0], moe_b_group[0], moe_w_expert[0], moe_b_expert[0],
                               tm=RR_TM, sub=RR_SUB)
    z = _moe_layer(routing, moe_w_gu, moe_w_down, 0)

    lambda_init = 0.8 - 0.6 * math.exp(-0.3 * 1)
    xt, p, vt = _norm_matmul_vt(xt, z, routing[1], norm_mix[1].reshape(1, d),
                                da_w_in[0].astype(BF16), tm=512, sub=256, vt_start=2 * DA_QK,
                                batch=batch, seq=seq, out_dtype=BF16)
    lam_params = jnp.stack([da_lq1[0], da_lk1[0], da_lq2[0], da_lk2[0]]).astype(F32)
    a = _diff_attn(p, vt, lam_params, da_norm[0].reshape(DA_DV, 1), lambda_init, batch=batch,
                   seq=seq)
    xt, *routing = _res_router(a, da_w_out[0].astype(BF16), xt, norm_ffn[1].reshape(1, d),
                               moe_w_group[1], moe_b_group[1], moe_w_expert[1], moe_b_expert[1],
                               tm=RR_TM, sub=RR_SUB)
    z = _moe_layer(routing, moe_w_gu, moe_w_down, 1)
    out = _combine_norm(xt, z, routing[1], final_norm.reshape(1, d), tm=1024)
    return out.reshape(batch, seq, d)
```

```python
import functools
import math

import jax
import jax.numpy as jnp
from jax import lax
from jax.experimental import pallas as pl
from jax.experimental.pallas import tpu as pltpu
from jax.experimental.pallas import tpu_sc as plsc

F32 = jnp.float32
BF16 = jnp.bfloat16
U32 = jnp.uint32

D_MODEL = 1024
EPS = 1e-6
ML_H = 4
ML_DV = 512
ML_DQK = 256
ML_QK = ML_H * ML_DQK
ML_V = ML_H * ML_DV
CONV_K = 4
ML_CHUNK = 256
ML_STEP_CHUNKS = 2
CONV_TAIL = 8
ML_PROJ_SUB = 256
DA_H = 8
DA_DH = 64
DA_DV = 128
DA_QK = DA_H * 2 * DA_DH
DA_EPS = 1e-5
DA_TQ = 2048
DA_TK = 2048
DA_CW = 512
DA_KEY_ALIGN = 256
DA_LAZY_LIMIT = 64.0
DA_REBASE = 8.0
LOG2E = 1.4426950408889634
N_GROUPS = 4
EPG = 8
N_EXPERTS = 32
D_EXPERT = 256
RR_TM = 1024
RR_SUB = 512
ROUTE_ROWS = 40
MOE_TR = 1024
MOE_SUB = 256
HALF_W = D_MODEL // 4
SC_WINDOW = 128
LANES = 128

VMEM_LIMIT = 48 * 1024 * 1024


def _cparams(sem, fuse_inputs=None):
    return pltpu.CompilerParams(dimension_semantics=sem, vmem_limit_bytes=VMEM_LIMIT,
                                allow_input_fusion=fuse_inputs)


def _norm_matmul_vt_kernel(x_ref, z_ref, meta_ref, g_ref, w_ref, x2_ref, o_ref, vt_ref,
                           *, vt_start, sub):
    for i in range(x_ref.shape[0] // sub):
        r = slice(i * sub, (i + 1) * sub)
        x = _moe_combined(x_ref.at[r, :], z_ref.at[:, r, :], meta_ref.at[r, :])
        x2_ref[r, :] = x
        ms = jnp.mean(x * x, axis=-1, keepdims=True)
        xn = (x * lax.rsqrt(ms + EPS) * g_ref[...]).astype(BF16)
        res = jnp.dot(xn, w_ref[...], preferred_element_type=F32)
        o_ref[r, :] = res.astype(o_ref.dtype)
        vt_ref[:, r] = res[:, vt_start:].T.astype(vt_ref.dtype)


def _norm_matmul_vt(x, z, meta, g, w, *, tm, sub, vt_start, batch, seq, out_dtype):
    t, d = x.shape
    n = w.shape[1]
    per_batch = seq // tm
    return pl.pallas_call(
        functools.partial(_norm_matmul_vt_kernel, vt_start=vt_start, sub=sub),
        out_shape=(jax.ShapeDtypeStruct((t, d), F32),
                   jax.ShapeDtypeStruct((t, n), out_dtype),
                   jax.ShapeDtypeStruct((batch, n - vt_start, seq), out_dtype)),
        grid=(t // tm,),
        in_specs=[
            pl.BlockSpec((tm, d), lambda i: (i, 0)),
            pl.BlockSpec((4, tm, HALF_W), lambda i: (0, i, 0)),
            pl.BlockSpec((tm, LANES), lambda i: (i, 0)),
            pl.BlockSpec((1, d), lambda i: (0, 0)),
            pl.BlockSpec((d, n), lambda i: (0, 0)),
        ],
        out_specs=(pl.BlockSpec((tm, d), lambda i: (i, 0)),
                   pl.BlockSpec((tm, n), lambda i: (i, 0)),
                   pl.BlockSpec((None, n - vt_start, tm),
                                lambda i: (i // per_batch, 0, i % per_batch))),
        compiler_params=_cparams(("parallel",), fuse_inputs=[False, False, False, False, True]),
        name="norm_matmul_vt",
    )(x, z, meta, g, w)


def _log_sigmoid(x):
    return jnp.minimum(x, 0.0) - jnp.log1p(jnp.exp(-jnp.abs(x)))


def _ml_proj_kernel(x_ref, g_ref, w_ref, wc_ref, wr_ref, bc_ref, br_ref, o_ref, oc_ref, or_ref,
                    xn_ref):
    j = pl.program_id(1)

    @pl.when(j == 0)
    def _():
        for i in range(x_ref.shape[0] // ML_PROJ_SUB):
            r = slice(i * ML_PROJ_SUB, (i + 1) * ML_PROJ_SUB)
            x = x_ref[r, :]
            ms = jnp.mean(x * x, axis=-1, keepdims=True)
            xn = (x * lax.rsqrt(ms + EPS) * g_ref[...]).astype(BF16)
            xn_ref[r, :] = xn
            o_ref[r, :] = jnp.dot(xn, w_ref[...], preferred_element_type=F32).astype(o_ref.dtype)
            gc = jnp.dot(xn, wc_ref[...], preferred_element_type=F32) + bc_ref[...]
            lane = lax.broadcasted_iota(jnp.int32, gc.shape, 1)
            oc_ref[r, :] = jnp.where(lane < ML_H, gc, _log_sigmoid(gc))
            gr = lax.dot_general(wr_ref[...], xn, (((1,), (1,)), ((), ())),
                                 preferred_element_type=F32) + br_ref[...]
            row = lax.broadcasted_iota(jnp.int32, gr.shape, 0)
            or_ref[:, r] = jnp.where(row < ML_H, gr, _log_sigmoid(gr))

    @pl.when(j > 0)
    def _():
        o_ref[...] = jnp.dot(xn_ref[...], w_ref[...],
                             preferred_element_type=F32).astype(o_ref.dtype)


def _ml_proj(x, g, w_in, b_i, b_f, *, tm, tn, batch, seq, out_dtype):
    t, d = x.shape
    ng = 2 * ML_H
    n_main = w_in.shape[1] - ng
    w_gates = w_in[:, n_main:]
    wc = jnp.zeros((d, LANES), BF16).at[:, :ng].set(w_gates.astype(BF16))
    wr = w_gates.T.astype(BF16)
    bias = jnp.concatenate([b_i, b_f]).astype(F32)
    bc = jnp.zeros((1, LANES), F32).at[0, :ng].set(bias)
    br = bias.reshape(ng, 1)
    return pl.pallas_call(
        _ml_proj_kernel,
        out_shape=(jax.ShapeDtypeStruct((t, n_main), out_dtype),
                   jax.ShapeDtypeStruct((t, LANES), F32),
                   jax.ShapeDtypeStruct((batch, ng, seq), F32)),
        grid=(t // tm, n_main // tn),
        in_specs=[
            pl.BlockSpec((tm, d), lambda i, j: (i, 0)),
            pl.BlockSpec((1, d), lambda i, j: (0, 0)),
            pl.BlockSpec((d, tn), lambda i, j: (0, j)),
            pl.BlockSpec((d, LANES), lambda i, j: (0, 0)),
            pl.BlockSpec((ng, d), lambda i, j: (0, 0)),
            pl.BlockSpec((1, LANES), lambda i, j: (0, 0)),
            pl.BlockSpec((ng, 1), lambda i, j: (0, 0)),
        ],
        out_specs=(pl.BlockSpec((tm, tn), lambda i, j: (i, j)),
                   pl.BlockSpec((tm, LANES), lambda i, j: (i, 0)),
                   pl.BlockSpec((None, ng, tm),
                                lambda i, j: (i // (seq // tm), 0, i % (seq // tm)))),
        scratch_shapes=[pltpu.VMEM((tm, d), BF16)],
        compiler_params=_cparams(("parallel", "arbitrary"),
                                 fuse_inputs=[False, False, True, False, False, False, False]),
        name="ml_proj",
    )(x, g, w_in.astype(BF16), wc, wr, bc, br)


def _split3(x):
    hi = x.astype(BF16)
    r = x - hi.astype(F32)
    mid = r.astype(BF16)
    lo = (r - mid.astype(F32)).astype(BF16)
    return hi, mid, lo


def _mlstm_kernel(*refs):
    for i in range(ML_STEP_CHUNKS):
        _mlstm_chunk(i, *refs)


def _mlstm_chunk(i, q_ref, k_ref, v_ref, o_ref, gc_ref, gr_ref, cw_ref, ng_ref, y_ref,
                 tail_ref, ct_ref, n_ref, m_ref, shift_ref, tril_ref, triu_ref, neg_ref):
    L = ML_CHUNK
    rows = slice(i * L, (i + 1) * L)

    def reset():
        tail_ref[...] = jnp.zeros_like(tail_ref)
        ct_ref[...] = jnp.zeros_like(ct_ref)
        n_ref[...] = jnp.zeros_like(n_ref)
        m_ref[...] = jnp.zeros_like(m_ref)
        r_i = lax.broadcasted_iota(jnp.int32, (L, L), 0)
        c_i = lax.broadcasted_iota(jnp.int32, (L, L), 1)
        tril_ref[...] = (c_i <= r_i).astype(BF16)
        triu_ref[...] = (r_i <= c_i).astype(BF16)
        neg_ref[...] = jnp.where(c_i <= r_i, 0.0, -jnp.inf)
        for j in range(CONV_K - 1):
            shift_ref[j * L:(j + 1) * L, :] = (c_i == r_i - (CONV_K - 1 - j)).astype(BF16)

    if i == 0:
        pl.when(pl.program_id(1) == 0)(reset)

    u = jnp.concatenate([q_ref[rows, :], k_ref[rows, :]], axis=1)
    uf = u.astype(F32)
    shifted = jnp.dot(shift_ref[...], u, preferred_element_type=F32)
    conv = uf * cw_ref[CONV_K - 1:CONV_K, :]
    head = None
    for j in range(CONV_K - 1):
        conv = conv + shifted[j * L:(j + 1) * L, :] * cw_ref[j:j + 1, :]
        part = tail_ref[pl.ds(CONV_TAIL - (CONV_K - 1) + j, CONV_TAIL), :] * cw_ref[j:j + 1, :]
        head = part if head is None else head + part
    conv = jnp.concatenate([conv[:CONV_TAIL] + head, conv[CONV_TAIL:]], axis=0)
    tail_ref[0:CONV_TAIL, :] = uf[L - CONV_TAIL:, :]
    qk = conv * jax.nn.sigmoid(conv)
    q_all = (qk[:, :ML_QK] * (ML_DQK ** -0.5)).astype(BF16)
    k_all = qk[:, ML_QK:]

    tril = tril_ref[...]
    triu = triu_ref[...]
    gc = gc_ref[rows, :]
    gr = gr_ref[:, rows]
    bc_all = sum(jnp.dot(tril, p, preferred_element_type=F32) for p in _split3(gc))
    br_all = sum(jnp.dot(p, triu, preferred_element_type=F32) for p in _split3(gr))

    for h in range(ML_H):
        qh = q_all[:, h * ML_DQK:(h + 1) * ML_DQK]
        kh_f = k_all[:, h * ML_DQK:(h + 1) * ML_DQK]
        kh = kh_f.astype(BF16)
        vh = v_ref[rows, h * ML_DV:(h + 1) * ML_DV]
        it_col = gc[:, h:h + 1]
        it_row = gr[h:h + 1, :]
        b_col = bc_all[:, ML_H + h:ML_H + h + 1]
        b_row = br_all[ML_H + h:ML_H + h + 1, :]
        m_prev = m_ref[h][:, 0:1]

        dmat = b_col + (it_row - b_row) + neg_ref[...]
        inter_log = b_col + m_prev
        m_t = jnp.maximum(inter_log, jnp.max(dmat, axis=1, keepdims=True))
        wts = jnp.exp(dmat - m_t)
        s = lax.dot_general(qh, kh, (((1,), (1,)), ((), ())), preferred_element_type=F32)
        sc = s * wts
        inter_scale = jnp.exp(inter_log - m_t)
        ct = ct_ref[h]
        num = (jnp.dot(sc.astype(BF16), vh, preferred_element_type=F32)
               + inter_scale * jnp.dot(qh, ct.astype(BF16), preferred_element_type=F32))
        n_row = n_ref[h]
        den = (jnp.sum(sc, axis=1, keepdims=True)
               + inter_scale * jnp.sum(qh.astype(F32) * n_row, axis=1, keepdims=True))
        h_out = num / jnp.maximum(jnp.abs(den), jnp.exp(-m_t))

        b_last = b_col[L - 1:L, :]
        lw_col = b_last - b_col + it_col
        lw_row = b_last - b_row + it_row
        m_new = jnp.maximum(b_last + m_prev, jnp.max(lw_row, axis=1, keepdims=True))
        ws_col = jnp.exp(lw_col - m_new)
        decay = jnp.exp(b_last + m_prev - m_new)
        kw = kh_f * ws_col
        ct_ref[h] = decay * ct + lax.dot_general(kw.astype(BF16), vh, (((0,), (0,)), ((), ())),
                                                  preferred_element_type=F32)
        n_ref[h] = decay * n_row + jnp.sum(kw, axis=0, keepdims=True)
        m_ref[h] = jnp.broadcast_to(m_new, (1, LANES))

        ms = jnp.mean(h_out * h_out, axis=1, keepdims=True)
        hn = h_out * lax.rsqrt(ms + EPS) * ng_ref[:, h * ML_DV:(h + 1) * ML_DV]
        og = o_ref[rows, h * ML_DV:(h + 1) * ML_DV].astype(F32)
        y_ref[rows, h * ML_DV:(h + 1) * ML_DV] = (hn * jax.nn.sigmoid(og)).astype(y_ref.dtype)


def _mlstm_core(p, gcol, grow, conv_w, norm_g, *, batch, seq):
    L = ML_CHUNK
    LB = ML_STEP_CHUNKS * L
    p3 = p.reshape(batch, seq, 2 * ML_QK + 2 * ML_V)
    gc3 = gcol.reshape(batch, seq, LANES)
    y = pl.pallas_call(
        _mlstm_kernel,
        out_shape=jax.ShapeDtypeStruct((batch, seq, ML_V), BF16),
        grid=(batch, seq // LB),
        in_specs=[
            pl.BlockSpec((None, LB, ML_QK), lambda b, c: (b, c, 0)),
            pl.BlockSpec((None, LB, ML_QK), lambda b, c: (b, c, 1)),
            pl.BlockSpec((None, LB, ML_V), lambda b, c: (b, c, 1)),
            pl.BlockSpec((None, LB, ML_V), lambda b, c: (b, c, 2)),
            pl.BlockSpec((None, LB, LANES), lambda b, c: (b, c, 0)),
            pl.BlockSpec((None, 2 * ML_H, LB), lambda b, c: (b, 0, c)),
            pl.BlockSpec((CONV_K, 2 * ML_QK), lambda b, c: (0, 0)),
            pl.BlockSpec((1, ML_V), lambda b, c: (0, 0)),
        ],
        out_specs=pl.BlockSpec((None, LB, ML_V), lambda b, c: (b, c, 0)),
        scratch_shapes=[
            pltpu.VMEM((2 * CONV_TAIL, 2 * ML_QK), F32),
            pltpu.VMEM((ML_H, ML_DQK, ML_DV), F32),
            pltpu.VMEM((ML_H, 1, ML_DQK), F32),
            pltpu.VMEM((ML_H, 1, LANES), F32),
            pltpu.VMEM(((CONV_K - 1) * L, L), BF16),
            pltpu.VMEM((L, L), BF16),
            pltpu.VMEM((L, L), BF16),
            pltpu.VMEM((L, L), F32),
        ],
        compiler_params=_cparams(("parallel", "arbitrary")),
        name="mlstm_core",
    )(p3, p3, p3, p3, gc3, grow, conv_w, norm_g)
    return y.reshape(batch * seq, ML_V)


def _diff_attn_kernel(q_ref, k_ref, vt_ref, lp_ref, ng_ref, o_ref, q2_ref, r_ref, m_ref, l_ref,
                      acc_ref, *, lambda_init):
    tq, tk, cw = DA_TQ, DA_TK, DA_CW
    qi = pl.program_id(2)
    q = q_ref[...].astype(F32) * (DA_DH ** -0.5 * LOG2E)
    lane = lax.broadcasted_iota(jnp.int32, q.shape, 1)
    q2_ref[0:tq, :] = jnp.where(lane < DA_DH, q, 0.0).astype(BF16)
    q2_ref[tq:2 * tq, :] = jnp.where(lane >= DA_DH, q, 0.0).astype(BF16)

    r_ref[...] = jnp.zeros_like(r_ref)
    m_ref[...] = jnp.full(m_ref.shape, -jnp.inf, F32)
    l_ref[...] = jnp.zeros_like(l_ref)
    acc_ref[...] = jnp.zeros_like(acc_ref)

    nch = 2 * tq // cw

    def keys_needed(c, diag):
        if diag is None:
            return tk
        visible = (c * cw) % tq + cw - diag * tk
        return max(0, min(tk, -(-visible // DA_KEY_ALIGN) * DA_KEY_ALIGN))

    def mask(s, c, diag):
        nk = s.shape[0]
        key = lax.broadcasted_iota(jnp.int32, (nk, cw), 0) + diag * tk
        qry = lax.broadcasted_iota(jnp.int32, (nk, cw), 1) + (c * cw) % tq
        return jnp.where(key <= qry, s, -jnp.inf)

    def scores(j, c, diag=None):
        nk = keys_needed(c, diag)
        if nk == 0:
            return None
        kb = k_ref[pl.ds(pl.multiple_of(j * tk, tk), nk), :]
        s = lax.dot_general(kb, q2_ref[c * cw:(c + 1) * cw, :], (((1,), (1,)), ((), ())),
                            preferred_element_type=F32)
        if diag is not None:
            s = mask(s, c, diag)
        return s, jnp.max(s, axis=0, keepdims=True)


    def exact_block(j, diag=None):
        start = pl.multiple_of(j * tk, tk)
        nxt = scores(j, 0, diag)
        for c in range(nch):
            cs = slice(c * cw, (c + 1) * cw)
            s, bmax = nxt
            if c + 1 < nch:
                nxt = scores(j, c + 1, diag)
            vbt = vt_ref[:, pl.ds(start, s.shape[0])]
            m_old = m_ref[:, cs]
            m_new = jnp.maximum(m_old, bmax)
            alpha = jnp.where(m_old == -jnp.inf, 0.0, jnp.exp2(r_ref[:, cs] - m_new))
            p = jnp.exp2(s - m_new)
            l_ref[:, cs] = alpha * l_ref[:, cs] + jnp.sum(p, axis=0, keepdims=True)
            acc_ref[:, cs] = alpha * acc_ref[:, cs] + jnp.dot(vbt, p.astype(BF16),
                                                               preferred_element_type=F32)
            m_ref[:, cs] = m_new
            r_ref[:, cs] = m_new

    def fast_block(j, diag=None):
        start = pl.multiple_of(j * tk, tk)
        nxt = scores(j, 0, diag)
        tent = []
        for c in range(nch):
            cs = slice(c * cw, (c + 1) * cw)
            s, bmax = nxt
            if c + 1 < nch:
                nxt = scores(j, c + 1, diag)
            vbt = vt_ref[:, pl.ds(start, s.shape[0])]
            r_old = r_ref[:, cs]
            p = jnp.exp2(s - r_old)
            tent.append((cs, bmax, r_old, m_ref[:, cs], jnp.sum(p, axis=0, keepdims=True),
                         jnp.dot(vbt, p.astype(BF16), preferred_element_type=F32)))
        worst = functools.reduce(jnp.maximum, [
            jnp.max(jnp.maximum(bmax - r_old,
                                jnp.where(m_old == -jnp.inf, r_old - bmax, -jnp.inf)),
                    axis=1, keepdims=True)
            for _, bmax, r_old, m_old, _, _ in tent])
        safe = worst <= DA_LAZY_LIMIT
        for cs, bmax, r_old, m_old, lt, acct in tent:
            m_new = jnp.maximum(m_old, bmax)
            r_new = jnp.where(jnp.abs(m_new - r_old) > DA_REBASE, m_new, r_old)
            scale = jnp.exp2(r_old - r_new)
            l_old = l_ref[:, cs]
            acc_old = acc_ref[:, cs]
            l_ref[:, cs] = jnp.where(safe, (l_old + lt) * scale, l_old)
            acc_ref[:, cs] = jnp.where(safe, (acc_old + acct) * scale, acc_old)
            m_ref[:, cs] = jnp.where(safe, m_new, m_old)
            r_ref[:, cs] = jnp.where(safe, r_new, r_old)

        @pl.when(jnp.logical_not(jnp.max(worst) <= DA_LAZY_LIMIT))
        def _():
            exact_block(j, diag)

    def fast_body(j, carry):
        fast_block(j)
        return carry

    assert tq == tk
    lax.fori_loop(0, qi, fast_body, 0)
    fast_block(qi, diag=0)


    lp = lp_ref[...]
    lam = (jnp.exp(jnp.sum(lp[0:1, :] * lp[1:2, :], axis=1, keepdims=True))
           - jnp.exp(jnp.sum(lp[2:3, :] * lp[3:4, :], axis=1, keepdims=True)) + lambda_init)
    out = acc_ref[...] / l_ref[...]
    o = out[:, :tq] - lam * out[:, tq:]
    ms = jnp.mean(o * o, axis=0, keepdims=True)
    on = o * lax.rsqrt(ms + DA_EPS) * ng_ref[...] * (1.0 - lambda_init)
    o_ref[...] = on.T.astype(o_ref.dtype)


def _diff_attn(p, vt, lam_params, norm_g, lambda_init, *, batch, seq):
    tq = DA_TQ
    p3 = p.reshape(batch, seq, 2 * DA_QK + DA_H * DA_DV)
    kern = functools.partial(_diff_attn_kernel, lambda_init=lambda_init)
    o = pl.pallas_call(
        kern,
        out_shape=jax.ShapeDtypeStruct((batch, seq, DA_H * DA_DV), BF16),
        grid=(batch, DA_H, seq // tq),
        in_specs=[
            pl.BlockSpec((None, tq, 2 * DA_DH), lambda b, h, i: (b, i, h)),
            pl.BlockSpec((None, seq, 2 * DA_DH), lambda b, h, i: (b, 0, DA_H + h)),
            pl.BlockSpec((None, DA_DV, seq), lambda b, h, i: (b, h, 0)),
            pl.BlockSpec((4, DA_DH), lambda b, h, i: (0, 0)),
            pl.BlockSpec((DA_DV, 1), lambda b, h, i: (0, 0)),
        ],
        out_specs=pl.BlockSpec((None, tq, DA_DV), lambda b, h, i: (b, i, h)),
        scratch_shapes=[
            pltpu.VMEM((2 * tq, 2 * DA_DH), BF16),
            pltpu.VMEM((1, 2 * tq), F32),
            pltpu.VMEM((1, 2 * tq), F32),
            pltpu.VMEM((1, 2 * tq), F32),
            pltpu.VMEM((DA_DV, 2 * tq), F32),
        ],
        compiler_params=_cparams(("parallel", "parallel", "arbitrary")),
        name="diff_attn",
    )(p3, p3, vt, lam_params, norm_g)
    return o.reshape(batch * seq, DA_H * DA_DV)


def _pack_halves(y):
    halves = []
    for h in range(2):
        base = h * 2 * HALF_W
        lo = y[:, base:base + HALF_W].astype(BF16).astype(F32)
        hi = y[:, base + HALF_W:base + 2 * HALF_W].astype(BF16).astype(F32)
        lo_bits = lax.bitcast_convert_type(lo, U32) >> 16
        hi_bits = lax.bitcast_convert_type(hi, U32)
        halves.append(hi_bits | lo_bits)
    return halves


def _unpack_halves(w0, w1):
    parts = []
    for w in (w0, w1):
        parts.append(lax.bitcast_convert_type(w << 16, F32))
        parts.append(lax.bitcast_convert_type(w & jnp.uint32(0xFFFF0000), F32))
    return jnp.concatenate(parts, axis=1)


def _route_logits(x, g_ref, w_ref, b_ref):
    ms = jnp.mean(x * x, axis=-1, keepdims=True)
    hn32 = x * lax.rsqrt(ms + EPS) * g_ref[...]
    lg = jnp.dot(hn32.astype(BF16), w_ref[...], preferred_element_type=F32) + b_ref[...]
    return _pack_halves(hn32), lg


def _route_decide(lg, run_ref, us_ref):
    n = lg.shape[0]
    lgt = lg.T[0:ROUTE_ROWS, :]
    row = lax.broadcasted_iota(jnp.int32, lgt.shape, 0).astype(F32)
    neg = -jnp.inf

    gmask = (row >= N_EXPERTS) & (row < N_EXPERTS + N_GROUPS)
    gl = jnp.where(gmask, lgt, neg)
    gmax = jnp.max(gl, axis=0, keepdims=True)
    gidx = jnp.min(jnp.where(gl == gmax, row, float(LANES)), axis=0, keepdims=True) - N_EXPERTS
    gsum = jnp.sum(jnp.where(gmask, jnp.exp(gl - gmax), 0.0), axis=0, keepdims=True)
    g_w = 1.0 / gsum

    emask = (row >= gidx * EPG) & (row < gidx * EPG + EPG)
    el = jnp.where(emask, lgt, neg)
    emax = jnp.max(el, axis=0, keepdims=True)
    eexp = jnp.where(emask, jnp.exp(el - emax), 0.0)
    ep = eexp / jnp.sum(eexp, axis=0, keepdims=True)
    ep = jnp.where(emask, ep, -1.0)
    p1 = jnp.max(ep, axis=0, keepdims=True)
    i1 = jnp.min(jnp.where(ep == p1, row, float(LANES)), axis=0, keepdims=True)
    ep2 = jnp.where(row == i1, -1.0, ep)
    p2 = jnp.max(ep2, axis=0, keepdims=True)
    i2 = jnp.min(jnp.where(ep2 == p2, row, float(LANES)), axis=0, keepdims=True)
    wsum = p1 + p2
    w1 = g_w * (p1 / wsum)
    w2 = g_w * (p2 / wsum)

    a1 = row == i1
    a2 = row == i2
    onehot = a1 | a2
    before = jnp.dot(onehot.astype(BF16), us_ref[...], preferred_element_type=F32) + run_ref[...]
    rank1 = jnp.sum(jnp.where(a1, before, 0.0), axis=0, keepdims=True)
    rank2 = jnp.sum(jnp.where(a2, before, 0.0), axis=0, keepdims=True)
    run_ref[...] += jnp.sum(onehot.astype(F32), axis=1, keepdims=True)

    return jnp.concatenate([i1, i2, rank1, rank2, w1, w2,
                            jnp.zeros((ROUTE_ROWS - 6, n), F32)], axis=0)


def _res_router_kernel(a_ref, w_ref, r_ref, g_ref, wr_ref, br_ref, o_ref, hp_ref, meta_ref,
                       metat_ref, cnt_ref, run_ref, us_ref):
    sub = us_ref.shape[0]

    @pl.when(pl.program_id(0) == 0)
    def _():
        run_ref[...] = jnp.zeros_like(run_ref)
        r_i = lax.broadcasted_iota(jnp.int32, (sub, sub), 0)
        c_i = lax.broadcasted_iota(jnp.int32, (sub, sub), 1)
        us_ref[...] = (r_i < c_i).astype(BF16)

    subs = [slice(i * sub, (i + 1) * sub) for i in range(a_ref.shape[0] // sub)]

    def project(r):
        x = r_ref[r, :] + jnp.dot(a_ref[r, :], w_ref[...], preferred_element_type=F32)
        o_ref[r, :] = x
        halves, lg = _route_logits(x, g_ref, wr_ref, br_ref)
        hp_ref[0, r, :] = halves[0]
        hp_ref[1, r, :] = halves[1]
        return lg

    nxt = project(subs[0])
    for i, r in enumerate(subs):
        lg = nxt
        if i + 1 < len(subs):
            nxt = project(subs[i + 1])
        metat = _route_decide(lg, run_ref, us_ref)
        metat_ref[:, r] = metat[0:8, :]
        meta_ref[r, :] = jnp.concatenate(
            [metat, jnp.zeros((LANES - ROUTE_ROWS, sub), F32)], axis=0).T
    cnt_ref[...] = jnp.broadcast_to(run_ref[...], cnt_ref.shape)


def _res_router(a, w, res, g, w_group, b_group, w_expert, b_expert, *, tm, sub):
    t, k = a.shape
    d = w.shape[1]
    wr = jnp.zeros((d, LANES), BF16)
    wr = wr.at[:, :N_EXPERTS].set(w_expert.astype(BF16))
    wr = wr.at[:, N_EXPERTS:N_EXPERTS + N_GROUPS].set(w_group.astype(BF16))
    br = jnp.zeros((1, LANES), F32)
    br = br.at[0, :N_EXPERTS].set(b_expert.astype(F32))
    br = br.at[0, N_EXPERTS:N_EXPERTS + N_GROUPS].set(b_group.astype(F32))
    return pl.pallas_call(
        _res_router_kernel,
        out_shape=(jax.ShapeDtypeStruct((t, d), F32),
                   jax.ShapeDtypeStruct((2, t, HALF_W), U32),
                   jax.ShapeDtypeStruct((t, LANES), F32),
                   jax.ShapeDtypeStruct((8, t), F32),
                   jax.ShapeDtypeStruct((ROUTE_ROWS, LANES), F32)),
        grid=(t // tm,),
        in_specs=[
            pl.BlockSpec((tm, k), lambda i: (i, 0)),
            pl.BlockSpec((k, d), lambda i: (0, 0)),
            pl.BlockSpec((tm, d), lambda i: (i, 0)),
            pl.BlockSpec((1, d), lambda i: (0, 0)),
            pl.BlockSpec((d, LANES), lambda i: (0, 0)),
            pl.BlockSpec((1, LANES), lambda i: (0, 0)),
        ],
        out_specs=(pl.BlockSpec((tm, d), lambda i: (i, 0)),
                   pl.BlockSpec((2, tm, HALF_W), lambda i: (0, i, 0)),
                   pl.BlockSpec((tm, LANES), lambda i: (i, 0)),
                   pl.BlockSpec((8, tm), lambda i: (0, i)),
                   pl.BlockSpec((ROUTE_ROWS, LANES), lambda i: (0, 0))),
        scratch_shapes=[pltpu.VMEM((ROUTE_ROWS, 1), F32), pltpu.VMEM((sub, sub), BF16)],
        compiler_params=_cparams(("arbitrary",),
                                 fuse_inputs=[False, True, False, False, False, False]),
        name="res_router",
    )(a, w, res, g, wr, br)


def _sc_mesh():
    return plsc.VectorSubcoreMesh(core_axis_name="c", subcore_axis_name="s")


def _sc_scatter2(x, i0, i1, n_out):
    n, d = x.shape

    @pl.kernel(out_type=jax.ShapeDtypeStruct((n_out, d), x.dtype), mesh=_sc_mesh())
    def k(x_hbm, i0_hbm, i1_hbm, o_hbm):
        def body(x_vmem, i0_vmem, i1_vmem):
            pltpu.sync_copy(x_vmem, o_hbm.at[i0_vmem.at[0]])
            pltpu.sync_copy(x_vmem, o_hbm.at[i1_vmem.at[0]])

        pltpu.emit_pipeline(
            body,
            grid=(n // SC_WINDOW,),
            in_specs=[pl.BlockSpec((SC_WINDOW, d), lambda i: (i, 0)),
                      pl.BlockSpec((1, SC_WINDOW), lambda i: (0, i)),
                      pl.BlockSpec((1, SC_WINDOW), lambda i: (0, i))],
            out_specs=[],
            core_axis_name=("c", "s"),
            dimension_semantics=(pltpu.PARALLEL,),
        )(x_hbm, i0_hbm, i1_hbm)

    return k(x, i0.reshape(1, n), i1.reshape(1, n))


def _sc_gather(x, idx):
    n = idx.shape[0]
    d = x.shape[1]

    @pl.kernel(out_type=jax.ShapeDtypeStruct((n, d), x.dtype), mesh=_sc_mesh())
    def k(x_hbm, i_hbm, o_hbm):
        def body(i_vmem, o_vmem):
            pltpu.sync_copy(x_hbm.at[i_vmem.at[0]], o_vmem)

        pltpu.emit_pipeline(
            body,
            grid=(n // SC_WINDOW,),
            in_specs=[pl.BlockSpec((1, SC_WINDOW), lambda i: (0, i))],
            out_specs=[pl.BlockSpec((SC_WINDOW, d), lambda i: (i, 0))],
            core_axis_name=("c", "s"),
            dimension_semantics=(pltpu.PARALLEL,),
        )(i_hbm, o_hbm)

    return k(x, idx.reshape(1, n))


def _experts_kernel(te_ref, nu_ref, xs_ref, wgu_ref, wd_ref, ys_ref, wgu_bf_ref, wd_bf_ref):
    j = pl.program_id(0)

    @pl.when(j < nu_ref[0])
    def _():
        @pl.when((j == 0) | (te_ref[j] != te_ref[jnp.maximum(j - 1, 0)]))
        def _():
            wgu_bf_ref[...] = wgu_ref[...].astype(BF16)
            wd_bf_ref[...] = wd_ref[...].astype(BF16)

        subs = [slice(i * MOE_SUB, (i + 1) * MOE_SUB) for i in range(MOE_TR // MOE_SUB)]
        xs = [_unpack_halves(xs_ref[0, r, :], xs_ref[1, r, :]).astype(BF16) for r in subs]
        gus = [jnp.dot(x, wgu_bf_ref[...], preferred_element_type=F32) for x in xs]
        acts = [(gu[:, :D_EXPERT] * jax.nn.sigmoid(gu[:, :D_EXPERT]) * gu[:, D_EXPERT:]).astype(BF16)
                for gu in gus]
        ys = [jnp.dot(a, wd_bf_ref[...], preferred_element_type=F32) for a in acts]
        for r, y in zip(subs, ys):
            halves = _pack_halves(y)
            ys_ref[0, r, :] = halves[0]
            ys_ref[1, r, :] = halves[1]


def _experts(xs, tile_expert, n_used, w_gu, w_down, layer):
    _, rows, _ = xs.shape
    d = w_gu.shape[2]
    return pl.pallas_call(
        _experts_kernel,
        out_shape=jax.ShapeDtypeStruct(xs.shape, U32),
        grid_spec=pltpu.PrefetchScalarGridSpec(
            num_scalar_prefetch=2,
            grid=(rows // MOE_TR,),
            in_specs=[
                pl.BlockSpec((2, MOE_TR, HALF_W), lambda j, te, nu: (0, j, 0)),
                pl.BlockSpec((None, None, d, 2 * D_EXPERT),
                             lambda j, te, nu: (layer, te[j], 0, 0)),
                pl.BlockSpec((None, None, D_EXPERT, d),
                             lambda j, te, nu: (layer, te[j], 0, 0)),
            ],
            out_specs=pl.BlockSpec((2, MOE_TR, HALF_W), lambda j, te, nu: (0, j, 0)),
            scratch_shapes=[pltpu.VMEM((d, 2 * D_EXPERT), BF16), pltpu.VMEM((D_EXPERT, d), BF16)],
        ),
        compiler_params=_cparams(("arbitrary",)),
        name="moe_experts",
    )(tile_expert, n_used, xs, w_gu, w_down)


def _moe_combined(x_ref, z_ref, meta_ref):
    meta = meta_ref[...]
    y_a = _unpack_halves(z_ref[0], z_ref[2])
    y_b = _unpack_halves(z_ref[1], z_ref[3])
    return x_ref[...] + meta[:, 4:5] * y_a + meta[:, 5:6] * y_b


def _combine_norm_kernel(x_ref, z_ref, meta_ref, g_ref, o_ref):
    out = _moe_combined(x_ref, z_ref, meta_ref)
    ms = jnp.mean(out * out, axis=-1, keepdims=True)
    o_ref[...] = out * lax.rsqrt(ms + EPS) * g_ref[...]


def _combine_norm(x, z, meta, g, *, tm):
    t, d = x.shape
    return pl.pallas_call(
        _combine_norm_kernel,
        out_shape=jax.ShapeDtypeStruct((t, d), F32),
        grid=(t // tm,),
        in_specs=[
            pl.BlockSpec((tm, d), lambda i: (i, 0)),
            pl.BlockSpec((4, tm, HALF_W), lambda i: (0, i, 0)),
            pl.BlockSpec((tm, LANES), lambda i: (i, 0)),
            pl.BlockSpec((1, d), lambda i: (0, 0)),
        ],
        out_specs=pl.BlockSpec((tm, d), lambda i: (i, 0)),
        compiler_params=_cparams(("parallel",)),
        name="moe_combine",
    )(x, z, meta, g)


def _moe_layer(routing, w_gu, w_down, layer):
    hp, meta, metat, cnt = routing
    t = meta.shape[0]
    n_tiles = 2 * t // MOE_TR + N_EXPERTS
    rows = n_tiles * MOE_TR
    counts = cnt[:N_EXPERTS, 0].astype(jnp.int32)
    tiles_e = (counts + MOE_TR - 1) // MOE_TR
    tiles_end = jnp.cumsum(tiles_e)
    row_off = (tiles_end - tiles_e) * MOE_TR
    experts = jnp.arange(N_EXPERTS, dtype=jnp.int32)[:, None]

    def region_start(e_row):
        return jnp.sum(jnp.where(e_row[None, :] == experts, row_off[:, None], 0), axis=0)

    e_a, e_b = metat[0].astype(jnp.int32), metat[1].astype(jnp.int32)
    pos_a = region_start(e_a) + metat[2].astype(jnp.int32)
    pos_b = region_start(e_b) + metat[3].astype(jnp.int32)
    tile_ids = jnp.arange(n_tiles, dtype=jnp.int32)
    tile_expert = jnp.minimum(
        jnp.sum((tile_ids[:, None] >= tiles_end[None, :]).astype(jnp.int32), axis=1),
        N_EXPERTS - 1)
    n_used = tiles_end[-1:].astype(jnp.int32)

    xs = _sc_scatter2(hp.reshape(2 * t, HALF_W),
                      jnp.concatenate([pos_a, pos_a + rows]),
                      jnp.concatenate([pos_b, pos_b + rows]), 2 * rows)
    ys = _experts(xs.reshape(2, rows, HALF_W), tile_expert, n_used,
                  w_gu, w_down, layer)
    z = _sc_gather(ys.reshape(2 * rows, HALF_W),
                   jnp.concatenate([pos_a, pos_b, pos_a + rows, pos_b + rows]))
    return z.reshape(4, t, HALF_W)


def kernel(x, norm_mix, norm_ffn, ml_w_in, ml_conv, ml_b_i, ml_b_f, ml_norm, ml_w_out, da_w_in, da_lq1, da_lk1, da_lq2, da_lk2, da_norm, da_w_out, moe_w_group, moe_b_group, moe_w_expert, moe_b_expert, moe_w_gu, moe_w_down, final_norm):
    batch, seq, d = x.shape
    xt = x.reshape(batch * seq, d)

    p, gcol, grow = _ml_proj(xt, norm_mix[0].reshape(1, d), ml_w_in[0], ml_b_i[0], ml_b_f[0],
                             tm=1024, tn=3072, batch=batch, seq=seq, out_dtype=BF16)
    assert p.shape[1] == 2 * ML_QK + 2 * ML_V
    y = _mlstm_core(p, gcol, grow, ml_conv[0], ml_norm[0].reshape(1, ML_V), batch=batch, seq=seq)
    xt, *routing = _res_router(y, ml_w_out[0].astype(BF16), xt, norm_ffn[0].reshape(1, d),
                               moe_w_group[0], moe_b_group[0], moe_w_expert[0], moe_b_expert[0],
                               tm=RR_TM, sub=RR_SUB)
    z = _moe_layer(routing, moe_w_gu, moe_w_down, 0)

    lambda_init = 0.8 - 0.6 * math.exp(-0.3 * 1)
    xt, p, vt = _norm_matmul_vt(xt, z, routing[1], norm_mix[1].reshape(1, d),
                                da_w_in[0].astype(BF16), tm=512, sub=256, vt_start=2 * DA_QK,
                                batch=batch, seq=seq, out_dtype=BF16)
    lam_params = jnp.stack([da_lq1[0], da_lk1[0], da_lq2[0], da_lk2[0]]).astype(F32)
    a = _diff_attn(p, vt, lam_params, da_norm[0].reshape(DA_DV, 1), lambda_init, batch=batch,
                   seq=seq)
    xt, *routing = _res_router(a, da_w_out[0].astype(BF16), xt, norm_ffn[1].reshape(1, d),
                               moe_w_group[1], moe_b_group[1], moe_w_expert[1], moe_b_expert[1],
                               tm=RR_TM, sub=RR_SUB)
    z = _moe_layer(routing, moe_w_gu, moe_w_down, 1)
    out = _combine_norm(xt, z, routing[1], final_norm.reshape(1, d), tm=1024)
    return out.reshape(batch, seq, d)
```
